```python
import jax, jax.numpy as jnp
from jax import lax
import numpy as np

D_MODEL = 1024
BATCH = 16
SEQ = 2048
DEPTH = 4

GRID_W = 64
CTX_LEN = 256
ROPE_BASE = 10000.0
EPS = 1e-6
NEG = -1e30
BLOCK = 128

FOURIER_WIDTH = D_MODEL // 4
FOURIER_GROUPS = 4
FOURIER_GROUP_DIM = FOURIER_WIDTH // FOURIER_GROUPS

MLA_HEADS = D_MODEL // 128
MLA_NOPE = 64
MLA_ROPE = 32
MLA_V = 64
MLA_QK_DIM = MLA_NOPE + MLA_ROPE
MLA_Q_RANK = D_MODEL // 4
MLA_KV_RANK = D_MODEL // 8

SWA_Q_HEADS = D_MODEL // 256
SWA_KV_HEADS = 2
SWA_HEAD_DIM = 64
SWA_WINDOW = 128

MIX_WIDTH = FOURIER_WIDTH + MLA_HEADS * MLA_V + SWA_Q_HEADS * SWA_HEAD_DIM
IN_SPLITS = (FOURIER_WIDTH, MLA_Q_RANK, MLA_KV_RANK, MLA_ROPE,
             SWA_Q_HEADS * SWA_HEAD_DIM, SWA_KV_HEADS * SWA_HEAD_DIM, SWA_KV_HEADS * SWA_HEAD_DIM)
IN_WIDTH = sum(IN_SPLITS)
D_FF = 2816
N_MOD = 9

kernel_name = "hybrid_fourier_mla_swa_macaron_dit"


def rms_norm(x, g):
    xf = x.astype(jnp.float32)
    y = xf * lax.rsqrt(jnp.mean(xf * xf, axis=-1, keepdims=True) + EPS)
    return (y * g.astype(jnp.float32)).astype(x.dtype)


def modulation(cvec, w, b):
    m = jax.nn.silu(cvec) @ w + b
    return m.reshape(cvec.shape[0], N_MOD, D_MODEL)


def modulate(x, shift, scale):
    return x * (1.0 + scale[:, None, :]) + shift[:, None, :]


def swiglu(x, w1, w3, w2):
    return (jax.nn.silu(x @ w1) * (x @ w3)) @ w2


def macaron_half(h, m, i, g, w1, w3, w2):
    y = swiglu(modulate(rms_norm(h, g), m[:, 3 * i], m[:, 3 * i + 1]), w1, w3, w2)
    return h + 0.5 * m[:, 3 * i + 2, None, :] * y


def split_cols(u, widths):
    offs = [int(o) for o in np.cumsum(widths)[:-1]]
    return jnp.split(u, offs, axis=-1)


def axial_rope_tables(rows, dim):
    row = jnp.repeat(jnp.arange(rows, dtype=jnp.float32), GRID_W)
    col = jnp.tile(jnp.arange(GRID_W, dtype=jnp.float32), rows)
    axis_dim = dim // 2
    inv = ROPE_BASE ** (-jnp.arange(0, axis_dim, 2, dtype=jnp.float32) / axis_dim)
    ang = jnp.concatenate([row[:, None] * inv, col[:, None] * inv], axis=-1)
    return jnp.cos(ang), jnp.sin(ang)


def apply_rope(x, cos, sin):
    xf = x.astype(jnp.float32).reshape(*x.shape[:-1], x.shape[-1] // 2, 2)
    x1, x2 = xf[..., 0], xf[..., 1]
    cb, sb = cos[None, :, None, :], sin[None, :, None, :]
    y = jnp.stack([x1 * cb - x2 * sb, x1 * sb + x2 * cb], axis=-1).reshape(x.shape)
    return y.astype(x.dtype)


def fourier_mix(u):
    B, n, _ = u.shape
    z = u.astype(jnp.float32).reshape(B, n, FOURIER_GROUPS, FOURIER_GROUP_DIM)
    y = jnp.fft.fft2(z, axes=(1, 3), norm="ortho").real
    return y.reshape(B, n, FOURIER_WIDTH).astype(u.dtype)


def mla_queries(u_cq, g_cq, w_uq, g_mq):
    B, n, _ = u_cq.shape
    q = (rms_norm(u_cq, g_cq) @ w_uq).reshape(B, n, MLA_HEADS, MLA_QK_DIM)
    return rms_norm(q, g_mq)


def mla_keys_values(u_ckv, u_kr, g_ckv, w_ukv, g_mk):
    B, n, _ = u_ckv.shape
    kv = (rms_norm(u_ckv, g_ckv) @ w_ukv).reshape(B, n, MLA_HEADS, MLA_NOPE + MLA_V)
    k_nope, v = kv[..., :MLA_NOPE], kv[..., MLA_NOPE:]
    k_rope = jnp.broadcast_to(u_kr[:, :, None, :], (B, n, MLA_HEADS, MLA_ROPE))
    k = rms_norm(jnp.concatenate([k_nope, k_rope], axis=-1), g_mk)
    return k, v


def rope_tail(t, cos, sin):
    return jnp.concatenate([t[..., :MLA_NOPE], apply_rope(t[..., MLA_NOPE:], cos, sin)], axis=-1)


def context_attention(q, k, v, sink):
    s = jnp.einsum('bqhgd,bkhd->bhgqk', q, k).astype(jnp.float32) * (q.shape[-1] ** -0.5)
    if sink is None:
        p = jax.nn.softmax(s, axis=-1)
    else:
        sl = jnp.broadcast_to(sink.astype(jnp.float32)[None, :, :, None, None], s.shape[:-1] + (1,))
        p = jax.nn.softmax(jnp.concatenate([sl, s], axis=-1), axis=-1)[..., 1:]
    return jnp.einsum('bhgqk,bkhd->bqhgd', p.astype(v.dtype), v)


def dense_joint_attention(q, k, v, k_ctx, v_ctx):
    B, S, H, Dk = q.shape
    nb = S // BLOCK
    k_all = jnp.concatenate([k_ctx, k], axis=1)
    v_all = jnp.concatenate([v_ctx, v], axis=1)
    qb = jnp.moveaxis(q.reshape(B, nb, BLOCK, H, Dk), 1, 0)

    def block(qblk):
        s = jnp.einsum('bqhd,bkhd->bhqk', qblk, k_all).astype(jnp.float32) * (Dk ** -0.5)
        p = jax.nn.softmax(s, axis=-1)
        return jnp.einsum('bhqk,bkhd->bqhd', p.astype(v_all.dtype), v_all)

    o = lax.map(block, qb)
    return jnp.moveaxis(o, 0, 1).reshape(B, S, H, v.shape[-1])


def banded_window_attention(q, k, v, k_ctx, v_ctx, sink):
    B, S, Hq, D = q.shape
    Hkv = k.shape[2]
    G = Hq // Hkv
    nb = S // BLOCK
    C = k_ctx.shape[1]
    qb = q.reshape(B, nb, BLOCK, Hkv, G, D)

    def neighbours(t):
        tb = jnp.pad(t, ((0, 0), (BLOCK, BLOCK), (0, 0), (0, 0))).reshape(B, nb + 2, BLOCK, Hkv, t.shape[-1])
        return jnp.concatenate([tb[:, :-2], tb[:, 1:-1], tb[:, 2:]], axis=2)

    kb, vb = neighbours(k), neighbours(v)
    q_pos = jnp.arange(S).reshape(nb, BLOCK)
    k_pos = (jnp.arange(nb)[:, None] - 1) * BLOCK + jnp.arange(3 * BLOCK)[None, :]
    kp = k_pos[:, None, :]
    valid = (jnp.abs(q_pos[:, :, None] - kp) <= SWA_WINDOW) & (kp >= 0) & (kp < S)
    scale = D ** -0.5
    s_loc = jnp.einsum('bnqhgd,bnkhd->bnhgqk', qb, kb).astype(jnp.float32) * scale
    s_loc = jnp.where(valid[None, :, None, None], s_loc, NEG)
    s_ctx = jnp.einsum('bnqhgd,bchd->bnhgqc', qb, k_ctx).astype(jnp.float32) * scale
    sl = jnp.broadcast_to(sink.reshape(Hkv, G).astype(jnp.float32)[None, None, :, :, None, None],
                          s_ctx.shape[:-1] + (1,))
    p = jax.nn.softmax(jnp.concatenate([sl, s_ctx, s_loc], axis=-1), axis=-1).astype(v.dtype)
    o = (jnp.einsum('bnhgqc,bchd->bnqhgd', p[..., 1:1 + C], v_ctx)
         + jnp.einsum('bnhgqk,bnkhd->bnqhgd', p[..., 1 + C:], vb))
    return o.reshape(B, S, Hq, D)


def token_mixing(n, nc, w_in, g_cq, w_uq, g_ckv, w_ukv, g_mq, g_mk, g_sq, g_sk, sink, w_out,
                 cos_m, sin_m, cos_s, sin_s, with_ctx_out):
    B, S, _ = n.shape
    C = nc.shape[1]
    G = SWA_Q_HEADS // SWA_KV_HEADS
    f_in, u_cq, u_ckv, u_kr, u_sq, u_sk, u_sv = split_cols(n @ w_in, IN_SPLITS)
    fc_in, uc_cq, uc_ckv, uc_kr, uc_sq, uc_sk, uc_sv = split_cols(nc @ w_in, IN_SPLITS)

    cmk, cmv = mla_keys_values(uc_ckv, uc_kr, g_ckv, w_ukv, g_mk)
    csk = rms_norm(uc_sk.reshape(B, C, SWA_KV_HEADS, SWA_HEAD_DIM), g_sk)
    csv = uc_sv.reshape(B, C, SWA_KV_HEADS, SWA_HEAD_DIM)

    mq = rope_tail(mla_queries(u_cq, g_cq, w_uq, g_mq), cos_m, sin_m)
    mk, mv = mla_keys_values(u_ckv, u_kr, g_ckv, w_ukv, g_mk)
    mk = rope_tail(mk, cos_m, sin_m)
    a = dense_joint_attention(mq, mk, mv, cmk, cmv)

    sq = apply_rope(rms_norm(u_sq.reshape(B, S, SWA_Q_HEADS, SWA_HEAD_DIM), g_sq), cos_s, sin_s)
    sk = apply_rope(rms_norm(u_sk.reshape(B, S, SWA_KV_HEADS, SWA_HEAD_DIM), g_sk), cos_s, sin_s)
    sv = u_sv.reshape(B, S, SWA_KV_HEADS, SWA_HEAD_DIM)
    w = banded_window_attention(sq, sk, sv, csk, csv, sink)

    out = jnp.concatenate([fourier_mix(f_in), a.reshape(B, S, -1), w.reshape(B, S, -1)], axis=-1) @ w_out
    if not with_ctx_out:
        return out, None

    cmq = mla_queries(uc_cq, g_cq, w_uq, g_mq)[:, :, :, None, :]
    ac = context_attention(cmq, cmk, cmv, None)
    csq = rms_norm(uc_sq.reshape(B, C, SWA_Q_HEADS, SWA_HEAD_DIM), g_sq).reshape(B, C, SWA_KV_HEADS, G, SWA_HEAD_DIM)
    wc = context_attention(csq, csk, csv, sink.reshape(SWA_KV_HEADS, G))
    out_c = jnp.concatenate([fourier_mix(fc_in), ac.reshape(B, C, -1), wc.reshape(B, C, -1)], axis=-1) @ w_out
    return out, out_c


def setup_inputs(seed: int = 0) -> dict:
    key = jax.random.key(seed)
    ks = jax.random.split(key, 26)
    nrm = jax.random.normal
    L, D = DEPTH, D_MODEL
    f32 = jnp.float32

    def gain(k, shape):
        return 1.0 + 0.05 * nrm(k, shape, f32)

    return {
        "x": nrm(ks[0], (BATCH, SEQ, D), f32),
        "c": nrm(ks[1], (BATCH, D), f32),
        "ctx": nrm(ks[2], (BATCH, CTX_LEN, D), f32),
        "c_ctx": nrm(ks[3], (D,), f32),
        "w_ada": nrm(ks[4], (L, D, N_MOD * D), f32) * (0.5 * D ** -0.5),
        "b_ada": 0.02 * nrm(ks[5], (L, N_MOD * D), f32),
        "g_ffn1": gain(ks[6], (L, D)),
        "w1_ffn1": nrm(ks[7], (L, D, D_FF), f32) * D ** -0.5,
        "w3_ffn1": nrm(ks[8], (L, D, D_FF), f32) * D ** -0.5,
        "w2_ffn1": nrm(ks[9], (L, D_FF, D), f32) * D_FF ** -0.5,
        "g_mix": gain(ks[10], (L, D)),
        "w_in": nrm(ks[11], (L, D, IN_WIDTH), f32) * D ** -0.5,
        "g_cq": gain(ks[12], (L, MLA_Q_RANK)),
        "w_uq": nrm(ks[13], (L, MLA_Q_RANK, MLA_HEADS * MLA_QK_DIM), f32) * MLA_Q_RANK ** -0.5,
        "g_ckv": gain(ks[14], (L, MLA_KV_RANK)),
        "w_ukv": nrm(ks[15], (L, MLA_KV_RANK, MLA_HEADS * (MLA_NOPE + MLA_V)), f32) * MLA_KV_RANK ** -0.5,
        "g_mla_q": gain(ks[16], (L, MLA_QK_DIM)),
        "g_mla_k": gain(ks[17], (L, MLA_QK_DIM)),
        "g_swa_q": gain(ks[18], (L, SWA_HEAD_DIM)),
        "g_swa_k": gain(ks[19], (L, SWA_HEAD_DIM)),
        "sink": 0.5 * nrm(ks[20], (L, SWA_Q_HEADS), f32),
        "w_out": nrm(ks[21], (L, MIX_WIDTH, D), f32) * MIX_WIDTH ** -0.5,
        "g_ffn2": gain(ks[22], (L, D)),
        "w1_ffn2": nrm(ks[23], (L, D, D_FF), f32) * D ** -0.5,
        "w3_ffn2": nrm(ks[24], (L, D, D_FF), f32) * D ** -0.5,
        "w2_ffn2": nrm(ks[25], (L, D_FF, D), f32) * D_FF ** -0.5,
    }


def reference(x, c, ctx, c_ctx, w_ada, b_ada, g_ffn1, w1_ffn1, w3_ffn1, w2_ffn1, g_mix, w_in,
              g_cq, w_uq, g_ckv, w_ukv, g_mla_q, g_mla_k, g_swa_q, g_swa_k, sink, w_out,
              g_ffn2, w1_ffn2, w3_ffn2, w2_ffn2):
    rows = x.shape[1] // GRID_W
    cos_m, sin_m = axial_rope_tables(rows, MLA_ROPE)
    cos_s, sin_s = axial_rope_tables(rows, SWA_HEAD_DIM)
    h, hc = x, ctx
    for l in range(DEPTH):
        last = l == DEPTH - 1
        m = modulation(c, w_ada[l], b_ada[l])
        mc = modulation(c_ctx[None, :], w_ada[l], b_ada[l])
        h = macaron_half(h, m, 0, g_ffn1[l], w1_ffn1[l], w3_ffn1[l], w2_ffn1[l])
        hc = macaron_half(hc, mc, 0, g_ffn1[l], w1_ffn1[l], w3_ffn1[l], w2_ffn1[l])
        n = modulate(rms_norm(h, g_mix[l]), m[:, 3], m[:, 4])
        nc = modulate(rms_norm(hc, g_mix[l]), mc[:, 3], mc[:, 4])
        out, out_c = token_mixing(n, nc, w_in[l], g_cq[l], w_uq[l], g_ckv[l], w_ukv[l],
                                  g_mla_q[l], g_mla_k[l], g_swa_q[l], g_swa_k[l], sink[l], w_out[l],
                                  cos_m, sin_m, cos_s, sin_s, not last)
        h = h + m[:, 5, None, :] * out
        h = macaron_half(h, m, 2, g_ffn2[l], w1_ffn2[l], w3_ffn2[l], w2_ffn2[l])
        if not last:
            hc = hc + mc[:, 5, None, :] * out_c
            hc = macaron_half(hc, mc, 2, g_ffn2[l], w1_ffn2[l], w3_ffn2[l], w2_ffn2[l])
    return h
```

```python
import functools

import numpy as np
import jax
import jax.numpy as jnp
from jax import lax
from jax.experimental import pallas as pl
from jax.experimental.pallas import tpu as pltpu

F32 = jnp.float32
BF16 = jnp.bfloat16

D_MODEL = 1024
GRID_W = 64
ROPE_BASE = 10000.0
EPS = 1e-6
NEG = -1e30
N_MOD = 9
D_FF = 2816

FOURIER_WIDTH = 256
FOURIER_GROUP_DIM = 64
MLA_HEADS = 8
MLA_NOPE = 64
MLA_ROPE = 32
MLA_V = 64
MLA_QK_DIM = MLA_NOPE + MLA_ROPE
MLA_Q_RANK = 256
MLA_KV_RANK = 128
SWA_Q_HEADS = 4
SWA_KV_HEADS = 2
SWA_HEAD_DIM = 64
SWA_WINDOW = 128
IN_SPLITS = (256, 256, 128, 32, 256, 128, 128)
IN_WIDTH = sum(IN_SPLITS)

LANES = 128
HALF = LANES // 2
IN_WIDTH_P = 10 * LANES
VMEM_LIMIT = 56 * 1024 * 1024

TOK_TILE = 256
Q_TILE = 256
MOD_ROWS = 24
MOD_COL_TILE = 2304

OFF_F, OFF_CQ, OFF_CKV, OFF_KR, OFF_SQA, OFF_SQB, OFF_SK, OFF_SV = (
    0, 256, 512, 640, 768, 896, 1024, 1152)


def _deinterleave(n):
    return np.concatenate([np.arange(0, n, 2), np.arange(1, n, 2)])


def _w_in_cols():
    zero = IN_WIDTH
    o_f, o_cq, o_ckv, o_kr, o_sq, o_sk, o_sv = np.cumsum((0,) + IN_SPLITS)[:-1]
    cols = np.full((IN_WIDTH_P,), zero, np.int32)
    cols[OFF_F:OFF_F + 256] = o_f + np.arange(256)
    cols[OFF_CQ:OFF_CQ + 256] = o_cq + np.arange(256)
    cols[OFF_CKV:OFF_CKV + 128] = o_ckv + np.arange(128)
    cols[OFF_KR + MLA_NOPE:OFF_KR + MLA_QK_DIM] = o_kr + _deinterleave(MLA_ROPE)
    head = _deinterleave(SWA_HEAD_DIM)
    cols[OFF_SQA:OFF_SQA + 64] = o_sq + 0 * 64 + head
    cols[OFF_SQA + 64:OFF_SQA + 128] = o_sq + 2 * 64 + head
    cols[OFF_SQB:OFF_SQB + 64] = o_sq + 1 * 64 + head
    cols[OFF_SQB + 64:OFF_SQB + 128] = o_sq + 3 * 64 + head
    cols[OFF_SK:OFF_SK + 64] = o_sk + head
    cols[OFF_SK + 64:OFF_SK + 128] = o_sk + 64 + head
    cols[OFF_SV:OFF_SV + 128] = o_sv + np.arange(128)
    return cols


def _mla_head_perm():
    return np.concatenate([np.arange(MLA_NOPE), MLA_NOPE + _deinterleave(MLA_ROPE)])


def _w_uq_cols():
    zero = MLA_HEADS * MLA_QK_DIM
    cols = np.full((MLA_HEADS * LANES,), zero, np.int32)
    for h in range(MLA_HEADS):
        cols[h * LANES:h * LANES + MLA_QK_DIM] = h * MLA_QK_DIM + _mla_head_perm()
    return cols


def _w_uk_cols():
    zero = MLA_HEADS * (MLA_NOPE + MLA_V)
    cols = np.full((MLA_HEADS * LANES,), zero, np.int32)
    for h in range(MLA_HEADS):
        cols[h * LANES:h * LANES + MLA_NOPE] = h * (MLA_NOPE + MLA_V) + np.arange(MLA_NOPE)
    return cols


def _w_uv_cols():
    return np.concatenate([h * (MLA_NOPE + MLA_V) + MLA_NOPE + np.arange(MLA_V)
                           for h in range(MLA_HEADS)]).astype(np.int32)


def _w_out_rows():
    base = FOURIER_WIDTH + MLA_HEADS * MLA_V
    swa = np.concatenate([base + h * SWA_HEAD_DIM + np.arange(SWA_HEAD_DIM) for h in (0, 2, 1, 3)])
    return np.concatenate([np.arange(base), swa]).astype(np.int32)


def _take_cols(w, cols):
    pad = jnp.zeros(w.shape[:-1] + (1,), w.dtype)
    return jnp.take(jnp.concatenate([w, pad], axis=-1), jnp.asarray(cols), axis=-1)


def _rope_tables(seq, ctx):
    rows = seq // GRID_W
    row = jnp.repeat(jnp.arange(rows, dtype=F32), GRID_W)
    col = jnp.tile(jnp.arange(GRID_W, dtype=F32), rows)

    def angles(dim):
        axis_dim = dim // 2
        inv = ROPE_BASE ** (-jnp.arange(0, axis_dim, 2, dtype=F32) / axis_dim)
        return jnp.concatenate([row[:, None] * inv, col[:, None] * inv], axis=-1)

    def build(ang, first_lanes, second_lanes):
        cos, sin = jnp.cos(ang), jnp.sin(ang)
        c = jnp.ones((seq, LANES), F32)
        sa = jnp.zeros((seq, LANES), F32)
        sb = jnp.zeros((seq, LANES), F32)
        for lo in first_lanes:
            c = c.at[:, lo:lo + ang.shape[1]].set(cos)
            sa = sa.at[:, lo:lo + ang.shape[1]].set(-sin)
        for lo in second_lanes:
            c = c.at[:, lo:lo + ang.shape[1]].set(cos)
            sb = sb.at[:, lo:lo + ang.shape[1]].set(sin)
        ident = (jnp.ones((ctx, LANES), F32), jnp.zeros((ctx, LANES), F32), jnp.zeros((ctx, LANES), F32))
        return tuple(jnp.concatenate([i, t], axis=0) for i, t in zip(ident, (c, sa, sb)))

    half_m = MLA_ROPE // 2
    mla = build(angles(MLA_ROPE), (MLA_NOPE,), (MLA_NOPE + half_m,))
    half_s = SWA_HEAD_DIM // 2
    swa = build(angles(SWA_HEAD_DIM), (0, HALF), (half_s, HALF + half_s))
    return mla + swa


def _dft_cos_sin(n, scale):
    j = jnp.arange(n, dtype=jnp.int32)
    jk = (j[:, None] * j[None, :]) % n
    ang = jk.astype(F32) * (2.0 * np.pi / n)
    return jnp.cos(ang) * scale, jnp.sin(ang) * scale


def _channel_dft():
    c, s = _dft_cos_sin(FOURIER_GROUP_DIM, FOURIER_GROUP_DIM ** -0.5)
    eye = jnp.eye(FOURIER_WIDTH // FOURIER_GROUP_DIM, dtype=F32)
    return jnp.concatenate([jnp.kron(eye, c), jnp.kron(eye, s)], axis=1).astype(BF16)


def _rms_scale(x, width):
    return lax.rsqrt(jnp.sum(x * x, axis=-1, keepdims=True) * (1.0 / width) + EPS)


def _norm_mod(x, g, shift, scale):
    y = x * _rms_scale(x, x.shape[-1]) * g
    return y * (1.0 + scale) + shift


def _swiglu(xn, w1_ref, w3_ref, w2_ref):
    a = jnp.dot(xn, w1_ref[...], preferred_element_type=F32)
    b = jnp.dot(xn, w3_ref[...], preferred_element_type=F32)
    g = (a / (1.0 + jnp.exp(-a))) * b
    return jnp.dot(g.astype(BF16), w2_ref[...], preferred_element_type=F32)


def _rope(x, cos, sin_a, sin_b, pair_dist):
    return (x * cos + pltpu.roll(x, LANES - pair_dist, 1) * sin_a
            + pltpu.roll(x, pair_dist, 1) * sin_b)


def _low_lanes(shape):
    return lax.broadcasted_iota(jnp.int32, shape, len(shape) - 1) < HALF


def _mod_kernel(c_ref, w_ref, b_ref, o_ref):
    cv = c_ref[...]
    s = (cv / (1.0 + jnp.exp(-cv))).astype(BF16)
    o_ref[...] = jnp.dot(s, w_ref[...].astype(BF16), preferred_element_type=F32) + b_ref[...]


def _modulation(cc, w_ada, b_ada):
    n_layers, d, width = w_ada.shape
    return pl.pallas_call(
        _mod_kernel,
        grid=(n_layers, width // MOD_COL_TILE),
        in_specs=[
            pl.BlockSpec((MOD_ROWS, d), lambda l, j: (0, 0)),
            pl.BlockSpec((None, d, MOD_COL_TILE), lambda l, j: (l, 0, j)),
            pl.BlockSpec((None, 1, MOD_COL_TILE), lambda l, j: (l, 0, j)),
        ],
        out_specs=pl.BlockSpec((None, MOD_ROWS, MOD_COL_TILE), lambda l, j: (l, 0, j)),
        out_shape=jax.ShapeDtypeStruct((n_layers, MOD_ROWS, width), F32),
        compiler_params=pltpu.CompilerParams(
            dimension_semantics=("arbitrary", "arbitrary"), vmem_limit_bytes=VMEM_LIMIT),
        name="adaln_modulation",
    )(cc, w_ada, b_ada.reshape(n_layers, 1, width))


def _ffn_proj_kernel(h_ref, mod_ref, g1_ref, w1_ref, w3_ref, w2_ref, gmix_ref, win_ref,
                     gcq_ref, wuq_ref, gmq_ref, gckv_ref, wuk_ref, wuv_ref, gmk_ref,
                     gsq_ref, gsk_ref, dft_ref,
                     cm_ref, sam_ref, sbm_ref, cs_ref, sas_ref, sbs_ref,
                     ho_ref, q_ref, k_ref, v_ref, sq_ref, sk_ref, sv_ref, z_ref):
    mod = mod_ref[...]
    h = h_ref[...]
    xn = _norm_mod(h, g1_ref[...], mod[0:1], mod[1:2]).astype(BF16)
    h = h + (0.5 * mod[2:3]) * _swiglu(xn, w1_ref, w3_ref, w2_ref)
    ho_ref[...] = h

    n = _norm_mod(h, gmix_ref[...], mod[3:4], mod[4:5]).astype(BF16)
    u = jnp.dot(n, win_ref[...], preferred_element_type=F32)

    f = u[:, OFF_F:OFF_F + FOURIER_WIDTH].astype(BF16)
    z_ref[...] = jnp.dot(f, dft_ref[...], preferred_element_type=F32).astype(BF16)

    cm, sam, sbm = cm_ref[...], sam_ref[...], sbm_ref[...]
    cq = u[:, OFF_CQ:OFF_CQ + MLA_Q_RANK]
    cqn = (cq * _rms_scale(cq, MLA_Q_RANK) * gcq_ref[...]).astype(BF16)
    q = jnp.dot(cqn, wuq_ref[...], preferred_element_type=F32)
    gmq = gmq_ref[...]
    q_scale = MLA_QK_DIM ** -0.5
    for hd in range(MLA_HEADS):
        qh = q[:, hd * LANES:(hd + 1) * LANES]
        qg = qh * (_rms_scale(qh, MLA_QK_DIM) * q_scale) * gmq
        q_ref[:, hd * LANES:(hd + 1) * LANES] = _rope(qg, cm, sam, sbm, MLA_ROPE // 2).astype(BF16)

    ckv = u[:, OFF_CKV:OFF_CKV + MLA_KV_RANK]
    ckvn = (ckv * _rms_scale(ckv, MLA_KV_RANK) * gckv_ref[...]).astype(BF16)
    kn = jnp.dot(ckvn, wuk_ref[...], preferred_element_type=F32)
    v_ref[...] = jnp.dot(ckvn, wuv_ref[...], preferred_element_type=F32).astype(BF16)
    gmk = gmk_ref[...]
    kr = u[:, OFF_KR:OFF_KR + LANES]
    kr_ss = jnp.sum(kr * kr, axis=-1, keepdims=True)
    kr_rot = _rope(kr * gmk, cm, sam, sbm, MLA_ROPE // 2)
    for hd in range(MLA_HEADS):
        kh = kn[:, hd * LANES:(hd + 1) * LANES]
        ss = jnp.sum(kh * kh, axis=-1, keepdims=True) + kr_ss
        rs = lax.rsqrt(ss * (1.0 / MLA_QK_DIM) + EPS)
        k_ref[:, hd * LANES:(hd + 1) * LANES] = (rs * (kh * gmk + kr_rot)).astype(BF16)

    cs, sas, sbs = cs_ref[...], sas_ref[...], sbs_ref[...]

    def two_head_norm_rope(x, g, scale):
        low = _low_lanes(x.shape)
        x2 = x * x
        lo = jnp.sum(jnp.where(low, x2, 0.0), axis=-1, keepdims=True)
        hi = jnp.sum(jnp.where(low, 0.0, x2), axis=-1, keepdims=True)
        rs = jnp.where(low, lax.rsqrt(lo * (1.0 / SWA_HEAD_DIM) + EPS),
                       lax.rsqrt(hi * (1.0 / SWA_HEAD_DIM) + EPS))
        return _rope(x * (rs * scale) * g, cs, sas, sbs, SWA_HEAD_DIM // 2).astype(BF16)

    gsq = gsq_ref[...]
    s_scale = SWA_HEAD_DIM ** -0.5
    sq_ref[:, 0:LANES] = two_head_norm_rope(u[:, OFF_SQA:OFF_SQA + LANES], gsq, s_scale)
    sq_ref[:, LANES:2 * LANES] = two_head_norm_rope(u[:, OFF_SQB:OFF_SQB + LANES], gsq, s_scale)
    sk_ref[...] = two_head_norm_rope(u[:, OFF_SK:OFF_SK + LANES], gsk_ref[...], 1.0)
    sv_ref[...] = u[:, OFF_SV:OFF_SV + LANES].astype(BF16)


def _const_spec(block_shape, index_map):
    return pl.BlockSpec(block_shape, index_map, pipeline_mode=pl.Buffered(1))


def _mod_index(layer, n_batch):
    return lambda b, t: (layer, jnp.where(t == 0, n_batch, b), 0, 0)


def _ffn_proj(layer, hh, mod, p, tabs):
    n_batch, t_all, d = hh.shape
    tm = TOK_TILE
    tok = lambda width: pl.BlockSpec((None, tm, width), lambda b, t: (b, t, 0))
    lw = lambda *shape: _const_spec((None,) + shape, lambda b, t: (layer,) + (0,) * len(shape))
    tab = pl.BlockSpec((tm, LANES), lambda b, t: (t, 0))
    in_specs = [
        tok(d),
        pl.BlockSpec((None, None, N_MOD, d), _mod_index(layer, n_batch)),
        lw(1, d), lw(d, D_FF), lw(d, D_FF), lw(D_FF, d), lw(1, d), lw(d, IN_WIDTH_P),
        lw(1, MLA_Q_RANK), lw(MLA_Q_RANK, MLA_HEADS * LANES), lw(1, LANES),
        lw(1, MLA_KV_RANK), lw(MLA_KV_RANK, MLA_HEADS * LANES), lw(MLA_KV_RANK, MLA_HEADS * MLA_V),
        lw(1, LANES), lw(1, LANES), lw(1, LANES),
        _const_spec((FOURIER_WIDTH, 2 * FOURIER_WIDTH), lambda b, t: (0, 0)),
        tab, tab, tab, tab, tab, tab,
    ]
    widths = (d, MLA_HEADS * LANES, MLA_HEADS * LANES, MLA_HEADS * MLA_V, 2 * LANES, LANES, LANES,
              2 * FOURIER_WIDTH)
    dtypes = (F32,) + (BF16,) * 7
    return pl.pallas_call(
        _ffn_proj_kernel,
        grid=(n_batch, t_all // tm),
        in_specs=in_specs,
        out_specs=[tok(w) for w in widths],
        out_shape=[jax.ShapeDtypeStruct((n_batch, t_all, w), dt) for w, dt in zip(widths, dtypes)],
        compiler_params=pltpu.CompilerParams(
            dimension_semantics=("arbitrary", "arbitrary"), vmem_limit_bytes=VMEM_LIMIT),
        name="ffn_half_and_projections",
    )(hh, mod, p["g_ffn1"], p["w1_ffn1"], p["w3_ffn1"], p["w2_ffn1"], p["g_mix"], p["w_in"],
      p["g_cq"], p["w_uq"], p["g_mq"], p["g_ckv"], p["w_uk"], p["w_uv"], p["g_mk"],
      p["g_sq"], p["g_sk"], p["dft64"], *tabs)


def _fourier_kernel(z_ref, cl_ref, sl_ref, cc_ref, sc_ref, o_ref, *, ctx, with_ctx):
    w = FOURIER_WIDTH

    def mix(c_ref, s_ref, lo, n):
        zc = z_ref[lo:lo + n, 0:w]
        zs = z_ref[lo:lo + n, w:2 * w]
        return (jnp.dot(c_ref[...], zc, preferred_element_type=F32)
                - jnp.dot(s_ref[...], zs, preferred_element_type=F32)).astype(BF16)

    n_lat = cl_ref.shape[0]
    if with_ctx:
        o_ref[0:ctx, :] = mix(cc_ref, sc_ref, 0, ctx)
        o_ref[ctx:ctx + n_lat, :] = mix(cl_ref, sl_ref, ctx, n_lat)
    else:
        o_ref[...] = mix(cl_ref, sl_ref, ctx, n_lat)


def _fourier(zcs, dfts, ctx, with_ctx):
    n_batch, t_all, _ = zcs.shape
    seq = t_all - ctx
    rows = t_all if with_ctx else seq
    full = lambda a: _const_spec(a.shape, lambda b: (0, 0))
    return pl.pallas_call(
        functools.partial(_fourier_kernel, ctx=ctx, with_ctx=with_ctx),
        grid=(n_batch,),
        in_specs=[pl.BlockSpec((None, t_all, 2 * FOURIER_WIDTH), lambda b: (b, 0, 0))]
        + [full(a) for a in dfts],
        out_specs=pl.BlockSpec((None, rows, FOURIER_WIDTH), lambda b: (b, 0, 0)),
        out_shape=jax.ShapeDtypeStruct((n_batch, rows, FOURIER_WIDTH), BF16),
        compiler_params=pltpu.CompilerParams(
            dimension_semantics=("arbitrary",), vmem_limit_bytes=VMEM_LIMIT),
        name="fourier_positions",
    )(zcs, *dfts)


def _mla_kernel(q_ref, k_ref, v_ref, o_ref, *, ctx, q_off):
    t_all = k_ref.shape[0]

    def attend(n_keys):
        outs = []
        for hd in range(2):
            q = q_ref[:, hd * LANES:(hd + 1) * LANES]
            k = k_ref[0:n_keys, hd * LANES:(hd + 1) * LANES]
            s = lax.dot_general(q, k, (((1,), (1,)), ((), ())), preferred_element_type=F32)
            m = jnp.max(s, axis=-1, keepdims=True)
            p = jnp.exp(s - m)
            den = jnp.sum(p, axis=-1, keepdims=True)
            r = jnp.dot(p.astype(BF16), v_ref[0:n_keys, :], preferred_element_type=F32)
            outs.append(r / den)
        o_ref[...] = jnp.where(_low_lanes(outs[0].shape), outs[0], outs[1]).astype(BF16)

    if q_off == 0:
        is_ctx = pl.program_id(2) == 0
        pl.when(is_ctx)(lambda: attend(ctx))
        pl.when(jnp.logical_not(is_ctx))(lambda: attend(t_all))
    else:
        attend(t_all)


def _mla(q, k, v, ctx, with_ctx):
    n_batch, t_all, _ = q.shape
    tq = Q_TILE
    q_off = 0 if with_ctx else ctx // tq
    n_q = t_all // tq - q_off
    pairs = MLA_HEADS // 2
    return pl.pallas_call(
        functools.partial(_mla_kernel, ctx=ctx, q_off=q_off),
        grid=(n_batch, pairs, n_q),
        in_specs=[
            pl.BlockSpec((None, tq, 2 * LANES), lambda b, h, i: (b, i + q_off, h)),
            pl.BlockSpec((None, t_all, 2 * LANES), lambda b, h, i: (b, 0, h)),
            pl.BlockSpec((None, t_all, LANES), lambda b, h, i: (b, 0, h)),
        ],
        out_specs=pl.BlockSpec((None, tq, LANES), lambda b, h, i: (b, i, h)),
        out_shape=jax.ShapeDtypeStruct((n_batch, n_q * tq, pairs * LANES), BF16),
        compiler_params=pltpu.CompilerParams(
            dimension_semantics=("arbitrary", "arbitrary", "arbitrary"), vmem_limit_bytes=VMEM_LIMIT),
        name="mla_attention",
    )(q, k, v)


def _swa_kernel(sink_ref, q_ref, k_ref, v_ref, o_ref, *, layer, ctx, q_off):
    t_all = k_ref.shape[0]
    tq = q_ref.shape[0]
    span = tq + 2 * SWA_WINDOW
    kc = k_ref[0:ctx, :]
    vc = v_ref[0:ctx, :]

    def heads(local):
        res = []
        for hq in range(SWA_Q_HEADS):
            slab = hq % 2
            use_low = hq < 2
            qs = q_ref[:, slab * LANES:(slab + 1) * LANES]
            low = _low_lanes(qs.shape)
            qm = jnp.where(low if use_low else jnp.logical_not(low), qs, jnp.zeros_like(qs))
            sink = sink_ref[layer, hq]
            s_c = lax.dot_general(qm, kc, (((1,), (1,)), ((), ())), preferred_element_type=F32)
            m = jnp.maximum(jnp.max(s_c, axis=-1, keepdims=True), sink)
            if local is not None:
                kw, vw, valid = local
                s_l = lax.dot_general(qm, kw, (((1,), (1,)), ((), ())), preferred_element_type=F32)
                s_l = jnp.where(valid, s_l, NEG)
                m = jnp.maximum(m, jnp.max(s_l, axis=-1, keepdims=True))
            p_c = jnp.exp(s_c - m)
            den = jnp.sum(p_c, axis=-1, keepdims=True) + jnp.exp(sink - m)
            r = jnp.dot(p_c.astype(BF16), vc, preferred_element_type=F32)
            if local is not None:
                p_l = jnp.exp(s_l - m)
                den = den + jnp.sum(p_l, axis=-1, keepdims=True)
                r = r + jnp.dot(p_l.astype(BF16), vw, preferred_element_type=F32)
            res.append(r / den)
        low = _low_lanes(res[0].shape)
        o_ref[:, 0:LANES] = jnp.where(low, res[0], res[2]).astype(BF16)
        o_ref[:, LANES:2 * LANES] = jnp.where(low, res[1], res[3]).astype(BF16)

    def latent_tile():
        i = pl.program_id(1) + q_off
        start = jnp.clip(i * tq - SWA_WINDOW, 0, t_all - span)
        start = pl.multiple_of(start, LANES)
        kw = k_ref[pl.ds(start, span), :]
        vw = v_ref[pl.ds(start, span), :]
        q_pos = i * tq - ctx + lax.broadcasted_iota(jnp.int32, (tq, span), 0)
        k_pos = start - ctx + lax.broadcasted_iota(jnp.int32, (tq, span), 1)
        valid = (jnp.abs(q_pos - k_pos) <= SWA_WINDOW) & (k_pos >= 0)
        heads((kw, vw, valid))

    if q_off == 0:
        is_ctx = pl.program_id(1) == 0
        pl.when(is_ctx)(lambda: heads(None))
        pl.when(jnp.logical_not(is_ctx))(latent_tile)
    else:
        latent_tile()


def _swa(layer, sink, q, k, v, ctx, with_ctx):
    n_batch, t_all, _ = q.shape
    tq = Q_TILE
    q_off = 0 if with_ctx else ctx // tq
    n_q = t_all // tq - q_off
    return pl.pallas_call(
        functools.partial(_swa_kernel, layer=layer, ctx=ctx, q_off=q_off),
        grid=(n_batch, n_q),
        in_specs=[
            pl.BlockSpec(memory_space=pltpu.SMEM),
            pl.BlockSpec((None, tq, 2 * LANES), lambda b, i: (b, i + q_off, 0)),
            pl.BlockSpec((None, t_all, LANES), lambda b, i: (b, 0, 0)),
            pl.BlockSpec((None, t_all, LANES), lambda b, i: (b, 0, 0)),
        ],
        out_specs=pl.BlockSpec((None, tq, 2 * LANES), lambda b, i: (b, i, 0)),
        out_shape=jax.ShapeDtypeStruct((n_batch, n_q * tq, 2 * LANES), BF16),
        compiler_params=pltpu.CompilerParams(
            dimension_semantics=("arbitrary", "arbitrary"), vmem_limit_bytes=VMEM_LIMIT),
        name="window_attention",
    )(sink, q, k, v)


def _out_ffn_kernel(h_ref, yf_ref, a_ref, w_ref, mod_ref, wo_ref, g2_ref, w1_ref, w3_ref, w2_ref, o_ref):
    mod = mod_ref[...]
    n_f = yf_ref.shape[-1]
    n_a = a_ref.shape[-1]
    mixed = (jnp.dot(yf_ref[...], wo_ref[0:n_f, :], preferred_element_type=F32)
             + jnp.dot(a_ref[...], wo_ref[n_f:n_f + n_a, :], preferred_element_type=F32)
             + jnp.dot(w_ref[...], wo_ref[n_f + n_a:, :], preferred_element_type=F32))
    h = h_ref[...] + mod[5:6] * mixed
    xn = _norm_mod(h, g2_ref[...], mod[6:7], mod[7:8]).astype(BF16)
    o_ref[...] = h + (0.5 * mod[8:9]) * _swiglu(xn, w1_ref, w3_ref, w2_ref)


def _out_ffn(layer, hh, yf, a, w, mod, p, ctx, with_ctx):
    n_batch, t_all, d = hh.shape
    tm = TOK_TILE
    t_off = 0 if with_ctx else ctx // tm
    n_t = t_all // tm - t_off
    lw = lambda *shape: _const_spec((None,) + shape, lambda b, t: (layer,) + (0,) * len(shape))
    mix = lambda width: pl.BlockSpec((None, tm, width), lambda b, t: (b, t, 0))
    if with_ctx:
        mod_map = _mod_index(layer, n_batch)
    else:
        mod_map = lambda b, t: (layer, b, 0, 0)
    return pl.pallas_call(
        _out_ffn_kernel,
        grid=(n_batch, n_t),
        in_specs=[
            pl.BlockSpec((None, tm, d), lambda b, t: (b, t + t_off, 0)),
            mix(yf.shape[-1]), mix(a.shape[-1]), mix(w.shape[-1]),
            pl.BlockSpec((None, None, N_MOD, d), mod_map),
            lw(d, d), lw(1, d), lw(d, D_FF), lw(d, D_FF), lw(D_FF, d),
        ],
        out_specs=pl.BlockSpec((None, tm, d), lambda b, t: (b, t, 0)),
        out_shape=jax.ShapeDtypeStruct((n_batch, n_t * tm, d), F32),
        compiler_params=pltpu.CompilerParams(
            dimension_semantics=("arbitrary", "arbitrary"), vmem_limit_bytes=VMEM_LIMIT),
        name="out_projection_and_ffn_half",
    )(hh, yf, a, w, mod, p["w_out"], p["g_ffn2"], p["w1_ffn2"], p["w3_ffn2"], p["w2_ffn2"])


def _prepare_params(g_ffn1, w1_ffn1, w3_ffn1, w2_ffn1, g_mix, w_in, g_cq, w_uq, g_ckv, w_ukv,
                    g_mla_q, g_mla_k, g_swa_q, g_swa_k, w_out, g_ffn2, w1_ffn2, w3_ffn2, w2_ffn2):
    row = lambda g: g[:, None, :]
    head = _deinterleave(SWA_HEAD_DIM)
    swa_gain = np.concatenate([head, head])
    mla_gain = np.concatenate([_mla_head_perm(), np.full((LANES - MLA_QK_DIM,), MLA_QK_DIM)])
    return {
        "g_ffn1": row(g_ffn1), "w1_ffn1": w1_ffn1.astype(BF16), "w3_ffn1": w3_ffn1.astype(BF16),
        "w2_ffn1": w2_ffn1.astype(BF16),
        "g_mix": row(g_mix), "w_in": _take_cols(w_in, _w_in_cols()).astype(BF16),
        "g_cq": row(g_cq), "w_uq": _take_cols(w_uq, _w_uq_cols()).astype(BF16),
        "g_mq": row(_take_cols(g_mla_q, mla_gain)),
        "g_ckv": row(g_ckv), "w_uk": _take_cols(w_ukv, _w_uk_cols()).astype(BF16),
        "w_uv": jnp.take(w_ukv, jnp.asarray(_w_uv_cols()), axis=-1).astype(BF16),
        "g_mk": row(_take_cols(g_mla_k, mla_gain)),
        "g_sq": row(jnp.take(g_swa_q, jnp.asarray(swa_gain), axis=-1)),
        "g_sk": row(jnp.take(g_swa_k, jnp.asarray(swa_gain), axis=-1)),
        "w_out": jnp.take(w_out, jnp.asarray(_w_out_rows()), axis=1).astype(BF16),
        "g_ffn2": row(g_ffn2), "w1_ffn2": w1_ffn2.astype(BF16), "w3_ffn2": w3_ffn2.astype(BF16),
        "w2_ffn2": w2_ffn2.astype(BF16),
        "dft64": _channel_dft(),
    }


def kernel(x, c, ctx, c_ctx, w_ada, b_ada, g_ffn1, w1_ffn1, w3_ffn1, w2_ffn1, g_mix, w_in, g_cq, w_uq,
           g_ckv, w_ukv, g_mla_q, g_mla_k, g_swa_q, g_swa_k, sink, w_out, g_ffn2, w1_ffn2, w3_ffn2,
           w2_ffn2):
    n_batch, seq, d = x.shape
    n_ctx = ctx.shape[1]
    depth = w_ada.shape[0]
    assert d == D_MODEL and seq % GRID_W == 0 and n_batch + 1 <= MOD_ROWS
    assert n_ctx % TOK_TILE == 0 and seq % TOK_TILE == 0 and n_ctx == Q_TILE
    assert w_ada.shape[-1] % MOD_COL_TILE == 0

    p = _prepare_params(g_ffn1, w1_ffn1, w3_ffn1, w2_ffn1, g_mix, w_in, g_cq, w_uq, g_ckv, w_ukv,
                        g_mla_q, g_mla_k, g_swa_q, g_swa_k, w_out, g_ffn2, w1_ffn2, w3_ffn2, w2_ffn2)
    tabs = _rope_tables(seq, n_ctx)
    c_lat, s_lat = _dft_cos_sin(seq, seq ** -0.5)
    c_ctx_dft, s_ctx_dft = _dft_cos_sin(n_ctx, n_ctx ** -0.5)
    dfts = tuple(m.astype(BF16) for m in (c_lat, s_lat, c_ctx_dft, s_ctx_dft))

    cc = jnp.concatenate([c, c_ctx[None, :], jnp.zeros((MOD_ROWS - n_batch - 1, d), F32)], axis=0)
    mod = _modulation(cc, w_ada, b_ada).reshape(depth, MOD_ROWS, N_MOD, d)

    hh = jnp.concatenate([ctx, x], axis=1)
    for layer in range(depth):
        with_ctx = layer != depth - 1
        hh, q, k, v, sq, sk, sv, zcs = _ffn_proj(layer, hh, mod, p, tabs)
        yf = _fourier(zcs, dfts, n_ctx, with_ctx)
        a = _mla(q, k, v, n_ctx, with_ctx)
        w = _swa(layer, sink, sq, sk, sv, n_ctx, with_ctx)
        hh = _out_ffn(layer, hh, yf, a, w, mod, p, n_ctx, with_ctx)
    return hh
```

```python
import functools

import numpy as np
import jax
import jax.numpy as jnp
from jax import lax
from jax.experimental import pallas as pl
from jax.experimental.pallas import tpu as pltpu

F32 = jnp.float32
BF16 = jnp.bfloat16

D_MODEL = 1024
GRID_W = 64
ROPE_BASE = 10000.0
EPS = 1e-6
NEG = -1e30
LOG2E = 1.4426950408889634
N_MOD = 9
D_FF = 2816

FOURIER_WIDTH = 256
FOURIER_GROUP_DIM = 64
MLA_HEADS = 8
MLA_NOPE = 64
MLA_ROPE = 32
MLA_V = 64
MLA_QK_DIM = MLA_NOPE + MLA_ROPE
MLA_Q_RANK = 256
MLA_KV_RANK = 128
SWA_Q_HEADS = 4
SWA_KV_HEADS = 2
SWA_HEAD_DIM = 64
SWA_WINDOW = 128
IN_SPLITS = (256, 256, 128, 32, 256, 128, 128)
IN_WIDTH = sum(IN_SPLITS)

LANES = 128
HALF = LANES // 2
IN_WIDTH_P = 10 * LANES
VMEM_LIMIT = 56 * 1024 * 1024

TOK_TILE = 256
Q_TILE = 512
MLA_GROUP = 8
MOD_ROWS = 24
MOD_COL_TILE = 2304

OFF_F, OFF_CQ, OFF_CKV, OFF_KR, OFF_SQA, OFF_SQB, OFF_SK, OFF_SV = (
    0, 256, 512, 640, 768, 896, 1024, 1152)


def _deinterleave(n):
    return np.concatenate([np.arange(0, n, 2), np.arange(1, n, 2)])


def _w_in_cols():
    zero = IN_WIDTH
    o_f, o_cq, o_ckv, o_kr, o_sq, o_sk, o_sv = np.cumsum((0,) + IN_SPLITS)[:-1]
    cols = np.full((IN_WIDTH_P,), zero, np.int32)
    cols[OFF_F:OFF_F + 256] = o_f + np.arange(256)
    cols[OFF_CQ:OFF_CQ + 256] = o_cq + np.arange(256)
    cols[OFF_CKV:OFF_CKV + 128] = o_ckv + np.arange(128)
    cols[OFF_KR + MLA_NOPE:OFF_KR + MLA_QK_DIM] = o_kr + _deinterleave(MLA_ROPE)
    head = _deinterleave(SWA_HEAD_DIM)
    cols[OFF_SQA:OFF_SQA + 64] = o_sq + 0 * 64 + head
    cols[OFF_SQA + 64:OFF_SQA + 128] = o_sq + 2 * 64 + head
    cols[OFF_SQB:OFF_SQB + 64] = o_sq + 1 * 64 + head
    cols[OFF_SQB + 64:OFF_SQB + 128] = o_sq + 3 * 64 + head
    cols[OFF_SK:OFF_SK + 64] = o_sk + head
    cols[OFF_SK + 64:OFF_SK + 128] = o_sk + 64 + head
    cols[OFF_SV:OFF_SV + 128] = o_sv + np.arange(128)
    return cols


def _mla_head_perm():
    return np.concatenate([np.arange(MLA_NOPE), MLA_NOPE + _deinterleave(MLA_ROPE)])


def _w_uq_cols():
    zero = MLA_HEADS * MLA_QK_DIM
    cols = np.full((MLA_HEADS * LANES,), zero, np.int32)
    for h in range(MLA_HEADS):
        cols[h * LANES:h * LANES + MLA_QK_DIM] = h * MLA_QK_DIM + _mla_head_perm()
    return cols


def _w_uk_cols():
    zero = MLA_HEADS * (MLA_NOPE + MLA_V)
    cols = np.full((MLA_HEADS * LANES,), zero, np.int32)
    for h in range(MLA_HEADS):
        cols[h * LANES:h * LANES + MLA_NOPE] = h * (MLA_NOPE + MLA_V) + np.arange(MLA_NOPE)
    return cols


def _w_uv_cols():
    return np.concatenate([h * (MLA_NOPE + MLA_V) + MLA_NOPE + np.arange(MLA_V)
                           for h in range(MLA_HEADS)]).astype(np.int32)


def _w_out_rows():
    base = FOURIER_WIDTH + MLA_HEADS * MLA_V
    swa = np.concatenate([base + h * SWA_HEAD_DIM + np.arange(SWA_HEAD_DIM) for h in (0, 2, 1, 3)])
    return np.concatenate([np.arange(base), swa]).astype(np.int32)


def _take_cols(w, cols):
    pad = jnp.zeros(w.shape[:-1] + (1,), w.dtype)
    return jnp.take(jnp.concatenate([w, pad], axis=-1), jnp.asarray(cols), axis=-1)


def _rope_tables(seq, ctx):
    rows = seq // GRID_W
    row = jnp.repeat(jnp.arange(rows, dtype=F32), GRID_W)
    col = jnp.tile(jnp.arange(GRID_W, dtype=F32), rows)

    def angles(dim):
        axis_dim = dim // 2
        inv = ROPE_BASE ** (-jnp.arange(0, axis_dim, 2, dtype=F32) / axis_dim)
        return jnp.concatenate([row[:, None] * inv, col[:, None] * inv], axis=-1)

    def build(ang, first_lanes, second_lanes):
        cos, sin = jnp.cos(ang), jnp.sin(ang)
        c = jnp.ones((seq, LANES), F32)
        sa = jnp.zeros((seq, LANES), F32)
        sb = jnp.zeros((seq, LANES), F32)
        for lo in first_lanes:
            c = c.at[:, lo:lo + ang.shape[1]].set(cos)
            sa = sa.at[:, lo:lo + ang.shape[1]].set(-sin)
        for lo in second_lanes:
            c = c.at[:, lo:lo + ang.shape[1]].set(cos)
            sb = sb.at[:, lo:lo + ang.shape[1]].set(sin)
        ident = (jnp.ones((ctx, LANES), F32), jnp.zeros((ctx, LANES), F32), jnp.zeros((ctx, LANES), F32))
        return tuple(jnp.concatenate([t, i], axis=0) for i, t in zip(ident, (c, sa, sb)))

    half_m = MLA_ROPE // 2
    mla = build(angles(MLA_ROPE), (MLA_NOPE,), (MLA_NOPE + half_m,))
    half_s = SWA_HEAD_DIM // 2
    swa = build(angles(SWA_HEAD_DIM), (0, HALF), (half_s, HALF + half_s))
    return mla + swa


def _dft_cos_sin(n, scale):
    j = jnp.arange(n, dtype=jnp.int32)
    jk = (j[:, None] * j[None, :]) % n
    ang = jk.astype(F32) * (2.0 * np.pi / n)
    return jnp.cos(ang) * scale, jnp.sin(ang) * scale


def _channel_dft():
    c, s = _dft_cos_sin(FOURIER_GROUP_DIM, FOURIER_GROUP_DIM ** -0.5)
    eye = jnp.eye(FOURIER_WIDTH // FOURIER_GROUP_DIM, dtype=F32)
    return jnp.concatenate([jnp.kron(eye, c), jnp.kron(eye, s)], axis=1).astype(BF16)


def _rms_scale(x, width):
    return lax.rsqrt(jnp.sum(x * x, axis=-1, keepdims=True) * (1.0 / width) + EPS)


def _norm_mod(x, g, shift, scale):
    y = x * _rms_scale(x, x.shape[-1]) * g
    return y * (1.0 + scale) + shift


def _swiglu(xn, w1_ref, w3_ref, w2_ref):
    a = jnp.dot(xn, w1_ref[...], preferred_element_type=F32)
    b = jnp.dot(xn, w3_ref[...], preferred_element_type=F32)
    g = (a / (1.0 + jnp.exp(-a))) * b
    return jnp.dot(g.astype(BF16), w2_ref[...], preferred_element_type=F32)


def _rope(x, cos, sin_a, sin_b, pair_dist):
    return (x * cos + pltpu.roll(x, LANES - pair_dist, 1) * sin_a
            + pltpu.roll(x, pair_dist, 1) * sin_b)


def _low_lanes(shape):
    return lax.broadcasted_iota(jnp.int32, shape, len(shape) - 1) < HALF


def _mod_kernel(c_ref, w_ref, b_ref, o_ref):
    cv = c_ref[...]
    s = (cv / (1.0 + jnp.exp(-cv))).astype(BF16)
    o_ref[...] = jnp.dot(s, w_ref[...].astype(BF16), preferred_element_type=F32) + b_ref[...]


def _modulation(cc, w_ada, b_ada):
    n_layers, d, width = w_ada.shape
    return pl.pallas_call(
        _mod_kernel,
        grid=(n_layers, width // MOD_COL_TILE),
        in_specs=[
            pl.BlockSpec((MOD_ROWS, d), lambda l, j: (0, 0)),
            pl.BlockSpec((None, d, MOD_COL_TILE), lambda l, j: (l, 0, j)),
            pl.BlockSpec((None, 1, MOD_COL_TILE), lambda l, j: (l, 0, j)),
        ],
        out_specs=pl.BlockSpec((None, MOD_ROWS, MOD_COL_TILE), lambda l, j: (l, 0, j)),
        out_shape=jax.ShapeDtypeStruct((n_layers, MOD_ROWS, width), F32),
        compiler_params=pltpu.CompilerParams(
            dimension_semantics=("arbitrary", "arbitrary"), vmem_limit_bytes=VMEM_LIMIT),
        name="adaln_modulation",
    )(cc, w_ada, b_ada.reshape(n_layers, 1, width))


def _ffn_proj_kernel(h_ref, mod_ref, g1_ref, w1_ref, w3_ref, w2_ref, gmix_ref, win_ref,
                     gcq_ref, wuq_ref, gmq_ref, gckv_ref, wuk_ref, wuv_ref, gmk_ref,
                     gsq_ref, gsk_ref, dft_ref,
                     cm_ref, sam_ref, sbm_ref, cs_ref, sas_ref, sbs_ref,
                     ho_ref, q_ref, k_ref, v_ref, sq_ref, sk_ref, sv_ref, z_ref):
    mod = mod_ref[...]
    h = h_ref[...]
    xn = _norm_mod(h, g1_ref[...], mod[0:1], mod[1:2]).astype(BF16)
    h = h + (0.5 * mod[2:3]) * _swiglu(xn, w1_ref, w3_ref, w2_ref)
    ho_ref[...] = h

    n = _norm_mod(h, gmix_ref[...], mod[3:4], mod[4:5]).astype(BF16)
    u = jnp.dot(n, win_ref[...], preferred_element_type=F32)

    f = u[:, OFF_F:OFF_F + FOURIER_WIDTH].astype(BF16)
    z_ref[...] = jnp.dot(f, dft_ref[...], preferred_element_type=F32).astype(BF16)

    cm, sam, sbm = cm_ref[...], sam_ref[...], sbm_ref[...]
    cq = u[:, OFF_CQ:OFF_CQ + MLA_Q_RANK]
    cqn = (cq * _rms_scale(cq, MLA_Q_RANK) * gcq_ref[...]).astype(BF16)
    q = jnp.dot(cqn, wuq_ref[...], preferred_element_type=F32)
    gmq = gmq_ref[...]
    q_scale = MLA_QK_DIM ** -0.5 * LOG2E
    for hd in range(MLA_HEADS):
        qh = q[:, hd * LANES:(hd + 1) * LANES]
        qg = qh * (_rms_scale(qh, MLA_QK_DIM) * q_scale) * gmq
        q_ref[:, hd * LANES:(hd + 1) * LANES] = _rope(qg, cm, sam, sbm, MLA_ROPE // 2).astype(BF16)

    ckv = u[:, OFF_CKV:OFF_CKV + MLA_KV_RANK]
    ckvn = (ckv * _rms_scale(ckv, MLA_KV_RANK) * gckv_ref[...]).astype(BF16)
    kn = jnp.dot(ckvn, wuk_ref[...], preferred_element_type=F32)
    v_ref[...] = jnp.dot(ckvn, wuv_ref[...], preferred_element_type=F32).astype(BF16)
    gmk = gmk_ref[...]
    kr = u[:, OFF_KR:OFF_KR + LANES]
    kr_ss = jnp.sum(kr * kr, axis=-1, keepdims=True)
    kr_rot = _rope(kr * gmk, cm, sam, sbm, MLA_ROPE // 2)
    for hd in range(MLA_HEADS):
        kh = kn[:, hd * LANES:(hd + 1) * LANES]
        ss = jnp.sum(kh * kh, axis=-1, keepdims=True) + kr_ss
        rs = lax.rsqrt(ss * (1.0 / MLA_QK_DIM) + EPS)
        k_ref[:, hd * LANES:(hd + 1) * LANES] = (rs * (kh * gmk + kr_rot)).astype(BF16)

    cs, sas, sbs = cs_ref[...], sas_ref[...], sbs_ref[...]

    def two_head_norm_rope(x, g, scale):
        low = _low_lanes(x.shape)
        x2 = x * x
        lo = jnp.sum(jnp.where(low, x2, 0.0), axis=-1, keepdims=True)
        hi = jnp.sum(jnp.where(low, 0.0, x2), axis=-1, keepdims=True)
        rs = jnp.where(low, lax.rsqrt(lo * (1.0 / SWA_HEAD_DIM) + EPS),
                       lax.rsqrt(hi * (1.0 / SWA_HEAD_DIM) + EPS))
        return _rope(x * (rs * scale) * g, cs, sas, sbs, SWA_HEAD_DIM // 2).astype(BF16)

    gsq = gsq_ref[...]
    s_scale = SWA_HEAD_DIM ** -0.5 * LOG2E
    sq_ref[:, 0:LANES] = two_head_norm_rope(u[:, OFF_SQA:OFF_SQA + LANES], gsq, s_scale)
    sq_ref[:, LANES:2 * LANES] = two_head_norm_rope(u[:, OFF_SQB:OFF_SQB + LANES], gsq, s_scale)
    sk_ref[...] = two_head_norm_rope(u[:, OFF_SK:OFF_SK + LANES], gsk_ref[...], 1.0)
    sv_ref[...] = u[:, OFF_SV:OFF_SV + LANES].astype(BF16)


def _const_spec(block_shape, index_map):
    return pl.BlockSpec(block_shape, index_map, pipeline_mode=pl.Buffered(1))


def _mod_index(layer, n_batch, n_lat_tiles):
    return lambda b, t: (layer, jnp.where(t >= n_lat_tiles, n_batch, b), 0, 0)


def _ffn_proj(layer, hh, mod, p, tabs, seq):
    n_batch, t_all, d = hh.shape
    tm = TOK_TILE
    tok = lambda width: pl.BlockSpec((None, tm, width), lambda b, t: (b, t, 0))
    lw = lambda *shape: _const_spec((None,) + shape, lambda b, t: (layer,) + (0,) * len(shape))
    tab = pl.BlockSpec((tm, LANES), lambda b, t: (t, 0))
    in_specs = [
        tok(d),
        pl.BlockSpec((None, None, N_MOD, d), _mod_index(layer, n_batch, seq // tm)),
        lw(1, d), lw(d, D_FF), lw(d, D_FF), lw(D_FF, d), lw(1, d), lw(d, IN_WIDTH_P),
        lw(1, MLA_Q_RANK), lw(MLA_Q_RANK, MLA_HEADS * LANES), lw(1, LANES),
        lw(1, MLA_KV_RANK), lw(MLA_KV_RANK, MLA_HEADS * LANES), lw(MLA_KV_RANK, MLA_HEADS * MLA_V),
        lw(1, LANES), lw(1, LANES), lw(1, LANES),
        _const_spec((FOURIER_WIDTH, 2 * FOURIER_WIDTH), lambda b, t: (0, 0)),
        tab, tab, tab, tab, tab, tab,
    ]
    widths = (d, MLA_HEADS * LANES, MLA_HEADS * LANES, MLA_HEADS * MLA_V, 2 * LANES, LANES, LANES,
              2 * FOURIER_WIDTH)
    dtypes = (F32,) + (BF16,) * 7
    return pl.pallas_call(
        _ffn_proj_kernel,
        grid=(n_batch, t_all // tm),
        in_specs=in_specs,
        out_specs=[tok(w) for w in widths],
        out_shape=[jax.ShapeDtypeStruct((n_batch, t_all, w), dt) for w, dt in zip(widths, dtypes)],
        compiler_params=pltpu.CompilerParams(
            dimension_semantics=("arbitrary", "arbitrary"), vmem_limit_bytes=VMEM_LIMIT),
        name="ffn_half_and_projections",
    )(hh, mod, p["g_ffn1"], p["w1_ffn1"], p["w3_ffn1"], p["w2_ffn1"], p["g_mix"], p["w_in"],
      p["g_cq"], p["w_uq"], p["g_mq"], p["g_ckv"], p["w_uk"], p["w_uv"], p["g_mk"],
      p["g_sq"], p["g_sk"], p["dft64"], *tabs)


def _fourier_kernel(z_ref, cl_ref, sl_ref, cc_ref, sc_ref, o_ref, *, ctx, with_ctx):
    w = FOURIER_WIDTH

    def mix(c_ref, s_ref, lo, n):
        zc = z_ref[lo:lo + n, 0:w]
        zs = z_ref[lo:lo + n, w:2 * w]
        return (jnp.dot(c_ref[...], zc, preferred_element_type=F32)
                - jnp.dot(s_ref[...], zs, preferred_element_type=F32)).astype(BF16)

    n_lat = cl_ref.shape[0]
    o_ref[0:n_lat, :] = mix(cl_ref, sl_ref, 0, n_lat)
    if with_ctx:
        o_ref[n_lat:n_lat + ctx, :] = mix(cc_ref, sc_ref, n_lat, ctx)


def _fourier(zcs, dfts, ctx, with_ctx):
    n_batch, t_all, _ = zcs.shape
    seq = t_all - ctx
    rows = t_all if with_ctx else seq
    full = lambda a: _const_spec(a.shape, lambda b: (0, 0))
    return pl.pallas_call(
        functools.partial(_fourier_kernel, ctx=ctx, with_ctx=with_ctx),
        grid=(n_batch,),
        in_specs=[pl.BlockSpec((None, t_all, 2 * FOURIER_WIDTH), lambda b: (b, 0, 0))]
        + [full(a) for a in dfts],
        out_specs=pl.BlockSpec((None, rows, FOURIER_WIDTH), lambda b: (b, 0, 0)),
        out_shape=jax.ShapeDtypeStruct((n_batch, rows, FOURIER_WIDTH), BF16),
        compiler_params=pltpu.CompilerParams(
            dimension_semantics=("arbitrary",), vmem_limit_bytes=VMEM_LIMIT),
        name="fourier_positions",
    )(zcs, *dfts)


def _mla_heads(q_ref, k_ref, v_ref, o_ref):
    n_heads = q_ref.shape[-1] // LANES
    for pair in range(n_heads // 2):
        outs = []
        vp = v_ref[:, pair * LANES:(pair + 1) * LANES]
        for hd in (2 * pair, 2 * pair + 1):
            q = q_ref[:, hd * LANES:(hd + 1) * LANES]
            k = k_ref[:, hd * LANES:(hd + 1) * LANES]
            s = lax.dot_general(q, k, (((1,), (1,)), ((), ())), preferred_element_type=F32)
            m = jnp.max(s, axis=-1, keepdims=True)
            p = jnp.exp2(s - m)
            den = jnp.sum(p, axis=-1, keepdims=True)
            r = jnp.dot(p.astype(BF16), vp, preferred_element_type=F32)
            outs.append(r / den)
        o_ref[:, pair * LANES:(pair + 1) * LANES] = jnp.where(
            _low_lanes(outs[0].shape), outs[0], outs[1]).astype(BF16)


def _mla_ctx_kernel(q_ref, k_ref, v_ref, prev_ref, o_ref):
    del prev_ref
    _mla_heads(q_ref, k_ref, v_ref, o_ref)


def _mla(q, k, v, seq, with_ctx):
    n_batch, t_all, _ = q.shape
    n_ctx = t_all - seq
    tq = Q_TILE
    g = MLA_GROUP
    v_w = MLA_HEADS * MLA_V
    params = lambda n: pltpu.CompilerParams(
        dimension_semantics=("arbitrary",) * n, vmem_limit_bytes=VMEM_LIMIT)
    a = pl.pallas_call(
        _mla_heads,
        grid=(n_batch, MLA_HEADS // g, seq // tq),
        in_specs=[
            pl.BlockSpec((None, tq, g * LANES), lambda b, h, i: (b, i, h)),
            pl.BlockSpec((None, t_all, g * LANES), lambda b, h, i: (b, 0, h)),
            pl.BlockSpec((None, t_all, g * MLA_V), lambda b, h, i: (b, 0, h)),
        ],
        out_specs=pl.BlockSpec((None, tq, g * MLA_V), lambda b, h, i: (b, i, h)),
        out_shape=jax.ShapeDtypeStruct((n_batch, t_all, v_w), BF16),
        compiler_params=params(3),
        name="mla_attention",
    )(q, k, v)
    if not with_ctx:
        return a
    c_blk = seq // n_ctx
    ctx_rows = lambda width: pl.BlockSpec((None, n_ctx, width), lambda b: (b, c_blk, 0))
    return pl.pallas_call(
        _mla_ctx_kernel,
        grid=(n_batch,),
        in_specs=[ctx_rows(MLA_HEADS * LANES), ctx_rows(MLA_HEADS * LANES), ctx_rows(v_w),
                  pl.BlockSpec(memory_space=pl.ANY)],
        out_specs=ctx_rows(v_w),
        out_shape=jax.ShapeDtypeStruct(a.shape, a.dtype),
        input_output_aliases={3: 0},
        compiler_params=params(1),
        name="mla_context_attention",
    )(q, k, v, a)


def _swa_heads(sink_ref, layer, q_ref, kc, vc, local, o_ref):
    res = []
    for hq in range(SWA_Q_HEADS):
        slab = hq % 2
        use_low = hq < 2
        qs = q_ref[:, slab * LANES:(slab + 1) * LANES]
        low = _low_lanes(qs.shape)
        qm = jnp.where(low if use_low else jnp.logical_not(low), qs, jnp.zeros_like(qs))
        sink = sink_ref[layer, hq] * LOG2E
        s_c = lax.dot_general(qm, kc, (((1,), (1,)), ((), ())), preferred_element_type=F32)
        m = jnp.maximum(jnp.max(s_c, axis=-1, keepdims=True), sink)
        if local is not None:
            kw, vw, valid = local
            s_l = lax.dot_general(qm, kw, (((1,), (1,)), ((), ())), preferred_element_type=F32)
            s_l = jnp.where(valid, s_l, NEG)
            m = jnp.maximum(m, jnp.max(s_l, axis=-1, keepdims=True))
        p_c = jnp.exp2(s_c - m)
        den = jnp.sum(p_c, axis=-1, keepdims=True) + jnp.exp2(sink - m)
        r = jnp.dot(p_c.astype(BF16), vc, preferred_element_type=F32)
        if local is not None:
            p_l = jnp.exp2(s_l - m)
            den = den + jnp.sum(p_l, axis=-1, keepdims=True)
            r = r + jnp.dot(p_l.astype(BF16), vw, preferred_element_type=F32)
        res.append(r / den)
    low = _low_lanes(res[0].shape)
    o_ref[:, 0:LANES] = jnp.where(low, res[0], res[2]).astype(BF16)
    o_ref[:, LANES:2 * LANES] = jnp.where(low, res[1], res[3]).astype(BF16)


def _swa_latent_kernel(sink_ref, q_ref, k_ref, v_ref, o_ref, *, layer, seq):
    tq = q_ref.shape[0]
    span = tq + 2 * SWA_WINDOW
    i = pl.program_id(1)
    start = pl.multiple_of(jnp.clip(i * tq - SWA_WINDOW, 0, seq - span), LANES)
    kw = k_ref[pl.ds(start, span), :]
    vw = v_ref[pl.ds(start, span), :]
    q_pos = i * tq + lax.broadcasted_iota(jnp.int32, (tq, span), 0)
    k_pos = start + lax.broadcasted_iota(jnp.int32, (tq, span), 1)
    dist = q_pos - k_pos
    valid = (dist <= SWA_WINDOW) & (dist >= -SWA_WINDOW)
    _swa_heads(sink_ref, layer, q_ref, k_ref[seq:, :], v_ref[seq:, :], (kw, vw, valid), o_ref)


def _swa_ctx_kernel(sink_ref, q_ref, k_ref, v_ref, prev_ref, o_ref, *, layer):
    del prev_ref
    _swa_heads(sink_ref, layer, q_ref, k_ref[...], v_ref[...], None, o_ref)


def _swa(layer, sink, q, k, v, seq, with_ctx):
    n_batch, t_all, _ = q.shape
    n_ctx = t_all - seq
    tq = Q_TILE
    params = lambda n: pltpu.CompilerParams(
        dimension_semantics=("arbitrary",) * n, vmem_limit_bytes=VMEM_LIMIT)
    smem = pl.BlockSpec(memory_space=pltpu.SMEM)
    w = pl.pallas_call(
        functools.partial(_swa_latent_kernel, layer=layer, seq=seq),
        grid=(n_batch, seq // tq),
        in_specs=[
            smem,
            pl.BlockSpec((None, tq, 2 * LANES), lambda b, i: (b, i, 0)),
            pl.BlockSpec((None, t_all, LANES), lambda b, i: (b, 0, 0)),
            pl.BlockSpec((None, t_all, LANES), lambda b, i: (b, 0, 0)),
        ],
        out_specs=pl.BlockSpec((None, tq, 2 * LANES), lambda b, i: (b, i, 0)),
        out_shape=jax.ShapeDtypeStruct((n_batch, t_all, 2 * LANES), BF16),
        compiler_params=params(2),
        name="window_attention",
    )(sink, q, k, v)
    if not with_ctx:
        return w
    c_blk = seq // n_ctx
    ctx_rows = lambda width: pl.BlockSpec((None, n_ctx, width), lambda b: (b, c_blk, 0))
    return pl.pallas_call(
        functools.partial(_swa_ctx_kernel, layer=layer),
        grid=(n_batch,),
        in_specs=[smem, ctx_rows(2 * LANES), ctx_rows(LANES), ctx_rows(LANES),
                  pl.BlockSpec(memory_space=pl.ANY)],
        out_specs=ctx_rows(2 * LANES),
        out_shape=jax.ShapeDtypeStruct(w.shape, w.dtype),
        input_output_aliases={4: 0},
        compiler_params=params(1),
        name="window_context_attention",
    )(sink, q, k, v, w)


def _out_ffn_kernel(h_ref, yf_ref, a_ref, w_ref, mod_ref, wo_ref, g2_ref, w1_ref, w3_ref, w2_ref, o_ref):
    mod = mod_ref[...]
    n_f = yf_ref.shape[-1]
    n_a = a_ref.shape[-1]
    mixed = (jnp.dot(yf_ref[...], wo_ref[0:n_f, :], preferred_element_type=F32)
             + jnp.dot(a_ref[...], wo_ref[n_f:n_f + n_a, :], preferred_element_type=F32)
             + jnp.dot(w_ref[...], wo_ref[n_f + n_a:, :], preferred_element_type=F32))
    h = h_ref[...] + mod[5:6] * mixed
    xn = _norm_mod(h, g2_ref[...], mod[6:7], mod[7:8]).astype(BF16)
    o_ref[...] = h + (0.5 * mod[8:9]) * _swiglu(xn, w1_ref, w3_ref, w2_ref)


def _out_ffn(layer, hh, yf, a, w, mod, p, seq, with_ctx):
    n_batch, t_all, d = hh.shape
    tm = TOK_TILE
    n_t = (t_all if with_ctx else seq) // tm
    lw = lambda *shape: _const_spec((None,) + shape, lambda b, t: (layer,) + (0,) * len(shape))
    mix = lambda width: pl.BlockSpec((None, tm, width), lambda b, t: (b, t, 0))
    return pl.pallas_call(
        _out_ffn_kernel,
        grid=(n_batch, n_t),
        in_specs=[
            mix(d), mix(yf.shape[-1]), mix(a.shape[-1]), mix(w.shape[-1]),
            pl.BlockSpec((None, None, N_MOD, d), _mod_index(layer, n_batch, seq // tm)),
            lw(d, d), lw(1, d), lw(d, D_FF), lw(d, D_FF), lw(D_FF, d),
        ],
        out_specs=pl.BlockSpec((None, tm, d), lambda b, t: (b, t, 0)),
        out_shape=jax.ShapeDtypeStruct((n_batch, n_t * tm, d), F32),
        compiler_params=pltpu.CompilerParams(
            dimension_semantics=("arbitrary", "arbitrary"), vmem_limit_bytes=VMEM_LIMIT),
        name="out_projection_and_ffn_half",
    )(hh, yf, a, w, mod, p["w_out"], p["g_ffn2"], p["w1_ffn2"], p["w3_ffn2"], p["w2_ffn2"])


def _prepare_params(g_ffn1, w1_ffn1, w3_ffn1, w2_ffn1, g_mix, w_in, g_cq, w_uq, g_ckv, w_ukv,
                    g_mla_q, g_mla_k, g_swa_q, g_swa_k, w_out, g_ffn2, w1_ffn2, w3_ffn2, w2_ffn2):
    row = lambda g: g[:, None, :]
    head = _deinterleave(SWA_HEAD_DIM)
    swa_gain = np.concatenate([head, head])
    mla_gain = np.concatenate([_mla_head_perm(), np.full((LANES - MLA_QK_DIM,), MLA_QK_DIM)])
    return {
        "g_ffn1": row(g_ffn1), "w1_ffn1": w1_ffn1.astype(BF16), "w3_ffn1": w3_ffn1.astype(BF16),
        "w2_ffn1": w2_ffn1.astype(BF16),
        "g_mix": row(g_mix), "w_in": _take_cols(w_in, _w_in_cols()).astype(BF16),
        "g_cq": row(g_cq), "w_uq": _take_cols(w_uq, _w_uq_cols()).astype(BF16),
        "g_mq": row(_take_cols(g_mla_q, mla_gain)),
        "g_ckv": row(g_ckv), "w_uk": _take_cols(w_ukv, _w_uk_cols()).astype(BF16),
        "w_uv": jnp.take(w_ukv, jnp.asarray(_w_uv_cols()), axis=-1).astype(BF16),
        "g_mk": row(_take_cols(g_mla_k, mla_gain)),
        "g_sq": row(jnp.take(g_swa_q, jnp.asarray(swa_gain), axis=-1)),
        "g_sk": row(jnp.take(g_swa_k, jnp.asarray(swa_gain), axis=-1)),
        "w_out": jnp.take(w_out, jnp.asarray(_w_out_rows()), axis=1).astype(BF16),
        "g_ffn2": row(g_ffn2), "w1_ffn2": w1_ffn2.astype(BF16), "w3_ffn2": w3_ffn2.astype(BF16),
        "w2_ffn2": w2_ffn2.astype(BF16),
        "dft64": _channel_dft(),
    }


def kernel(x, c, ctx, c_ctx, w_ada, b_ada, g_ffn1, w1_ffn1, w3_ffn1, w2_ffn1, g_mix, w_in, g_cq, w_uq,
           g_ckv, w_ukv, g_mla_q, g_mla_k, g_swa_q, g_swa_k, sink, w_out, g_ffn2, w1_ffn2, w3_ffn2,
           w2_ffn2):
    n_batch, seq, d = x.shape
    n_ctx = ctx.shape[1]
    depth = w_ada.shape[0]
    assert d == D_MODEL and seq % GRID_W == 0 and n_batch + 1 <= MOD_ROWS
    assert n_ctx % TOK_TILE == 0 and seq % TOK_TILE == 0 and seq % Q_TILE == 0 and seq % n_ctx == 0
    assert Q_TILE + 2 * SWA_WINDOW <= seq and w_ada.shape[-1] % MOD_COL_TILE == 0

    p = _prepare_params(g_ffn1, w1_ffn1, w3_ffn1, w2_ffn1, g_mix, w_in, g_cq, w_uq, g_ckv, w_ukv,
                        g_mla_q, g_mla_k, g_swa_q, g_swa_k, w_out, g_ffn2, w1_ffn2, w3_ffn2, w2_ffn2)
    tabs = _rope_tables(seq, n_ctx)
    c_lat, s_lat = _dft_cos_sin(seq, seq ** -0.5)
    c_ctx_dft, s_ctx_dft = _dft_cos_sin(n_ctx, n_ctx ** -0.5)
    dfts = tuple(m.astype(BF16) for m in (c_lat, s_lat, c_ctx_dft, s_ctx_dft))

    cc = jnp.concatenate([c, c_ctx[None, :], jnp.zeros((MOD_ROWS - n_batch - 1, d), F32)], axis=0)
    mod = _modulation(cc, w_ada, b_ada).reshape(depth, MOD_ROWS, N_MOD, d)

    hh = jnp.concatenate([x, ctx], axis=1)
    for layer in range(depth):
        with_ctx = layer != depth - 1
        hh, q, k, v, sq, sk, sv, zcs = _ffn_proj(layer, hh, mod, p, tabs, seq)
        yf = _fourier(zcs, dfts, n_ctx, with_ctx)
        a = _mla(q, k, v, seq, with_ctx)
        w = _swa(layer, sink, sq, sk, sv, seq, with_ctx)
        hh = _out_ffn(layer, hh, yf, a, w, mod, p, seq, with_ctx)
    return hh
```

```python
import functools

import numpy as np
import jax
import jax.numpy as jnp
from jax import lax
from jax.experimental import pallas as pl
from jax.experimental.pallas import tpu as pltpu

F32 = jnp.float32
BF16 = jnp.bfloat16

D_MODEL = 1024
GRID_W = 64
ROPE_BASE = 10000.0
EPS = 1e-6
NEG = -1e30
LOG2E = 1.4426950408889634
N_MOD = 9
D_FF = 2816

FOURIER_WIDTH = 256
FOURIER_GROUP_DIM = 64
MLA_HEADS = 8
MLA_NOPE = 64
MLA_ROPE = 32
MLA_V = 64
MLA_QK_DIM = MLA_NOPE + MLA_ROPE
MLA_Q_RANK = 256
MLA_KV_RANK = 128
SWA_Q_HEADS = 4
SWA_KV_HEADS = 2
SWA_HEAD_DIM = 64
SWA_WINDOW = 128
IN_SPLITS = (256, 256, 128, 32, 256, 128, 128)
IN_WIDTH = sum(IN_SPLITS)

LANES = 128
HALF = LANES // 2
IN_WIDTH_P = 10 * LANES
VMEM_LIMIT = 56 * 1024 * 1024

TOK_TILE = 256
Q_TILE = 512
MLA_GROUP = 8
MOD_ROWS = 24
MOD_COL_TILE = 2304

OFF_F, OFF_CQ, OFF_CKV, OFF_KR, OFF_SQA, OFF_SQB, OFF_SK, OFF_SV = (
    0, 256, 512, 640, 768, 896, 1024, 1152)


def _deinterleave(n):
    return np.concatenate([np.arange(0, n, 2), np.arange(1, n, 2)])


def _w_in_cols():
    zero = IN_WIDTH
    o_f, o_cq, o_ckv, o_kr, o_sq, o_sk, o_sv = np.cumsum((0,) + IN_SPLITS)[:-1]
    cols = np.full((IN_WIDTH_P,), zero, np.int32)
    cols[OFF_F:OFF_F + 256] = o_f + np.arange(256)
    cols[OFF_CQ:OFF_CQ + 256] = o_cq + np.arange(256)
    cols[OFF_CKV:OFF_CKV + 128] = o_ckv + np.arange(128)
    cols[OFF_KR + MLA_NOPE:OFF_KR + MLA_QK_DIM] = o_kr + _deinterleave(MLA_ROPE)
    head = _deinterleave(SWA_HEAD_DIM)
    cols[OFF_SQA:OFF_SQA + 64] = o_sq + 0 * 64 + head
    cols[OFF_SQA + 64:OFF_SQA + 128] = o_sq + 2 * 64 + head
    cols[OFF_SQB:OFF_SQB + 64] = o_sq + 1 * 64 + head
    cols[OFF_SQB + 64:OFF_SQB + 128] = o_sq + 3 * 64 + head
    cols[OFF_SK:OFF_SK + 64] = o_sk + head
    cols[OFF_SK + 64:OFF_SK + 128] = o_sk + 64 + head
    cols[OFF_SV:OFF_SV + 128] = o_sv + np.arange(128)
    return cols


def _mla_head_perm():
    return np.concatenate([np.arange(MLA_NOPE), MLA_NOPE + _deinterleave(MLA_ROPE)])


def _w_uq_cols():
    zero = MLA_HEADS * MLA_QK_DIM
    cols = np.full((MLA_HEADS * LANES,), zero, np.int32)
    for h in range(MLA_HEADS):
        cols[h * LANES:h * LANES + MLA_QK_DIM] = h * MLA_QK_DIM + _mla_head_perm()
    return cols


def _w_uk_cols():
    zero = MLA_HEADS * (MLA_NOPE + MLA_V)
    cols = np.full((MLA_HEADS * LANES,), zero, np.int32)
    for h in range(MLA_HEADS):
        cols[h * LANES:h * LANES + MLA_NOPE] = h * (MLA_NOPE + MLA_V) + np.arange(MLA_NOPE)
    return cols


def _w_uv_cols():
    return np.concatenate([h * (MLA_NOPE + MLA_V) + MLA_NOPE + np.arange(MLA_V)
                           for h in range(MLA_HEADS)]).astype(np.int32)


def _w_out_rows():
    base = FOURIER_WIDTH + MLA_HEADS * MLA_V
    swa = np.concatenate([base + h * SWA_HEAD_DIM + np.arange(SWA_HEAD_DIM) for h in (0, 2, 1, 3)])
    return np.concatenate([np.arange(base), swa]).astype(np.int32)


def _take_cols(w, cols):
    pad = jnp.zeros(w.shape[:-1] + (1,), w.dtype)
    return jnp.take(jnp.concatenate([w, pad], axis=-1), jnp.asarray(cols), axis=-1)


def _rope_tables(seq, ctx):
    rows = seq // GRID_W
    row = jnp.repeat(jnp.arange(rows, dtype=F32), GRID_W)
    col = jnp.tile(jnp.arange(GRID_W, dtype=F32), rows)

    def angles(dim):
        axis_dim = dim // 2
        inv = ROPE_BASE ** (-jnp.arange(0, axis_dim, 2, dtype=F32) / axis_dim)
        return jnp.concatenate([row[:, None] * inv, col[:, None] * inv], axis=-1)

    def build(ang, first_lanes, second_lanes):
        cos, sin = jnp.cos(ang), jnp.sin(ang)
        c = jnp.ones((seq, LANES), F32)
        sa = jnp.zeros((seq, LANES), F32)
        sb = jnp.zeros((seq, LANES), F32)
        for lo in first_lanes:
            c = c.at[:, lo:lo + ang.shape[1]].set(cos)
            sa = sa.at[:, lo:lo + ang.shape[1]].set(-sin)
        for lo in second_lanes:
            c = c.at[:, lo:lo + ang.shape[1]].set(cos)
            sb = sb.at[:, lo:lo + ang.shape[1]].set(sin)
        ident = (jnp.ones((ctx, LANES), F32), jnp.zeros((ctx, LANES), F32), jnp.zeros((ctx, LANES), F32))
        return tuple(jnp.concatenate([t, i], axis=0) for i, t in zip(ident, (c, sa, sb)))

    half_m = MLA_ROPE // 2
    mla = build(angles(MLA_ROPE), (MLA_NOPE,), (MLA_NOPE + half_m,))
    half_s = SWA_HEAD_DIM // 2
    swa = build(angles(SWA_HEAD_DIM), (0, HALF), (half_s, HALF + half_s))
    return mla + swa


def _dft_cos_sin(n, scale):
    j = jnp.arange(n, dtype=jnp.int32)
    jk = (j[:, None] * j[None, :]) % n
    ang = jk.astype(F32) * (2.0 * np.pi / n)
    return jnp.cos(ang) * scale, jnp.sin(ang) * scale


def _channel_dft():
    c, s = _dft_cos_sin(FOURIER_GROUP_DIM, FOURIER_GROUP_DIM ** -0.5)
    eye = jnp.eye(FOURIER_WIDTH // FOURIER_GROUP_DIM, dtype=F32)
    return jnp.concatenate([jnp.kron(eye, c), jnp.kron(eye, s)], axis=1).astype(BF16)


def _rms_scale(x, width):
    return lax.rsqrt(jnp.sum(x * x, axis=-1, keepdims=True) * (1.0 / width) + EPS)


def _norm_mod(x, g, shift, scale):
    y = x * _rms_scale(x, x.shape[-1]) * g
    return y * (1.0 + scale) + shift


def _swiglu(xn, w1_ref, w3_ref, w2_ref):
    a = jnp.dot(xn, w1_ref[...], preferred_element_type=F32)
    b = jnp.dot(xn, w3_ref[...], preferred_element_type=F32)
    g = (a / (1.0 + jnp.exp(-a))) * b
    return jnp.dot(g.astype(BF16), w2_ref[...], preferred_element_type=F32)


def _rope(x, cos, sin_a, sin_b, pair_dist):
    return (x * cos + pltpu.roll(x, LANES - pair_dist, 1) * sin_a
            + pltpu.roll(x, pair_dist, 1) * sin_b)


def _low_lanes(shape):
    return lax.broadcasted_iota(jnp.int32, shape, len(shape) - 1) < HALF


def _mod_kernel(c_ref, w_ref, b_ref, o_ref):
    cv = c_ref[...]
    s = (cv / (1.0 + jnp.exp(-cv))).astype(BF16)
    o_ref[...] = jnp.dot(s, w_ref[...].astype(BF16), preferred_element_type=F32) + b_ref[...]


def _modulation(cc, w_ada, b_ada):
    n_layers, d, width = w_ada.shape
    return pl.pallas_call(
        _mod_kernel,
        grid=(n_layers, width // MOD_COL_TILE),
        in_specs=[
            pl.BlockSpec((MOD_ROWS, d), lambda l, j: (0, 0)),
            pl.BlockSpec((None, d, MOD_COL_TILE), lambda l, j: (l, 0, j)),
            pl.BlockSpec((None, 1, MOD_COL_TILE), lambda l, j: (l, 0, j)),
        ],
        out_specs=pl.BlockSpec((None, MOD_ROWS, MOD_COL_TILE), lambda l, j: (l, 0, j)),
        out_shape=jax.ShapeDtypeStruct((n_layers, MOD_ROWS, width), F32),
        compiler_params=pltpu.CompilerParams(
            dimension_semantics=("arbitrary", "arbitrary"), vmem_limit_bytes=VMEM_LIMIT),
        name="adaln_modulation",
    )(cc, w_ada, b_ada.reshape(n_layers, 1, width))


def _ffn_proj_kernel(h_ref, mod_ref, modp_ref, g1_ref, w1_ref, w3_ref, w2_ref, gmix_ref, win_ref,
                     gcq_ref, wuq_ref, gmq_ref, gckv_ref, wuk_ref, wuv_ref, gmk_ref,
                     gsq_ref, gsk_ref, dft_ref,
                     cm_ref, sam_ref, sbm_ref, cs_ref, sas_ref, sbs_ref,
                     ho_ref, q_ref, k_ref, v_ref, sq_ref, sk_ref, sv_ref, z_ref, hprev_ref):
    @pl.when(pl.program_id(0) == 0)
    def _():
        hprev_ref[...] = jnp.zeros_like(hprev_ref)

    modp = modp_ref[...]
    n = _norm_mod(hprev_ref[...], gmix_ref[...], modp[3:4], modp[4:5]).astype(BF16)
    u = jnp.dot(n, win_ref[...], preferred_element_type=F32)

    f = u[:, OFF_F:OFF_F + FOURIER_WIDTH].astype(BF16)
    z_ref[...] = jnp.dot(f, dft_ref[...], preferred_element_type=F32).astype(BF16)

    cm, sam, sbm = cm_ref[...], sam_ref[...], sbm_ref[...]
    cq = u[:, OFF_CQ:OFF_CQ + MLA_Q_RANK]
    cqn = (cq * _rms_scale(cq, MLA_Q_RANK) * gcq_ref[...]).astype(BF16)
    q = jnp.dot(cqn, wuq_ref[...], preferred_element_type=F32)
    gmq = gmq_ref[...]
    q_scale = MLA_QK_DIM ** -0.5 * LOG2E
    for hd in range(MLA_HEADS):
        qh = q[:, hd * LANES:(hd + 1) * LANES]
        qg = qh * (_rms_scale(qh, MLA_QK_DIM) * q_scale) * gmq
        q_ref[:, hd * LANES:(hd + 1) * LANES] = _rope(qg, cm, sam, sbm, MLA_ROPE // 2).astype(BF16)

    ckv = u[:, OFF_CKV:OFF_CKV + MLA_KV_RANK]
    ckvn = (ckv * _rms_scale(ckv, MLA_KV_RANK) * gckv_ref[...]).astype(BF16)
    kn = jnp.dot(ckvn, wuk_ref[...], preferred_element_type=F32)
    v_ref[...] = jnp.dot(ckvn, wuv_ref[...], preferred_element_type=F32).astype(BF16)
    gmk = gmk_ref[...]
    kr = u[:, OFF_KR:OFF_KR + LANES]
    kr_ss = jnp.sum(kr * kr, axis=-1, keepdims=True)
    kr_rot = _rope(kr * gmk, cm, sam, sbm, MLA_ROPE // 2)
    for hd in range(MLA_HEADS):
        kh = kn[:, hd * LANES:(hd + 1) * LANES]
        ss = jnp.sum(kh * kh, axis=-1, keepdims=True) + kr_ss
        rs = lax.rsqrt(ss * (1.0 / MLA_QK_DIM) + EPS)
        k_ref[:, hd * LANES:(hd + 1) * LANES] = (rs * (kh * gmk + kr_rot)).astype(BF16)

    cs, sas, sbs = cs_ref[...], sas_ref[...], sbs_ref[...]

    def two_head_norm_rope(x, g, scale):
        low = _low_lanes(x.shape)
        x2 = x * x
        lo = jnp.sum(jnp.where(low, x2, 0.0), axis=-1, keepdims=True)
        hi = jnp.sum(jnp.where(low, 0.0, x2), axis=-1, keepdims=True)
        rs = jnp.where(low, lax.rsqrt(lo * (1.0 / SWA_HEAD_DIM) + EPS),
                       lax.rsqrt(hi * (1.0 / SWA_HEAD_DIM) + EPS))
        return _rope(x * (rs * scale) * g, cs, sas, sbs, SWA_HEAD_DIM // 2).astype(BF16)

    gsq = gsq_ref[...]
    s_scale = SWA_HEAD_DIM ** -0.5 * LOG2E
    sq_ref[:, 0:LANES] = two_head_norm_rope(u[:, OFF_SQA:OFF_SQA + LANES], gsq, s_scale)
    sq_ref[:, LANES:2 * LANES] = two_head_norm_rope(u[:, OFF_SQB:OFF_SQB + LANES], gsq, s_scale)
    sk_ref[...] = two_head_norm_rope(u[:, OFF_SK:OFF_SK + LANES], gsk_ref[...], 1.0)
    sv_ref[...] = u[:, OFF_SV:OFF_SV + LANES].astype(BF16)

    mod = mod_ref[...]
    h = h_ref[...]
    xn = _norm_mod(h, g1_ref[...], mod[0:1], mod[1:2]).astype(BF16)
    h = h + (0.5 * mod[2:3]) * _swiglu(xn, w1_ref, w3_ref, w2_ref)
    ho_ref[...] = h
    hprev_ref[...] = h


def _const_spec(block_shape, index_map):
    return pl.BlockSpec(block_shape, index_map, pipeline_mode=pl.Buffered(1))


def _mod_index(layer, n_batch, n_lat_tiles):
    return lambda b, t: (layer, jnp.where(t >= n_lat_tiles, n_batch, b), 0, 0)


def _ffn_proj(layer, hh, mod, p, tabs, seq):
    n_batch, t_all, d = hh.shape
    tm = TOK_TILE
    tpb = t_all // tm
    n_tiles = n_batch * tpb
    n_lat = seq // tm
    cur = lambda s: jnp.minimum(s, n_tiles - 1)
    prev = lambda s: jnp.maximum(s - 1, 0)
    tile = lambda which, width: pl.BlockSpec(
        (None, tm, width), lambda s: (which(s) // tpb, which(s) % tpb, 0))

    def mod_row(which):
        def index(s):
            b, t = which(s) // tpb, which(s) % tpb
            return (layer, jnp.where(t >= n_lat, n_batch, b), 0, 0)
        return pl.BlockSpec((None, None, N_MOD, d), index)

    lw = lambda *shape: _const_spec((None,) + shape, lambda s: (layer,) + (0,) * len(shape))
    tab = pl.BlockSpec((tm, LANES), lambda s: (prev(s) % tpb, 0))
    in_specs = [
        tile(cur, d), mod_row(cur), mod_row(prev),
        lw(1, d), lw(d, D_FF), lw(d, D_FF), lw(D_FF, d), lw(1, d), lw(d, IN_WIDTH_P),
        lw(1, MLA_Q_RANK), lw(MLA_Q_RANK, MLA_HEADS * LANES), lw(1, LANES),
        lw(1, MLA_KV_RANK), lw(MLA_KV_RANK, MLA_HEADS * LANES), lw(MLA_KV_RANK, MLA_HEADS * MLA_V),
        lw(1, LANES), lw(1, LANES), lw(1, LANES),
        _const_spec((FOURIER_WIDTH, 2 * FOURIER_WIDTH), lambda s: (0, 0)),
        tab, tab, tab, tab, tab, tab,
    ]
    widths = (d, MLA_HEADS * LANES, MLA_HEADS * LANES, MLA_HEADS * MLA_V, 2 * LANES, LANES, LANES,
              2 * FOURIER_WIDTH)
    dtypes = (F32,) + (BF16,) * 7
    return pl.pallas_call(
        _ffn_proj_kernel,
        grid=(n_tiles + 1,),
        in_specs=in_specs,
        out_specs=[tile(cur, d)] + [tile(prev, w) for w in widths[1:]],
        out_shape=[jax.ShapeDtypeStruct((n_batch, t_all, w), dt) for w, dt in zip(widths, dtypes)],
        scratch_shapes=[pltpu.VMEM((tm, d), F32)],
        compiler_params=pltpu.CompilerParams(
            dimension_semantics=("arbitrary",), vmem_limit_bytes=VMEM_LIMIT),
        name="ffn_half_and_projections",
    )(hh, mod, mod, p["g_ffn1"], p["w1_ffn1"], p["w3_ffn1"], p["w2_ffn1"], p["g_mix"], p["w_in"],
      p["g_cq"], p["w_uq"], p["g_mq"], p["g_ckv"], p["w_uk"], p["w_uv"], p["g_mk"],
      p["g_sq"], p["g_sk"], p["dft64"], *tabs)


def _fourier_kernel(z_ref, cl_ref, sl_ref, cc_ref, sc_ref, o_ref, *, ctx, with_ctx):
    w = FOURIER_WIDTH

    def mix(c_ref, s_ref, lo, n):
        zc = z_ref[lo:lo + n, 0:w]
        zs = z_ref[lo:lo + n, w:2 * w]
        return (jnp.dot(c_ref[...], zc, preferred_element_type=F32)
                - jnp.dot(s_ref[...], zs, preferred_element_type=F32)).astype(BF16)

    n_lat = cl_ref.shape[0]
    o_ref[0:n_lat, :] = mix(cl_ref, sl_ref, 0, n_lat)
    if with_ctx:
        o_ref[n_lat:n_lat + ctx, :] = mix(cc_ref, sc_ref, n_lat, ctx)


def _fourier(zcs, dfts, ctx, with_ctx):
    n_batch, t_all, _ = zcs.shape
    seq = t_all - ctx
    rows = t_all if with_ctx else seq
    full = lambda a: _const_spec(a.shape, lambda b: (0, 0))
    return pl.pallas_call(
        functools.partial(_fourier_kernel, ctx=ctx, with_ctx=with_ctx),
        grid=(n_batch,),
        in_specs=[pl.BlockSpec((None, t_all, 2 * FOURIER_WIDTH), lambda b: (b, 0, 0))]
        + [full(a) for a in dfts],
        out_specs=pl.BlockSpec((None, rows, FOURIER_WIDTH), lambda b: (b, 0, 0)),
        out_shape=jax.ShapeDtypeStruct((n_batch, rows, FOURIER_WIDTH), BF16),
        compiler_params=pltpu.CompilerParams(
            dimension_semantics=("arbitrary",), vmem_limit_bytes=VMEM_LIMIT),
        name="fourier_positions",
    )(zcs, *dfts)


def _mla_heads(q_ref, k_ref, v_ref, o_ref):
    n_heads = q_ref.shape[-1] // LANES
    for pair in range(n_heads // 2):
        outs = []
        vp = v_ref[:, pair * LANES:(pair + 1) * LANES]
        for hd in (2 * pair, 2 * pair + 1):
            q = q_ref[:, hd * LANES:(hd + 1) * LANES]
            k = k_ref[:, hd * LANES:(hd + 1) * LANES]
            s = lax.dot_general(q, k, (((1,), (1,)), ((), ())), preferred_element_type=F32)
            m = jnp.max(s, axis=-1, keepdims=True)
            p = jnp.exp2(s - m)
            den = jnp.sum(p, axis=-1, keepdims=True)
            r = jnp.dot(p.astype(BF16), vp, preferred_element_type=F32)
            outs.append(r / den)
        o_ref[:, pair * LANES:(pair + 1) * LANES] = jnp.where(
            _low_lanes(outs[0].shape), outs[0], outs[1]).astype(BF16)


def _mla_ctx_kernel(q_ref, k_ref, v_ref, prev_ref, o_ref):
    del prev_ref
    _mla_heads(q_ref, k_ref, v_ref, o_ref)


def _mla(q, k, v, seq, with_ctx):
    n_batch, t_all, _ = q.shape
    n_ctx = t_all - seq
    tq = Q_TILE
    g = MLA_GROUP
    v_w = MLA_HEADS * MLA_V
    params = lambda n: pltpu.CompilerParams(
        dimension_semantics=("arbitrary",) * n, vmem_limit_bytes=VMEM_LIMIT)
    a = pl.pallas_call(
        _mla_heads,
        grid=(n_batch, MLA_HEADS // g, seq // tq),
        in_specs=[
            pl.BlockSpec((None, tq, g * LANES), lambda b, h, i: (b, i, h)),
            pl.BlockSpec((None, t_all, g * LANES), lambda b, h, i: (b, 0, h)),
            pl.BlockSpec((None, t_all, g * MLA_V), lambda b, h, i: (b, 0, h)),
        ],
        out_specs=pl.BlockSpec((None, tq, g * MLA_V), lambda b, h, i: (b, i, h)),
        out_shape=jax.ShapeDtypeStruct((n_batch, t_all, v_w), BF16),
        compiler_params=params(3),
        name="mla_attention",
    )(q, k, v)
    if not with_ctx:
        return a
    c_blk = seq // n_ctx
    ctx_rows = lambda width: pl.BlockSpec((None, n_ctx, width), lambda b: (b, c_blk, 0))
    return pl.pallas_call(
        _mla_ctx_kernel,
        grid=(n_batch,),
        in_specs=[ctx_rows(MLA_HEADS * LANES), ctx_rows(MLA_HEADS * LANES), ctx_rows(v_w),
                  pl.BlockSpec(memory_space=pl.ANY)],
        out_specs=ctx_rows(v_w),
        out_shape=jax.ShapeDtypeStruct(a.shape, a.dtype),
        input_output_aliases={3: 0},
        compiler_params=params(1),
        name="mla_context_attention",
    )(q, k, v, a)


def _swa_heads(sink_ref, layer, q_ref, kc, vc, local, o_ref):
    res = []
    for hq in range(SWA_Q_HEADS):
        slab = hq % 2
        use_low = hq < 2
        qs = q_ref[:, slab * LANES:(slab + 1) * LANES]
        low = _low_lanes(qs.shape)
        qm = jnp.where(low if use_low else jnp.logical_not(low), qs, jnp.zeros_like(qs))
        sink = sink_ref[layer, hq] * LOG2E
        s_c = lax.dot_general(qm, kc, (((1,), (1,)), ((), ())), preferred_element_type=F32)
        m = jnp.maximum(jnp.max(s_c, axis=-1, keepdims=True), sink)
        if local is not None:
            kw, vw, valid = local
            s_l = lax.dot_general(qm, kw, (((1,), (1,)), ((), ())), preferred_element_type=F32)
            s_l = jnp.where(valid, s_l, NEG)
            m = jnp.maximum(m, jnp.max(s_l, axis=-1, keepdims=True))
        p_c = jnp.exp2(s_c - m)
        den = jnp.sum(p_c, axis=-1, keepdims=True) + jnp.exp2(sink - m)
        r = jnp.dot(p_c.astype(BF16), vc, preferred_element_type=F32)
        if local is not None:
            p_l = jnp.exp2(s_l - m)
            den = den + jnp.sum(p_l, axis=-1, keepdims=True)
            r = r + jnp.dot(p_l.astype(BF16), vw, preferred_element_type=F32)
        res.append(r / den)
    low = _low_lanes(res[0].shape)
    o_ref[:, 0:LANES] = jnp.where(low, res[0], res[2]).astype(BF16)
    o_ref[:, LANES:2 * LANES] = jnp.where(low, res[1], res[3]).astype(BF16)


def _swa_latent_kernel(sink_ref, q_ref, k_ref, v_ref, o_ref, *, layer, seq):
    tq = q_ref.shape[0]
    span = tq + 2 * SWA_WINDOW
    i = pl.program_id(1)
    start = pl.multiple_of(jnp.clip(i * tq - SWA_WINDOW, 0, seq - span), LANES)
    kw = k_ref[pl.ds(start, span), :]
    vw = v_ref[pl.ds(start, span), :]
    q_pos = i * tq + lax.broadcasted_iota(jnp.int32, (tq, span), 0)
    k_pos = start + lax.broadcasted_iota(jnp.int32, (tq, span), 1)
    dist = q_pos - k_pos
    valid = (dist <= SWA_WINDOW) & (dist >= -SWA_WINDOW)
    _swa_heads(sink_ref, layer, q_ref, k_ref[seq:, :], v_ref[seq:, :], (kw, vw, valid), o_ref)


def _swa_ctx_kernel(sink_ref, q_ref, k_ref, v_ref, prev_ref, o_ref, *, layer):
    del prev_ref
    _swa_heads(sink_ref, layer, q_ref, k_ref[...], v_ref[...], None, o_ref)


def _swa(layer, sink, q, k, v, seq, with_ctx):
    n_batch, t_all, _ = q.shape
    n_ctx = t_all - seq
    tq = Q_TILE
    params = lambda n: pltpu.CompilerParams(
        dimension_semantics=("arbitrary",) * n, vmem_limit_bytes=VMEM_LIMIT)
    smem = pl.BlockSpec(memory_space=pltpu.SMEM)
    w = pl.pallas_call(
        functools.partial(_swa_latent_kernel, layer=layer, seq=seq),
        grid=(n_batch, seq // tq),
        in_specs=[
            smem,
            pl.BlockSpec((None, tq, 2 * LANES), lambda b, i: (b, i, 0)),
            pl.BlockSpec((None, t_all, LANES), lambda b, i: (b, 0, 0)),
            pl.BlockSpec((None, t_all, LANES), lambda b, i: (b, 0, 0)),
        ],
        out_specs=pl.BlockSpec((None, tq, 2 * LANES), lambda b, i: (b, i, 0)),
        out_shape=jax.ShapeDtypeStruct((n_batch, t_all, 2 * LANES), BF16),
        compiler_params=params(2),
        name="window_attention",
    )(sink, q, k, v)
    if not with_ctx:
        return w
    c_blk = seq // n_ctx
    ctx_rows = lambda width: pl.BlockSpec((None, n_ctx, width), lambda b: (b, c_blk, 0))
    return pl.pallas_call(
        functools.partial(_swa_ctx_kernel, layer=layer),
        grid=(n_batch,),
        in_specs=[smem, ctx_rows(2 * LANES), ctx_rows(LANES), ctx_rows(LANES),
                  pl.BlockSpec(memory_space=pl.ANY)],
        out_specs=ctx_rows(2 * LANES),
        out_shape=jax.ShapeDtypeStruct(w.shape, w.dtype),
        input_output_aliases={4: 0},
        compiler_params=params(1),
        name="window_context_attention",
    )(sink, q, k, v, w)


def _out_ffn_kernel(h_ref, yf_ref, a_ref, w_ref, mod_ref, wo_ref, g2_ref, w1_ref, w3_ref, w2_ref, o_ref):
    mod = mod_ref[...]
    n_f = yf_ref.shape[-1]
    n_a = a_ref.shape[-1]
    mixed = (jnp.dot(yf_ref[...], wo_ref[0:n_f, :], preferred_element_type=F32)
             + jnp.dot(a_ref[...], wo_ref[n_f:n_f + n_a, :], preferred_element_type=F32)
             + jnp.dot(w_ref[...], wo_ref[n_f + n_a:, :], preferred_element_type=F32))
    h = h_ref[...] + mod[5:6] * mixed
    xn = _norm_mod(h, g2_ref[...], mod[6:7], mod[7:8]).astype(BF16)
    o_ref[...] = h + (0.5 * mod[8:9]) * _swiglu(xn, w1_ref, w3_ref, w2_ref)


def _out_ffn(layer, hh, yf, a, w, mod, p, seq, with_ctx):
    n_batch, t_all, d = hh.shape
    tm = TOK_TILE
    n_t = (t_all if with_ctx else seq) // tm
    lw = lambda *shape: _const_spec((None,) + shape, lambda b, t: (layer,) + (0,) * len(shape))
    mix = lambda width: pl.BlockSpec((None, tm, width), lambda b, t: (b, t, 0))
    return pl.pallas_call(
        _out_ffn_kernel,
        grid=(n_batch, n_t),
        in_specs=[
            mix(d), mix(yf.shape[-1]), mix(a.shape[-1]), mix(w.shape[-1]),
            pl.BlockSpec((None, None, N_MOD, d), _mod_index(layer, n_batch, seq // tm)),
            lw(d, d), lw(1, d), lw(d, D_FF), lw(d, D_FF), lw(D_FF, d),
        ],
        out_specs=pl.BlockSpec((None, tm, d), lambda b, t: (b, t, 0)),
        out_shape=jax.ShapeDtypeStruct((n_batch, n_t * tm, d), F32),
        compiler_params=pltpu.CompilerParams(
            dimension_semantics=("arbitrary", "arbitrary"), vmem_limit_bytes=VMEM_LIMIT),
        name="out_projection_and_ffn_half",
    )(hh, yf, a, w, mod, p["w_out"], p["g_ffn2"], p["w1_ffn2"], p["w3_ffn2"], p["w2_ffn2"])


def _prepare_params(g_ffn1, w1_ffn1, w3_ffn1, w2_ffn1, g_mix, w_in, g_cq, w_uq, g_ckv, w_ukv,
                    g_mla_q, g_mla_k, g_swa_q, g_swa_k, w_out, g_ffn2, w1_ffn2, w3_ffn2, w2_ffn2):
    row = lambda g: g[:, None, :]
    head = _deinterleave(SWA_HEAD_DIM)
    swa_gain = np.concatenate([head, head])
    mla_gain = np.concatenate([_mla_head_perm(), np.full((LANES - MLA_QK_DIM,), MLA_QK_DIM)])
    return {
        "g_ffn1": row(g_ffn1), "w1_ffn1": w1_ffn1.astype(BF16), "w3_ffn1": w3_ffn1.astype(BF16),
        "w2_ffn1": w2_ffn1.astype(BF16),
        "g_mix": row(g_mix), "w_in": _take_cols(w_in, _w_in_cols()).astype(BF16),
        "g_cq": row(g_cq), "w_uq": _take_cols(w_uq, _w_uq_cols()).astype(BF16),
        "g_mq": row(_take_cols(g_mla_q, mla_gain)),
        "g_ckv": row(g_ckv), "w_uk": _take_cols(w_ukv, _w_uk_cols()).astype(BF16),
        "w_uv": jnp.take(w_ukv, jnp.asarray(_w_uv_cols()), axis=-1).astype(BF16),
        "g_mk": row(_take_cols(g_mla_k, mla_gain)),
        "g_sq": row(jnp.take(g_swa_q, jnp.asarray(swa_gain), axis=-1)),
        "g_sk": row(jnp.take(g_swa_k, jnp.asarray(swa_gain), axis=-1)),
        "w_out": jnp.take(w_out, jnp.asarray(_w_out_rows()), axis=1).astype(BF16),
        "g_ffn2": row(g_ffn2), "w1_ffn2": w1_ffn2.astype(BF16), "w3_ffn2": w3_ffn2.astype(BF16),
        "w2_ffn2": w2_ffn2.astype(BF16),
        "dft64": _channel_dft(),
    }


def kernel(x, c, ctx, c_ctx, w_ada, b_ada, g_ffn1, w1_ffn1, w3_ffn1, w2_ffn1, g_mix, w_in, g_cq, w_uq,
           g_ckv, w_ukv, g_mla_q, g_mla_k, g_swa_q, g_swa_k, sink, w_out, g_ffn2, w1_ffn2, w3_ffn2,
           w2_ffn2):
    n_batch, seq, d = x.shape
    n_ctx = ctx.shape[1]
    depth = w_ada.shape[0]
    assert d == D_MODEL and seq % GRID_W == 0 and n_batch + 1 <= MOD_ROWS
    assert n_ctx % TOK_TILE == 0 and seq % TOK_TILE == 0 and seq % Q_TILE == 0 and seq % n_ctx == 0
    assert Q_TILE + 2 * SWA_WINDOW <= seq and w_ada.shape[-1] % MOD_COL_TILE == 0

    p = _prepare_params(g_ffn1, w1_ffn1, w3_ffn1, w2_ffn1, g_mix, w_in, g_cq, w_uq, g_ckv, w_ukv,
                        g_mla_q, g_mla_k, g_swa_q, g_swa_k, w_out, g_ffn2, w1_ffn2, w3_ffn2, w2_ffn2)
    tabs = _rope_tables(seq, n_ctx)
    c_lat, s_lat = _dft_cos_sin(seq, seq ** -0.5)
    c_ctx_dft, s_ctx_dft = _dft_cos_sin(n_ctx, n_ctx ** -0.5)
    dfts = tuple(m.astype(BF16) for m in (c_lat, s_lat, c_ctx_dft, s_ctx_dft))

    cc = jnp.concatenate([c, c_ctx[None, :], jnp.zeros((MOD_ROWS - n_batch - 1, d), F32)], axis=0)
    mod = _modulation(cc, w_ada, b_ada).reshape(depth, MOD_ROWS, N_MOD, d)

    hh = jnp.concatenate([x, ctx], axis=1)
    for layer in range(depth):
        with_ctx = layer != depth - 1
        hh, q, k, v, sq, sk, sv, zcs = _ffn_proj(layer, hh, mod, p, tabs, seq)
        yf = _fourier(zcs, dfts, n_ctx, with_ctx)
        a = _mla(q, k, v, seq, with_ctx)
        w = _swa(layer, sink, sq, sk, sv, seq, with_ctx)
        hh = _out_ffn(layer, hh, yf, a, w, mod, p, seq, with_ctx)
    return hh
```

```python
import functools

import numpy as np
import jax
import jax.numpy as jnp
from jax import lax
from jax.experimental import pallas as pl
from jax.experimental.pallas import tpu as pltpu

F32 = jnp.float32
BF16 = jnp.bfloat16

D_MODEL = 1024
GRID_W = 64
ROPE_BASE = 10000.0
EPS = 1e-6
NEG = -1e30
LOG2E = 1.4426950408889634
N_MOD = 9
D_FF = 2816

FOURIER_WIDTH = 256
FOURIER_GROUP_DIM = 64
MLA_HEADS = 8
MLA_NOPE = 64
MLA_ROPE = 32
MLA_V = 64
MLA_QK_DIM = MLA_NOPE + MLA_ROPE
MLA_Q_RANK = 256
MLA_KV_RANK = 128
SWA_Q_HEADS = 4
SWA_KV_HEADS = 2
SWA_HEAD_DIM = 64
SWA_WINDOW = 128
IN_SPLITS = (256, 256, 128, 32, 256, 128, 128)
IN_WIDTH = sum(IN_SPLITS)

LANES = 128
HALF = LANES // 2
IN_WIDTH_P = 10 * LANES
VMEM_LIMIT = 56 * 1024 * 1024

TOK_TILE = 256
Q_TILE = 1024
MLA_Q_TILE = 512
ROW_CHUNK = 512
MOD_ROWS = 24
MOD_COL_TILE = 2304

OFF_F, OFF_CQ, OFF_CKV, OFF_KR, OFF_SQA, OFF_SQB, OFF_SK, OFF_SV = (
    0, 256, 512, 640, 768, 896, 1024, 1152)


def _deinterleave(n):
    return np.concatenate([np.arange(0, n, 2), np.arange(1, n, 2)])


def _w_in_cols():
    zero = IN_WIDTH
    o_f, o_cq, o_ckv, o_kr, o_sq, o_sk, o_sv = np.cumsum((0,) + IN_SPLITS)[:-1]
    cols = np.full((IN_WIDTH_P,), zero, np.int32)
    cols[OFF_F:OFF_F + 256] = o_f + np.arange(256)
    cols[OFF_CQ:OFF_CQ + 256] = o_cq + np.arange(256)
    cols[OFF_CKV:OFF_CKV + 128] = o_ckv + np.arange(128)
    cols[OFF_KR + MLA_NOPE:OFF_KR + MLA_QK_DIM] = o_kr + _deinterleave(MLA_ROPE)
    head = _deinterleave(SWA_HEAD_DIM)
    cols[OFF_SQA:OFF_SQA + 64] = o_sq + 0 * 64 + head
    cols[OFF_SQA + 64:OFF_SQA + 128] = o_sq + 2 * 64 + head
    cols[OFF_SQB:OFF_SQB + 64] = o_sq + 1 * 64 + head
    cols[OFF_SQB + 64:OFF_SQB + 128] = o_sq + 3 * 64 + head
    cols[OFF_SK:OFF_SK + 64] = o_sk + head
    cols[OFF_SK + 64:OFF_SK + 128] = o_sk + 64 + head
    cols[OFF_SV:OFF_SV + 128] = o_sv + np.arange(128)
    return cols


def _mla_head_perm():
    return np.concatenate([np.arange(MLA_NOPE), MLA_NOPE + _deinterleave(MLA_ROPE)])


def _w_uq_cols():
    zero = MLA_HEADS * MLA_QK_DIM
    cols = np.full((MLA_HEADS * LANES,), zero, np.int32)
    for h in range(MLA_HEADS):
        cols[h * LANES:h * LANES + MLA_QK_DIM] = h * MLA_QK_DIM + _mla_head_perm()
    return cols


def _w_uk_cols():
    zero = MLA_HEADS * (MLA_NOPE + MLA_V)
    cols = np.full((MLA_HEADS * LANES,), zero, np.int32)
    for h in range(MLA_HEADS):
        cols[h * LANES:h * LANES + MLA_NOPE] = h * (MLA_NOPE + MLA_V) + np.arange(MLA_NOPE)
    return cols


def _w_uv_cols():
    return np.concatenate([h * (MLA_NOPE + MLA_V) + MLA_NOPE + np.arange(MLA_V)
                           for h in range(MLA_HEADS)]).astype(np.int32)


def _w_out_rows():
    base = FOURIER_WIDTH + MLA_HEADS * MLA_V
    swa = np.concatenate([base + h * SWA_HEAD_DIM + np.arange(SWA_HEAD_DIM) for h in (0, 2, 1, 3)])
    return np.concatenate([np.arange(base), swa]).astype(np.int32)


def _take_cols(w, cols):
    pad = jnp.zeros(w.shape[:-1] + (1,), w.dtype)
    return jnp.take(jnp.concatenate([w, pad], axis=-1), jnp.asarray(cols), axis=-1)


def _rope_tables(seq, ctx):
    rows = seq // GRID_W
    pad = jnp.zeros((ctx,), F32)
    row = jnp.concatenate([jnp.repeat(jnp.arange(rows, dtype=F32), GRID_W), pad])[:, None]
    col = jnp.concatenate([jnp.tile(jnp.arange(GRID_W, dtype=F32), rows), pad])[:, None]

    def build(dim, first_lanes, second_lanes):
        axis_dim = dim // 2
        n_freq = axis_dim // 2
        inv = ROPE_BASE ** (-jnp.arange(0, axis_dim, 2, dtype=F32) / axis_dim)
        freq = np.full((LANES,), n_freq, np.int32)
        by_row = np.zeros((LANES,), bool)
        m_a = np.zeros((LANES,), np.float32)
        m_b = np.zeros((LANES,), np.float32)
        for lanes, mask in ((first_lanes, m_a), (second_lanes, m_b)):
            for lo in lanes:
                freq[lo:lo + 2 * n_freq] = np.concatenate([np.arange(n_freq), np.arange(n_freq)])
                by_row[lo:lo + n_freq] = True
                mask[lo:lo + 2 * n_freq] = 1.0
        inv_lane = jnp.take(jnp.concatenate([inv, jnp.zeros((1,), F32)]), jnp.asarray(freq))[None, :]
        ang = jnp.where(jnp.asarray(by_row)[None, :], row * inv_lane, col * inv_lane)
        sin = jnp.sin(ang)
        return jnp.cos(ang), sin * jnp.asarray(-m_a)[None, :], sin * jnp.asarray(m_b)[None, :]

    half_m = MLA_ROPE // 2
    mla = build(MLA_ROPE, (MLA_NOPE,), (MLA_NOPE + half_m,))
    half_s = SWA_HEAD_DIM // 2
    swa = build(SWA_HEAD_DIM, (0, HALF), (half_s, HALF + half_s))
    return mla + swa


def _dft_cos_sin(n, scale):
    def direct(rows_j, n_mod):
        k = jnp.arange(n, dtype=jnp.int32)
        ang = ((rows_j[:, None] * k[None, :]) % n_mod).astype(F32) * (2.0 * np.pi / n_mod)
        return jnp.cos(ang), jnp.sin(ang)

    inner = FOURIER_GROUP_DIM
    if n <= inner or n % inner:
        c, s = direct(jnp.arange(n, dtype=jnp.int32), n)
        return c * scale, s * scale
    outer = n // inner
    ca, sa = direct(jnp.arange(outer, dtype=jnp.int32), outer)
    cb, sb = direct(jnp.arange(inner, dtype=jnp.int32), n)
    cb, sb = cb * scale, sb * scale
    c = ca[:, None, :] * cb[None, :, :] - sa[:, None, :] * sb[None, :, :]
    s = sa[:, None, :] * cb[None, :, :] + ca[:, None, :] * sb[None, :, :]
    return c.reshape(n, n), s.reshape(n, n)


def _channel_dft():
    c, s = _dft_cos_sin(FOURIER_GROUP_DIM, FOURIER_GROUP_DIM ** -0.5)
    eye = jnp.eye(FOURIER_WIDTH // FOURIER_GROUP_DIM, dtype=F32)
    return jnp.concatenate([jnp.kron(eye, c), jnp.kron(eye, s)], axis=1).astype(BF16)


def _rms_scale(x, width):
    return lax.rsqrt(jnp.sum(x * x, axis=-1, keepdims=True) * (1.0 / width) + EPS)


def _norm_mod(x, g, shift, scale):
    y = x * _rms_scale(x, x.shape[-1]) * g
    return y * (1.0 + scale) + shift


def _swiglu(xn, w1_ref, w3_ref, w2_ref):
    a = jnp.dot(xn, w1_ref[...], preferred_element_type=F32)
    b = jnp.dot(xn, w3_ref[...], preferred_element_type=F32)
    g = (a / (1.0 + jnp.exp(-a))) * b
    return jnp.dot(g.astype(BF16), w2_ref[...], preferred_element_type=F32)


def _rope(x, cos, sin_a, sin_b, pair_dist):
    return (x * cos + pltpu.roll(x, LANES - pair_dist, 1) * sin_a
            + pltpu.roll(x, pair_dist, 1) * sin_b)


def _low_lanes(shape):
    return lax.broadcasted_iota(jnp.int32, shape, len(shape) - 1) < HALF


def _mod_kernel(c_ref, w_ref, b_ref, o_ref):
    cv = c_ref[...]
    s = (cv / (1.0 + jnp.exp(-cv))).astype(BF16)
    o_ref[...] = jnp.dot(s, w_ref[...].astype(BF16), preferred_element_type=F32) + b_ref[...]


def _modulation(cc, w_ada, b_ada):
    n_layers, d, width = w_ada.shape
    return pl.pallas_call(
        _mod_kernel,
        grid=(n_layers, width // MOD_COL_TILE),
        in_specs=[
            pl.BlockSpec((MOD_ROWS, d), lambda l, j: (0, 0)),
            pl.BlockSpec((None, d, MOD_COL_TILE), lambda l, j: (l, 0, j)),
            pl.BlockSpec((None, 1, MOD_COL_TILE), lambda l, j: (l, 0, j)),
        ],
        out_specs=pl.BlockSpec((None, MOD_ROWS, MOD_COL_TILE), lambda l, j: (l, 0, j)),
        out_shape=jax.ShapeDtypeStruct((n_layers, MOD_ROWS, width), F32),
        compiler_params=pltpu.CompilerParams(
            dimension_semantics=("arbitrary", "arbitrary"), vmem_limit_bytes=VMEM_LIMIT),
        name="adaln_modulation",
    )(cc, w_ada, b_ada.reshape(n_layers, 1, width))


def _ffn_proj_kernel(h_ref, mod_ref, modp_ref, g1_ref, w1_ref, w3_ref, w2_ref, gmix_ref, win_ref,
                     gcq_ref, wuq_ref, gmq_ref, gckv_ref, wuk_ref, wuv_ref, gmk_ref,
                     gsq_ref, gsk_ref, dft_ref,
                     cm_ref, sam_ref, sbm_ref, cs_ref, sas_ref, sbs_ref,
                     ho_ref, q_ref, k_ref, v_ref, sq_ref, sk_ref, sv_ref, z_ref, hprev_ref):
    @pl.when(pl.program_id(0) == 0)
    def _():
        hprev_ref[...] = jnp.zeros_like(hprev_ref)

    modp = modp_ref[...]
    n = _norm_mod(hprev_ref[...], gmix_ref[...], modp[3:4], modp[4:5]).astype(BF16)
    u = jnp.dot(n, win_ref[...], preferred_element_type=F32)

    f = u[:, OFF_F:OFF_F + FOURIER_WIDTH].astype(BF16)
    z_ref[...] = jnp.dot(f, dft_ref[...], preferred_element_type=F32).astype(BF16)

    cm, sam, sbm = cm_ref[...], sam_ref[...], sbm_ref[...]
    cq = u[:, OFF_CQ:OFF_CQ + MLA_Q_RANK]
    cqn = (cq * _rms_scale(cq, MLA_Q_RANK) * gcq_ref[...]).astype(BF16)
    q = jnp.dot(cqn, wuq_ref[...], preferred_element_type=F32)
    gmq = gmq_ref[...]
    q_scale = MLA_QK_DIM ** -0.5 * LOG2E
    for hd in range(MLA_HEADS):
        qh = q[:, hd * LANES:(hd + 1) * LANES]
        qg = qh * (_rms_scale(qh, MLA_QK_DIM) * q_scale) * gmq
        q_ref[:, hd * LANES:(hd + 1) * LANES] = _rope(qg, cm, sam, sbm, MLA_ROPE // 2).astype(BF16)

    ckv = u[:, OFF_CKV:OFF_CKV + MLA_KV_RANK]
    ckvn = (ckv * _rms_scale(ckv, MLA_KV_RANK) * gckv_ref[...]).astype(BF16)
    kn = jnp.dot(ckvn, wuk_ref[...], preferred_element_type=F32)
    v_ref[...] = jnp.dot(ckvn, wuv_ref[...], preferred_element_type=F32).astype(BF16)
    gmk = gmk_ref[...]
    kr = u[:, OFF_KR:OFF_KR + LANES]
    kr_ss = jnp.sum(kr * kr, axis=-1, keepdims=True)
    kr_rot = _rope(kr * gmk, cm, sam, sbm, MLA_ROPE // 2)
    for hd in range(MLA_HEADS):
        kh = kn[:, hd * LANES:(hd + 1) * LANES]
        ss = jnp.sum(kh * kh, axis=-1, keepdims=True) + kr_ss
        rs = lax.rsqrt(ss * (1.0 / MLA_QK_DIM) + EPS)
        k_ref[:, hd * LANES:(hd + 1) * LANES] = (rs * (kh * gmk + kr_rot)).astype(BF16)

    cs, sas, sbs = cs_ref[...], sas_ref[...], sbs_ref[...]

    def two_head_norm_rope(x, g, scale):
        low = _low_lanes(x.shape)
        x2 = x * x
        lo = jnp.sum(jnp.where(low, x2, 0.0), axis=-1, keepdims=True)
        hi = jnp.sum(jnp.where(low, 0.0, x2), axis=-1, keepdims=True)
        rs = jnp.where(low, lax.rsqrt(lo * (1.0 / SWA_HEAD_DIM) + EPS),
                       lax.rsqrt(hi * (1.0 / SWA_HEAD_DIM) + EPS))
        return _rope(x * (rs * scale) * g, cs, sas, sbs, SWA_HEAD_DIM // 2).astype(BF16)

    gsq = gsq_ref[...]
    s_scale = SWA_HEAD_DIM ** -0.5 * LOG2E
    sq_ref[:, 0:LANES] = two_head_norm_rope(u[:, OFF_SQA:OFF_SQA + LANES], gsq, s_scale)
    sq_ref[:, LANES:2 * LANES] = two_head_norm_rope(u[:, OFF_SQB:OFF_SQB + LANES], gsq, s_scale)
    sk_ref[...] = two_head_norm_rope(u[:, OFF_SK:OFF_SK + LANES], gsk_ref[...], 1.0)
    sv_ref[...] = u[:, OFF_SV:OFF_SV + LANES].astype(BF16)

    mod = mod_ref[...]
    h = h_ref[...]
    xn = _norm_mod(h, g1_ref[...], mod[0:1], mod[1:2]).astype(BF16)
    h = h + (0.5 * mod[2:3]) * _swiglu(xn, w1_ref, w3_ref, w2_ref)
    ho_ref[...] = h
    hprev_ref[...] = h


def _const_spec(block_shape, index_map):
    return pl.BlockSpec(block_shape, index_map, pipeline_mode=pl.Buffered(1))


def _mod_index(layer, n_batch, n_lat_tiles):
    return lambda b, t: (layer, jnp.where(t >= n_lat_tiles, n_batch, b), 0, 0)


def _ffn_proj(layer, hh, mod, p, tabs, seq):
    n_batch, t_all, d = hh.shape
    tm = TOK_TILE
    tpb = t_all // tm
    n_tiles = n_batch * tpb
    n_lat = seq // tm
    cur = lambda s: jnp.minimum(s, n_tiles - 1)
    prev = lambda s: jnp.maximum(s - 1, 0)
    tile = lambda which, width: pl.BlockSpec(
        (None, tm, width), lambda s: (which(s) // tpb, which(s) % tpb, 0))

    def mod_row(which):
        def index(s):
            b, t = which(s) // tpb, which(s) % tpb
            return (layer, jnp.where(t >= n_lat, n_batch, b), 0, 0)
        return pl.BlockSpec((None, None, N_MOD, d), index)

    lw = lambda *shape: _const_spec((None,) + shape, lambda s: (layer,) + (0,) * len(shape))
    tab = pl.BlockSpec((tm, LANES), lambda s: (prev(s) % tpb, 0))
    in_specs = [
        tile(cur, d), mod_row(cur), mod_row(prev),
        lw(1, d), lw(d, D_FF), lw(d, D_FF), lw(D_FF, d), lw(1, d), lw(d, IN_WIDTH_P),
        lw(1, MLA_Q_RANK), lw(MLA_Q_RANK, MLA_HEADS * LANES), lw(1, LANES),
        lw(1, MLA_KV_RANK), lw(MLA_KV_RANK, MLA_HEADS * LANES), lw(MLA_KV_RANK, MLA_HEADS * MLA_V),
        lw(1, LANES), lw(1, LANES), lw(1, LANES),
        _const_spec((FOURIER_WIDTH, 2 * FOURIER_WIDTH), lambda s: (0, 0)),
        tab, tab, tab, tab, tab, tab,
    ]
    widths = (d, MLA_HEADS * LANES, MLA_HEADS * LANES, MLA_HEADS * MLA_V, 2 * LANES, LANES, LANES,
              2 * FOURIER_WIDTH)
    dtypes = (F32,) + (BF16,) * 7
    return pl.pallas_call(
        _ffn_proj_kernel,
        grid=(n_tiles + 1,),
        in_specs=in_specs,
        out_specs=[tile(cur, d)] + [tile(prev, w) for w in widths[1:]],
        out_shape=[jax.ShapeDtypeStruct((n_batch, t_all, w), dt) for w, dt in zip(widths, dtypes)],
        scratch_shapes=[pltpu.VMEM((tm, d), F32)],
        compiler_params=pltpu.CompilerParams(
            dimension_semantics=("arbitrary",), vmem_limit_bytes=VMEM_LIMIT),
        name="ffn_half_and_projections",
    )(hh, mod, mod, p["g_ffn1"], p["w1_ffn1"], p["w3_ffn1"], p["w2_ffn1"], p["g_mix"], p["w_in"],
      p["g_cq"], p["w_uq"], p["g_mq"], p["g_ckv"], p["w_uk"], p["w_uv"], p["g_mk"],
      p["g_sq"], p["g_sk"], p["dft64"], *tabs)


def _fourier_kernel(z_ref, cl_ref, sl_ref, cc_ref, sc_ref, o_ref, *, ctx, with_ctx):
    w = FOURIER_WIDTH

    def mix(c_ref, s_ref, lo, n):
        zc = z_ref[lo:lo + n, 0:w]
        zs = z_ref[lo:lo + n, w:2 * w]
        return (jnp.dot(c_ref[...], zc, preferred_element_type=F32)
                - jnp.dot(s_ref[...], zs, preferred_element_type=F32)).astype(BF16)

    n_lat = cl_ref.shape[0]
    o_ref[0:n_lat, :] = mix(cl_ref, sl_ref, 0, n_lat)
    if with_ctx:
        o_ref[n_lat:n_lat + ctx, :] = mix(cc_ref, sc_ref, n_lat, ctx)


def _fourier(zcs, dfts, ctx, with_ctx):
    n_batch, t_all, _ = zcs.shape
    seq = t_all - ctx
    rows = t_all if with_ctx else seq
    full = lambda a: _const_spec(a.shape, lambda b: (0, 0))
    return pl.pallas_call(
        functools.partial(_fourier_kernel, ctx=ctx, with_ctx=with_ctx),
        grid=(n_batch,),
        in_specs=[pl.BlockSpec((None, t_all, 2 * FOURIER_WIDTH), lambda b: (b, 0, 0))]
        + [full(a) for a in dfts],
        out_specs=pl.BlockSpec((None, rows, FOURIER_WIDTH), lambda b: (b, 0, 0)),
        out_shape=jax.ShapeDtypeStruct((n_batch, rows, FOURIER_WIDTH), BF16),
        compiler_params=pltpu.CompilerParams(
            dimension_semantics=("arbitrary",), vmem_limit_bytes=VMEM_LIMIT),
        name="fourier_positions",
    )(zcs, *dfts)


def _scores(q, k):
    return lax.dot_general(q, k, (((1,), (1,)), ((), ())), preferred_element_type=F32)


def _mla_heads(q_ref, k_ref, v_ref, o_ref, head0_scores=None):
    n_heads = q_ref.shape[-1] // LANES
    for pair in range(n_heads // 2):
        outs = []
        vp = v_ref[:, pair * LANES:(pair + 1) * LANES]
        for hd in (2 * pair, 2 * pair + 1):
            if hd == 0 and head0_scores is not None:
                s = head0_scores
            else:
                s = _scores(q_ref[:, hd * LANES:(hd + 1) * LANES], k_ref[:, hd * LANES:(hd + 1) * LANES])
            m = jnp.max(s, axis=-1, keepdims=True)
            p = jnp.exp2(s - m)
            den = jnp.sum(p, axis=-1, keepdims=True)
            r = jnp.dot(p.astype(BF16), vp, preferred_element_type=F32)
            outs.append(r / den)
        o_ref[:, pair * LANES:(pair + 1) * LANES] = jnp.where(
            _low_lanes(outs[0].shape), outs[0], outs[1]).astype(BF16)


def _mla_latent_kernel(q_ref, k_ref, v_ref, qn_ref, kn_ref, o_ref, s0_ref):
    @pl.when(pl.program_id(0) == 0)
    def _():
        s0_ref[...] = _scores(q_ref[:, 0:LANES], k_ref[:, 0:LANES])

    _mla_heads(q_ref, k_ref, v_ref, o_ref, head0_scores=s0_ref[...])
    s0_ref[...] = _scores(qn_ref[...], kn_ref[...])


def _mla_ctx_kernel(q_ref, k_ref, v_ref, prev_ref, o_ref):
    del prev_ref
    _mla_heads(q_ref, k_ref, v_ref, o_ref)


def _mla(q, k, v, seq, with_ctx):
    n_batch, t_all, _ = q.shape
    n_ctx = t_all - seq
    tq = MLA_Q_TILE
    tpb = seq // tq
    n_tiles = n_batch * tpb
    qk_w = MLA_HEADS * LANES
    v_w = MLA_HEADS * MLA_V
    params = lambda n: pltpu.CompilerParams(
        dimension_semantics=("arbitrary",) * n, vmem_limit_bytes=VMEM_LIMIT)
    nxt = lambda s: jnp.minimum(s + 1, n_tiles - 1)
    a = pl.pallas_call(
        _mla_latent_kernel,
        grid=(n_tiles,),
        in_specs=[
            pl.BlockSpec((None, tq, qk_w), lambda s: (s // tpb, s % tpb, 0)),
            pl.BlockSpec((None, t_all, qk_w), lambda s: (s // tpb, 0, 0)),
            pl.BlockSpec((None, t_all, v_w), lambda s: (s // tpb, 0, 0)),
            pl.BlockSpec((None, tq, LANES), lambda s: (nxt(s) // tpb, nxt(s) % tpb, 0)),
            pl.BlockSpec((None, t_all, LANES), lambda s: (nxt(s) // tpb, 0, 0)),
        ],
        out_specs=pl.BlockSpec((None, tq, v_w), lambda s: (s // tpb, s % tpb, 0)),
        out_shape=jax.ShapeDtypeStruct((n_batch, t_all, v_w), BF16),
        scratch_shapes=[pltpu.VMEM((tq, t_all), F32)],
        compiler_params=params(1),
        name="mla_attention",
    )(q, k, v, q, k)
    if not with_ctx:
        return a
    c_blk = seq // n_ctx
    ctx_rows = lambda width: pl.BlockSpec((None, n_ctx, width), lambda b: (b, c_blk, 0))
    return pl.pallas_call(
        _mla_ctx_kernel,
        grid=(n_batch,),
        in_specs=[ctx_rows(MLA_HEADS * LANES), ctx_rows(MLA_HEADS * LANES), ctx_rows(v_w),
                  pl.BlockSpec(memory_space=pl.ANY)],
        out_specs=ctx_rows(v_w),
        out_shape=jax.ShapeDtypeStruct(a.shape, a.dtype),
        input_output_aliases={3: 0},
        compiler_params=params(1),
        name="mla_context_attention",
    )(q, k, v, a)


def _swa_heads(sink_ref, layer, q_ref, kc, vc, local, o_ref, r0=0, rows=None):
    rows = q_ref.shape[0] if rows is None else rows
    res = []
    for hq in range(SWA_Q_HEADS):
        slab = hq % 2
        use_low = hq < 2
        qs = q_ref[r0:r0 + rows, slab * LANES:(slab + 1) * LANES]
        low = _low_lanes(qs.shape)
        qm = jnp.where(low if use_low else jnp.logical_not(low), qs, jnp.zeros_like(qs))
        sink = sink_ref[layer, hq] * LOG2E
        s_c = lax.dot_general(qm, kc, (((1,), (1,)), ((), ())), preferred_element_type=F32)
        m = jnp.maximum(jnp.max(s_c, axis=-1, keepdims=True), sink)
        if local is not None:
            kw, vw, valid = local
            s_l = lax.dot_general(qm, kw, (((1,), (1,)), ((), ())), preferred_element_type=F32)
            s_l = jnp.where(valid, s_l, NEG)
            m = jnp.maximum(m, jnp.max(s_l, axis=-1, keepdims=True))
        p_c = jnp.exp2(s_c - m)
        den = jnp.sum(p_c, axis=-1, keepdims=True) + jnp.exp2(sink - m)
        r = jnp.dot(p_c.astype(BF16), vc, preferred_element_type=F32)
        if local is not None:
            p_l = jnp.exp2(s_l - m)
            den = den + jnp.sum(p_l, axis=-1, keepdims=True)
            r = r + jnp.dot(p_l.astype(BF16), vw, preferred_element_type=F32)
        res.append(r / den)
    low = _low_lanes(res[0].shape)
    o_ref[r0:r0 + rows, 0:LANES] = jnp.where(low, res[0], res[2]).astype(BF16)
    o_ref[r0:r0 + rows, LANES:2 * LANES] = jnp.where(low, res[1], res[3]).astype(BF16)


def _swa_latent_kernel(sink_ref, q_ref, k_ref, v_ref, o_ref, *, layer, seq):
    tq = q_ref.shape[0]
    rows = min(tq, ROW_CHUNK)
    span = rows + 2 * SWA_WINDOW
    kc, vc = k_ref[seq:, :], v_ref[seq:, :]
    for r0 in range(0, tq, rows):
        first = pl.program_id(1) * tq + r0
        start = pl.multiple_of(jnp.clip(first - SWA_WINDOW, 0, seq - span), LANES)
        kw = k_ref[pl.ds(start, span), :]
        vw = v_ref[pl.ds(start, span), :]
        q_pos = first + lax.broadcasted_iota(jnp.int32, (rows, span), 0)
        k_pos = start + lax.broadcasted_iota(jnp.int32, (rows, span), 1)
        dist = q_pos - k_pos
        valid = (dist <= SWA_WINDOW) & (dist >= -SWA_WINDOW)
        _swa_heads(sink_ref, layer, q_ref, kc, vc, (kw, vw, valid), o_ref, r0, rows)


def _swa_ctx_kernel(sink_ref, q_ref, k_ref, v_ref, prev_ref, o_ref, *, layer):
    del prev_ref
    _swa_heads(sink_ref, layer, q_ref, k_ref[...], v_ref[...], None, o_ref)


def _swa(layer, sink, q, k, v, seq, with_ctx):
    n_batch, t_all, _ = q.shape
    n_ctx = t_all - seq
    tq = Q_TILE
    params = lambda n: pltpu.CompilerParams(
        dimension_semantics=("arbitrary",) * n, vmem_limit_bytes=VMEM_LIMIT)
    smem = pl.BlockSpec(memory_space=pltpu.SMEM)
    w = pl.pallas_call(
        functools.partial(_swa_latent_kernel, layer=layer, seq=seq),
        grid=(n_batch, seq // tq),
        in_specs=[
            smem,
            pl.BlockSpec((None, tq, 2 * LANES), lambda b, i: (b, i, 0)),
            pl.BlockSpec((None, t_all, LANES), lambda b, i: (b, 0, 0)),
            pl.BlockSpec((None, t_all, LANES), lambda b, i: (b, 0, 0)),
        ],
        out_specs=pl.BlockSpec((None, tq, 2 * LANES), lambda b, i: (b, i, 0)),
        out_shape=jax.ShapeDtypeStruct((n_batch, t_all, 2 * LANES), BF16),
        compiler_params=params(2),
        name="window_attention",
    )(sink, q, k, v)
    if not with_ctx:
        return w
    c_blk = seq // n_ctx
    ctx_rows = lambda width: pl.BlockSpec((None, n_ctx, width), lambda b: (b, c_blk, 0))
    return pl.pallas_call(
        functools.partial(_swa_ctx_kernel, layer=layer),
        grid=(n_batch,),
        in_specs=[smem, ctx_rows(2 * LANES), ctx_rows(LANES), ctx_rows(LANES),
                  pl.BlockSpec(memory_space=pl.ANY)],
        out_specs=ctx_rows(2 * LANES),
        out_shape=jax.ShapeDtypeStruct(w.shape, w.dtype),
        input_output_aliases={4: 0},
        compiler_params=params(1),
        name="window_context_attention",
    )(sink, q, k, v, w)


def _out_ffn_kernel(h_ref, yf_ref, a_ref, w_ref, mod_ref, wo_ref, g2_ref, w1_ref, w3_ref, w2_ref, o_ref):
    mod = mod_ref[...]
    n_f = yf_ref.shape[-1]
    n_a = a_ref.shape[-1]
    mixed = (jnp.dot(yf_ref[...], wo_ref[0:n_f, :], preferred_element_type=F32)
             + jnp.dot(a_ref[...], wo_ref[n_f:n_f + n_a, :], preferred_element_type=F32)
             + jnp.dot(w_ref[...], wo_ref[n_f + n_a:, :], preferred_element_type=F32))
    h = h_ref[...] + mod[5:6] * mixed
    xn = _norm_mod(h, g2_ref[...], mod[6:7], mod[7:8]).astype(BF16)
    o_ref[...] = h + (0.5 * mod[8:9]) * _swiglu(xn, w1_ref, w3_ref, w2_ref)


def _out_ffn(layer, hh, yf, a, w, mod, p, seq, with_ctx):
    n_batch, t_all, d = hh.shape
    tm = TOK_TILE
    n_t = (t_all if with_ctx else seq) // tm
    lw = lambda *shape: _const_spec((None,) + shape, lambda b, t: (layer,) + (0,) * len(shape))
    mix = lambda width: pl.BlockSpec((None, tm, width), lambda b, t: (b, t, 0))
    return pl.pallas_call(
        _out_ffn_kernel,
        grid=(n_batch, n_t),
        in_specs=[
            mix(d), mix(yf.shape[-1]), mix(a.shape[-1]), mix(w.shape[-1]),
            pl.BlockSpec((None, None, N_MOD, d), _mod_index(layer, n_batch, seq // tm)),
            lw(d, d), lw(1, d), lw(d, D_FF), lw(d, D_FF), lw(D_FF, d),
        ],
        out_specs=pl.BlockSpec((None, tm, d), lambda b, t: (b, t, 0)),
        out_shape=jax.ShapeDtypeStruct((n_batch, n_t * tm, d), F32),
        compiler_params=pltpu.CompilerParams(
            dimension_semantics=("arbitrary", "arbitrary"), vmem_limit_bytes=VMEM_LIMIT),
        name="out_projection_and_ffn_half",
    )(hh, yf, a, w, mod, p["w_out"], p["g_ffn2"], p["w1_ffn2"], p["w3_ffn2"], p["w2_ffn2"])


def _prepare_params(g_ffn1, w1_ffn1, w3_ffn1, w2_ffn1, g_mix, w_in, g_cq, w_uq, g_ckv, w_ukv,
                    g_mla_q, g_mla_k, g_swa_q, g_swa_k, w_out, g_ffn2, w1_ffn2, w3_ffn2, w2_ffn2):
    row = lambda g: g[:, None, :]
    head = _deinterleave(SWA_HEAD_DIM)
    swa_gain = np.concatenate([head, head])
    mla_gain = np.concatenate([_mla_head_perm(), np.full((LANES - MLA_QK_DIM,), MLA_QK_DIM)])
    return {
        "g_ffn1": row(g_ffn1), "w1_ffn1": w1_ffn1.astype(BF16), "w3_ffn1": w3_ffn1.astype(BF16),
        "w2_ffn1": w2_ffn1.astype(BF16),
        "g_mix": row(g_mix), "w_in": _take_cols(w_in, _w_in_cols()).astype(BF16),
        "g_cq": row(g_cq), "w_uq": _take_cols(w_uq, _w_uq_cols()).astype(BF16),
        "g_mq": row(_take_cols(g_mla_q, mla_gain)),
        "g_ckv": row(g_ckv), "w_uk": _take_cols(w_ukv, _w_uk_cols()).astype(BF16),
        "w_uv": jnp.take(w_ukv, jnp.asarray(_w_uv_cols()), axis=-1).astype(BF16),
        "g_mk": row(_take_cols(g_mla_k, mla_gain)),
        "g_sq": row(jnp.take(g_swa_q, jnp.asarray(swa_gain), axis=-1)),
        "g_sk": row(jnp.take(g_swa_k, jnp.asarray(swa_gain), axis=-1)),
        "w_out": jnp.take(w_out, jnp.asarray(_w_out_rows()), axis=1).astype(BF16),
        "g_ffn2": row(g_ffn2), "w1_ffn2": w1_ffn2.astype(BF16), "w3_ffn2": w3_ffn2.astype(BF16),
        "w2_ffn2": w2_ffn2.astype(BF16),
        "dft64": _channel_dft(),
    }


def kernel(x, c, ctx, c_ctx, w_ada, b_ada, g_ffn1, w1_ffn1, w3_ffn1, w2_ffn1, g_mix, w_in, g_cq, w_uq,
           g_ckv, w_ukv, g_mla_q, g_mla_k, g_swa_q, g_swa_k, sink, w_out, g_ffn2, w1_ffn2, w3_ffn2,
           w2_ffn2):
    n_batch, seq, d = x.shape
    n_ctx = ctx.shape[1]
    depth = w_ada.shape[0]
    assert d == D_MODEL and seq % GRID_W == 0 and n_batch + 1 <= MOD_ROWS
    assert n_ctx % TOK_TILE == 0 and seq % TOK_TILE == 0 and seq % n_ctx == 0
    assert seq % Q_TILE == 0 and seq % MLA_Q_TILE == 0
    assert Q_TILE % ROW_CHUNK == 0 and ROW_CHUNK + 2 * SWA_WINDOW <= seq
    assert w_ada.shape[-1] % MOD_COL_TILE == 0

    p = _prepare_params(g_ffn1, w1_ffn1, w3_ffn1, w2_ffn1, g_mix, w_in, g_cq, w_uq, g_ckv, w_ukv,
                        g_mla_q, g_mla_k, g_swa_q, g_swa_k, w_out, g_ffn2, w1_ffn2, w3_ffn2, w2_ffn2)
    tabs = _rope_tables(seq, n_ctx)
    c_lat, s_lat = _dft_cos_sin(seq, seq ** -0.5)
    c_ctx_dft, s_ctx_dft = _dft_cos_sin(n_ctx, n_ctx ** -0.5)
    dfts = tuple(m.astype(BF16) for m in (c_lat, s_lat, c_ctx_dft, s_ctx_dft))

    cc = jnp.concatenate([c, c_ctx[None, :], jnp.zeros((MOD_ROWS - n_batch - 1, d), F32)], axis=0)
    mod = _modulation(cc, w_ada, b_ada).reshape(depth, MOD_ROWS, N_MOD, d)

    hh = jnp.concatenate([x, ctx], axis=1)
    for layer in range(depth):
        with_ctx = layer != depth - 1
        hh, q, k, v, sq, sk, sv, zcs = _ffn_proj(layer, hh, mod, p, tabs, seq)
        yf = _fourier(zcs, dfts, n_ctx, with_ctx)
        a = _mla(q, k, v, seq, with_ctx)
        w = _swa(layer, sink, sq, sk, sv, seq, with_ctx)
        hh = _out_ffn(layer, hh, yf, a, w, mod, p, seq, with_ctx)
    return hh
```

```python
import functools

import numpy as np
import jax
import jax.numpy as jnp
from jax import lax
from jax.experimental import pallas as pl
from jax.experimental.pallas import tpu as pltpu

F32 = jnp.float32
BF16 = jnp.bfloat16

D_MODEL = 1024
GRID_W = 64
ROPE_BASE = 10000.0
EPS = 1e-6
NEG = -1e30
LOG2E = 1.4426950408889634
N_MOD = 9
D_FF = 2816

FOURIER_WIDTH = 256
FOURIER_GROUP_DIM = 64
MLA_HEADS = 8
MLA_NOPE = 64
MLA_ROPE = 32
MLA_V = 64
MLA_QK_DIM = MLA_NOPE + MLA_ROPE
MLA_Q_RANK = 256
MLA_KV_RANK = 128
SWA_Q_HEADS = 4
SWA_KV_HEADS = 2
SWA_HEAD_DIM = 64
SWA_WINDOW = 128
IN_SPLITS = (256, 256, 128, 32, 256, 128, 128)
IN_WIDTH = sum(IN_SPLITS)

LANES = 128
HALF = LANES // 2
IN_WIDTH_P = 10 * LANES
MLA_V_SLABS = MLA_HEADS
VMEM_LIMIT = 56 * 1024 * 1024

TOK_TILE = 256
Q_TILE = 1024
MLA_Q_TILE = 512
ROW_CHUNK = 512
MOD_ROWS = 24
MOD_COL_TILE = 2304

OFF_F, OFF_CQ, OFF_CKV, OFF_KR, OFF_SQA, OFF_SQB, OFF_SK, OFF_SV = (
    0, 256, 512, 640, 768, 896, 1024, 1152)


def _deinterleave(n):
    return np.concatenate([np.arange(0, n, 2), np.arange(1, n, 2)])


def _w_in_cols():
    zero = IN_WIDTH
    o_f, o_cq, o_ckv, o_kr, o_sq, o_sk, o_sv = np.cumsum((0,) + IN_SPLITS)[:-1]
    cols = np.full((IN_WIDTH_P,), zero, np.int32)
    cols[OFF_F:OFF_F + 256] = o_f + np.arange(256)
    cols[OFF_CQ:OFF_CQ + 256] = o_cq + np.arange(256)
    cols[OFF_CKV:OFF_CKV + 128] = o_ckv + np.arange(128)
    cols[OFF_KR + MLA_NOPE:OFF_KR + MLA_QK_DIM] = o_kr + _deinterleave(MLA_ROPE)
    head = _deinterleave(SWA_HEAD_DIM)
    cols[OFF_SQA:OFF_SQA + 64] = o_sq + 0 * 64 + head
    cols[OFF_SQA + 64:OFF_SQA + 128] = o_sq + 2 * 64 + head
    cols[OFF_SQB:OFF_SQB + 64] = o_sq + 1 * 64 + head
    cols[OFF_SQB + 64:OFF_SQB + 128] = o_sq + 3 * 64 + head
    cols[OFF_SK:OFF_SK + 64] = o_sk + head
    cols[OFF_SK + 64:OFF_SK + 128] = o_sk + 64 + head
    cols[OFF_SV:OFF_SV + 128] = o_sv + np.arange(128)
    return cols


def _mla_head_perm():
    return np.concatenate([np.arange(MLA_NOPE), MLA_NOPE + _deinterleave(MLA_ROPE)])


def _w_uq_cols():
    zero = MLA_HEADS * MLA_QK_DIM
    cols = np.full((MLA_HEADS * LANES,), zero, np.int32)
    for h in range(MLA_HEADS):
        cols[h * LANES:h * LANES + MLA_QK_DIM] = h * MLA_QK_DIM + _mla_head_perm()
    return cols


def _w_uk_cols():
    zero = MLA_HEADS * (MLA_NOPE + MLA_V)
    cols = np.full((MLA_HEADS * LANES,), zero, np.int32)
    for h in range(MLA_HEADS):
        cols[h * LANES:h * LANES + MLA_NOPE] = h * (MLA_NOPE + MLA_V) + np.arange(MLA_NOPE)
    return cols


def _w_uv_cols():
    return np.concatenate([h * (MLA_NOPE + MLA_V) + MLA_NOPE + np.arange(MLA_V)
                           for h in range(MLA_HEADS)]).astype(np.int32)


def _w_out_rows():
    base = FOURIER_WIDTH + MLA_HEADS * MLA_V
    swa = np.concatenate([base + h * SWA_HEAD_DIM + np.arange(SWA_HEAD_DIM) for h in (0, 2, 1, 3)])
    return np.concatenate([np.arange(base), swa]).astype(np.int32)


def _take_cols(w, cols):
    pad = jnp.zeros(w.shape[:-1] + (1,), w.dtype)
    return jnp.take(jnp.concatenate([w, pad], axis=-1), jnp.asarray(cols), axis=-1)


def _rope_tables(seq, ctx):
    rows = seq // GRID_W
    pad = jnp.zeros((ctx,), F32)
    row = jnp.concatenate([jnp.repeat(jnp.arange(rows, dtype=F32), GRID_W), pad])[:, None]
    col = jnp.concatenate([jnp.tile(jnp.arange(GRID_W, dtype=F32), rows), pad])[:, None]

    def build(dim, first_lanes, second_lanes):
        axis_dim = dim // 2
        n_freq = axis_dim // 2
        inv = ROPE_BASE ** (-jnp.arange(0, axis_dim, 2, dtype=F32) / axis_dim)
        freq = np.full((LANES,), n_freq, np.int32)
        by_row = np.zeros((LANES,), bool)
        m_a = np.zeros((LANES,), np.float32)
        m_b = np.zeros((LANES,), np.float32)
        for lanes, mask in ((first_lanes, m_a), (second_lanes, m_b)):
            for lo in lanes:
                freq[lo:lo + 2 * n_freq] = np.concatenate([np.arange(n_freq), np.arange(n_freq)])
                by_row[lo:lo + n_freq] = True
                mask[lo:lo + 2 * n_freq] = 1.0
        inv_lane = jnp.take(jnp.concatenate([inv, jnp.zeros((1,), F32)]), jnp.asarray(freq))[None, :]
        ang = jnp.where(jnp.asarray(by_row)[None, :], row * inv_lane, col * inv_lane)
        sin = jnp.sin(ang)
        return jnp.cos(ang), sin * jnp.asarray(-m_a)[None, :], sin * jnp.asarray(m_b)[None, :]

    half_m = MLA_ROPE // 2
    mla = build(MLA_ROPE, (MLA_NOPE,), (MLA_NOPE + half_m,))
    half_s = SWA_HEAD_DIM // 2
    swa = build(SWA_HEAD_DIM, (0, HALF), (half_s, HALF + half_s))
    return mla + swa


def _dft_cos_sin(n, scale):
    def direct(rows_j, n_mod):
        k = jnp.arange(n, dtype=jnp.int32)
        ang = ((rows_j[:, None] * k[None, :]) % n_mod).astype(F32) * (2.0 * np.pi / n_mod)
        return jnp.cos(ang), jnp.sin(ang)

    inner = FOURIER_GROUP_DIM
    if n <= inner or n % inner:
        c, s = direct(jnp.arange(n, dtype=jnp.int32), n)
        return c * scale, s * scale
    outer = n // inner
    ca, sa = direct(jnp.arange(outer, dtype=jnp.int32), outer)
    cb, sb = direct(jnp.arange(inner, dtype=jnp.int32), n)
    cb, sb = cb * scale, sb * scale
    c = ca[:, None, :] * cb[None, :, :] - sa[:, None, :] * sb[None, :, :]
    s = sa[:, None, :] * cb[None, :, :] + ca[:, None, :] * sb[None, :, :]
    return c.reshape(n, n), s.reshape(n, n)


def _channel_dft():
    c, s = _dft_cos_sin(FOURIER_GROUP_DIM, FOURIER_GROUP_DIM ** -0.5)
    eye = jnp.eye(FOURIER_WIDTH // FOURIER_GROUP_DIM, dtype=F32)
    return jnp.concatenate([jnp.kron(eye, c), jnp.kron(eye, s)], axis=1).astype(BF16)


def _rms_scale(x, width):
    return lax.rsqrt(jnp.sum(x * x, axis=-1, keepdims=True) * (1.0 / width) + EPS)


def _norm_mod(x, g, shift, scale):
    y = x * _rms_scale(x, x.shape[-1]) * g
    return y * (1.0 + scale) + shift


def _swiglu(xn, w1_ref, w3_ref, w2_ref):
    a = jnp.dot(xn, w1_ref[...], preferred_element_type=F32)
    b = jnp.dot(xn, w3_ref[...], preferred_element_type=F32)
    g = (a / (1.0 + jnp.exp(-a))) * b
    return jnp.dot(g.astype(BF16), w2_ref[...], preferred_element_type=F32)


def _rope(x, cos, sin_a, sin_b, pair_dist):
    return (x * cos + pltpu.roll(x, LANES - pair_dist, 1) * sin_a
            + pltpu.roll(x, pair_dist, 1) * sin_b)


def _low_lanes(shape):
    return lax.broadcasted_iota(jnp.int32, shape, len(shape) - 1) < HALF


def _mod_kernel(c_ref, w_ref, b_ref, o_ref):
    cv = c_ref[...]
    s = (cv / (1.0 + jnp.exp(-cv))).astype(BF16)
    o_ref[...] = jnp.dot(s, w_ref[...].astype(BF16), preferred_element_type=F32) + b_ref[...]


def _modulation(cc, w_ada, b_ada):
    n_layers, d, width = w_ada.shape
    return pl.pallas_call(
        _mod_kernel,
        grid=(n_layers, width // MOD_COL_TILE),
        in_specs=[
            pl.BlockSpec((MOD_ROWS, d), lambda l, j: (0, 0)),
            pl.BlockSpec((None, d, MOD_COL_TILE), lambda l, j: (l, 0, j)),
            pl.BlockSpec((None, 1, MOD_COL_TILE), lambda l, j: (l, 0, j)),
        ],
        out_specs=pl.BlockSpec((None, MOD_ROWS, MOD_COL_TILE), lambda l, j: (l, 0, j)),
        out_shape=jax.ShapeDtypeStruct((n_layers, MOD_ROWS, width), F32),
        compiler_params=pltpu.CompilerParams(
            dimension_semantics=("arbitrary", "arbitrary"), vmem_limit_bytes=VMEM_LIMIT),
        name="adaln_modulation",
    )(cc, w_ada, b_ada.reshape(n_layers, 1, width))


def _ffn_proj_kernel(*refs, n_src, n_tiles, tpb, n_lat):
    h_refs = refs[:n_src]
    (mod_ref, modp_ref, g1_ref, w1_ref, w3_ref, w2_ref, gmix_ref, win_ref,
     gcq_ref, wuq_ref, gmq_ref, gckv_ref, wuk_ref, wuv_ref, gmk_ref, gsq_ref, gsk_ref, dft_ref,
     cm_ref, sam_ref, sbm_ref, cs_ref, sas_ref, sbs_ref,
     ho_ref, q_ref, k_ref, v_ref, sq_ref, sk_ref, sv_ref, z_ref, hprev_ref) = refs[n_src:]
    step = pl.program_id(0)

    @pl.when(step == 0)
    def _():
        hprev_ref[...] = jnp.zeros_like(hprev_ref)

    modp = modp_ref[...]
    n = _norm_mod(hprev_ref[...], gmix_ref[...], modp[3:4], modp[4:5]).astype(BF16)
    u = jnp.dot(n, win_ref[...], preferred_element_type=F32)

    f = u[:, OFF_F:OFF_F + FOURIER_WIDTH].astype(BF16)
    z_ref[...] = jnp.dot(f, dft_ref[...], preferred_element_type=F32).astype(BF16)

    cm, sam, sbm = cm_ref[...], sam_ref[...], sbm_ref[...]
    cq = u[:, OFF_CQ:OFF_CQ + MLA_Q_RANK]
    cqn = (cq * _rms_scale(cq, MLA_Q_RANK) * gcq_ref[...]).astype(BF16)
    q = jnp.dot(cqn, wuq_ref[...], preferred_element_type=F32)
    gmq = gmq_ref[...]
    q_scale = MLA_QK_DIM ** -0.5 * LOG2E
    for hd in range(MLA_HEADS):
        qh = q[:, hd * LANES:(hd + 1) * LANES]
        qg = qh * (_rms_scale(qh, MLA_QK_DIM) * q_scale) * gmq
        q_ref[:, hd * LANES:(hd + 1) * LANES] = _rope(qg, cm, sam, sbm, MLA_ROPE // 2).astype(BF16)

    ckv = u[:, OFF_CKV:OFF_CKV + MLA_KV_RANK]
    ckvn = (ckv * _rms_scale(ckv, MLA_KV_RANK) * gckv_ref[...]).astype(BF16)
    kn = jnp.dot(ckvn, wuk_ref[...], preferred_element_type=F32)
    vv = jnp.dot(ckvn, wuv_ref[...], preferred_element_type=F32).astype(BF16)
    ones = jnp.ones((vv.shape[0], LANES), BF16)
    for pair in range(MLA_HEADS // 2):
        v_ref[:, 2 * pair * LANES:(2 * pair + 1) * LANES] = vv[:, pair * LANES:(pair + 1) * LANES]
        v_ref[:, (2 * pair + 1) * LANES:(2 * pair + 2) * LANES] = ones
    gmk = gmk_ref[...]
    kr = u[:, OFF_KR:OFF_KR + LANES]
    kr_ss = jnp.sum(kr * kr, axis=-1, keepdims=True)
    kr_rot = _rope(kr * gmk, cm, sam, sbm, MLA_ROPE // 2)
    for hd in range(MLA_HEADS):
        kh = kn[:, hd * LANES:(hd + 1) * LANES]
        ss = jnp.sum(kh * kh, axis=-1, keepdims=True) + kr_ss
        rs = lax.rsqrt(ss * (1.0 / MLA_QK_DIM) + EPS)
        k_ref[:, hd * LANES:(hd + 1) * LANES] = (rs * (kh * gmk + kr_rot)).astype(BF16)

    cs, sas, sbs = cs_ref[...], sas_ref[...], sbs_ref[...]

    def two_head_norm_rope(x, g, scale):
        low = _low_lanes(x.shape)
        x2 = x * x
        lo = jnp.sum(jnp.where(low, x2, 0.0), axis=-1, keepdims=True)
        hi = jnp.sum(jnp.where(low, 0.0, x2), axis=-1, keepdims=True)
        rs = jnp.where(low, lax.rsqrt(lo * (1.0 / SWA_HEAD_DIM) + EPS),
                       lax.rsqrt(hi * (1.0 / SWA_HEAD_DIM) + EPS))
        return _rope(x * (rs * scale) * g, cs, sas, sbs, SWA_HEAD_DIM // 2).astype(BF16)

    gsq = gsq_ref[...]
    s_scale = SWA_HEAD_DIM ** -0.5 * LOG2E
    sq_ref[:, 0:LANES] = two_head_norm_rope(u[:, OFF_SQA:OFF_SQA + LANES], gsq, s_scale)
    sq_ref[:, LANES:2 * LANES] = two_head_norm_rope(u[:, OFF_SQB:OFF_SQB + LANES], gsq, s_scale)
    sk_ref[...] = two_head_norm_rope(u[:, OFF_SK:OFF_SK + LANES], gsk_ref[...], 1.0)
    sv_ref[:, 0:LANES] = u[:, OFF_SV:OFF_SV + LANES].astype(BF16)
    sv_ref[:, LANES:2 * LANES] = ones

    mod = mod_ref[...]
    if n_src == 1:
        h = h_refs[0][...]
    else:
        cur = jnp.minimum(step, n_tiles - 1)
        h = jnp.where(cur % tpb >= n_lat, h_refs[1][...], h_refs[0][...])
    xn = _norm_mod(h, g1_ref[...], mod[0:1], mod[1:2]).astype(BF16)
    h = h + (0.5 * mod[2:3]) * _swiglu(xn, w1_ref, w3_ref, w2_ref)
    ho_ref[...] = h
    hprev_ref[...] = h


def _const_spec(block_shape, index_map):
    return pl.BlockSpec(block_shape, index_map, pipeline_mode=pl.Buffered(1))


def _mod_index(layer, n_batch, n_lat_tiles):
    return lambda b, t: (layer, jnp.where(t >= n_lat_tiles, n_batch, b), 0, 0)


def _ffn_proj(layer, tokens, mod, p, tabs, seq, t_all):
    n_batch, _, d = tokens[0].shape
    tm = TOK_TILE
    tpb = t_all // tm
    n_tiles = n_batch * tpb
    n_lat = seq // tm
    cur = lambda s: jnp.minimum(s, n_tiles - 1)
    prev = lambda s: jnp.maximum(s - 1, 0)
    tile = lambda which, width: pl.BlockSpec(
        (None, tm, width), lambda s: (which(s) // tpb, which(s) % tpb, 0))

    def token_specs(which):
        if len(tokens) == 1:
            return [tile(which, d)]
        lat = pl.BlockSpec((None, tm, d),
                           lambda s: (which(s) // tpb, jnp.minimum(which(s) % tpb, n_lat - 1), 0))
        con = pl.BlockSpec((None, tm, d),
                           lambda s: (which(s) // tpb, jnp.maximum(which(s) % tpb - n_lat, 0), 0))
        return [lat, con]

    def mod_row(which):
        def index(s):
            b, t = which(s) // tpb, which(s) % tpb
            return (layer, jnp.where(t >= n_lat, n_batch, b), 0, 0)
        return pl.BlockSpec((None, None, N_MOD, d), index)

    lw = lambda *shape: _const_spec((None,) + shape, lambda s: (layer,) + (0,) * len(shape))
    tab = pl.BlockSpec((tm, LANES), lambda s: (prev(s) % tpb, 0))
    in_specs = token_specs(cur) + [
        mod_row(cur), mod_row(prev),
        lw(1, d), lw(d, D_FF), lw(d, D_FF), lw(D_FF, d), lw(1, d), lw(d, IN_WIDTH_P),
        lw(1, MLA_Q_RANK), lw(MLA_Q_RANK, MLA_HEADS * LANES), lw(1, LANES),
        lw(1, MLA_KV_RANK), lw(MLA_KV_RANK, MLA_HEADS * LANES), lw(MLA_KV_RANK, MLA_HEADS * MLA_V),
        lw(1, LANES), lw(1, LANES), lw(1, LANES),
        _const_spec((FOURIER_WIDTH, 2 * FOURIER_WIDTH), lambda s: (0, 0)),
        tab, tab, tab, tab, tab, tab,
    ]
    widths = (d, MLA_HEADS * LANES, MLA_HEADS * LANES, MLA_V_SLABS * LANES, 2 * LANES, LANES, 2 * LANES,
              2 * FOURIER_WIDTH)
    dtypes = (F32,) + (BF16,) * 7
    return pl.pallas_call(
        functools.partial(_ffn_proj_kernel, n_src=len(tokens), n_tiles=n_tiles, tpb=tpb, n_lat=n_lat),
        grid=(n_tiles + 1,),
        in_specs=in_specs,
        out_specs=[tile(cur, d)] + [tile(prev, w) for w in widths[1:]],
        out_shape=[jax.ShapeDtypeStruct((n_batch, t_all, w), dt) for w, dt in zip(widths, dtypes)],
        scratch_shapes=[pltpu.VMEM((tm, d), F32)],
        compiler_params=pltpu.CompilerParams(
            dimension_semantics=("arbitrary",), vmem_limit_bytes=VMEM_LIMIT),
        name="ffn_half_and_projections",
    )(*tokens, mod, mod, p["g_ffn1"], p["w1_ffn1"], p["w3_ffn1"], p["w2_ffn1"], p["g_mix"], p["w_in"],
      p["g_cq"], p["w_uq"], p["g_mq"], p["g_ckv"], p["w_uk"], p["w_uv"], p["g_mk"],
      p["g_sq"], p["g_sk"], p["dft64"], *tabs)


def _fourier_kernel(z_ref, cl_ref, sl_ref, cc_ref, sc_ref, o_ref, *, ctx, with_ctx):
    w = FOURIER_WIDTH

    def mix(c_ref, s_ref, lo, n):
        zc = z_ref[lo:lo + n, 0:w]
        zs = z_ref[lo:lo + n, w:2 * w]
        return (jnp.dot(c_ref[...], zc, preferred_element_type=F32)
                - jnp.dot(s_ref[...], zs, preferred_element_type=F32)).astype(BF16)

    n_lat = cl_ref.shape[0]
    o_ref[0:n_lat, :] = mix(cl_ref, sl_ref, 0, n_lat)
    if with_ctx:
        o_ref[n_lat:n_lat + ctx, :] = mix(cc_ref, sc_ref, n_lat, ctx)


def _fourier(zcs, dfts, ctx, with_ctx):
    n_batch, t_all, _ = zcs.shape
    seq = t_all - ctx
    rows = t_all if with_ctx else seq
    full = lambda a: _const_spec(a.shape, lambda b: (0, 0))
    return pl.pallas_call(
        functools.partial(_fourier_kernel, ctx=ctx, with_ctx=with_ctx),
        grid=(n_batch,),
        in_specs=[pl.BlockSpec((None, t_all, 2 * FOURIER_WIDTH), lambda b: (b, 0, 0))]
        + [full(a) for a in dfts],
        out_specs=pl.BlockSpec((None, rows, FOURIER_WIDTH), lambda b: (b, 0, 0)),
        out_shape=jax.ShapeDtypeStruct((n_batch, rows, FOURIER_WIDTH), BF16),
        compiler_params=pltpu.CompilerParams(
            dimension_semantics=("arbitrary",), vmem_limit_bytes=VMEM_LIMIT),
        name="fourier_positions",
    )(zcs, *dfts)


def _scores(q, k):
    return lax.dot_general(q, k, (((1,), (1,)), ((), ())), preferred_element_type=F32)


def _mla_heads(q_ref, k_ref, v_ref, o_ref, head0_scores=None):
    n_heads = q_ref.shape[-1] // LANES
    for pair in range(n_heads // 2):
        outs = []
        vp = v_ref[:, 2 * pair * LANES:(2 * pair + 2) * LANES]
        for hd in (2 * pair, 2 * pair + 1):
            if hd == 0 and head0_scores is not None:
                s = head0_scores
            else:
                s = _scores(q_ref[:, hd * LANES:(hd + 1) * LANES], k_ref[:, hd * LANES:(hd + 1) * LANES])
            m = jnp.max(s, axis=-1, keepdims=True)
            p = jnp.exp2(s - m).astype(BF16)
            r = jnp.dot(p, vp, preferred_element_type=F32)
            outs.append(r[:, 0:LANES] / r[:, LANES:2 * LANES])
        o_ref[:, pair * LANES:(pair + 1) * LANES] = jnp.where(
            _low_lanes(outs[0].shape), outs[0], outs[1]).astype(BF16)


def _mla_latent_kernel(q_ref, k_ref, v_ref, qn_ref, kn_ref, o_ref, s0_ref):
    @pl.when(pl.program_id(0) == 0)
    def _():
        s0_ref[...] = _scores(q_ref[:, 0:LANES], k_ref[:, 0:LANES])

    _mla_heads(q_ref, k_ref, v_ref, o_ref, head0_scores=s0_ref[...])
    s0_ref[...] = _scores(qn_ref[...], kn_ref[...])


def _mla_ctx_kernel(q_ref, k_ref, v_ref, prev_ref, o_ref):
    del prev_ref
    _mla_heads(q_ref, k_ref, v_ref, o_ref)


def _mla(q, k, v, seq, with_ctx):
    n_batch, t_all, _ = q.shape
    n_ctx = t_all - seq
    tq = MLA_Q_TILE
    tpb = seq // tq
    n_tiles = n_batch * tpb
    qk_w = MLA_HEADS * LANES
    v_in = MLA_V_SLABS * LANES
    v_w = MLA_HEADS * MLA_V
    params = lambda n: pltpu.CompilerParams(
        dimension_semantics=("arbitrary",) * n, vmem_limit_bytes=VMEM_LIMIT)
    nxt = lambda s: jnp.minimum(s + 1, n_tiles - 1)
    a = pl.pallas_call(
        _mla_latent_kernel,
        grid=(n_tiles,),
        in_specs=[
            pl.BlockSpec((None, tq, qk_w), lambda s: (s // tpb, s % tpb, 0)),
            pl.BlockSpec((None, t_all, qk_w), lambda s: (s // tpb, 0, 0)),
            pl.BlockSpec((None, t_all, v_in), lambda s: (s // tpb, 0, 0)),
            pl.BlockSpec((None, tq, LANES), lambda s: (nxt(s) // tpb, nxt(s) % tpb, 0)),
            pl.BlockSpec((None, t_all, LANES), lambda s: (nxt(s) // tpb, 0, 0)),
        ],
        out_specs=pl.BlockSpec((None, tq, v_w), lambda s: (s // tpb, s % tpb, 0)),
        out_shape=jax.ShapeDtypeStruct((n_batch, t_all, v_w), BF16),
        scratch_shapes=[pltpu.VMEM((tq, t_all), F32)],
        compiler_params=params(1),
        name="mla_attention",
    )(q, k, v, q, k)
    if not with_ctx:
        return a
    c_blk = seq // n_ctx
    ctx_rows = lambda width: pl.BlockSpec((None, n_ctx, width), lambda b: (b, c_blk, 0))
    return pl.pallas_call(
        _mla_ctx_kernel,
        grid=(n_batch,),
        in_specs=[ctx_rows(qk_w), ctx_rows(qk_w), ctx_rows(v_in), pl.BlockSpec(memory_space=pl.ANY)],
        out_specs=ctx_rows(v_w),
        out_shape=jax.ShapeDtypeStruct(a.shape, a.dtype),
        input_output_aliases={3: 0},
        compiler_params=params(1),
        name="mla_context_attention",
    )(q, k, v, a)


def _swa_heads(sink_ref, layer, q_ref, keys, vals, valid, o_ref, r0=0, rows=None):
    rows = q_ref.shape[0] if rows is None else rows
    res = []
    for hq in range(SWA_Q_HEADS):
        slab = hq % 2
        use_low = hq < 2
        qs = q_ref[r0:r0 + rows, slab * LANES:(slab + 1) * LANES]
        low = _low_lanes(qs.shape)
        qm = jnp.where(low if use_low else jnp.logical_not(low), qs, jnp.zeros_like(qs))
        sink = sink_ref[layer, hq] * LOG2E
        s = _scores(qm, keys)
        if valid is not None:
            s = jnp.where(valid, s, NEG)
        m = jnp.maximum(jnp.max(s, axis=-1, keepdims=True), sink)
        p = jnp.exp2(s - m).astype(BF16)
        r = jnp.dot(p, vals, preferred_element_type=F32)
        res.append(r[:, 0:LANES] / (r[:, LANES:2 * LANES] + jnp.exp2(sink - m)))
    low = _low_lanes(res[0].shape)
    o_ref[r0:r0 + rows, 0:LANES] = jnp.where(low, res[0], res[2]).astype(BF16)
    o_ref[r0:r0 + rows, LANES:2 * LANES] = jnp.where(low, res[1], res[3]).astype(BF16)


def _swa_latent_kernel(sink_ref, q_ref, k_ref, v_ref, o_ref, *, layer, seq):
    tq = q_ref.shape[0]
    rows = min(tq, ROW_CHUNK)
    span = rows + 2 * SWA_WINDOW
    n_keys = span + k_ref.shape[0] - seq
    kc, vc = k_ref[seq:, :], v_ref[seq:, :]
    for r0 in range(0, tq, rows):
        first = pl.program_id(1) * tq + r0
        start = pl.multiple_of(jnp.clip(first - SWA_WINDOW, 0, seq - span), LANES)
        keys = jnp.concatenate([k_ref[pl.ds(start, span), :], kc], axis=0)
        vals = jnp.concatenate([v_ref[pl.ds(start, span), :], vc], axis=0)
        col = lax.broadcasted_iota(jnp.int32, (rows, n_keys), 1)
        dist = first + lax.broadcasted_iota(jnp.int32, (rows, n_keys), 0) - (start + col)
        valid = ((dist <= SWA_WINDOW) & (dist >= -SWA_WINDOW)) | (col >= span)
        _swa_heads(sink_ref, layer, q_ref, keys, vals, valid, o_ref, r0, rows)


def _swa_ctx_kernel(sink_ref, q_ref, k_ref, v_ref, prev_ref, o_ref, *, layer):
    del prev_ref
    _swa_heads(sink_ref, layer, q_ref, k_ref[...], v_ref[...], None, o_ref)


def _swa(layer, sink, q, k, v, seq, with_ctx):
    n_batch, t_all, _ = q.shape
    n_ctx = t_all - seq
    tq = Q_TILE
    params = lambda n: pltpu.CompilerParams(
        dimension_semantics=("arbitrary",) * n, vmem_limit_bytes=VMEM_LIMIT)
    smem = pl.BlockSpec(memory_space=pltpu.SMEM)
    w = pl.pallas_call(
        functools.partial(_swa_latent_kernel, layer=layer, seq=seq),
        grid=(n_batch, seq // tq),
        in_specs=[
            smem,
            pl.BlockSpec((None, tq, 2 * LANES), lambda b, i: (b, i, 0)),
            pl.BlockSpec((None, t_all, LANES), lambda b, i: (b, 0, 0)),
            pl.BlockSpec((None, t_all, 2 * LANES), lambda b, i: (b, 0, 0)),
        ],
        out_specs=pl.BlockSpec((None, tq, 2 * LANES), lambda b, i: (b, i, 0)),
        out_shape=jax.ShapeDtypeStruct((n_batch, t_all, 2 * LANES), BF16),
        compiler_params=params(2),
        name="window_attention",
    )(sink, q, k, v)
    if not with_ctx:
        return w
    c_blk = seq // n_ctx
    ctx_rows = lambda width: pl.BlockSpec((None, n_ctx, width), lambda b: (b, c_blk, 0))
    return pl.pallas_call(
        functools.partial(_swa_ctx_kernel, layer=layer),
        grid=(n_batch,),
        in_specs=[smem, ctx_rows(2 * LANES), ctx_rows(LANES), ctx_rows(2 * LANES),
                  pl.BlockSpec(memory_space=pl.ANY)],
        out_specs=ctx_rows(2 * LANES),
        out_shape=jax.ShapeDtypeStruct(w.shape, w.dtype),
        input_output_aliases={4: 0},
        compiler_params=params(1),
        name="window_context_attention",
    )(sink, q, k, v, w)


def _out_ffn_kernel(h_ref, yf_ref, a_ref, w_ref, mod_ref, wo_ref, g2_ref, w1_ref, w3_ref, w2_ref, o_ref):
    mod = mod_ref[...]
    n_f = yf_ref.shape[-1]
    n_a = a_ref.shape[-1]
    mixed = (jnp.dot(yf_ref[...], wo_ref[0:n_f, :], preferred_element_type=F32)
             + jnp.dot(a_ref[...], wo_ref[n_f:n_f + n_a, :], preferred_element_type=F32)
             + jnp.dot(w_ref[...], wo_ref[n_f + n_a:, :], preferred_element_type=F32))
    h = h_ref[...] + mod[5:6] * mixed
    xn = _norm_mod(h, g2_ref[...], mod[6:7], mod[7:8]).astype(BF16)
    o_ref[...] = h + (0.5 * mod[8:9]) * _swiglu(xn, w1_ref, w3_ref, w2_ref)


def _out_ffn(layer, hh, yf, a, w, mod, p, seq, with_ctx):
    n_batch, t_all, d = hh.shape
    tm = TOK_TILE
    n_t = (t_all if with_ctx else seq) // tm
    lw = lambda *shape: _const_spec((None,) + shape, lambda b, t: (layer,) + (0,) * len(shape))
    mix = lambda width: pl.BlockSpec((None, tm, width), lambda b, t: (b, t, 0))
    return pl.pallas_call(
        _out_ffn_kernel,
        grid=(n_batch, n_t),
        in_specs=[
            mix(d), mix(yf.shape[-1]), mix(a.shape[-1]), mix(w.shape[-1]),
            pl.BlockSpec((None, None, N_MOD, d), _mod_index(layer, n_batch, seq // tm)),
            lw(d, d), lw(1, d), lw(d, D_FF), lw(d, D_FF), lw(D_FF, d),
        ],
        out_specs=pl.BlockSpec((None, tm, d), lambda b, t: (b, t, 0)),
        out_shape=jax.ShapeDtypeStruct((n_batch, n_t * tm, d), F32),
        compiler_params=pltpu.CompilerParams(
            dimension_semantics=("arbitrary", "arbitrary"), vmem_limit_bytes=VMEM_LIMIT),
        name="out_projection_and_ffn_half",
    )(hh, yf, a, w, mod, p["w_out"], p["g_ffn2"], p["w1_ffn2"], p["w3_ffn2"], p["w2_ffn2"])


def _prepare_params(g_ffn1, w1_ffn1, w3_ffn1, w2_ffn1, g_mix, w_in, g_cq, w_uq, g_ckv, w_ukv,
                    g_mla_q, g_mla_k, g_swa_q, g_swa_k, w_out, g_ffn2, w1_ffn2, w3_ffn2, w2_ffn2):
    row = lambda g: g[:, None, :]
    head = _deinterleave(SWA_HEAD_DIM)
    swa_gain = np.concatenate([head, head])
    mla_gain = np.concatenate([_mla_head_perm(), np.full((LANES - MLA_QK_DIM,), MLA_QK_DIM)])
    return {
        "g_ffn1": row(g_ffn1), "w1_ffn1": w1_ffn1.astype(BF16), "w3_ffn1": w3_ffn1.astype(BF16),
        "w2_ffn1": w2_ffn1.astype(BF16),
        "g_mix": row(g_mix), "w_in": _take_cols(w_in, _w_in_cols()).astype(BF16),
        "g_cq": row(g_cq), "w_uq": _take_cols(w_uq, _w_uq_cols()).astype(BF16),
        "g_mq": row(_take_cols(g_mla_q, mla_gain)),
        "g_ckv": row(g_ckv), "w_uk": _take_cols(w_ukv, _w_uk_cols()).astype(BF16),
        "w_uv": jnp.take(w_ukv, jnp.asarray(_w_uv_cols()), axis=-1).astype(BF16),
        "g_mk": row(_take_cols(g_mla_k, mla_gain)),
        "g_sq": row(jnp.take(g_swa_q, jnp.asarray(swa_gain), axis=-1)),
        "g_sk": row(jnp.take(g_swa_k, jnp.asarray(swa_gain), axis=-1)),
        "w_out": jnp.take(w_out, jnp.asarray(_w_out_rows()), axis=1).astype(BF16),
        "g_ffn2": row(g_ffn2), "w1_ffn2": w1_ffn2.astype(BF16), "w3_ffn2": w3_ffn2.astype(BF16),
        "w2_ffn2": w2_ffn2.astype(BF16),
        "dft64": _channel_dft(),
    }


def kernel(x, c, ctx, c_ctx, w_ada, b_ada, g_ffn1, w1_ffn1, w3_ffn1, w2_ffn1, g_mix, w_in, g_cq, w_uq,
           g_ckv, w_ukv, g_mla_q, g_mla_k, g_swa_q, g_swa_k, sink, w_out, g_ffn2, w1_ffn2, w3_ffn2,
           w2_ffn2):
    n_batch, seq, d = x.shape
    n_ctx = ctx.shape[1]
    depth = w_ada.shape[0]
    assert d == D_MODEL and seq % GRID_W == 0 and n_batch + 1 <= MOD_ROWS
    assert n_ctx % TOK_TILE == 0 and seq % TOK_TILE == 0 and seq % n_ctx == 0
    assert seq % Q_TILE == 0 and seq % MLA_Q_TILE == 0
    assert Q_TILE % ROW_CHUNK == 0 and ROW_CHUNK + 2 * SWA_WINDOW <= seq
    assert w_ada.shape[-1] % MOD_COL_TILE == 0

    p = _prepare_params(g_ffn1, w1_ffn1, w3_ffn1, w2_ffn1, g_mix, w_in, g_cq, w_uq, g_ckv, w_ukv,
                        g_mla_q, g_mla_k, g_swa_q, g_swa_k, w_out, g_ffn2, w1_ffn2, w3_ffn2, w2_ffn2)
    tabs = _rope_tables(seq, n_ctx)
    c_lat, s_lat = _dft_cos_sin(seq, seq ** -0.5)
    c_ctx_dft, s_ctx_dft = _dft_cos_sin(n_ctx, n_ctx ** -0.5)
    dfts = tuple(m.astype(BF16) for m in (c_lat, s_lat, c_ctx_dft, s_ctx_dft))

    cc = jnp.concatenate([c, c_ctx[None, :], jnp.zeros((MOD_ROWS - n_batch - 1, d), F32)], axis=0)
    mod = _modulation(cc, w_ada, b_ada).reshape(depth, MOD_ROWS, N_MOD, d)

    tokens = (x, ctx)
    for layer in range(depth):
        with_ctx = layer != depth - 1
        hh, q, k, v, sq, sk, sv, zcs = _ffn_proj(layer, tokens, mod, p, tabs, seq, seq + n_ctx)
        yf = _fourier(zcs, dfts, n_ctx, with_ctx)
        a = _mla(q, k, v, seq, with_ctx)
        w = _swa(layer, sink, sq, sk, sv, seq, with_ctx)
        hh = _out_ffn(layer, hh, yf, a, w, mod, p, seq, with_ctx)
        tokens = (hh,)
    return hh
```

```python
import functools

import numpy as np
import jax
import jax.numpy as jnp
from jax import lax
from jax.experimental import pallas as pl
from jax.experimental.pallas import tpu as pltpu

F32 = jnp.float32
BF16 = jnp.bfloat16

D_MODEL = 1024
GRID_W = 64
ROPE_BASE = 10000.0
EPS = 1e-6
NEG = -1e30
LOG2E = 1.4426950408889634
N_MOD = 9
D_FF = 2816

FOURIER_WIDTH = 256
FOURIER_GROUP_DIM = 64
MLA_HEADS = 8
MLA_NOPE = 64
MLA_ROPE = 32
MLA_V = 64
MLA_QK_DIM = MLA_NOPE + MLA_ROPE
MLA_Q_RANK = 256
MLA_KV_RANK = 128
SWA_Q_HEADS = 4
SWA_KV_HEADS = 2
SWA_HEAD_DIM = 64
SWA_WINDOW = 128
IN_SPLITS = (256, 256, 128, 32, 256, 128, 128)
IN_WIDTH = sum(IN_SPLITS)

LANES = 128
HALF = LANES // 2
IN_WIDTH_P = 10 * LANES
MLA_V_SLABS = MLA_HEADS
VMEM_LIMIT = 56 * 1024 * 1024

TOK_TILE = 256
Q_TILE = 1024
MLA_Q_TILE = 512
ROW_CHUNK = 256
MOD_ROWS = 24
MOD_COL_TILE = 2304

OFF_F, OFF_CQ, OFF_CKV, OFF_KR, OFF_SQA, OFF_SQB, OFF_SK, OFF_SV = (
    0, 256, 512, 640, 768, 896, 1024, 1152)


def _deinterleave(n):
    return np.concatenate([np.arange(0, n, 2), np.arange(1, n, 2)])


def _w_in_cols():
    zero = IN_WIDTH
    o_f, o_cq, o_ckv, o_kr, o_sq, o_sk, o_sv = np.cumsum((0,) + IN_SPLITS)[:-1]
    cols = np.full((IN_WIDTH_P,), zero, np.int32)
    cols[OFF_F:OFF_F + 256] = o_f + np.arange(256)
    cols[OFF_CQ:OFF_CQ + 256] = o_cq + np.arange(256)
    cols[OFF_CKV:OFF_CKV + 128] = o_ckv + np.arange(128)
    cols[OFF_KR + MLA_NOPE:OFF_KR + MLA_QK_DIM] = o_kr + _deinterleave(MLA_ROPE)
    head = _deinterleave(SWA_HEAD_DIM)
    cols[OFF_SQA:OFF_SQA + 64] = o_sq + 0 * 64 + head
    cols[OFF_SQA + 64:OFF_SQA + 128] = o_sq + 2 * 64 + head
    cols[OFF_SQB:OFF_SQB + 64] = o_sq + 1 * 64 + head
    cols[OFF_SQB + 64:OFF_SQB + 128] = o_sq + 3 * 64 + head
    cols[OFF_SK:OFF_SK + 64] = o_sk + head
    cols[OFF_SK + 64:OFF_SK + 128] = o_sk + 64 + head
    cols[OFF_SV:OFF_SV + 128] = o_sv + np.arange(128)
    return cols


def _mla_head_perm():
    return np.concatenate([np.arange(MLA_NOPE), MLA_NOPE + _deinterleave(MLA_ROPE)])


def _w_uq_cols():
    zero = MLA_HEADS * MLA_QK_DIM
    cols = np.full((MLA_HEADS * LANES,), zero, np.int32)
    for h in range(MLA_HEADS):
        cols[h * LANES:h * LANES + MLA_QK_DIM] = h * MLA_QK_DIM + _mla_head_perm()
    return cols


def _w_uk_cols():
    zero = MLA_HEADS * (MLA_NOPE + MLA_V)
    cols = np.full((MLA_HEADS * LANES,), zero, np.int32)
    for h in range(MLA_HEADS):
        cols[h * LANES:h * LANES + MLA_NOPE] = h * (MLA_NOPE + MLA_V) + np.arange(MLA_NOPE)
    return cols


def _w_uv_cols():
    return np.concatenate([h * (MLA_NOPE + MLA_V) + MLA_NOPE + np.arange(MLA_V)
                           for h in range(MLA_HEADS)]).astype(np.int32)


def _w_out_rows():
    base = FOURIER_WIDTH + MLA_HEADS * MLA_V
    swa = np.concatenate([base + h * SWA_HEAD_DIM + np.arange(SWA_HEAD_DIM) for h in (0, 2, 1, 3)])
    return np.concatenate([np.arange(base), swa]).astype(np.int32)


def _take_cols(w, cols, axis=-1):
    axis = axis % w.ndim
    n = w.shape[axis]
    cols = [int(c) for c in cols]
    pieces, i = [], 0
    while i < len(cols):
        j = i + 1
        if cols[i] == n:
            while j < len(cols) and cols[j] == n:
                j += 1
            shape = w.shape[:axis] + (j - i,) + w.shape[axis + 1:]
            pieces.append(jnp.zeros(shape, w.dtype))
        else:
            stride = cols[j] - cols[i] if j < len(cols) and cols[j] - cols[i] in (1, 2) else 1
            while j < len(cols) and cols[j] != n and cols[j] == cols[j - 1] + stride:
                j += 1
            pieces.append(lax.slice_in_dim(w, cols[i], cols[j - 1] + 1, stride, axis))
        i = j
    return jnp.concatenate(pieces, axis=axis)


def _rope_tables(seq, ctx):
    rows = seq // GRID_W
    pad = jnp.zeros((ctx,), F32)
    row = jnp.concatenate([jnp.repeat(jnp.arange(rows, dtype=F32), GRID_W), pad])[:, None]
    col = jnp.concatenate([jnp.tile(jnp.arange(GRID_W, dtype=F32), rows), pad])[:, None]

    def build(dim, first_lanes, second_lanes):
        axis_dim = dim // 2
        n_freq = axis_dim // 2
        inv = ROPE_BASE ** (-jnp.arange(0, axis_dim, 2, dtype=F32) / axis_dim)
        freq = np.full((LANES,), n_freq, np.int32)
        by_row = np.zeros((LANES,), bool)
        m_a = np.zeros((LANES,), np.float32)
        m_b = np.zeros((LANES,), np.float32)
        for lanes, mask in ((first_lanes, m_a), (second_lanes, m_b)):
            for lo in lanes:
                freq[lo:lo + 2 * n_freq] = np.concatenate([np.arange(n_freq), np.arange(n_freq)])
                by_row[lo:lo + n_freq] = True
                mask[lo:lo + 2 * n_freq] = 1.0
        inv_lane = jnp.take(jnp.concatenate([inv, jnp.zeros((1,), F32)]), jnp.asarray(freq))[None, :]
        ang = jnp.where(jnp.asarray(by_row)[None, :], row * inv_lane, col * inv_lane)
        sin = jnp.sin(ang)
        return jnp.cos(ang), sin * jnp.asarray(-m_a)[None, :], sin * jnp.asarray(m_b)[None, :]

    half_m = MLA_ROPE // 2
    mla = build(MLA_ROPE, (MLA_NOPE,), (MLA_NOPE + half_m,))
    half_s = SWA_HEAD_DIM // 2
    swa = build(SWA_HEAD_DIM, (0, HALF), (half_s, HALF + half_s))
    return mla + swa


def _dft_cos_sin(n, scale):
    def direct(rows_j, n_mod):
        k = jnp.arange(n, dtype=jnp.int32)
        ang = ((rows_j[:, None] * k[None, :]) % n_mod).astype(F32) * (2.0 * np.pi / n_mod)
        return jnp.cos(ang), jnp.sin(ang)

    inner = FOURIER_GROUP_DIM
    if n <= inner or n % inner:
        c, s = direct(jnp.arange(n, dtype=jnp.int32), n)
        return c * scale, s * scale
    outer = n // inner
    ca, sa = direct(jnp.arange(outer, dtype=jnp.int32), outer)
    cb, sb = direct(jnp.arange(inner, dtype=jnp.int32), n)
    cb, sb = cb * scale, sb * scale
    c = ca[:, None, :] * cb[None, :, :] - sa[:, None, :] * sb[None, :, :]
    s = sa[:, None, :] * cb[None, :, :] + ca[:, None, :] * sb[None, :, :]
    return c.reshape(n, n), s.reshape(n, n)


def _channel_dft():
    c, s = _dft_cos_sin(FOURIER_GROUP_DIM, FOURIER_GROUP_DIM ** -0.5)
    eye = jnp.eye(FOURIER_WIDTH // FOURIER_GROUP_DIM, dtype=F32)
    return jnp.concatenate([jnp.kron(eye, c), jnp.kron(eye, s)], axis=1).astype(BF16)


def _rms_scale(x, width):
    return lax.rsqrt(jnp.sum(x * x, axis=-1, keepdims=True) * (1.0 / width) + EPS)


def _norm_mod(x, g, shift, scale):
    y = x * _rms_scale(x, x.shape[-1]) * g
    return y * (1.0 + scale) + shift


def _swiglu(xn, w1_ref, w3_ref, w2_ref):
    a = jnp.dot(xn, w1_ref[...], preferred_element_type=F32)
    b = jnp.dot(xn, w3_ref[...], preferred_element_type=F32)
    g = (a / (1.0 + jnp.exp(-a))) * b
    return jnp.dot(g.astype(BF16), w2_ref[...], preferred_element_type=F32)


def _rope(x, cos, sin_a, sin_b, pair_dist):
    return (x * cos + pltpu.roll(x, LANES - pair_dist, 1) * sin_a
            + pltpu.roll(x, pair_dist, 1) * sin_b)


def _low_lanes(shape):
    return lax.broadcasted_iota(jnp.int32, shape, len(shape) - 1) < HALF


def _mod_kernel(c_ref, w_ref, b_ref, o_ref):
    cv = c_ref[...]
    s = (cv / (1.0 + jnp.exp(-cv))).astype(BF16)
    o_ref[...] = jnp.dot(s, w_ref[...].astype(BF16), preferred_element_type=F32) + b_ref[...]


def _modulation(cc, w_ada, b_ada):
    n_layers, d, width = w_ada.shape
    return pl.pallas_call(
        _mod_kernel,
        grid=(n_layers, width // MOD_COL_TILE),
        in_specs=[
            pl.BlockSpec((MOD_ROWS, d), lambda l, j: (0, 0)),
            pl.BlockSpec((None, d, MOD_COL_TILE), lambda l, j: (l, 0, j)),
            pl.BlockSpec((None, 1, MOD_COL_TILE), lambda l, j: (l, 0, j)),
        ],
        out_specs=pl.BlockSpec((None, MOD_ROWS, MOD_COL_TILE), lambda l, j: (l, 0, j)),
        out_shape=jax.ShapeDtypeStruct((n_layers, MOD_ROWS, width), F32),
        compiler_params=pltpu.CompilerParams(
            dimension_semantics=("arbitrary", "arbitrary"), vmem_limit_bytes=VMEM_LIMIT),
        name="adaln_modulation",
    )(cc, w_ada, b_ada.reshape(n_layers, 1, width))


def _ffn_proj_kernel(*refs, n_src, n_tiles, tpb, n_lat):
    h_refs = refs[:n_src]
    (mod_ref, modp_ref, g1_ref, w1_ref, w3_ref, w2_ref, gmix_ref, win_ref,
     gcq_ref, wuq_ref, gmq_ref, gckv_ref, wuk_ref, wuv_ref, gmk_ref, gsq_ref, gsk_ref, dft_ref,
     cm_ref, sam_ref, sbm_ref, cs_ref, sas_ref, sbs_ref,
     ho_ref, q_ref, k_ref, v_ref, sq_ref, sk_ref, sv_ref, z_ref, hprev_ref) = refs[n_src:]
    step = pl.program_id(0)

    @pl.when(step == 0)
    def _():
        hprev_ref[...] = jnp.zeros_like(hprev_ref)

    modp = modp_ref[...]
    n = _norm_mod(hprev_ref[...], gmix_ref[...], modp[3:4], modp[4:5]).astype(BF16)
    u = jnp.dot(n, win_ref[...], preferred_element_type=F32)

    f = u[:, OFF_F:OFF_F + FOURIER_WIDTH].astype(BF16)
    z_ref[...] = jnp.dot(f, dft_ref[...], preferred_element_type=F32).astype(BF16)

    cm, sam, sbm = cm_ref[...], sam_ref[...], sbm_ref[...]
    cq = u[:, OFF_CQ:OFF_CQ + MLA_Q_RANK]
    cqn = (cq * _rms_scale(cq, MLA_Q_RANK) * gcq_ref[...]).astype(BF16)
    q = jnp.dot(cqn, wuq_ref[...], preferred_element_type=F32)
    gmq = gmq_ref[...]
    q_scale = MLA_QK_DIM ** -0.5 * LOG2E
    for hd in range(MLA_HEADS):
        qh = q[:, hd * LANES:(hd + 1) * LANES]
        qg = qh * (_rms_scale(qh, MLA_QK_DIM) * q_scale) * gmq
        q_ref[:, hd * LANES:(hd + 1) * LANES] = _rope(qg, cm, sam, sbm, MLA_ROPE // 2).astype(BF16)

    ckv = u[:, OFF_CKV:OFF_CKV + MLA_KV_RANK]
    ckvn = (ckv * _rms_scale(ckv, MLA_KV_RANK) * gckv_ref[...]).astype(BF16)
    kn = jnp.dot(ckvn, wuk_ref[...], preferred_element_type=F32)
    vv = jnp.dot(ckvn, wuv_ref[...], preferred_element_type=F32).astype(BF16)
    ones = jnp.ones((vv.shape[0], LANES), BF16)
    for pair in range(MLA_HEADS // 2):
        v_ref[:, 2 * pair * LANES:(2 * pair + 1) * LANES] = vv[:, pair * LANES:(pair + 1) * LANES]
        v_ref[:, (2 * pair + 1) * LANES:(2 * pair + 2) * LANES] = ones
    gmk = gmk_ref[...]
    kr = u[:, OFF_KR:OFF_KR + LANES]
    kr_ss = jnp.sum(kr * kr, axis=-1, keepdims=True)
    kr_rot = _rope(kr * gmk, cm, sam, sbm, MLA_ROPE // 2)
    for hd in range(MLA_HEADS):
        kh = kn[:, hd * LANES:(hd + 1) * LANES]
        ss = jnp.sum(kh * kh, axis=-1, keepdims=True) + kr_ss
        rs = lax.rsqrt(ss * (1.0 / MLA_QK_DIM) + EPS)
        k_ref[:, hd * LANES:(hd + 1) * LANES] = (rs * (kh * gmk + kr_rot)).astype(BF16)

    cs, sas, sbs = cs_ref[...], sas_ref[...], sbs_ref[...]

    def two_head_norm_rope(x, g, scale):
        low = _low_lanes(x.shape)
        x2 = x * x
        lo = jnp.sum(jnp.where(low, x2, 0.0), axis=-1, keepdims=True)
        hi = jnp.sum(jnp.where(low, 0.0, x2), axis=-1, keepdims=True)
        rs = jnp.where(low, lax.rsqrt(lo * (1.0 / SWA_HEAD_DIM) + EPS),
                       lax.rsqrt(hi * (1.0 / SWA_HEAD_DIM) + EPS))
        return _rope(x * (rs * scale) * g, cs, sas, sbs, SWA_HEAD_DIM // 2).astype(BF16)

    gsq = gsq_ref[...]
    s_scale = SWA_HEAD_DIM ** -0.5 * LOG2E
    sq_ref[:, 0:LANES] = two_head_norm_rope(u[:, OFF_SQA:OFF_SQA + LANES], gsq, s_scale)
    sq_ref[:, LANES:2 * LANES] = two_head_norm_rope(u[:, OFF_SQB:OFF_SQB + LANES], gsq, s_scale)
    sk_ref[...] = two_head_norm_rope(u[:, OFF_SK:OFF_SK + LANES], gsk_ref[...], 1.0)
    sv_ref[:, 0:LANES] = u[:, OFF_SV:OFF_SV + LANES].astype(BF16)
    sv_ref[:, LANES:2 * LANES] = ones

    mod = mod_ref[...]
    if n_src == 1:
        h = h_refs[0][...]
    else:
        cur = jnp.minimum(step, n_tiles - 1)
        h = jnp.where(cur % tpb >= n_lat, h_refs[1][...], h_refs[0][...])
    xn = _norm_mod(h, g1_ref[...], mod[0:1], mod[1:2]).astype(BF16)
    h = h + (0.5 * mod[2:3]) * _swiglu(xn, w1_ref, w3_ref, w2_ref)
    ho_ref[...] = h
    hprev_ref[...] = h


def _const_spec(block_shape, index_map):
    return pl.BlockSpec(block_shape, index_map, pipeline_mode=pl.Buffered(1))


def _mod_index(layer, n_batch, n_lat_tiles):
    return lambda b, t: (layer, jnp.where(t >= n_lat_tiles, n_batch, b), 0, 0)


def _ffn_proj(layer, tokens, mod, p, tabs, seq, t_all):
    n_batch, _, d = tokens[0].shape
    tm = TOK_TILE
    tpb = t_all // tm
    n_tiles = n_batch * tpb
    n_lat = seq // tm
    cur = lambda s: jnp.minimum(s, n_tiles - 1)
    prev = lambda s: jnp.maximum(s - 1, 0)
    tile = lambda which, width: pl.BlockSpec(
        (None, tm, width), lambda s: (which(s) // tpb, which(s) % tpb, 0))

    def token_specs(which):
        if len(tokens) == 1:
            return [tile(which, d)]
        lat = pl.BlockSpec((None, tm, d),
                           lambda s: (which(s) // tpb, jnp.minimum(which(s) % tpb, n_lat - 1), 0))
        con = pl.BlockSpec((None, tm, d),
                           lambda s: (which(s) // tpb, jnp.maximum(which(s) % tpb - n_lat, 0), 0))
        return [lat, con]

    def mod_row(which):
        def index(s):
            b, t = which(s) // tpb, which(s) % tpb
            return (layer, jnp.where(t >= n_lat, n_batch, b), 0, 0)
        return pl.BlockSpec((None, None, N_MOD, d), index)

    lw = lambda *shape: _const_spec((None,) + shape, lambda s: (layer,) + (0,) * len(shape))
    tab = pl.BlockSpec((tm, LANES), lambda s: (prev(s) % tpb, 0))
    in_specs = token_specs(cur) + [
        mod_row(cur), mod_row(prev),
        lw(1, d), lw(d, D_FF), lw(d, D_FF), lw(D_FF, d), lw(1, d), lw(d, IN_WIDTH_P),
        lw(1, MLA_Q_RANK), lw(MLA_Q_RANK, MLA_HEADS * LANES), lw(1, LANES),
        lw(1, MLA_KV_RANK), lw(MLA_KV_RANK, MLA_HEADS * LANES), lw(MLA_KV_RANK, MLA_HEADS * MLA_V),
        lw(1, LANES), lw(1, LANES), lw(1, LANES),
        _const_spec((FOURIER_WIDTH, 2 * FOURIER_WIDTH), lambda s: (0, 0)),
        tab, tab, tab, tab, tab, tab,
    ]
    widths = (d, MLA_HEADS * LANES, MLA_HEADS * LANES, MLA_V_SLABS * LANES, 2 * LANES, LANES, 2 * LANES,
              2 * FOURIER_WIDTH)
    dtypes = (F32,) + (BF16,) * 7
    return pl.pallas_call(
        functools.partial(_ffn_proj_kernel, n_src=len(tokens), n_tiles=n_tiles, tpb=tpb, n_lat=n_lat),
        grid=(n_tiles + 1,),
        in_specs=in_specs,
        out_specs=[tile(cur, d)] + [tile(prev, w) for w in widths[1:]],
        out_shape=[jax.ShapeDtypeStruct((n_batch, t_all, w), dt) for w, dt in zip(widths, dtypes)],
        scratch_shapes=[pltpu.VMEM((tm, d), F32)],
        compiler_params=pltpu.CompilerParams(
            dimension_semantics=("arbitrary",), vmem_limit_bytes=VMEM_LIMIT),
        name="ffn_half_and_projections",
    )(*tokens, mod, mod, p["g_ffn1"], p["w1_ffn1"], p["w3_ffn1"], p["w2_ffn1"], p["g_mix"], p["w_in"],
      p["g_cq"], p["w_uq"], p["g_mq"], p["g_ckv"], p["w_uk"], p["w_uv"], p["g_mk"],
      p["g_sq"], p["g_sk"], p["dft64"], *tabs)


def _fourier_kernel(z_ref, cl_ref, sl_ref, cc_ref, sc_ref, o_ref, *, ctx, with_ctx):
    w = FOURIER_WIDTH

    def mix(c_ref, s_ref, lo, n):
        zc = z_ref[lo:lo + n, 0:w]
        zs = z_ref[lo:lo + n, w:2 * w]
        return (jnp.dot(c_ref[...], zc, preferred_element_type=F32)
                - jnp.dot(s_ref[...], zs, preferred_element_type=F32)).astype(BF16)

    n_lat = cl_ref.shape[0]
    o_ref[0:n_lat, :] = mix(cl_ref, sl_ref, 0, n_lat)
    if with_ctx:
        o_ref[n_lat:n_lat + ctx, :] = mix(cc_ref, sc_ref, n_lat, ctx)


def _fourier(zcs, dfts, ctx, with_ctx):
    n_batch, t_all, _ = zcs.shape
    seq = t_all - ctx
    rows = t_all if with_ctx else seq
    full = lambda a: _const_spec(a.shape, lambda b: (0, 0))
    return pl.pallas_call(
        functools.partial(_fourier_kernel, ctx=ctx, with_ctx=with_ctx),
        grid=(n_batch,),
        in_specs=[pl.BlockSpec((None, t_all, 2 * FOURIER_WIDTH), lambda b: (b, 0, 0))]
        + [full(a) for a in dfts],
        out_specs=pl.BlockSpec((None, rows, FOURIER_WIDTH), lambda b: (b, 0, 0)),
        out_shape=jax.ShapeDtypeStruct((n_batch, rows, FOURIER_WIDTH), BF16),
        compiler_params=pltpu.CompilerParams(
            dimension_semantics=("arbitrary",), vmem_limit_bytes=VMEM_LIMIT),
        name="fourier_positions",
    )(zcs, *dfts)


def _scores(q, k):
    return lax.dot_general(q, k, (((1,), (1,)), ((), ())), preferred_element_type=F32)


def _mla_heads(q_ref, k_ref, v_ref, o_ref, head0_scores=None):
    n_heads = q_ref.shape[-1] // LANES
    for pair in range(n_heads // 2):
        outs = []
        vp = v_ref[:, 2 * pair * LANES:(2 * pair + 2) * LANES]
        for hd in (2 * pair, 2 * pair + 1):
            if hd == 0 and head0_scores is not None:
                s = head0_scores
            else:
                s = _scores(q_ref[:, hd * LANES:(hd + 1) * LANES], k_ref[:, hd * LANES:(hd + 1) * LANES])
            m = jnp.max(s, axis=-1, keepdims=True)
            p = jnp.exp2(s - m).astype(BF16)
            r = jnp.dot(p, vp, preferred_element_type=F32)
            outs.append(r[:, 0:LANES] / r[:, LANES:2 * LANES])
        o_ref[:, pair * LANES:(pair + 1) * LANES] = jnp.where(
            _low_lanes(outs[0].shape), outs[0], outs[1]).astype(BF16)


def _mla_latent_kernel(q_ref, k_ref, v_ref, qn_ref, kn_ref, o_ref, s0_ref):
    @pl.when(pl.program_id(0) == 0)
    def _():
        s0_ref[...] = _scores(q_ref[:, 0:LANES], k_ref[:, 0:LANES])

    _mla_heads(q_ref, k_ref, v_ref, o_ref, head0_scores=s0_ref[...])
    s0_ref[...] = _scores(qn_ref[...], kn_ref[...])


def _mla_ctx_kernel(q_ref, k_ref, v_ref, prev_ref, o_ref):
    del prev_ref
    _mla_heads(q_ref, k_ref, v_ref, o_ref)


def _mla(q, k, v, seq, with_ctx):
    n_batch, t_all, _ = q.shape
    n_ctx = t_all - seq
    tq = MLA_Q_TILE
    tpb = seq // tq
    n_tiles = n_batch * tpb
    qk_w = MLA_HEADS * LANES
    v_in = MLA_V_SLABS * LANES
    v_w = MLA_HEADS * MLA_V
    params = lambda n: pltpu.CompilerParams(
        dimension_semantics=("arbitrary",) * n, vmem_limit_bytes=VMEM_LIMIT)
    nxt = lambda s: jnp.minimum(s + 1, n_tiles - 1)
    a = pl.pallas_call(
        _mla_latent_kernel,
        grid=(n_tiles,),
        in_specs=[
            pl.BlockSpec((None, tq, qk_w), lambda s: (s // tpb, s % tpb, 0)),
            pl.BlockSpec((None, t_all, qk_w), lambda s: (s // tpb, 0, 0)),
            pl.BlockSpec((None, t_all, v_in), lambda s: (s // tpb, 0, 0)),
            pl.BlockSpec((None, tq, LANES), lambda s: (nxt(s) // tpb, nxt(s) % tpb, 0)),
            pl.BlockSpec((None, t_all, LANES), lambda s: (nxt(s) // tpb, 0, 0)),
        ],
        out_specs=pl.BlockSpec((None, tq, v_w), lambda s: (s // tpb, s % tpb, 0)),
        out_shape=jax.ShapeDtypeStruct((n_batch, t_all, v_w), BF16),
        scratch_shapes=[pltpu.VMEM((tq, t_all), F32)],
        compiler_params=params(1),
        name="mla_attention",
    )(q, k, v, q, k)
    if not with_ctx:
        return a
    c_blk = seq // n_ctx
    ctx_rows = lambda width: pl.BlockSpec((None, n_ctx, width), lambda b: (b, c_blk, 0))
    return pl.pallas_call(
        _mla_ctx_kernel,
        grid=(n_batch,),
        in_specs=[ctx_rows(qk_w), ctx_rows(qk_w), ctx_rows(v_in), pl.BlockSpec(memory_space=pl.ANY)],
        out_specs=ctx_rows(v_w),
        out_shape=jax.ShapeDtypeStruct(a.shape, a.dtype),
        input_output_aliases={3: 0},
        compiler_params=params(1),
        name="mla_context_attention",
    )(q, k, v, a)


def _swa_heads(sink_ref, layer, q_ref, keys, vals, valid, o_ref, r0=0, rows=None):
    rows = q_ref.shape[0] if rows is None else rows
    res = []
    for hq in range(SWA_Q_HEADS):
        slab = hq % 2
        use_low = hq < 2
        qs = q_ref[r0:r0 + rows, slab * LANES:(slab + 1) * LANES]
        low = _low_lanes(qs.shape)
        qm = jnp.where(low if use_low else jnp.logical_not(low), qs, jnp.zeros_like(qs))
        sink = sink_ref[layer, hq] * LOG2E
        s = _scores(qm, keys)
        if valid is not None:
            s = jnp.where(valid, s, NEG)
        m = jnp.maximum(jnp.max(s, axis=-1, keepdims=True), sink)
        p = jnp.exp2(s - m).astype(BF16)
        r = jnp.dot(p, vals, preferred_element_type=F32)
        res.append(r[:, 0:LANES] / (r[:, LANES:2 * LANES] + jnp.exp2(sink - m)))
    low = _low_lanes(res[0].shape)
    o_ref[r0:r0 + rows, 0:LANES] = jnp.where(low, res[0], res[2]).astype(BF16)
    o_ref[r0:r0 + rows, LANES:2 * LANES] = jnp.where(low, res[1], res[3]).astype(BF16)


def _swa_latent_kernel(sink_ref, q_ref, k_ref, v_ref, o_ref, *, layer, seq):
    tq = q_ref.shape[0]
    rows = min(tq, ROW_CHUNK)
    span = rows + 2 * SWA_WINDOW
    n_keys = span + k_ref.shape[0] - seq
    kc, vc = k_ref[seq:, :], v_ref[seq:, :]
    for r0 in range(0, tq, rows):
        first = pl.program_id(1) * tq + r0
        start = pl.multiple_of(jnp.clip(first - SWA_WINDOW, 0, seq - span), LANES)
        keys = jnp.concatenate([k_ref[pl.ds(start, span), :], kc], axis=0)
        vals = jnp.concatenate([v_ref[pl.ds(start, span), :], vc], axis=0)
        col = lax.broadcasted_iota(jnp.int32, (rows, n_keys), 1)
        dist = first + lax.broadcasted_iota(jnp.int32, (rows, n_keys), 0) - (start + col)
        valid = ((dist <= SWA_WINDOW) & (dist >= -SWA_WINDOW)) | (col >= span)
        _swa_heads(sink_ref, layer, q_ref, keys, vals, valid, o_ref, r0, rows)


def _swa_ctx_kernel(sink_ref, q_ref, k_ref, v_ref, prev_ref, o_ref, *, layer):
    del prev_ref
    _swa_heads(sink_ref, layer, q_ref, k_ref[...], v_ref[...], None, o_ref)


def _swa(layer, sink, q, k, v, seq, with_ctx):
    n_batch, t_all, _ = q.shape
    n_ctx = t_all - seq
    tq = Q_TILE
    params = lambda n: pltpu.CompilerParams(
        dimension_semantics=("arbitrary",) * n, vmem_limit_bytes=VMEM_LIMIT)
    smem = pl.BlockSpec(memory_space=pltpu.SMEM)
    w = pl.pallas_call(
        functools.partial(_swa_latent_kernel, layer=layer, seq=seq),
        grid=(n_batch, seq // tq),
        in_specs=[
            smem,
            pl.BlockSpec((None, tq, 2 * LANES), lambda b, i: (b, i, 0)),
            pl.BlockSpec((None, t_all, LANES), lambda b, i: (b, 0, 0)),
            pl.BlockSpec((None, t_all, 2 * LANES), lambda b, i: (b, 0, 0)),
        ],
        out_specs=pl.BlockSpec((None, tq, 2 * LANES), lambda b, i: (b, i, 0)),
        out_shape=jax.ShapeDtypeStruct((n_batch, t_all, 2 * LANES), BF16),
        compiler_params=params(2),
        name="window_attention",
    )(sink, q, k, v)
    if not with_ctx:
        return w
    c_blk = seq // n_ctx
    ctx_rows = lambda width: pl.BlockSpec((None, n_ctx, width), lambda b: (b, c_blk, 0))
    return pl.pallas_call(
        functools.partial(_swa_ctx_kernel, layer=layer),
        grid=(n_batch,),
        in_specs=[smem, ctx_rows(2 * LANES), ctx_rows(LANES), ctx_rows(2 * LANES),
                  pl.BlockSpec(memory_space=pl.ANY)],
        out_specs=ctx_rows(2 * LANES),
        out_shape=jax.ShapeDtypeStruct(w.shape, w.dtype),
        input_output_aliases={4: 0},
        compiler_params=params(1),
        name="window_context_attention",
    )(sink, q, k, v, w)


def _out_ffn_kernel(h_ref, yf_ref, a_ref, w_ref, mod_ref, wo_ref, g2_ref, w1_ref, w3_ref, w2_ref, o_ref):
    mod = mod_ref[...]
    n_f = yf_ref.shape[-1]
    n_a = a_ref.shape[-1]
    mixed = (jnp.dot(yf_ref[...], wo_ref[0:n_f, :], preferred_element_type=F32)
             + jnp.dot(a_ref[...], wo_ref[n_f:n_f + n_a, :], preferred_element_type=F32)
             + jnp.dot(w_ref[...], wo_ref[n_f + n_a:, :], preferred_element_type=F32))
    h = h_ref[...] + mod[5:6] * mixed
    xn = _norm_mod(h, g2_ref[...], mod[6:7], mod[7:8]).astype(BF16)
    o_ref[...] = h + (0.5 * mod[8:9]) * _swiglu(xn, w1_ref, w3_ref, w2_ref)


def _out_ffn(layer, hh, yf, a, w, mod, p, seq, with_ctx):
    n_batch, t_all, d = hh.shape
    tm = TOK_TILE
    n_t = (t_all if with_ctx else seq) // tm
    lw = lambda *shape: _const_spec((None,) + shape, lambda b, t: (layer,) + (0,) * len(shape))
    mix = lambda width: pl.BlockSpec((None, tm, width), lambda b, t: (b, t, 0))
    return pl.pallas_call(
        _out_ffn_kernel,
        grid=(n_batch, n_t),
        in_specs=[
            mix(d), mix(yf.shape[-1]), mix(a.shape[-1]), mix(w.shape[-1]),
            pl.BlockSpec((None, None, N_MOD, d), _mod_index(layer, n_batch, seq // tm)),
            lw(d, d), lw(1, d), lw(d, D_FF), lw(d, D_FF), lw(D_FF, d),
        ],
        out_specs=pl.BlockSpec((None, tm, d), lambda b, t: (b, t, 0)),
        out_shape=jax.ShapeDtypeStruct((n_batch, n_t * tm, d), F32),
        compiler_params=pltpu.CompilerParams(
            dimension_semantics=("arbitrary", "arbitrary"), vmem_limit_bytes=VMEM_LIMIT),
        name="out_projection_and_ffn_half",
    )(hh, yf, a, w, mod, p["w_out"], p["g_ffn2"], p["w1_ffn2"], p["w3_ffn2"], p["w2_ffn2"])


def _prepare_params(g_ffn1, w1_ffn1, w3_ffn1, w2_ffn1, g_mix, w_in, g_cq, w_uq, g_ckv, w_ukv,
                    g_mla_q, g_mla_k, g_swa_q, g_swa_k, w_out, g_ffn2, w1_ffn2, w3_ffn2, w2_ffn2):
    row = lambda g: g[:, None, :]
    head = _deinterleave(SWA_HEAD_DIM)
    swa_gain = np.concatenate([head, head])
    mla_gain = np.concatenate([_mla_head_perm(), np.full((LANES - MLA_QK_DIM,), MLA_QK_DIM)])
    return {
        "g_ffn1": row(g_ffn1), "w1_ffn1": w1_ffn1.astype(BF16), "w3_ffn1": w3_ffn1.astype(BF16),
        "w2_ffn1": w2_ffn1.astype(BF16),
        "g_mix": row(g_mix), "w_in": _take_cols(w_in, _w_in_cols()).astype(BF16),
        "g_cq": row(g_cq), "w_uq": _take_cols(w_uq, _w_uq_cols()).astype(BF16),
        "g_mq": row(_take_cols(g_mla_q, mla_gain)),
        "g_ckv": row(g_ckv), "w_uk": _take_cols(w_ukv, _w_uk_cols()).astype(BF16),
        "w_uv": _take_cols(w_ukv, _w_uv_cols()).astype(BF16),
        "g_mk": row(_take_cols(g_mla_k, mla_gain)),
        "g_sq": row(_take_cols(g_swa_q, swa_gain)),
        "g_sk": row(_take_cols(g_swa_k, swa_gain)),
        "w_out": _take_cols(w_out, _w_out_rows(), axis=1).astype(BF16),
        "g_ffn2": row(g_ffn2), "w1_ffn2": w1_ffn2.astype(BF16), "w3_ffn2": w3_ffn2.astype(BF16),
        "w2_ffn2": w2_ffn2.astype(BF16),
        "dft64": _channel_dft(),
    }


def kernel(x, c, ctx, c_ctx, w_ada, b_ada, g_ffn1, w1_ffn1, w3_ffn1, w2_ffn1, g_mix, w_in, g_cq, w_uq,
           g_ckv, w_ukv, g_mla_q, g_mla_k, g_swa_q, g_swa_k, sink, w_out, g_ffn2, w1_ffn2, w3_ffn2,
           w2_ffn2):
    n_batch, seq, d = x.shape
    n_ctx = ctx.shape[1]
    depth = w_ada.shape[0]
    assert d == D_MODEL and seq % GRID_W == 0 and n_batch + 1 <= MOD_ROWS
    assert n_ctx % TOK_TILE == 0 and seq % TOK_TILE == 0 and seq % n_ctx == 0
    assert seq % Q_TILE == 0 and seq % MLA_Q_TILE == 0
    assert Q_TILE % ROW_CHUNK == 0 and ROW_CHUNK + 2 * SWA_WINDOW <= seq
    assert w_ada.shape[-1] % MOD_COL_TILE == 0

    p = _prepare_params(g_ffn1, w1_ffn1, w3_ffn1, w2_ffn1, g_mix, w_in, g_cq, w_uq, g_ckv, w_ukv,
                        g_mla_q, g_mla_k, g_swa_q, g_swa_k, w_out, g_ffn2, w1_ffn2, w3_ffn2, w2_ffn2)
    tabs = _rope_tables(seq, n_ctx)
    c_lat, s_lat = _dft_cos_sin(seq, seq ** -0.5)
    c_ctx_dft, s_ctx_dft = _dft_cos_sin(n_ctx, n_ctx ** -0.5)
    dfts = tuple(m.astype(BF16) for m in (c_lat, s_lat, c_ctx_dft, s_ctx_dft))

    cc = jnp.concatenate([c, c_ctx[None, :], jnp.zeros((MOD_ROWS - n_batch - 1, d), F32)], axis=0)
    mod = _modulation(cc, w_ada, b_ada).reshape(depth, MOD_ROWS, N_MOD, d)

    tokens = (x, ctx)
    for layer in range(depth):
        with_ctx = layer != depth - 1
        hh, q, k, v, sq, sk, sv, zcs = _ffn_proj(layer, tokens, mod, p, tabs, seq, seq + n_ctx)
        yf = _fourier(zcs, dfts, n_ctx, with_ctx)
        a = _mla(q, k, v, seq, with_ctx)
        w = _swa(layer, sink, sq, sk, sv, seq, with_ctx)
        hh = _out_ffn(layer, hh, yf, a, w, mod, p, seq, with_ctx)
        tokens = (hh,)
    return hh
```

```python
import functools

import numpy as np
import jax
import jax.numpy as jnp
from jax import lax
from jax.experimental import pallas as pl
from jax.experimental.pallas import tpu as pltpu

F32 = jnp.float32
BF16 = jnp.bfloat16

D_MODEL = 1024
GRID_W = 64
ROPE_BASE = 10000.0
EPS = 1e-6
NEG = -1e30
LOG2E = 1.4426950408889634
N_MOD = 9
D_FF = 2816

FOURIER_WIDTH = 256
FOURIER_GROUP_DIM = 64
MLA_HEADS = 8
MLA_NOPE = 64
MLA_ROPE = 32
MLA_V = 64
MLA_QK_DIM = MLA_NOPE + MLA_ROPE
MLA_Q_RANK = 256
MLA_KV_RANK = 128
SWA_Q_HEADS = 4
SWA_KV_HEADS = 2
SWA_HEAD_DIM = 64
SWA_WINDOW = 128
IN_SPLITS = (256, 256, 128, 32, 256, 128, 128)
IN_WIDTH = sum(IN_SPLITS)

LANES = 128
HALF = LANES // 2
IN_WIDTH_P = 10 * LANES
MLA_V_SLABS = MLA_HEADS
VMEM_LIMIT = 56 * 1024 * 1024

TOK_TILE = 256
Q_TILE = 1024
MLA_Q_TILE = 512
ROW_CHUNK = 256
MOD_ROWS = 24
MOD_COL_TILE = 2304

OFF_F, OFF_CQ, OFF_CKV, OFF_KR, OFF_SQA, OFF_SQB, OFF_SK, OFF_SV = (
    0, 256, 512, 640, 768, 896, 1024, 1152)


def _w_in_cols():
    zero = IN_WIDTH
    o_f, o_cq, o_ckv, o_kr, o_sq, o_sk, o_sv = np.cumsum((0,) + IN_SPLITS)[:-1]
    cols = np.full((IN_WIDTH_P,), zero, np.int32)
    cols[OFF_F:OFF_F + 256] = o_f + np.arange(256)
    cols[OFF_CQ:OFF_CQ + 256] = o_cq + np.arange(256)
    cols[OFF_CKV:OFF_CKV + 128] = o_ckv + np.arange(128)
    cols[OFF_KR + MLA_NOPE:OFF_KR + MLA_QK_DIM] = o_kr + np.arange(MLA_ROPE)
    head = np.arange(SWA_HEAD_DIM)
    cols[OFF_SQA:OFF_SQA + 64] = o_sq + 0 * 64 + head
    cols[OFF_SQA + 64:OFF_SQA + 128] = o_sq + 2 * 64 + head
    cols[OFF_SQB:OFF_SQB + 64] = o_sq + 1 * 64 + head
    cols[OFF_SQB + 64:OFF_SQB + 128] = o_sq + 3 * 64 + head
    cols[OFF_SK:OFF_SK + 64] = o_sk + head
    cols[OFF_SK + 64:OFF_SK + 128] = o_sk + 64 + head
    cols[OFF_SV:OFF_SV + 128] = o_sv + np.arange(128)
    return cols


def _pad_heads(w, n_heads):
    lead, width = w.shape[:-1], w.shape[-1] // n_heads
    w = jnp.pad(w.reshape(lead + (n_heads, width)), [(0, 0)] * (len(lead) + 1) + [(0, LANES - width)])
    return w.reshape(lead + (n_heads * LANES,))


def _w_out_rows():
    base = FOURIER_WIDTH + MLA_HEADS * MLA_V
    swa = np.concatenate([base + h * SWA_HEAD_DIM + np.arange(SWA_HEAD_DIM) for h in (0, 2, 1, 3)])
    return np.concatenate([np.arange(base), swa]).astype(np.int32)


def _take_cols(w, cols, axis=-1):
    axis = axis % w.ndim
    n = w.shape[axis]
    cols = [int(c) for c in cols]
    pieces, i = [], 0
    while i < len(cols):
        j = i + 1
        if cols[i] == n:
            while j < len(cols) and cols[j] == n:
                j += 1
            shape = w.shape[:axis] + (j - i,) + w.shape[axis + 1:]
            pieces.append(jnp.zeros(shape, w.dtype))
        else:
            stride = cols[j] - cols[i] if j < len(cols) and cols[j] - cols[i] in (1, 2) else 1
            while j < len(cols) and cols[j] != n and cols[j] == cols[j - 1] + stride:
                j += 1
            pieces.append(lax.slice_in_dim(w, cols[i], cols[j - 1] + 1, stride, axis))
        i = j
    return jnp.concatenate(pieces, axis=axis)


def _rope_tables(seq, ctx):
    rows = seq // GRID_W
    pad = jnp.zeros((ctx,), F32)
    row = jnp.concatenate([jnp.repeat(jnp.arange(rows, dtype=F32), GRID_W), pad])[:, None]
    col = jnp.concatenate([jnp.tile(jnp.arange(GRID_W, dtype=F32), rows), pad])[:, None]

    def build(dim, section_starts):
        axis_dim = dim // 2
        n_freq = axis_dim // 2
        freq = np.zeros((LANES,), np.float32)
        active = np.zeros((LANES,), bool)
        by_row = np.zeros((LANES,), bool)
        m_a = np.zeros((LANES,), np.float32)
        m_b = np.zeros((LANES,), np.float32)
        pair = np.arange(dim) // 2
        for lo in section_starts:
            freq[lo:lo + dim] = 2 * (pair % n_freq)
            active[lo:lo + dim] = True
            by_row[lo:lo + dim] = pair < n_freq
            m_a[lo:lo + dim:2] = 1.0
            m_b[lo + 1:lo + dim:2] = 1.0
        inv_lane = jnp.where(jnp.asarray(active), ROPE_BASE ** (-jnp.asarray(freq) / axis_dim), 0.0)[None, :]
        ang = jnp.where(jnp.asarray(by_row)[None, :], row * inv_lane, col * inv_lane)
        sin = jnp.sin(ang)
        return jnp.cos(ang), sin * jnp.asarray(-m_a)[None, :], sin * jnp.asarray(m_b)[None, :]

    return build(MLA_ROPE, (MLA_NOPE,)) + build(SWA_HEAD_DIM, (0, HALF))


def _dft_cos_sin(n, scale):
    def direct(rows_j, n_mod):
        k = jnp.arange(n, dtype=jnp.int32)
        ang = ((rows_j[:, None] * k[None, :]) % n_mod).astype(F32) * (2.0 * np.pi / n_mod)
        return jnp.cos(ang), jnp.sin(ang)

    inner = FOURIER_GROUP_DIM
    if n <= inner or n % inner:
        c, s = direct(jnp.arange(n, dtype=jnp.int32), n)
        return c * scale, s * scale
    outer = n // inner
    ca, sa = direct(jnp.arange(outer, dtype=jnp.int32), outer)
    cb, sb = direct(jnp.arange(inner, dtype=jnp.int32), n)
    cb, sb = cb * scale, sb * scale
    c = ca[:, None, :] * cb[None, :, :] - sa[:, None, :] * sb[None, :, :]
    s = sa[:, None, :] * cb[None, :, :] + ca[:, None, :] * sb[None, :, :]
    return c.reshape(n, n), s.reshape(n, n)


def _channel_dft():
    c, s = _dft_cos_sin(FOURIER_GROUP_DIM, FOURIER_GROUP_DIM ** -0.5)
    eye = jnp.eye(FOURIER_WIDTH // FOURIER_GROUP_DIM, dtype=F32)
    return jnp.concatenate([jnp.kron(eye, c), jnp.kron(eye, s)], axis=1).astype(BF16)


def _rms_scale(x, width):
    return lax.rsqrt(jnp.sum(x * x, axis=-1, keepdims=True) * (1.0 / width) + EPS)


def _norm_mod(x, g, shift, scale):
    y = x * _rms_scale(x, x.shape[-1]) * g
    return y * (1.0 + scale) + shift


def _swiglu(xn, w1_ref, w3_ref, w2_ref):
    a = jnp.dot(xn, w1_ref[...], preferred_element_type=F32)
    b = jnp.dot(xn, w3_ref[...], preferred_element_type=F32)
    g = (a / (1.0 + jnp.exp(-a))) * b
    return jnp.dot(g.astype(BF16), w2_ref[...], preferred_element_type=F32)


def _rope(x, cos, sin_a, sin_b):
    return x * cos + pltpu.roll(x, LANES - 1, 1) * sin_a + pltpu.roll(x, 1, 1) * sin_b


def _low_lanes(shape):
    return lax.broadcasted_iota(jnp.int32, shape, len(shape) - 1) < HALF


def _mod_kernel(c_ref, w_ref, b_ref, o_ref):
    cv = c_ref[...]
    s = (cv / (1.0 + jnp.exp(-cv))).astype(BF16)
    o_ref[...] = jnp.dot(s, w_ref[...].astype(BF16), preferred_element_type=F32) + b_ref[...]


def _modulation(cc, w_ada, b_ada):
    n_layers, d, width = w_ada.shape
    return pl.pallas_call(
        _mod_kernel,
        grid=(n_layers, width // MOD_COL_TILE),
        in_specs=[
            pl.BlockSpec((MOD_ROWS, d), lambda l, j: (0, 0)),
            pl.BlockSpec((None, d, MOD_COL_TILE), lambda l, j: (l, 0, j)),
            pl.BlockSpec((None, 1, MOD_COL_TILE), lambda l, j: (l, 0, j)),
        ],
        out_specs=pl.BlockSpec((None, MOD_ROWS, MOD_COL_TILE), lambda l, j: (l, 0, j)),
        out_shape=jax.ShapeDtypeStruct((n_layers, MOD_ROWS, width), F32),
        compiler_params=pltpu.CompilerParams(
            dimension_semantics=("arbitrary", "arbitrary"), vmem_limit_bytes=VMEM_LIMIT),
        name="adaln_modulation",
    )(cc, w_ada, b_ada.reshape(n_layers, 1, width))


def _ffn_proj_kernel(*refs, n_src, n_tiles, tpb, n_lat):
    h_refs = refs[:n_src]
    (mod_ref, modp_ref, g1_ref, w1_ref, w3_ref, w2_ref, gmix_ref, win_ref,
     gcq_ref, wuq_ref, gmq_ref, gckv_ref, wuk_ref, wuv_ref, gmk_ref, gsq_ref, gsk_ref, dft_ref,
     cm_ref, sam_ref, sbm_ref, cs_ref, sas_ref, sbs_ref,
     ho_ref, q_ref, k_ref, v_ref, sq_ref, sk_ref, sv_ref, z_ref, hprev_ref) = refs[n_src:]
    step = pl.program_id(0)

    @pl.when(step == 0)
    def _():
        hprev_ref[...] = jnp.zeros_like(hprev_ref)

    modp = modp_ref[...]
    n = _norm_mod(hprev_ref[...], gmix_ref[...], modp[3:4], modp[4:5]).astype(BF16)
    u = jnp.dot(n, win_ref[...], preferred_element_type=F32)

    f = u[:, OFF_F:OFF_F + FOURIER_WIDTH].astype(BF16)
    z_ref[...] = jnp.dot(f, dft_ref[...], preferred_element_type=F32).astype(BF16)

    cm, sam, sbm = cm_ref[...], sam_ref[...], sbm_ref[...]
    cq = u[:, OFF_CQ:OFF_CQ + MLA_Q_RANK]
    cqn = (cq * _rms_scale(cq, MLA_Q_RANK) * gcq_ref[...]).astype(BF16)
    q = jnp.dot(cqn, wuq_ref[...], preferred_element_type=F32)
    gmq = gmq_ref[...]
    q_scale = MLA_QK_DIM ** -0.5 * LOG2E
    for hd in range(MLA_HEADS):
        qh = q[:, hd * LANES:(hd + 1) * LANES]
        qg = qh * (_rms_scale(qh, MLA_QK_DIM) * q_scale) * gmq
        q_ref[:, hd * LANES:(hd + 1) * LANES] = _rope(qg, cm, sam, sbm).astype(BF16)

    ckv = u[:, OFF_CKV:OFF_CKV + MLA_KV_RANK]
    ckvn = (ckv * _rms_scale(ckv, MLA_KV_RANK) * gckv_ref[...]).astype(BF16)
    kn = jnp.dot(ckvn, wuk_ref[...], preferred_element_type=F32)
    vv = jnp.dot(ckvn, wuv_ref[...], preferred_element_type=F32).astype(BF16)
    ones = jnp.ones((vv.shape[0], LANES), BF16)
    for pair in range(MLA_HEADS // 2):
        v_ref[:, 2 * pair * LANES:(2 * pair + 1) * LANES] = vv[:, pair * LANES:(pair + 1) * LANES]
        v_ref[:, (2 * pair + 1) * LANES:(2 * pair + 2) * LANES] = ones
    gmk = gmk_ref[...]
    kr = u[:, OFF_KR:OFF_KR + LANES]
    kr_ss = jnp.sum(kr * kr, axis=-1, keepdims=True)
    kr_rot = _rope(kr * gmk, cm, sam, sbm)
    for hd in range(MLA_HEADS):
        kh = kn[:, hd * LANES:(hd + 1) * LANES]
        ss = jnp.sum(kh * kh, axis=-1, keepdims=True) + kr_ss
        rs = lax.rsqrt(ss * (1.0 / MLA_QK_DIM) + EPS)
        k_ref[:, hd * LANES:(hd + 1) * LANES] = (rs * (kh * gmk + kr_rot)).astype(BF16)

    cs, sas, sbs = cs_ref[...], sas_ref[...], sbs_ref[...]

    def two_head_norm_rope(x, g, scale):
        low = _low_lanes(x.shape)
        x2 = x * x
        lo = jnp.sum(jnp.where(low, x2, 0.0), axis=-1, keepdims=True)
        hi = jnp.sum(jnp.where(low, 0.0, x2), axis=-1, keepdims=True)
        rs = jnp.where(low, lax.rsqrt(lo * (1.0 / SWA_HEAD_DIM) + EPS),
                       lax.rsqrt(hi * (1.0 / SWA_HEAD_DIM) + EPS))
        return _rope(x * (rs * scale) * g, cs, sas, sbs).astype(BF16)

    gsq = gsq_ref[...]
    s_scale = SWA_HEAD_DIM ** -0.5 * LOG2E
    sq_ref[:, 0:LANES] = two_head_norm_rope(u[:, OFF_SQA:OFF_SQA + LANES], gsq, s_scale)
    sq_ref[:, LANES:2 * LANES] = two_head_norm_rope(u[:, OFF_SQB:OFF_SQB + LANES], gsq, s_scale)
    sk_ref[...] = two_head_norm_rope(u[:, OFF_SK:OFF_SK + LANES], gsk_ref[...], 1.0)
    sv_ref[:, 0:LANES] = u[:, OFF_SV:OFF_SV + LANES].astype(BF16)
    sv_ref[:, LANES:2 * LANES] = ones

    mod = mod_ref[...]
    if n_src == 1:
        h = h_refs[0][...]
    else:
        cur = jnp.minimum(step, n_tiles - 1)
        h = jnp.where(cur % tpb >= n_lat, h_refs[1][...], h_refs[0][...])
    xn = _norm_mod(h, g1_ref[...], mod[0:1], mod[1:2]).astype(BF16)
    h = h + (0.5 * mod[2:3]) * _swiglu(xn, w1_ref, w3_ref, w2_ref)
    ho_ref[...] = h
    hprev_ref[...] = h


def _const_spec(block_shape, index_map):
    return pl.BlockSpec(block_shape, index_map, pipeline_mode=pl.Buffered(1))


def _mod_index(layer, n_batch, n_lat_tiles):
    return lambda b, t: (layer, jnp.where(t >= n_lat_tiles, n_batch, b), 0, 0)


def _ffn_proj(layer, tokens, mod, p, tabs, seq, t_all):
    n_batch, _, d = tokens[0].shape
    tm = TOK_TILE
    tpb = t_all // tm
    n_tiles = n_batch * tpb
    n_lat = seq // tm
    cur = lambda s: jnp.minimum(s, n_tiles - 1)
    prev = lambda s: jnp.maximum(s - 1, 0)
    tile = lambda which, width: pl.BlockSpec(
        (None, tm, width), lambda s: (which(s) // tpb, which(s) % tpb, 0))

    def token_specs(which):
        if len(tokens) == 1:
            return [tile(which, d)]
        lat = pl.BlockSpec((None, tm, d),
                           lambda s: (which(s) // tpb, jnp.minimum(which(s) % tpb, n_lat - 1), 0))
        con = pl.BlockSpec((None, tm, d),
                           lambda s: (which(s) // tpb, jnp.maximum(which(s) % tpb - n_lat, 0), 0))
        return [lat, con]

    def mod_row(which):
        def index(s):
            b, t = which(s) // tpb, which(s) % tpb
            return (layer, jnp.where(t >= n_lat, n_batch, b), 0, 0)
        return pl.BlockSpec((None, None, N_MOD, d), index)

    lw = lambda *shape: _const_spec((None,) + shape, lambda s: (layer,) + (0,) * len(shape))
    tab = pl.BlockSpec((tm, LANES), lambda s: (prev(s) % tpb, 0))
    in_specs = token_specs(cur) + [
        mod_row(cur), mod_row(prev),
        lw(1, d), lw(d, D_FF), lw(d, D_FF), lw(D_FF, d), lw(1, d), lw(d, IN_WIDTH_P),
        lw(1, MLA_Q_RANK), lw(MLA_Q_RANK, MLA_HEADS * LANES), lw(1, LANES),
        lw(1, MLA_KV_RANK), lw(MLA_KV_RANK, MLA_HEADS * LANES), lw(MLA_KV_RANK, MLA_HEADS * MLA_V),
        lw(1, LANES), lw(1, LANES), lw(1, LANES),
        _const_spec((FOURIER_WIDTH, 2 * FOURIER_WIDTH), lambda s: (0, 0)),
        tab, tab, tab, tab, tab, tab,
    ]
    widths = (d, MLA_HEADS * LANES, MLA_HEADS * LANES, MLA_V_SLABS * LANES, 2 * LANES, LANES, 2 * LANES,
              2 * FOURIER_WIDTH)
    dtypes = (F32,) + (BF16,) * 7
    return pl.pallas_call(
        functools.partial(_ffn_proj_kernel, n_src=len(tokens), n_tiles=n_tiles, tpb=tpb, n_lat=n_lat),
        grid=(n_tiles + 1,),
        in_specs=in_specs,
        out_specs=[tile(cur, d)] + [tile(prev, w) for w in widths[1:]],
        out_shape=[jax.ShapeDtypeStruct((n_batch, t_all, w), dt) for w, dt in zip(widths, dtypes)],
        scratch_shapes=[pltpu.VMEM((tm, d), F32)],
        compiler_params=pltpu.CompilerParams(
            dimension_semantics=("arbitrary",), vmem_limit_bytes=VMEM_LIMIT),
        name="ffn_half_and_projections",
    )(*tokens, mod, mod, p["g_ffn1"], p["w1_ffn1"], p["w3_ffn1"], p["w2_ffn1"], p["g_mix"], p["w_in"],
      p["g_cq"], p["w_uq"], p["g_mq"], p["g_ckv"], p["w_uk"], p["w_uv"], p["g_mk"],
      p["g_sq"], p["g_sk"], p["dft64"], *tabs)


def _fourier_kernel(z_ref, cl_ref, sl_ref, cc_ref, sc_ref, o_ref, *, ctx, with_ctx):
    w = FOURIER_WIDTH

    def mix(c_ref, s_ref, lo, n):
        zc = z_ref[lo:lo + n, 0:w]
        zs = z_ref[lo:lo + n, w:2 * w]
        return (jnp.dot(c_ref[...], zc, preferred_element_type=F32)
                - jnp.dot(s_ref[...], zs, preferred_element_type=F32)).astype(BF16)

    n_lat = cl_ref.shape[0]
    o_ref[0:n_lat, :] = mix(cl_ref, sl_ref, 0, n_lat)
    if with_ctx:
        o_ref[n_lat:n_lat + ctx, :] = mix(cc_ref, sc_ref, n_lat, ctx)


def _fourier(zcs, dfts, ctx, with_ctx):
    n_batch, t_all, _ = zcs.shape
    seq = t_all - ctx
    rows = t_all if with_ctx else seq
    full = lambda a: _const_spec(a.shape, lambda b: (0, 0))
    return pl.pallas_call(
        functools.partial(_fourier_kernel, ctx=ctx, with_ctx=with_ctx),
        grid=(n_batch,),
        in_specs=[pl.BlockSpec((None, t_all, 2 * FOURIER_WIDTH), lambda b: (b, 0, 0))]
        + [full(a) for a in dfts],
        out_specs=pl.BlockSpec((None, rows, FOURIER_WIDTH), lambda b: (b, 0, 0)),
        out_shape=jax.ShapeDtypeStruct((n_batch, rows, FOURIER_WIDTH), BF16),
        compiler_params=pltpu.CompilerParams(
            dimension_semantics=("arbitrary",), vmem_limit_bytes=VMEM_LIMIT),
        name="fourier_positions",
    )(zcs, *dfts)


def _scores(q, k):
    return lax.dot_general(q, k, (((1,), (1,)), ((), ())), preferred_element_type=F32)


def _mla_heads(q_ref, k_ref, v_ref, o_ref, head0_scores=None):
    n_heads = q_ref.shape[-1] // LANES
    for pair in range(n_heads // 2):
        outs = []
        vp = v_ref[:, 2 * pair * LANES:(2 * pair + 2) * LANES]
        for hd in (2 * pair, 2 * pair + 1):
            if hd == 0 and head0_scores is not None:
                s = head0_scores
            else:
                s = _scores(q_ref[:, hd * LANES:(hd + 1) * LANES], k_ref[:, hd * LANES:(hd + 1) * LANES])
            m = jnp.max(s, axis=-1, keepdims=True)
            p = jnp.exp2(s - m).astype(BF16)
            r = jnp.dot(p, vp, preferred_element_type=F32)
            outs.append(r[:, 0:LANES] / r[:, LANES:2 * LANES])
        o_ref[:, pair * LANES:(pair + 1) * LANES] = jnp.where(
            _low_lanes(outs[0].shape), outs[0], outs[1]).astype(BF16)


def _mla_latent_kernel(q_ref, k_ref, v_ref, qn_ref, kn_ref, o_ref, s0_ref):
    @pl.when(pl.program_id(0) == 0)
    def _():
        s0_ref[...] = _scores(q_ref[:, 0:LANES], k_ref[:, 0:LANES])

    _mla_heads(q_ref, k_ref, v_ref, o_ref, head0_scores=s0_ref[...])
    s0_ref[...] = _scores(qn_ref[...], kn_ref[...])


def _mla_ctx_kernel(q_ref, k_ref, v_ref, prev_ref, o_ref):
    del prev_ref
    _mla_heads(q_ref, k_ref, v_ref, o_ref)


def _mla(q, k, v, seq, with_ctx):
    n_batch, t_all, _ = q.shape
    n_ctx = t_all - seq
    tq = MLA_Q_TILE
    tpb = seq // tq
    n_tiles = n_batch * tpb
    qk_w = MLA_HEADS * LANES
    v_in = MLA_V_SLABS * LANES
    v_w = MLA_HEADS * MLA_V
    params = lambda n: pltpu.CompilerParams(
        dimension_semantics=("arbitrary",) * n, vmem_limit_bytes=VMEM_LIMIT)
    nxt = lambda s: jnp.minimum(s + 1, n_tiles - 1)
    a = pl.pallas_call(
        _mla_latent_kernel,
        grid=(n_tiles,),
        in_specs=[
            pl.BlockSpec((None, tq, qk_w), lambda s: (s // tpb, s % tpb, 0)),
            pl.BlockSpec((None, t_all, qk_w), lambda s: (s // tpb, 0, 0)),
            pl.BlockSpec((None, t_all, v_in), lambda s: (s // tpb, 0, 0)),
            pl.BlockSpec((None, tq, LANES), lambda s: (nxt(s) // tpb, nxt(s) % tpb, 0)),
            pl.BlockSpec((None, t_all, LANES), lambda s: (nxt(s) // tpb, 0, 0)),
        ],
        out_specs=pl.BlockSpec((None, tq, v_w), lambda s: (s // tpb, s % tpb, 0)),
        out_shape=jax.ShapeDtypeStruct((n_batch, t_all, v_w), BF16),
        scratch_shapes=[pltpu.VMEM((tq, t_all), F32)],
        compiler_params=params(1),
        name="mla_attention",
    )(q, k, v, q, k)
    if not with_ctx:
        return a
    c_blk = seq // n_ctx
    ctx_rows = lambda width: pl.BlockSpec((None, n_ctx, width), lambda b: (b, c_blk, 0))
    return pl.pallas_call(
        _mla_ctx_kernel,
        grid=(n_batch,),
        in_specs=[ctx_rows(qk_w), ctx_rows(qk_w), ctx_rows(v_in), pl.BlockSpec(memory_space=pl.ANY)],
        out_specs=ctx_rows(v_w),
        out_shape=jax.ShapeDtypeStruct(a.shape, a.dtype),
        input_output_aliases={3: 0},
        compiler_params=params(1),
        name="mla_context_attention",
    )(q, k, v, a)


def _swa_heads(sink_ref, layer, q_ref, keys, vals, valid, o_ref, r0=0, rows=None):
    rows = q_ref.shape[0] if rows is None else rows
    res = []
    for hq in range(SWA_Q_HEADS):
        slab = hq % 2
        use_low = hq < 2
        qs = q_ref[r0:r0 + rows, slab * LANES:(slab + 1) * LANES]
        low = _low_lanes(qs.shape)
        qm = jnp.where(low if use_low else jnp.logical_not(low), qs, jnp.zeros_like(qs))
        sink = sink_ref[layer, hq] * LOG2E
        s = _scores(qm, keys)
        if valid is not None:
            s = jnp.where(valid, s, NEG)
        m = jnp.maximum(jnp.max(s, axis=-1, keepdims=True), sink)
        p = jnp.exp2(s - m).astype(BF16)
        r = jnp.dot(p, vals, preferred_element_type=F32)
        res.append(r[:, 0:LANES] / (r[:, LANES:2 * LANES] + jnp.exp2(sink - m)))
    low = _low_lanes(res[0].shape)
    o_ref[r0:r0 + rows, 0:LANES] = jnp.where(low, res[0], res[2]).astype(BF16)
    o_ref[r0:r0 + rows, LANES:2 * LANES] = jnp.where(low, res[1], res[3]).astype(BF16)


def _swa_latent_kernel(sink_ref, q_ref, k_ref, v_ref, o_ref, *, layer, seq):
    tq = q_ref.shape[0]
    rows = min(tq, ROW_CHUNK)
    span = rows + 2 * SWA_WINDOW
    n_keys = span + k_ref.shape[0] - seq
    kc, vc = k_ref[seq:, :], v_ref[seq:, :]
    for r0 in range(0, tq, rows):
        first = pl.program_id(1) * tq + r0
        start = pl.multiple_of(jnp.clip(first - SWA_WINDOW, 0, seq - span), LANES)
        keys = jnp.concatenate([k_ref[pl.ds(start, span), :], kc], axis=0)
        vals = jnp.concatenate([v_ref[pl.ds(start, span), :], vc], axis=0)
        col = lax.broadcasted_iota(jnp.int32, (rows, n_keys), 1)
        dist = first + lax.broadcasted_iota(jnp.int32, (rows, n_keys), 0) - (start + col)
        valid = ((dist <= SWA_WINDOW) & (dist >= -SWA_WINDOW)) | (col >= span)
        _swa_heads(sink_ref, layer, q_ref, keys, vals, valid, o_ref, r0, rows)


def _swa_ctx_kernel(sink_ref, q_ref, k_ref, v_ref, prev_ref, o_ref, *, layer):
    del prev_ref
    _swa_heads(sink_ref, layer, q_ref, k_ref[...], v_ref[...], None, o_ref)


def _swa(layer, sink, q, k, v, seq, with_ctx):
    n_batch, t_all, _ = q.shape
    n_ctx = t_all - seq
    tq = Q_TILE
    params = lambda n: pltpu.CompilerParams(
        dimension_semantics=("arbitrary",) * n, vmem_limit_bytes=VMEM_LIMIT)
    smem = pl.BlockSpec(memory_space=pltpu.SMEM)
    w = pl.pallas_call(
        functools.partial(_swa_latent_kernel, layer=layer, seq=seq),
        grid=(n_batch, seq // tq),
        in_specs=[
            smem,
            pl.BlockSpec((None, tq, 2 * LANES), lambda b, i: (b, i, 0)),
            pl.BlockSpec((None, t_all, LANES), lambda b, i: (b, 0, 0)),
            pl.BlockSpec((None, t_all, 2 * LANES), lambda b, i: (b, 0, 0)),
        ],
        out_specs=pl.BlockSpec((None, tq, 2 * LANES), lambda b, i: (b, i, 0)),
        out_shape=jax.ShapeDtypeStruct((n_batch, t_all, 2 * LANES), BF16),
        compiler_params=params(2),
        name="window_attention",
    )(sink, q, k, v)
    if not with_ctx:
        return w
    c_blk = seq // n_ctx
    ctx_rows = lambda width: pl.BlockSpec((None, n_ctx, width), lambda b: (b, c_blk, 0))
    return pl.pallas_call(
        functools.partial(_swa_ctx_kernel, layer=layer),
        grid=(n_batch,),
        in_specs=[smem, ctx_rows(2 * LANES), ctx_rows(LANES), ctx_rows(2 * LANES),
                  pl.BlockSpec(memory_space=pl.ANY)],
        out_specs=ctx_rows(2 * LANES),
        out_shape=jax.ShapeDtypeStruct(w.shape, w.dtype),
        input_output_aliases={4: 0},
        compiler_params=params(1),
        name="window_context_attention",
    )(sink, q, k, v, w)


def _out_ffn_kernel(h_ref, yf_ref, a_ref, w_ref, mod_ref, wo_ref, g2_ref, w1_ref, w3_ref, w2_ref, o_ref):
    mod = mod_ref[...]
    n_f = yf_ref.shape[-1]
    n_a = a_ref.shape[-1]
    mixed = (jnp.dot(yf_ref[...], wo_ref[0:n_f, :], preferred_element_type=F32)
             + jnp.dot(a_ref[...], wo_ref[n_f:n_f + n_a, :], preferred_element_type=F32)
             + jnp.dot(w_ref[...], wo_ref[n_f + n_a:, :], preferred_element_type=F32))
    h = h_ref[...] + mod[5:6] * mixed
    xn = _norm_mod(h, g2_ref[...], mod[6:7], mod[7:8]).astype(BF16)
    o_ref[...] = h + (0.5 * mod[8:9]) * _swiglu(xn, w1_ref, w3_ref, w2_ref)


def _out_ffn(layer, hh, yf, a, w, mod, p, seq, with_ctx):
    n_batch, t_all, d = hh.shape
    tm = TOK_TILE
    n_t = (t_all if with_ctx else seq) // tm
    lw = lambda *shape: _const_spec((None,) + shape, lambda b, t: (layer,) + (0,) * len(shape))
    mix = lambda width: pl.BlockSpec((None, tm, width), lambda b, t: (b, t, 0))
    return pl.pallas_call(
        _out_ffn_kernel,
        grid=(n_batch, n_t),
        in_specs=[
            mix(d), mix(yf.shape[-1]), mix(a.shape[-1]), mix(w.shape[-1]),
            pl.BlockSpec((None, None, N_MOD, d), _mod_index(layer, n_batch, seq // tm)),
            lw(d, d), lw(1, d), lw(d, D_FF), lw(d, D_FF), lw(D_FF, d),
        ],
        out_specs=pl.BlockSpec((None, tm, d), lambda b, t: (b, t, 0)),
        out_shape=jax.ShapeDtypeStruct((n_batch, n_t * tm, d), F32),
        compiler_params=pltpu.CompilerParams(
            dimension_semantics=("arbitrary", "arbitrary"), vmem_limit_bytes=VMEM_LIMIT),
        name="out_projection_and_ffn_half",
    )(hh, yf, a, w, mod, p["w_out"], p["g_ffn2"], p["w1_ffn2"], p["w3_ffn2"], p["w2_ffn2"])


def _prepare_params(g_ffn1, w1_ffn1, w3_ffn1, w2_ffn1, g_mix, w_in, g_cq, w_uq, g_ckv, w_ukv,
                    g_mla_q, g_mla_k, g_swa_q, g_swa_k, w_out, g_ffn2, w1_ffn2, w3_ffn2, w2_ffn2):
    row = lambda g: g[:, None, :]
    two_heads = lambda g: jnp.concatenate([g, g], axis=-1)
    kv = w_ukv.reshape(w_ukv.shape[:-1] + (MLA_HEADS, MLA_NOPE + MLA_V))
    flat = lambda t: t.reshape(t.shape[:-2] + (t.shape[-2] * t.shape[-1],))
    return {
        "g_ffn1": row(g_ffn1), "w1_ffn1": w1_ffn1.astype(BF16), "w3_ffn1": w3_ffn1.astype(BF16),
        "w2_ffn1": w2_ffn1.astype(BF16),
        "g_mix": row(g_mix), "w_in": _take_cols(w_in, _w_in_cols()).astype(BF16),
        "g_cq": row(g_cq), "w_uq": _pad_heads(w_uq, MLA_HEADS).astype(BF16),
        "g_mq": row(_pad_heads(g_mla_q, 1)),
        "g_ckv": row(g_ckv), "w_uk": _pad_heads(flat(kv[..., :MLA_NOPE]), MLA_HEADS).astype(BF16),
        "w_uv": flat(kv[..., MLA_NOPE:]).astype(BF16),
        "g_mk": row(_pad_heads(g_mla_k, 1)),
        "g_sq": row(two_heads(g_swa_q)),
        "g_sk": row(two_heads(g_swa_k)),
        "w_out": _take_cols(w_out, _w_out_rows(), axis=1).astype(BF16),
        "g_ffn2": row(g_ffn2), "w1_ffn2": w1_ffn2.astype(BF16), "w3_ffn2": w3_ffn2.astype(BF16),
        "w2_ffn2": w2_ffn2.astype(BF16),
        "dft64": _channel_dft(),
    }


def kernel(x, c, ctx, c_ctx, w_ada, b_ada, g_ffn1, w1_ffn1, w3_ffn1, w2_ffn1, g_mix, w_in, g_cq, w_uq,
           g_ckv, w_ukv, g_mla_q, g_mla_k, g_swa_q, g_swa_k, sink, w_out, g_ffn2, w1_ffn2, w3_ffn2,
           w2_ffn2):
    n_batch, seq, d = x.shape
    n_ctx = ctx.shape[1]
    depth = w_ada.shape[0]
    assert d == D_MODEL and seq % GRID_W == 0 and n_batch + 1 <= MOD_ROWS
    assert n_ctx % TOK_TILE == 0 and seq % TOK_TILE == 0 and seq % n_ctx == 0
    assert seq % Q_TILE == 0 and seq % MLA_Q_TILE == 0
    assert Q_TILE % ROW_CHUNK == 0 and ROW_CHUNK + 2 * SWA_WINDOW <= seq
    assert w_ada.shape[-1] % MOD_COL_TILE == 0

    p = _prepare_params(g_ffn1, w1_ffn1, w3_ffn1, w2_ffn1, g_mix, w_in, g_cq, w_uq, g_ckv, w_ukv,
                        g_mla_q, g_mla_k, g_swa_q, g_swa_k, w_out, g_ffn2, w1_ffn2, w3_ffn2, w2_ffn2)
    tabs = _rope_tables(seq, n_ctx)
    c_lat, s_lat = _dft_cos_sin(seq, seq ** -0.5)
    c_ctx_dft, s_ctx_dft = _dft_cos_sin(n_ctx, n_ctx ** -0.5)
    dfts = tuple(m.astype(BF16) for m in (c_lat, s_lat, c_ctx_dft, s_ctx_dft))

    cc = jnp.concatenate([c, c_ctx[None, :], jnp.zeros((MOD_ROWS - n_batch - 1, d), F32)], axis=0)
    mod = _modulation(cc, w_ada, b_ada).reshape(depth, MOD_ROWS, N_MOD, d)

    tokens = (x, ctx)
    for layer in range(depth):
        with_ctx = layer != depth - 1
        hh, q, k, v, sq, sk, sv, zcs = _ffn_proj(layer, tokens, mod, p, tabs, seq, seq + n_ctx)
        yf = _fourier(zcs, dfts, n_ctx, with_ctx)
        a = _mla(q, k, v, seq, with_ctx)
        w = _swa(layer, sink, sq, sk, sv, seq, with_ctx)
        hh = _out_ffn(layer, hh, yf, a, w, mod, p, seq, with_ctx)
        tokens = (hh,)
    return hh
```

```python
import functools

import numpy as np
import jax
import jax.numpy as jnp
from jax import lax
from jax.experimental import pallas as pl
from jax.experimental.pallas import tpu as pltpu

F32 = jnp.float32
BF16 = jnp.bfloat16

D_MODEL = 1024
GRID_W = 64
ROPE_BASE = 10000.0
EPS = 1e-6
NEG = -1e30
LOG2E = 1.4426950408889634
N_MOD = 9
D_FF = 2816

FOURIER_WIDTH = 256
FOURIER_GROUP_DIM = 64
MLA_HEADS = 8
MLA_NOPE = 64
MLA_ROPE = 32
MLA_V = 64
MLA_QK_DIM = MLA_NOPE + MLA_ROPE
MLA_Q_RANK = 256
MLA_KV_RANK = 128
SWA_Q_HEADS = 4
SWA_KV_HEADS = 2
SWA_HEAD_DIM = 64
SWA_WINDOW = 128
IN_SPLITS = (256, 256, 128, 32, 256, 128, 128)
IN_WIDTH = sum(IN_SPLITS)

LANES = 128
HALF = LANES // 2
IN_WIDTH_P = 10 * LANES
MLA_V_SLABS = MLA_HEADS
VMEM_LIMIT = 56 * 1024 * 1024

TOK_TILE = 256
OUT_TILE = 2 * TOK_TILE
Q_TILE = 1024
MLA_Q_TILE = 512
ROW_CHUNK = 256
MOD_ROWS = 24
MOD_COL_TILE = 2304

OFF_F, OFF_CQ, OFF_CKV, OFF_KR, OFF_SQA, OFF_SQB, OFF_SK, OFF_SV = (
    0, 256, 512, 640, 768, 896, 1024, 1152)


def _w_in_cols():
    zero = IN_WIDTH
    o_f, o_cq, o_ckv, o_kr, o_sq, o_sk, o_sv = np.cumsum((0,) + IN_SPLITS)[:-1]
    cols = np.full((IN_WIDTH_P,), zero, np.int32)
    cols[OFF_F:OFF_F + 256] = o_f + np.arange(256)
    cols[OFF_CQ:OFF_CQ + 256] = o_cq + np.arange(256)
    cols[OFF_CKV:OFF_CKV + 128] = o_ckv + np.arange(128)
    cols[OFF_KR + MLA_NOPE:OFF_KR + MLA_QK_DIM] = o_kr + np.arange(MLA_ROPE)
    head = np.arange(SWA_HEAD_DIM)
    cols[OFF_SQA:OFF_SQA + 64] = o_sq + 0 * 64 + head
    cols[OFF_SQA + 64:OFF_SQA + 128] = o_sq + 2 * 64 + head
    cols[OFF_SQB:OFF_SQB + 64] = o_sq + 1 * 64 + head
    cols[OFF_SQB + 64:OFF_SQB + 128] = o_sq + 3 * 64 + head
    cols[OFF_SK:OFF_SK + 64] = o_sk + head
    cols[OFF_SK + 64:OFF_SK + 128] = o_sk + 64 + head
    cols[OFF_SV:OFF_SV + 128] = o_sv + np.arange(128)
    return cols


def _pad_heads(w, n_heads):
    lead, width = w.shape[:-1], w.shape[-1] // n_heads
    w = jnp.pad(w.reshape(lead + (n_heads, width)), [(0, 0)] * (len(lead) + 1) + [(0, LANES - width)])
    return w.reshape(lead + (n_heads * LANES,))


def _w_out_rows():
    base = FOURIER_WIDTH + MLA_HEADS * MLA_V
    swa = np.concatenate([base + h * SWA_HEAD_DIM + np.arange(SWA_HEAD_DIM) for h in (0, 2, 1, 3)])
    return np.concatenate([np.arange(base), swa]).astype(np.int32)


def _take_cols(w, cols, axis=-1):
    axis = axis % w.ndim
    n = w.shape[axis]
    cols = [int(c) for c in cols]
    pieces, i = [], 0
    while i < len(cols):
        j = i + 1
        if cols[i] == n:
            while j < len(cols) and cols[j] == n:
                j += 1
            shape = w.shape[:axis] + (j - i,) + w.shape[axis + 1:]
            pieces.append(jnp.zeros(shape, w.dtype))
        else:
            stride = cols[j] - cols[i] if j < len(cols) and cols[j] - cols[i] in (1, 2) else 1
            while j < len(cols) and cols[j] != n and cols[j] == cols[j - 1] + stride:
                j += 1
            pieces.append(lax.slice_in_dim(w, cols[i], cols[j - 1] + 1, stride, axis))
        i = j
    return jnp.concatenate(pieces, axis=axis)


def _rope_tables(seq, ctx):
    rows = seq // GRID_W
    pad = jnp.zeros((ctx,), F32)
    row = jnp.concatenate([jnp.repeat(jnp.arange(rows, dtype=F32), GRID_W), pad])[:, None]
    col = jnp.concatenate([jnp.tile(jnp.arange(GRID_W, dtype=F32), rows), pad])[:, None]

    def build(dim, section_starts):
        axis_dim = dim // 2
        n_freq = axis_dim // 2
        freq = np.zeros((LANES,), np.float32)
        active = np.zeros((LANES,), bool)
        by_row = np.zeros((LANES,), bool)
        m_a = np.zeros((LANES,), np.float32)
        m_b = np.zeros((LANES,), np.float32)
        pair = np.arange(dim) // 2
        for lo in section_starts:
            freq[lo:lo + dim] = 2 * (pair % n_freq)
            active[lo:lo + dim] = True
            by_row[lo:lo + dim] = pair < n_freq
            m_a[lo:lo + dim:2] = 1.0
            m_b[lo + 1:lo + dim:2] = 1.0
        inv_lane = jnp.where(jnp.asarray(active), ROPE_BASE ** (-jnp.asarray(freq) / axis_dim), 0.0)[None, :]
        ang = jnp.where(jnp.asarray(by_row)[None, :], row * inv_lane, col * inv_lane)
        sin = jnp.sin(ang)
        return jnp.cos(ang), sin * jnp.asarray(-m_a)[None, :], sin * jnp.asarray(m_b)[None, :]

    return build(MLA_ROPE, (MLA_NOPE,)) + build(SWA_HEAD_DIM, (0, HALF))


def _dft_cos_sin(n, scale):
    def direct(rows_j, n_mod):
        k = jnp.arange(n, dtype=jnp.int32)
        ang = ((rows_j[:, None] * k[None, :]) % n_mod).astype(F32) * (2.0 * np.pi / n_mod)
        return jnp.cos(ang), jnp.sin(ang)

    inner = FOURIER_GROUP_DIM
    if n <= inner or n % inner:
        c, s = direct(jnp.arange(n, dtype=jnp.int32), n)
        return c * scale, s * scale
    outer = n // inner
    ca, sa = direct(jnp.arange(outer, dtype=jnp.int32), outer)
    cb, sb = direct(jnp.arange(inner, dtype=jnp.int32), n)
    cb, sb = cb * scale, sb * scale
    c = ca[:, None, :] * cb[None, :, :] - sa[:, None, :] * sb[None, :, :]
    s = sa[:, None, :] * cb[None, :, :] + ca[:, None, :] * sb[None, :, :]
    return c.reshape(n, n), s.reshape(n, n)


def _channel_dft():
    c, s = _dft_cos_sin(FOURIER_GROUP_DIM, FOURIER_GROUP_DIM ** -0.5)
    eye = jnp.eye(FOURIER_WIDTH // FOURIER_GROUP_DIM, dtype=F32)
    return jnp.concatenate([jnp.kron(eye, c), jnp.kron(eye, s)], axis=1).astype(BF16)


def _rms_scale(x, width):
    return lax.rsqrt(jnp.sum(x * x, axis=-1, keepdims=True) * (1.0 / width) + EPS)


def _norm_mod(x, g, shift, scale):
    y = x * _rms_scale(x, x.shape[-1]) * g
    return y * (1.0 + scale) + shift


def _swiglu(xn, w1_ref, w3_ref, w2_ref):
    a = jnp.dot(xn, w1_ref[...], preferred_element_type=F32)
    b = jnp.dot(xn, w3_ref[...], preferred_element_type=F32)
    g = (a / (1.0 + jnp.exp(-a))) * b
    return jnp.dot(g.astype(BF16), w2_ref[...], preferred_element_type=F32)


def _rope(x, cos, sin_a, sin_b):
    return x * cos + pltpu.roll(x, LANES - 1, 1) * sin_a + pltpu.roll(x, 1, 1) * sin_b


def _low_lanes(shape):
    return lax.broadcasted_iota(jnp.int32, shape, len(shape) - 1) < HALF


def _mod_kernel(c_ref, w_ref, b_ref, o_ref):
    cv = c_ref[...]
    s = (cv / (1.0 + jnp.exp(-cv))).astype(BF16)
    o_ref[...] = jnp.dot(s, w_ref[...].astype(BF16), preferred_element_type=F32) + b_ref[...]


def _modulation(cc, w_ada, b_ada):
    n_layers, d, width = w_ada.shape
    return pl.pallas_call(
        _mod_kernel,
        grid=(n_layers, width // MOD_COL_TILE),
        in_specs=[
            pl.BlockSpec((MOD_ROWS, d), lambda l, j: (0, 0)),
            pl.BlockSpec((None, d, MOD_COL_TILE), lambda l, j: (l, 0, j)),
            pl.BlockSpec((None, 1, MOD_COL_TILE), lambda l, j: (l, 0, j)),
        ],
        out_specs=pl.BlockSpec((None, MOD_ROWS, MOD_COL_TILE), lambda l, j: (l, 0, j)),
        out_shape=jax.ShapeDtypeStruct((n_layers, MOD_ROWS, width), F32),
        compiler_params=pltpu.CompilerParams(
            dimension_semantics=("arbitrary", "arbitrary"), vmem_limit_bytes=VMEM_LIMIT),
        name="adaln_modulation",
    )(cc, w_ada, b_ada.reshape(n_layers, 1, width))


def _ffn_proj_kernel(*refs, n_src, n_tiles, tpb, n_lat):
    h_refs = refs[:n_src]
    (mod_ref, modp_ref, g1_ref, w1_ref, w3_ref, w2_ref, gmix_ref, win_ref,
     gcq_ref, wuq_ref, gmq_ref, gckv_ref, wuk_ref, wuv_ref, gmk_ref, gsq_ref, gsk_ref, dft_ref,
     cm_ref, sam_ref, sbm_ref, cs_ref, sas_ref, sbs_ref,
     ho_ref, q_ref, k_ref, v_ref, sq_ref, sk_ref, sv_ref, z_ref, hprev_ref) = refs[n_src:]
    step = pl.program_id(0)

    @pl.when(step == 0)
    def _():
        hprev_ref[...] = jnp.zeros_like(hprev_ref)

    modp = modp_ref[...]
    n = _norm_mod(hprev_ref[...], gmix_ref[...], modp[3:4], modp[4:5]).astype(BF16)
    u = jnp.dot(n, win_ref[...], preferred_element_type=F32)

    f = u[:, OFF_F:OFF_F + FOURIER_WIDTH].astype(BF16)
    z_ref[...] = jnp.dot(f, dft_ref[...], preferred_element_type=F32).astype(BF16)

    cm, sam, sbm = cm_ref[...], sam_ref[...], sbm_ref[...]
    cq = u[:, OFF_CQ:OFF_CQ + MLA_Q_RANK]
    cqn = (cq * _rms_scale(cq, MLA_Q_RANK) * gcq_ref[...]).astype(BF16)
    q = jnp.dot(cqn, wuq_ref[...], preferred_element_type=F32)
    gmq = gmq_ref[...]
    q_scale = MLA_QK_DIM ** -0.5 * LOG2E
    for hd in range(MLA_HEADS):
        qh = q[:, hd * LANES:(hd + 1) * LANES]
        qg = qh * (_rms_scale(qh, MLA_QK_DIM) * q_scale) * gmq
        q_ref[:, hd * LANES:(hd + 1) * LANES] = _rope(qg, cm, sam, sbm).astype(BF16)

    ckv = u[:, OFF_CKV:OFF_CKV + MLA_KV_RANK]
    ckvn = (ckv * _rms_scale(ckv, MLA_KV_RANK) * gckv_ref[...]).astype(BF16)
    kn = jnp.dot(ckvn, wuk_ref[...], preferred_element_type=F32)
    vv = jnp.dot(ckvn, wuv_ref[...], preferred_element_type=F32).astype(BF16)
    ones = jnp.ones((vv.shape[0], LANES), BF16)
    for pair in range(MLA_HEADS // 2):
        v_ref[:, 2 * pair * LANES:(2 * pair + 1) * LANES] = vv[:, pair * LANES:(pair + 1) * LANES]
        v_ref[:, (2 * pair + 1) * LANES:(2 * pair + 2) * LANES] = ones
    gmk = gmk_ref[...]
    kr = u[:, OFF_KR:OFF_KR + LANES]
    kr_ss = jnp.sum(kr * kr, axis=-1, keepdims=True)
    kr_rot = _rope(kr * gmk, cm, sam, sbm)
    for hd in range(MLA_HEADS):
        kh = kn[:, hd * LANES:(hd + 1) * LANES]
        ss = jnp.sum(kh * kh, axis=-1, keepdims=True) + kr_ss
        rs = lax.rsqrt(ss * (1.0 / MLA_QK_DIM) + EPS)
        k_ref[:, hd * LANES:(hd + 1) * LANES] = (rs * (kh * gmk + kr_rot)).astype(BF16)

    cs, sas, sbs = cs_ref[...], sas_ref[...], sbs_ref[...]

    def two_head_norm_rope(x, g, scale):
        low = _low_lanes(x.shape)
        x2 = x * x
        lo = jnp.sum(jnp.where(low, x2, 0.0), axis=-1, keepdims=True)
        hi = jnp.sum(jnp.where(low, 0.0, x2), axis=-1, keepdims=True)
        rs = jnp.where(low, lax.rsqrt(lo * (1.0 / SWA_HEAD_DIM) + EPS),
                       lax.rsqrt(hi * (1.0 / SWA_HEAD_DIM) + EPS))
        return _rope(x * (rs * scale) * g, cs, sas, sbs).astype(BF16)

    gsq = gsq_ref[...]
    s_scale = SWA_HEAD_DIM ** -0.5 * LOG2E
    sq_ref[:, 0:LANES] = two_head_norm_rope(u[:, OFF_SQA:OFF_SQA + LANES], gsq, s_scale)
    sq_ref[:, LANES:2 * LANES] = two_head_norm_rope(u[:, OFF_SQB:OFF_SQB + LANES], gsq, s_scale)
    sk_ref[...] = two_head_norm_rope(u[:, OFF_SK:OFF_SK + LANES], gsk_ref[...], 1.0)
    sv_ref[:, 0:LANES] = u[:, OFF_SV:OFF_SV + LANES].astype(BF16)
    sv_ref[:, LANES:2 * LANES] = ones

    mod = mod_ref[...]
    if n_src == 1:
        h = h_refs[0][...]
    else:
        cur = jnp.minimum(step, n_tiles - 1)
        h = jnp.where(cur % tpb >= n_lat, h_refs[1][...], h_refs[0][...])
    xn = _norm_mod(h, g1_ref[...], mod[0:1], mod[1:2]).astype(BF16)
    h = h + (0.5 * mod[2:3]) * _swiglu(xn, w1_ref, w3_ref, w2_ref)
    ho_ref[...] = h
    hprev_ref[...] = h


def _const_spec(block_shape, index_map):
    return pl.BlockSpec(block_shape, index_map, pipeline_mode=pl.Buffered(1))


def _ffn_proj(layer, tokens, mod, p, tabs, seq, t_all):
    n_batch, _, d = tokens[0].shape
    tm = TOK_TILE
    tpb = t_all // tm
    n_tiles = n_batch * tpb
    n_lat = seq // tm
    cur = lambda s: jnp.minimum(s, n_tiles - 1)
    prev = lambda s: jnp.maximum(s - 1, 0)
    tile = lambda which, width: pl.BlockSpec(
        (None, tm, width), lambda s: (which(s) // tpb, which(s) % tpb, 0))

    def token_specs(which):
        if len(tokens) == 1:
            return [tile(which, d)]
        lat = pl.BlockSpec((None, tm, d),
                           lambda s: (which(s) // tpb, jnp.minimum(which(s) % tpb, n_lat - 1), 0))
        con = pl.BlockSpec((None, tm, d),
                           lambda s: (which(s) // tpb, jnp.maximum(which(s) % tpb - n_lat, 0), 0))
        return [lat, con]

    def mod_row(which):
        def index(s):
            b, t = which(s) // tpb, which(s) % tpb
            return (layer, jnp.where(t >= n_lat, n_batch, b), 0, 0)
        return pl.BlockSpec((None, None, N_MOD, d), index)

    lw = lambda *shape: _const_spec((None,) + shape, lambda s: (layer,) + (0,) * len(shape))
    tab = pl.BlockSpec((tm, LANES), lambda s: (prev(s) % tpb, 0))
    in_specs = token_specs(cur) + [
        mod_row(cur), mod_row(prev),
        lw(1, d), lw(d, D_FF), lw(d, D_FF), lw(D_FF, d), lw(1, d), lw(d, IN_WIDTH_P),
        lw(1, MLA_Q_RANK), lw(MLA_Q_RANK, MLA_HEADS * LANES), lw(1, LANES),
        lw(1, MLA_KV_RANK), lw(MLA_KV_RANK, MLA_HEADS * LANES), lw(MLA_KV_RANK, MLA_HEADS * MLA_V),
        lw(1, LANES), lw(1, LANES), lw(1, LANES),
        _const_spec((FOURIER_WIDTH, 2 * FOURIER_WIDTH), lambda s: (0, 0)),
        tab, tab, tab, tab, tab, tab,
    ]
    widths = (d, MLA_HEADS * LANES, MLA_HEADS * LANES, MLA_V_SLABS * LANES, 2 * LANES, LANES, 2 * LANES,
              2 * FOURIER_WIDTH)
    dtypes = (F32,) + (BF16,) * 7
    return pl.pallas_call(
        functools.partial(_ffn_proj_kernel, n_src=len(tokens), n_tiles=n_tiles, tpb=tpb, n_lat=n_lat),
        grid=(n_tiles + 1,),
        in_specs=in_specs,
        out_specs=[tile(cur, d)] + [tile(prev, w) for w in widths[1:]],
        out_shape=[jax.ShapeDtypeStruct((n_batch, t_all, w), dt) for w, dt in zip(widths, dtypes)],
        scratch_shapes=[pltpu.VMEM((tm, d), F32)],
        compiler_params=pltpu.CompilerParams(
            dimension_semantics=("arbitrary",), vmem_limit_bytes=VMEM_LIMIT),
        name="ffn_half_and_projections",
    )(*tokens, mod, mod, p["g_ffn1"], p["w1_ffn1"], p["w3_ffn1"], p["w2_ffn1"], p["g_mix"], p["w_in"],
      p["g_cq"], p["w_uq"], p["g_mq"], p["g_ckv"], p["w_uk"], p["w_uv"], p["g_mk"],
      p["g_sq"], p["g_sk"], p["dft64"], *tabs)


def _fourier_kernel(z_ref, cl_ref, sl_ref, cc_ref, sc_ref, o_ref, *, ctx, with_ctx):
    w = FOURIER_WIDTH

    def mix(c_ref, s_ref, lo, n):
        zc = z_ref[lo:lo + n, 0:w]
        zs = z_ref[lo:lo + n, w:2 * w]
        return (jnp.dot(c_ref[...], zc, preferred_element_type=F32)
                - jnp.dot(s_ref[...], zs, preferred_element_type=F32)).astype(BF16)

    n_lat = cl_ref.shape[0]
    o_ref[0:n_lat, :] = mix(cl_ref, sl_ref, 0, n_lat)
    if with_ctx:
        o_ref[n_lat:n_lat + ctx, :] = mix(cc_ref, sc_ref, n_lat, ctx)


def _fourier(zcs, dfts, ctx, with_ctx):
    n_batch, t_all, _ = zcs.shape
    seq = t_all - ctx
    rows = t_all if with_ctx else seq
    full = lambda a: _const_spec(a.shape, lambda b: (0, 0))
    return pl.pallas_call(
        functools.partial(_fourier_kernel, ctx=ctx, with_ctx=with_ctx),
        grid=(n_batch,),
        in_specs=[pl.BlockSpec((None, t_all, 2 * FOURIER_WIDTH), lambda b: (b, 0, 0))]
        + [full(a) for a in dfts],
        out_specs=pl.BlockSpec((None, rows, FOURIER_WIDTH), lambda b: (b, 0, 0)),
        out_shape=jax.ShapeDtypeStruct((n_batch, rows, FOURIER_WIDTH), BF16),
        compiler_params=pltpu.CompilerParams(
            dimension_semantics=("arbitrary",), vmem_limit_bytes=VMEM_LIMIT),
        name="fourier_positions",
    )(zcs, *dfts)


def _scores(q, k):
    return lax.dot_general(q, k, (((1,), (1,)), ((), ())), preferred_element_type=F32)


def _mla_heads(q_ref, k_ref, v_ref, o_ref, head0_scores=None):
    n_heads = q_ref.shape[-1] // LANES
    for pair in range(n_heads // 2):
        outs = []
        vp = v_ref[:, 2 * pair * LANES:(2 * pair + 2) * LANES]
        for hd in (2 * pair, 2 * pair + 1):
            if hd == 0 and head0_scores is not None:
                s = head0_scores
            else:
                s = _scores(q_ref[:, hd * LANES:(hd + 1) * LANES], k_ref[:, hd * LANES:(hd + 1) * LANES])
            m = jnp.max(s, axis=-1, keepdims=True)
            p = jnp.exp2(s - m).astype(BF16)
            r = jnp.dot(p, vp, preferred_element_type=F32)
            outs.append(r[:, 0:LANES] / r[:, LANES:2 * LANES])
        o_ref[:, pair * LANES:(pair + 1) * LANES] = jnp.where(
            _low_lanes(outs[0].shape), outs[0], outs[1]).astype(BF16)


def _mla_latent_kernel(q_ref, k_ref, v_ref, qn_ref, kn_ref, o_ref, s0_ref):
    @pl.when(pl.program_id(0) == 0)
    def _():
        s0_ref[...] = _scores(q_ref[:, 0:LANES], k_ref[:, 0:LANES])

    _mla_heads(q_ref, k_ref, v_ref, o_ref, head0_scores=s0_ref[...])
    s0_ref[...] = _scores(qn_ref[...], kn_ref[...])


def _mla_ctx_kernel(q_ref, k_ref, v_ref, prev_ref, o_ref):
    del prev_ref
    _mla_heads(q_ref, k_ref, v_ref, o_ref)


def _mla(q, k, v, seq, with_ctx):
    n_batch, t_all, _ = q.shape
    n_ctx = t_all - seq
    tq = MLA_Q_TILE
    tpb = seq // tq
    n_tiles = n_batch * tpb
    qk_w = MLA_HEADS * LANES
    v_in = MLA_V_SLABS * LANES
    v_w = MLA_HEADS * MLA_V
    params = lambda n: pltpu.CompilerParams(
        dimension_semantics=("arbitrary",) * n, vmem_limit_bytes=VMEM_LIMIT)
    nxt = lambda s: jnp.minimum(s + 1, n_tiles - 1)
    a = pl.pallas_call(
        _mla_latent_kernel,
        grid=(n_tiles,),
        in_specs=[
            pl.BlockSpec((None, tq, qk_w), lambda s: (s // tpb, s % tpb, 0)),
            pl.BlockSpec((None, t_all, qk_w), lambda s: (s // tpb, 0, 0)),
            pl.BlockSpec((None, t_all, v_in), lambda s: (s // tpb, 0, 0)),
            pl.BlockSpec((None, tq, LANES), lambda s: (nxt(s) // tpb, nxt(s) % tpb, 0)),
            pl.BlockSpec((None, t_all, LANES), lambda s: (nxt(s) // tpb, 0, 0)),
        ],
        out_specs=pl.BlockSpec((None, tq, v_w), lambda s: (s // tpb, s % tpb, 0)),
        out_shape=jax.ShapeDtypeStruct((n_batch, t_all, v_w), BF16),
        scratch_shapes=[pltpu.VMEM((tq, t_all), F32)],
        compiler_params=params(1),
        name="mla_attention",
    )(q, k, v, q, k)
    if not with_ctx:
        return a
    c_blk = seq // n_ctx
    ctx_rows = lambda width: pl.BlockSpec((None, n_ctx, width), lambda b: (b, c_blk, 0))
    return pl.pallas_call(
        _mla_ctx_kernel,
        grid=(n_batch,),
        in_specs=[ctx_rows(qk_w), ctx_rows(qk_w), ctx_rows(v_in), pl.BlockSpec(memory_space=pl.ANY)],
        out_specs=ctx_rows(v_w),
        out_shape=jax.ShapeDtypeStruct(a.shape, a.dtype),
        input_output_aliases={3: 0},
        compiler_params=params(1),
        name="mla_context_attention",
    )(q, k, v, a)


def _swa_heads(sink_ref, layer, q_ref, keys, vals, valid, o_ref, r0=0, rows=None):
    rows = q_ref.shape[0] if rows is None else rows
    res = []
    for hq in range(SWA_Q_HEADS):
        slab = hq % 2
        use_low = hq < 2
        qs = q_ref[r0:r0 + rows, slab * LANES:(slab + 1) * LANES]
        low = _low_lanes(qs.shape)
        qm = jnp.where(low if use_low else jnp.logical_not(low), qs, jnp.zeros_like(qs))
        sink = sink_ref[layer, hq] * LOG2E
        s = _scores(qm, keys)
        if valid is not None:
            s = jnp.where(valid, s, NEG)
        m = jnp.maximum(jnp.max(s, axis=-1, keepdims=True), sink)
        p = jnp.exp2(s - m).astype(BF16)
        r = jnp.dot(p, vals, preferred_element_type=F32)
        res.append(r[:, 0:LANES] / (r[:, LANES:2 * LANES] + jnp.exp2(sink - m)))
    low = _low_lanes(res[0].shape)
    o_ref[r0:r0 + rows, 0:LANES] = jnp.where(low, res[0], res[2]).astype(BF16)
    o_ref[r0:r0 + rows, LANES:2 * LANES] = jnp.where(low, res[1], res[3]).astype(BF16)


def _swa_latent_kernel(sink_ref, q_ref, k_ref, v_ref, o_ref, *, layer, seq):
    tq = q_ref.shape[0]
    rows = min(tq, ROW_CHUNK)
    span = rows + 2 * SWA_WINDOW
    n_keys = span + k_ref.shape[0] - seq
    kc, vc = k_ref[seq:, :], v_ref[seq:, :]
    for r0 in range(0, tq, rows):
        first = pl.program_id(1) * tq + r0
        start = pl.multiple_of(jnp.clip(first - SWA_WINDOW, 0, seq - span), LANES)
        keys = jnp.concatenate([k_ref[pl.ds(start, span), :], kc], axis=0)
        vals = jnp.concatenate([v_ref[pl.ds(start, span), :], vc], axis=0)
        col = lax.broadcasted_iota(jnp.int32, (rows, n_keys), 1)
        dist = first + lax.broadcasted_iota(jnp.int32, (rows, n_keys), 0) - (start + col)
        valid = ((dist <= SWA_WINDOW) & (dist >= -SWA_WINDOW)) | (col >= span)
        _swa_heads(sink_ref, layer, q_ref, keys, vals, valid, o_ref, r0, rows)


def _swa_ctx_kernel(sink_ref, q_ref, k_ref, v_ref, prev_ref, o_ref, *, layer):
    del prev_ref
    _swa_heads(sink_ref, layer, q_ref, k_ref[...], v_ref[...], None, o_ref)


def _swa(layer, sink, q, k, v, seq, with_ctx):
    n_batch, t_all, _ = q.shape
    n_ctx = t_all - seq
    tq = Q_TILE
    params = lambda n: pltpu.CompilerParams(
        dimension_semantics=("arbitrary",) * n, vmem_limit_bytes=VMEM_LIMIT)
    smem = pl.BlockSpec(memory_space=pltpu.SMEM)
    w = pl.pallas_call(
        functools.partial(_swa_latent_kernel, layer=layer, seq=seq),
        grid=(n_batch, seq // tq),
        in_specs=[
            smem,
            pl.BlockSpec((None, tq, 2 * LANES), lambda b, i: (b, i, 0)),
            pl.BlockSpec((None, t_all, LANES), lambda b, i: (b, 0, 0)),
            pl.BlockSpec((None, t_all, 2 * LANES), lambda b, i: (b, 0, 0)),
        ],
        out_specs=pl.BlockSpec((None, tq, 2 * LANES), lambda b, i: (b, i, 0)),
        out_shape=jax.ShapeDtypeStruct((n_batch, t_all, 2 * LANES), BF16),
        compiler_params=params(2),
        name="window_attention",
    )(sink, q, k, v)
    if not with_ctx:
        return w
    c_blk = seq // n_ctx
    ctx_rows = lambda width: pl.BlockSpec((None, n_ctx, width), lambda b: (b, c_blk, 0))
    return pl.pallas_call(
        functools.partial(_swa_ctx_kernel, layer=layer),
        grid=(n_batch,),
        in_specs=[smem, ctx_rows(2 * LANES), ctx_rows(LANES), ctx_rows(2 * LANES),
                  pl.BlockSpec(memory_space=pl.ANY)],
        out_specs=ctx_rows(2 * LANES),
        out_shape=jax.ShapeDtypeStruct(w.shape, w.dtype),
        input_output_aliases={4: 0},
        compiler_params=params(1),
        name="window_context_attention",
    )(sink, q, k, v, w)


def _out_ffn_kernel(h_ref, yf_ref, a_ref, w_ref, moda_ref, modb_ref, wo_ref, g2_ref, w1_ref, w3_ref, w2_ref,
                    o_ref):
    half = TOK_TILE
    n_f = yf_ref.shape[-1]
    n_a = a_ref.shape[-1]
    mixed = (jnp.dot(yf_ref[...], wo_ref[0:n_f, :], preferred_element_type=F32)
             + jnp.dot(a_ref[...], wo_ref[n_f:n_f + n_a, :], preferred_element_type=F32)
             + jnp.dot(w_ref[...], wo_ref[n_f + n_a:, :], preferred_element_type=F32))
    g2 = g2_ref[...]
    hs, xs = [], []
    for i, mod_ref in enumerate((moda_ref, modb_ref)):
        mod = mod_ref[...]
        rows = slice(i * half, (i + 1) * half)
        h = h_ref[rows, :] + mod[5:6] * mixed[rows, :]
        hs.append(h)
        xs.append(_norm_mod(h, g2, mod[6:7], mod[7:8]).astype(BF16))
    y = _swiglu(jnp.concatenate(xs, axis=0), w1_ref, w3_ref, w2_ref)
    for i, mod_ref in enumerate((moda_ref, modb_ref)):
        rows = slice(i * half, (i + 1) * half)
        o_ref[rows, :] = hs[i] + (0.5 * mod_ref[8:9, :]) * y[rows, :]


def _out_ffn(layer, hh, yf, a, w, mod, p, seq, with_ctx):
    n_batch, t_all, d = hh.shape
    half, tm = TOK_TILE, OUT_TILE
    hpb = t_all // half
    n_lat = seq // half
    lw = lambda *shape: _const_spec((None,) + shape, lambda *_: (layer,) + (0,) * len(shape))

    def mod_row(half_index):
        def index(*g):
            j = half_index(*g)
            return (layer, jnp.where(j % hpb >= n_lat, n_batch, j // hpb), 0, 0)
        return pl.BlockSpec((None, None, N_MOD, d), index)

    if with_ctx:
        flat = lambda t: t.reshape(n_batch * t_all, t.shape[-1])
        arrays = [flat(t) for t in (hh, yf, a, w)]
        grid = (n_batch * t_all // tm,)
        rows = lambda width: pl.BlockSpec((tm, width), lambda j: (j, 0))
        mods = [mod_row(lambda j: 2 * j), mod_row(lambda j: 2 * j + 1)]
        out_shape = jax.ShapeDtypeStruct((n_batch * t_all, d), F32)
    else:
        arrays = [hh, yf, a, w]
        grid = (n_batch, seq // tm)
        rows = lambda width: pl.BlockSpec((None, tm, width), lambda b, t: (b, t, 0))
        mods = [mod_row(lambda b, t: b * hpb), mod_row(lambda b, t: b * hpb)]
        out_shape = jax.ShapeDtypeStruct((n_batch, seq, d), F32)
    out = pl.pallas_call(
        _out_ffn_kernel,
        grid=grid,
        in_specs=[rows(t.shape[-1]) for t in arrays] + mods
        + [lw(d, d), lw(1, d), lw(d, D_FF), lw(d, D_FF), lw(D_FF, d)],
        out_specs=rows(d),
        out_shape=out_shape,
        compiler_params=pltpu.CompilerParams(
            dimension_semantics=("arbitrary",) * len(grid), vmem_limit_bytes=VMEM_LIMIT),
        name="out_projection_and_ffn_half",
    )(*arrays, mod, mod, p["w_out"], p["g_ffn2"], p["w1_ffn2"], p["w3_ffn2"], p["w2_ffn2"])
    return out.reshape(n_batch, t_all, d) if with_ctx else out


def _prepare_params(g_ffn1, w1_ffn1, w3_ffn1, w2_ffn1, g_mix, w_in, g_cq, w_uq, g_ckv, w_ukv,
                    g_mla_q, g_mla_k, g_swa_q, g_swa_k, w_out, g_ffn2, w1_ffn2, w3_ffn2, w2_ffn2):
    row = lambda g: g[:, None, :]
    two_heads = lambda g: jnp.concatenate([g, g], axis=-1)
    kv = w_ukv.reshape(w_ukv.shape[:-1] + (MLA_HEADS, MLA_NOPE + MLA_V))
    flat = lambda t: t.reshape(t.shape[:-2] + (t.shape[-2] * t.shape[-1],))
    return {
        "g_ffn1": row(g_ffn1), "w1_ffn1": w1_ffn1.astype(BF16), "w3_ffn1": w3_ffn1.astype(BF16),
        "w2_ffn1": w2_ffn1.astype(BF16),
        "g_mix": row(g_mix), "w_in": _take_cols(w_in, _w_in_cols()).astype(BF16),
        "g_cq": row(g_cq), "w_uq": _pad_heads(w_uq, MLA_HEADS).astype(BF16),
        "g_mq": row(_pad_heads(g_mla_q, 1)),
        "g_ckv": row(g_ckv), "w_uk": _pad_heads(flat(kv[..., :MLA_NOPE]), MLA_HEADS).astype(BF16),
        "w_uv": flat(kv[..., MLA_NOPE:]).astype(BF16),
        "g_mk": row(_pad_heads(g_mla_k, 1)),
        "g_sq": row(two_heads(g_swa_q)),
        "g_sk": row(two_heads(g_swa_k)),
        "w_out": _take_cols(w_out, _w_out_rows(), axis=1).astype(BF16),
        "g_ffn2": row(g_ffn2), "w1_ffn2": w1_ffn2.astype(BF16), "w3_ffn2": w3_ffn2.astype(BF16),
        "w2_ffn2": w2_ffn2.astype(BF16),
        "dft64": _channel_dft(),
    }


def kernel(x, c, ctx, c_ctx, w_ada, b_ada, g_ffn1, w1_ffn1, w3_ffn1, w2_ffn1, g_mix, w_in, g_cq, w_uq,
           g_ckv, w_ukv, g_mla_q, g_mla_k, g_swa_q, g_swa_k, sink, w_out, g_ffn2, w1_ffn2, w3_ffn2,
           w2_ffn2):
    n_batch, seq, d = x.shape
    n_ctx = ctx.shape[1]
    depth = w_ada.shape[0]
    assert d == D_MODEL and seq % GRID_W == 0 and n_batch + 1 <= MOD_ROWS
    assert n_ctx % TOK_TILE == 0 and seq % TOK_TILE == 0 and seq % n_ctx == 0
    assert seq % OUT_TILE == 0 and (n_batch * (seq + n_ctx)) % OUT_TILE == 0
    assert seq % Q_TILE == 0 and seq % MLA_Q_TILE == 0
    assert Q_TILE % ROW_CHUNK == 0 and ROW_CHUNK + 2 * SWA_WINDOW <= seq
    assert w_ada.shape[-1] % MOD_COL_TILE == 0

    p = _prepare_params(g_ffn1, w1_ffn1, w3_ffn1, w2_ffn1, g_mix, w_in, g_cq, w_uq, g_ckv, w_ukv,
                        g_mla_q, g_mla_k, g_swa_q, g_swa_k, w_out, g_ffn2, w1_ffn2, w3_ffn2, w2_ffn2)
    tabs = _rope_tables(seq, n_ctx)
    c_lat, s_lat = _dft_cos_sin(seq, seq ** -0.5)
    c_ctx_dft, s_ctx_dft = _dft_cos_sin(n_ctx, n_ctx ** -0.5)
    dfts = tuple(m.astype(BF16) for m in (c_lat, s_lat, c_ctx_dft, s_ctx_dft))

    cc = jnp.concatenate([c, c_ctx[None, :], jnp.zeros((MOD_ROWS - n_batch - 1, d), F32)], axis=0)
    mod = _modulation(cc, w_ada, b_ada).reshape(depth, MOD_ROWS, N_MOD, d)

    tokens = (x, ctx)
    for layer in range(depth):
        with_ctx = layer != depth - 1
        hh, q, k, v, sq, sk, sv, zcs = _ffn_proj(layer, tokens, mod, p, tabs, seq, seq + n_ctx)
        yf = _fourier(zcs, dfts, n_ctx, with_ctx)
        a = _mla(q, k, v, seq, with_ctx)
        w = _swa(layer, sink, sq, sk, sv, seq, with_ctx)
        hh = _out_ffn(layer, hh, yf, a, w, mod, p, seq, with_ctx)
        tokens = (hh,)
    return hh
```

```python
import functools

import numpy as np
import jax
import jax.numpy as jnp
from jax import lax
from jax.experimental import pallas as pl
from jax.experimental.pallas import tpu as pltpu

F32 = jnp.float32
BF16 = jnp.bfloat16

D_MODEL = 1024
GRID_W = 64
ROPE_BASE = 10000.0
EPS = 1e-6
NEG = -1e30
LOG2E = 1.4426950408889634
N_MOD = 9
D_FF = 2816

FOURIER_WIDTH = 256
FOURIER_GROUP_DIM = 64
MLA_HEADS = 8
MLA_NOPE = 64
MLA_ROPE = 32
MLA_V = 64
MLA_QK_DIM = MLA_NOPE + MLA_ROPE
MLA_Q_RANK = 256
MLA_KV_RANK = 128
SWA_Q_HEADS = 4
SWA_KV_HEADS = 2
SWA_HEAD_DIM = 64
SWA_WINDOW = 128
IN_SPLITS = (256, 256, 128, 32, 256, 128, 128)
IN_WIDTH = sum(IN_SPLITS)

LANES = 128
HALF = LANES // 2
IN_WIDTH_P = 10 * LANES
MLA_V_SLABS = MLA_HEADS
VMEM_LIMIT = 56 * 1024 * 1024

TOK_TILE = 256
OUT_PARTS = 2
OUT_PARTS_LATENT = 2
Q_TILE = 1024
MLA_Q_TILE = 512
ROW_CHUNK = 256
MOD_ROWS = 24
MOD_COL_TILE = 2304

OFF_F, OFF_CQ, OFF_CKV, OFF_KR, OFF_SQA, OFF_SQB, OFF_SK, OFF_SV = (
    0, 256, 512, 640, 768, 896, 1024, 1152)


def _w_in_cols():
    zero = IN_WIDTH
    o_f, o_cq, o_ckv, o_kr, o_sq, o_sk, o_sv = np.cumsum((0,) + IN_SPLITS)[:-1]
    cols = np.full((IN_WIDTH_P,), zero, np.int32)
    cols[OFF_F:OFF_F + 256] = o_f + np.arange(256)
    cols[OFF_CQ:OFF_CQ + 256] = o_cq + np.arange(256)
    cols[OFF_CKV:OFF_CKV + 128] = o_ckv + np.arange(128)
    cols[OFF_KR + MLA_NOPE:OFF_KR + MLA_QK_DIM] = o_kr + np.arange(MLA_ROPE)
    head = np.arange(SWA_HEAD_DIM)
    cols[OFF_SQA:OFF_SQA + 64] = o_sq + 0 * 64 + head
    cols[OFF_SQA + 64:OFF_SQA + 128] = o_sq + 2 * 64 + head
    cols[OFF_SQB:OFF_SQB + 64] = o_sq + 1 * 64 + head
    cols[OFF_SQB + 64:OFF_SQB + 128] = o_sq + 3 * 64 + head
    cols[OFF_SK:OFF_SK + 64] = o_sk + head
    cols[OFF_SK + 64:OFF_SK + 128] = o_sk + 64 + head
    cols[OFF_SV:OFF_SV + 128] = o_sv + np.arange(128)
    return cols


def _pad_heads(w, n_heads):
    lead, width = w.shape[:-1], w.shape[-1] // n_heads
    w = jnp.pad(w.reshape(lead + (n_heads, width)), [(0, 0)] * (len(lead) + 1) + [(0, LANES - width)])
    return w.reshape(lead + (n_heads * LANES,))


def _w_out_rows():
    base = FOURIER_WIDTH + MLA_HEADS * MLA_V
    swa = np.concatenate([base + h * SWA_HEAD_DIM + np.arange(SWA_HEAD_DIM) for h in (0, 2, 1, 3)])
    return np.concatenate([np.arange(base), swa]).astype(np.int32)


def _take_cols(w, cols, axis=-1):
    axis = axis % w.ndim
    n = w.shape[axis]
    cols = [int(c) for c in cols]
    pieces, i = [], 0
    while i < len(cols):
        j = i + 1
        if cols[i] == n:
            while j < len(cols) and cols[j] == n:
                j += 1
            shape = w.shape[:axis] + (j - i,) + w.shape[axis + 1:]
            pieces.append(jnp.zeros(shape, w.dtype))
        else:
            stride = cols[j] - cols[i] if j < len(cols) and cols[j] - cols[i] in (1, 2) else 1
            while j < len(cols) and cols[j] != n and cols[j] == cols[j - 1] + stride:
                j += 1
            pieces.append(lax.slice_in_dim(w, cols[i], cols[j - 1] + 1, stride, axis))
        i = j
    return jnp.concatenate(pieces, axis=axis)


def _rope_tables(seq, ctx):
    rows = seq // GRID_W
    pad = jnp.zeros((ctx,), F32)
    row = jnp.concatenate([jnp.repeat(jnp.arange(rows, dtype=F32), GRID_W), pad])[:, None]
    col = jnp.concatenate([jnp.tile(jnp.arange(GRID_W, dtype=F32), rows), pad])[:, None]

    def build(dim, section_starts):
        axis_dim = dim // 2
        n_freq = axis_dim // 2
        freq = np.zeros((LANES,), np.float32)
        active = np.zeros((LANES,), bool)
        by_row = np.zeros((LANES,), bool)
        m_a = np.zeros((LANES,), np.float32)
        m_b = np.zeros((LANES,), np.float32)
        pair = np.arange(dim) // 2
        for lo in section_starts:
            freq[lo:lo + dim] = 2 * (pair % n_freq)
            active[lo:lo + dim] = True
            by_row[lo:lo + dim] = pair < n_freq
            m_a[lo:lo + dim:2] = 1.0
            m_b[lo + 1:lo + dim:2] = 1.0
        inv_lane = jnp.where(jnp.asarray(active), ROPE_BASE ** (-jnp.asarray(freq) / axis_dim), 0.0)[None, :]
        ang = jnp.where(jnp.asarray(by_row)[None, :], row * inv_lane, col * inv_lane)
        sin = jnp.sin(ang)
        return jnp.cos(ang), sin * jnp.asarray(-m_a)[None, :], sin * jnp.asarray(m_b)[None, :]

    return build(MLA_ROPE, (MLA_NOPE,)) + build(SWA_HEAD_DIM, (0, HALF))


def _dft_cos_sin(n, scale):
    def direct(rows_j, n_mod):
        k = jnp.arange(n, dtype=jnp.int32)
        ang = ((rows_j[:, None] * k[None, :]) % n_mod).astype(F32) * (2.0 * np.pi / n_mod)
        return jnp.cos(ang), jnp.sin(ang)

    inner = FOURIER_GROUP_DIM
    if n <= inner or n % inner:
        c, s = direct(jnp.arange(n, dtype=jnp.int32), n)
        return c * scale, s * scale
    outer = n // inner
    ca, sa = direct(jnp.arange(outer, dtype=jnp.int32), outer)
    cb, sb = direct(jnp.arange(inner, dtype=jnp.int32), n)
    cb, sb = cb * scale, sb * scale
    c = ca[:, None, :] * cb[None, :, :] - sa[:, None, :] * sb[None, :, :]
    s = sa[:, None, :] * cb[None, :, :] + ca[:, None, :] * sb[None, :, :]
    return c.reshape(n, n), s.reshape(n, n)


def _channel_dft():
    c, s = _dft_cos_sin(FOURIER_GROUP_DIM, FOURIER_GROUP_DIM ** -0.5)
    eye = jnp.eye(FOURIER_WIDTH // FOURIER_GROUP_DIM, dtype=F32)
    return jnp.concatenate([jnp.kron(eye, c), jnp.kron(eye, s)], axis=1).astype(BF16)


def _rms_scale(x, width):
    return lax.rsqrt(jnp.sum(x * x, axis=-1, keepdims=True) * (1.0 / width) + EPS)


def _norm_mod(x, g, shift, scale):
    y = x * _rms_scale(x, x.shape[-1]) * g
    return y * (1.0 + scale) + shift


def _swiglu(xn, w1_ref, w3_ref, w2_ref):
    a = jnp.dot(xn, w1_ref[...], preferred_element_type=F32)
    b = jnp.dot(xn, w3_ref[...], preferred_element_type=F32)
    g = (a / (1.0 + jnp.exp(-a))) * b
    return jnp.dot(g.astype(BF16), w2_ref[...], preferred_element_type=F32)


def _rope(x, cos, sin_a, sin_b):
    return x * cos + pltpu.roll(x, LANES - 1, 1) * sin_a + pltpu.roll(x, 1, 1) * sin_b


def _low_lanes(shape):
    return lax.broadcasted_iota(jnp.int32, shape, len(shape) - 1) < HALF


def _mod_kernel(c_ref, w_ref, b_ref, o_ref):
    cv = c_ref[...]
    s = (cv / (1.0 + jnp.exp(-cv))).astype(BF16)
    o_ref[...] = jnp.dot(s, w_ref[...].astype(BF16), preferred_element_type=F32) + b_ref[...]


def _modulation(cc, w_ada, b_ada):
    n_layers, d, width = w_ada.shape
    return pl.pallas_call(
        _mod_kernel,
        grid=(n_layers, width // MOD_COL_TILE),
        in_specs=[
            pl.BlockSpec((MOD_ROWS, d), lambda l, j: (0, 0)),
            pl.BlockSpec((None, d, MOD_COL_TILE), lambda l, j: (l, 0, j)),
            pl.BlockSpec((None, 1, MOD_COL_TILE), lambda l, j: (l, 0, j)),
        ],
        out_specs=pl.BlockSpec((None, MOD_ROWS, MOD_COL_TILE), lambda l, j: (l, 0, j)),
        out_shape=jax.ShapeDtypeStruct((n_layers, MOD_ROWS, width), F32),
        compiler_params=pltpu.CompilerParams(
            dimension_semantics=("arbitrary", "arbitrary"), vmem_limit_bytes=VMEM_LIMIT),
        name="adaln_modulation",
    )(cc, w_ada, b_ada.reshape(n_layers, 1, width))


def _ffn_proj_kernel(*refs, n_src, n_tiles, tpb, n_lat):
    h_refs = refs[:n_src]
    (mod_ref, modp_ref, g1_ref, w1_ref, w3_ref, w2_ref, gmix_ref, win_ref,
     gcq_ref, wuq_ref, gmq_ref, gckv_ref, wuk_ref, wuv_ref, gmk_ref, gsq_ref, gsk_ref, dft_ref,
     cm_ref, sam_ref, sbm_ref, cs_ref, sas_ref, sbs_ref,
     ho_ref, q_ref, k_ref, v_ref, sq_ref, sk_ref, sv_ref, z_ref, hprev_ref) = refs[n_src:]
    step = pl.program_id(0)

    @pl.when(step == 0)
    def _():
        hprev_ref[...] = jnp.zeros_like(hprev_ref)

    modp = modp_ref[...]
    n = _norm_mod(hprev_ref[...], gmix_ref[...], modp[3:4], modp[4:5]).astype(BF16)
    u = jnp.dot(n, win_ref[...], preferred_element_type=F32)

    f = u[:, OFF_F:OFF_F + FOURIER_WIDTH].astype(BF16)
    z_ref[...] = jnp.dot(f, dft_ref[...], preferred_element_type=F32).astype(BF16)

    cm, sam, sbm = cm_ref[...], sam_ref[...], sbm_ref[...]
    cq = u[:, OFF_CQ:OFF_CQ + MLA_Q_RANK]
    cqn = (cq * _rms_scale(cq, MLA_Q_RANK) * gcq_ref[...]).astype(BF16)
    q = jnp.dot(cqn, wuq_ref[...], preferred_element_type=F32)
    gmq = gmq_ref[...]
    q_scale = MLA_QK_DIM ** -0.5 * LOG2E
    for hd in range(MLA_HEADS):
        qh = q[:, hd * LANES:(hd + 1) * LANES]
        qg = qh * (_rms_scale(qh, MLA_QK_DIM) * q_scale) * gmq
        q_ref[:, hd * LANES:(hd + 1) * LANES] = _rope(qg, cm, sam, sbm).astype(BF16)

    ckv = u[:, OFF_CKV:OFF_CKV + MLA_KV_RANK]
    ckvn = (ckv * _rms_scale(ckv, MLA_KV_RANK) * gckv_ref[...]).astype(BF16)
    kn = jnp.dot(ckvn, wuk_ref[...], preferred_element_type=F32)
    vv = jnp.dot(ckvn, wuv_ref[...], preferred_element_type=F32).astype(BF16)
    ones = jnp.ones((vv.shape[0], LANES), BF16)
    for pair in range(MLA_HEADS // 2):
        v_ref[:, 2 * pair * LANES:(2 * pair + 1) * LANES] = vv[:, pair * LANES:(pair + 1) * LANES]
        v_ref[:, (2 * pair + 1) * LANES:(2 * pair + 2) * LANES] = ones
    gmk = gmk_ref[...]
    kr = u[:, OFF_KR:OFF_KR + LANES]
    kr_ss = jnp.sum(kr * kr, axis=-1, keepdims=True)
    kr_rot = _rope(kr * gmk, cm, sam, sbm)
    for hd in range(MLA_HEADS):
        kh = kn[:, hd * LANES:(hd + 1) * LANES]
        ss = jnp.sum(kh * kh, axis=-1, keepdims=True) + kr_ss
        rs = lax.rsqrt(ss * (1.0 / MLA_QK_DIM) + EPS)
        k_ref[:, hd * LANES:(hd + 1) * LANES] = (rs * (kh * gmk + kr_rot)).astype(BF16)

    cs, sas, sbs = cs_ref[...], sas_ref[...], sbs_ref[...]

    def two_head_norm_rope(x, g, scale):
        low = _low_lanes(x.shape)
        x2 = x * x
        lo = jnp.sum(jnp.where(low, x2, 0.0), axis=-1, keepdims=True)
        hi = jnp.sum(jnp.where(low, 0.0, x2), axis=-1, keepdims=True)
        rs = jnp.where(low, lax.rsqrt(lo * (1.0 / SWA_HEAD_DIM) + EPS),
                       lax.rsqrt(hi * (1.0 / SWA_HEAD_DIM) + EPS))
        return _rope(x * (rs * scale) * g, cs, sas, sbs).astype(BF16)

    gsq = gsq_ref[...]
    s_scale = SWA_HEAD_DIM ** -0.5 * LOG2E
    sq_ref[:, 0:LANES] = two_head_norm_rope(u[:, OFF_SQA:OFF_SQA + LANES], gsq, s_scale)
    sq_ref[:, LANES:2 * LANES] = two_head_norm_rope(u[:, OFF_SQB:OFF_SQB + LANES], gsq, s_scale)
    sk_ref[...] = two_head_norm_rope(u[:, OFF_SK:OFF_SK + LANES], gsk_ref[...], 1.0)
    sv_ref[:, 0:LANES] = u[:, OFF_SV:OFF_SV + LANES].astype(BF16)
    sv_ref[:, LANES:2 * LANES] = ones

    mod = mod_ref[...]
    if n_src == 1:
        h = h_refs[0][...]
    else:
        cur = jnp.minimum(step, n_tiles - 1)
        h = jnp.where(cur % tpb >= n_lat, h_refs[1][...], h_refs[0][...])
    xn = _norm_mod(h, g1_ref[...], mod[0:1], mod[1:2]).astype(BF16)
    h = h + (0.5 * mod[2:3]) * _swiglu(xn, w1_ref, w3_ref, w2_ref)
    ho_ref[...] = h
    hprev_ref[...] = h


def _const_spec(block_shape, index_map):
    return pl.BlockSpec(block_shape, index_map, pipeline_mode=pl.Buffered(1))


def _ffn_proj(layer, tokens, mod, p, tabs, seq, t_all):
    n_batch, _, d = tokens[0].shape
    tm = TOK_TILE
    tpb = t_all // tm
    n_tiles = n_batch * tpb
    n_lat = seq // tm
    cur = lambda s: jnp.minimum(s, n_tiles - 1)
    prev = lambda s: jnp.maximum(s - 1, 0)
    tile = lambda which, width: pl.BlockSpec(
        (None, tm, width), lambda s: (which(s) // tpb, which(s) % tpb, 0))

    def token_specs(which):
        if len(tokens) == 1:
            return [tile(which, d)]
        lat = pl.BlockSpec((None, tm, d),
                           lambda s: (which(s) // tpb, jnp.minimum(which(s) % tpb, n_lat - 1), 0))
        con = pl.BlockSpec((None, tm, d),
                           lambda s: (which(s) // tpb, jnp.maximum(which(s) % tpb - n_lat, 0), 0))
        return [lat, con]

    def mod_row(which):
        def index(s):
            b, t = which(s) // tpb, which(s) % tpb
            return (layer, jnp.where(t >= n_lat, n_batch, b), 0, 0)
        return pl.BlockSpec((None, None, N_MOD, d), index)

    lw = lambda *shape: _const_spec((None,) + shape, lambda s: (layer,) + (0,) * len(shape))
    tab = pl.BlockSpec((tm, LANES), lambda s: (prev(s) % tpb, 0))
    in_specs = token_specs(cur) + [
        mod_row(cur), mod_row(prev),
        lw(1, d), lw(d, D_FF), lw(d, D_FF), lw(D_FF, d), lw(1, d), lw(d, IN_WIDTH_P),
        lw(1, MLA_Q_RANK), lw(MLA_Q_RANK, MLA_HEADS * LANES), lw(1, LANES),
        lw(1, MLA_KV_RANK), lw(MLA_KV_RANK, MLA_HEADS * LANES), lw(MLA_KV_RANK, MLA_HEADS * MLA_V),
        lw(1, LANES), lw(1, LANES), lw(1, LANES),
        _const_spec((FOURIER_WIDTH, 2 * FOURIER_WIDTH), lambda s: (0, 0)),
        tab, tab, tab, tab, tab, tab,
    ]
    widths = (d, MLA_HEADS * LANES, MLA_HEADS * LANES, MLA_V_SLABS * LANES, 2 * LANES, LANES, 2 * LANES,
              2 * FOURIER_WIDTH)
    dtypes = (F32,) + (BF16,) * 7
    return pl.pallas_call(
        functools.partial(_ffn_proj_kernel, n_src=len(tokens), n_tiles=n_tiles, tpb=tpb, n_lat=n_lat),
        grid=(n_tiles + 1,),
        in_specs=in_specs,
        out_specs=[tile(cur, d)] + [tile(prev, w) for w in widths[1:]],
        out_shape=[jax.ShapeDtypeStruct((n_batch, t_all, w), dt) for w, dt in zip(widths, dtypes)],
        scratch_shapes=[pltpu.VMEM((tm, d), F32)],
        compiler_params=pltpu.CompilerParams(
            dimension_semantics=("arbitrary",), vmem_limit_bytes=VMEM_LIMIT),
        name="ffn_half_and_projections",
    )(*tokens, mod, mod, p["g_ffn1"], p["w1_ffn1"], p["w3_ffn1"], p["w2_ffn1"], p["g_mix"], p["w_in"],
      p["g_cq"], p["w_uq"], p["g_mq"], p["g_ckv"], p["w_uk"], p["w_uv"], p["g_mk"],
      p["g_sq"], p["g_sk"], p["dft64"], *tabs)


def _fourier_kernel(z_ref, cl_ref, sl_ref, cc_ref, sc_ref, o_ref, *, ctx, with_ctx):
    w = FOURIER_WIDTH

    def mix(c_ref, s_ref, lo, n):
        zc = z_ref[lo:lo + n, 0:w]
        zs = z_ref[lo:lo + n, w:2 * w]
        return (jnp.dot(c_ref[...], zc, preferred_element_type=F32)
                - jnp.dot(s_ref[...], zs, preferred_element_type=F32)).astype(BF16)

    n_lat = cl_ref.shape[0]
    o_ref[0:n_lat, :] = mix(cl_ref, sl_ref, 0, n_lat)
    if with_ctx:
        o_ref[n_lat:n_lat + ctx, :] = mix(cc_ref, sc_ref, n_lat, ctx)


def _fourier(zcs, dfts, ctx, with_ctx):
    n_batch, t_all, _ = zcs.shape
    seq = t_all - ctx
    rows = t_all if with_ctx else seq
    full = lambda a: _const_spec(a.shape, lambda b: (0, 0))
    return pl.pallas_call(
        functools.partial(_fourier_kernel, ctx=ctx, with_ctx=with_ctx),
        grid=(n_batch,),
        in_specs=[pl.BlockSpec((None, t_all, 2 * FOURIER_WIDTH), lambda b: (b, 0, 0))]
        + [full(a) for a in dfts],
        out_specs=pl.BlockSpec((None, rows, FOURIER_WIDTH), lambda b: (b, 0, 0)),
        out_shape=jax.ShapeDtypeStruct((n_batch, rows, FOURIER_WIDTH), BF16),
        compiler_params=pltpu.CompilerParams(
            dimension_semantics=("arbitrary",), vmem_limit_bytes=VMEM_LIMIT),
        name="fourier_positions",
    )(zcs, *dfts)


def _scores(q, k):
    return lax.dot_general(q, k, (((1,), (1,)), ((), ())), preferred_element_type=F32)


def _mla_heads(q_ref, k_ref, v_ref, o_ref, head0_scores=None):
    n_heads = q_ref.shape[-1] // LANES
    for pair in range(n_heads // 2):
        outs = []
        vp = v_ref[:, 2 * pair * LANES:(2 * pair + 2) * LANES]
        for hd in (2 * pair, 2 * pair + 1):
            if hd == 0 and head0_scores is not None:
                s = head0_scores
            else:
                s = _scores(q_ref[:, hd * LANES:(hd + 1) * LANES], k_ref[:, hd * LANES:(hd + 1) * LANES])
            m = jnp.max(s, axis=-1, keepdims=True)
            p = jnp.exp2(s - m).astype(BF16)
            r = jnp.dot(p, vp, preferred_element_type=F32)
            outs.append(r[:, 0:LANES] / r[:, LANES:2 * LANES])
        o_ref[:, pair * LANES:(pair + 1) * LANES] = jnp.where(
            _low_lanes(outs[0].shape), outs[0], outs[1]).astype(BF16)


def _mla_latent_kernel(q_ref, k_ref, v_ref, qn_ref, kn_ref, o_ref, s0_ref):
    @pl.when(pl.program_id(0) == 0)
    def _():
        s0_ref[...] = _scores(q_ref[:, 0:LANES], k_ref[:, 0:LANES])

    _mla_heads(q_ref, k_ref, v_ref, o_ref, head0_scores=s0_ref[...])
    s0_ref[...] = _scores(qn_ref[...], kn_ref[...])


def _mla_ctx_kernel(q_ref, k_ref, v_ref, prev_ref, o_ref):
    del prev_ref
    _mla_heads(q_ref, k_ref, v_ref, o_ref)


def _mla(q, k, v, seq, with_ctx):
    n_batch, t_all, _ = q.shape
    n_ctx = t_all - seq
    tq = MLA_Q_TILE
    tpb = seq // tq
    n_tiles = n_batch * tpb
    qk_w = MLA_HEADS * LANES
    v_in = MLA_V_SLABS * LANES
    v_w = MLA_HEADS * MLA_V
    params = lambda n: pltpu.CompilerParams(
        dimension_semantics=("arbitrary",) * n, vmem_limit_bytes=VMEM_LIMIT)
    nxt = lambda s: jnp.minimum(s + 1, n_tiles - 1)
    a = pl.pallas_call(
        _mla_latent_kernel,
        grid=(n_tiles,),
        in_specs=[
            pl.BlockSpec((None, tq, qk_w), lambda s: (s // tpb, s % tpb, 0)),
            pl.BlockSpec((None, t_all, qk_w), lambda s: (s // tpb, 0, 0)),
            pl.BlockSpec((None, t_all, v_in), lambda s: (s // tpb, 0, 0)),
            pl.BlockSpec((None, tq, LANES), lambda s: (nxt(s) // tpb, nxt(s) % tpb, 0)),
            pl.BlockSpec((None, t_all, LANES), lambda s: (nxt(s) // tpb, 0, 0)),
        ],
        out_specs=pl.BlockSpec((None, tq, v_w), lambda s: (s // tpb, s % tpb, 0)),
        out_shape=jax.ShapeDtypeStruct((n_batch, t_all, v_w), BF16),
        scratch_shapes=[pltpu.VMEM((tq, t_all), F32)],
        compiler_params=params(1),
        name="mla_attention",
    )(q, k, v, q, k)
    if not with_ctx:
        return a
    c_blk = seq // n_ctx
    ctx_rows = lambda width: pl.BlockSpec((None, n_ctx, width), lambda b: (b, c_blk, 0))
    return pl.pallas_call(
        _mla_ctx_kernel,
        grid=(n_batch,),
        in_specs=[ctx_rows(qk_w), ctx_rows(qk_w), ctx_rows(v_in), pl.BlockSpec(memory_space=pl.ANY)],
        out_specs=ctx_rows(v_w),
        out_shape=jax.ShapeDtypeStruct(a.shape, a.dtype),
        input_output_aliases={3: 0},
        compiler_params=params(1),
        name="mla_context_attention",
    )(q, k, v, a)


def _swa_heads(sink_ref, layer, q_ref, keys, vals, valid, o_ref, r0=0, rows=None):
    rows = q_ref.shape[0] if rows is None else rows
    res = []
    for hq in range(SWA_Q_HEADS):
        slab = hq % 2
        use_low = hq < 2
        qs = q_ref[r0:r0 + rows, slab * LANES:(slab + 1) * LANES]
        low = _low_lanes(qs.shape)
        qm = jnp.where(low if use_low else jnp.logical_not(low), qs, jnp.zeros_like(qs))
        sink = sink_ref[layer, hq] * LOG2E
        s = _scores(qm, keys)
        if valid is not None:
            s = jnp.where(valid, s, NEG)
        m = jnp.maximum(jnp.max(s, axis=-1, keepdims=True), sink)
        p = jnp.exp2(s - m).astype(BF16)
        r = jnp.dot(p, vals, preferred_element_type=F32)
        res.append(r[:, 0:LANES] / (r[:, LANES:2 * LANES] + jnp.exp2(sink - m)))
    low = _low_lanes(res[0].shape)
    o_ref[r0:r0 + rows, 0:LANES] = jnp.where(low, res[0], res[2]).astype(BF16)
    o_ref[r0:r0 + rows, LANES:2 * LANES] = jnp.where(low, res[1], res[3]).astype(BF16)


def _swa_latent_kernel(sink_ref, q_ref, k_ref, v_ref, o_ref, *, layer, seq):
    tq = q_ref.shape[0]
    rows = min(tq, ROW_CHUNK)
    span = rows + 2 * SWA_WINDOW
    n_keys = span + k_ref.shape[0] - seq
    kc, vc = k_ref[seq:, :], v_ref[seq:, :]
    for r0 in range(0, tq, rows):
        first = pl.program_id(1) * tq + r0
        start = pl.multiple_of(jnp.clip(first - SWA_WINDOW, 0, seq - span), LANES)
        keys = jnp.concatenate([k_ref[pl.ds(start, span), :], kc], axis=0)
        vals = jnp.concatenate([v_ref[pl.ds(start, span), :], vc], axis=0)
        col = lax.broadcasted_iota(jnp.int32, (rows, n_keys), 1)
        dist = first + lax.broadcasted_iota(jnp.int32, (rows, n_keys), 0) - (start + col)
        valid = ((dist <= SWA_WINDOW) & (dist >= -SWA_WINDOW)) | (col >= span)
        _swa_heads(sink_ref, layer, q_ref, keys, vals, valid, o_ref, r0, rows)


def _swa_ctx_kernel(sink_ref, q_ref, k_ref, v_ref, prev_ref, o_ref, *, layer):
    del prev_ref
    _swa_heads(sink_ref, layer, q_ref, k_ref[...], v_ref[...], None, o_ref)


def _swa(layer, sink, q, k, v, seq, with_ctx):
    n_batch, t_all, _ = q.shape
    n_ctx = t_all - seq
    tq = Q_TILE
    params = lambda n: pltpu.CompilerParams(
        dimension_semantics=("arbitrary",) * n, vmem_limit_bytes=VMEM_LIMIT)
    smem = pl.BlockSpec(memory_space=pltpu.SMEM)
    w = pl.pallas_call(
        functools.partial(_swa_latent_kernel, layer=layer, seq=seq),
        grid=(n_batch, seq // tq),
        in_specs=[
            smem,
            pl.BlockSpec((None, tq, 2 * LANES), lambda b, i: (b, i, 0)),
            pl.BlockSpec((None, t_all, LANES), lambda b, i: (b, 0, 0)),
            pl.BlockSpec((None, t_all, 2 * LANES), lambda b, i: (b, 0, 0)),
        ],
        out_specs=pl.BlockSpec((None, tq, 2 * LANES), lambda b, i: (b, i, 0)),
        out_shape=jax.ShapeDtypeStruct((n_batch, t_all, 2 * LANES), BF16),
        compiler_params=params(2),
        name="window_attention",
    )(sink, q, k, v)
    if not with_ctx:
        return w
    c_blk = seq // n_ctx
    ctx_rows = lambda width: pl.BlockSpec((None, n_ctx, width), lambda b: (b, c_blk, 0))
    return pl.pallas_call(
        functools.partial(_swa_ctx_kernel, layer=layer),
        grid=(n_batch,),
        in_specs=[smem, ctx_rows(2 * LANES), ctx_rows(LANES), ctx_rows(2 * LANES),
                  pl.BlockSpec(memory_space=pl.ANY)],
        out_specs=ctx_rows(2 * LANES),
        out_shape=jax.ShapeDtypeStruct(w.shape, w.dtype),
        input_output_aliases={4: 0},
        compiler_params=params(1),
        name="window_context_attention",
    )(sink, q, k, v, w)


def _out_ffn_kernel(h_ref, yf_ref, a_ref, w_ref, *refs):
    mod_refs = refs[:-6]
    wo_ref, g2_ref, w1_ref, w3_ref, w2_ref, o_ref = refs[-6:]
    half = TOK_TILE
    n_f = yf_ref.shape[-1]
    n_a = a_ref.shape[-1]
    mixed = (jnp.dot(yf_ref[...], wo_ref[0:n_f, :], preferred_element_type=F32)
             + jnp.dot(a_ref[...], wo_ref[n_f:n_f + n_a, :], preferred_element_type=F32)
             + jnp.dot(w_ref[...], wo_ref[n_f + n_a:, :], preferred_element_type=F32))
    g2 = g2_ref[...]
    hs, xs = [], []
    for i, mod_ref in enumerate(mod_refs):
        mod = mod_ref[...]
        rows = slice(i * half, (i + 1) * half)
        h = h_ref[rows, :] + mod[5:6] * mixed[rows, :]
        hs.append(h)
        xs.append(_norm_mod(h, g2, mod[6:7], mod[7:8]).astype(BF16))
    y = _swiglu(jnp.concatenate(xs, axis=0), w1_ref, w3_ref, w2_ref)
    for i, mod_ref in enumerate(mod_refs):
        rows = slice(i * half, (i + 1) * half)
        o_ref[rows, :] = hs[i] + (0.5 * mod_ref[8:9, :]) * y[rows, :]


def _out_ffn(layer, hh, yf, a, w, mod, p, seq, with_ctx):
    n_batch, t_all, d = hh.shape
    half = TOK_TILE
    parts = OUT_PARTS if with_ctx else OUT_PARTS_LATENT
    tm = parts * half
    hpb = t_all // half
    n_lat = seq // half
    lw = lambda *shape: _const_spec((None,) + shape, lambda *_: (layer,) + (0,) * len(shape))

    def mod_row(half_index):
        def index(*g):
            j = half_index(*g)
            return (layer, jnp.where(j % hpb >= n_lat, n_batch, j // hpb), 0, 0)
        return pl.BlockSpec((None, None, N_MOD, d), index)

    if with_ctx:
        flat = lambda t: t.reshape(n_batch * t_all, t.shape[-1])
        arrays = [flat(t) for t in (hh, yf, a, w)]
        grid = (n_batch * t_all // tm,)
        rows = lambda width: pl.BlockSpec((tm, width), lambda j: (j, 0))
        mods = [mod_row(lambda j, i=i: parts * j + i) for i in range(parts)]
        out_shape = jax.ShapeDtypeStruct((n_batch * t_all, d), F32)
    else:
        arrays = [hh, yf, a, w]
        grid = (n_batch, seq // tm)
        rows = lambda width: pl.BlockSpec((None, tm, width), lambda b, t: (b, t, 0))
        mods = [mod_row(lambda b, t: b * hpb)] * parts
        out_shape = jax.ShapeDtypeStruct((n_batch, seq, d), F32)
    out = pl.pallas_call(
        _out_ffn_kernel,
        grid=grid,
        in_specs=[rows(t.shape[-1]) for t in arrays] + mods
        + [lw(d, d), lw(1, d), lw(d, D_FF), lw(d, D_FF), lw(D_FF, d)],
        out_specs=rows(d),
        out_shape=out_shape,
        compiler_params=pltpu.CompilerParams(
            dimension_semantics=("arbitrary",) * len(grid), vmem_limit_bytes=VMEM_LIMIT),
        name="out_projection_and_ffn_half",
    )(*arrays, *([mod] * parts), p["w_out"], p["g_ffn2"], p["w1_ffn2"], p["w3_ffn2"], p["w2_ffn2"])
    return out.reshape(n_batch, t_all, d) if with_ctx else out


def _prepare_params(g_ffn1, w1_ffn1, w3_ffn1, w2_ffn1, g_mix, w_in, g_cq, w_uq, g_ckv, w_ukv,
                    g_mla_q, g_mla_k, g_swa_q, g_swa_k, w_out, g_ffn2, w1_ffn2, w3_ffn2, w2_ffn2):
    row = lambda g: g[:, None, :]
    two_heads = lambda g: jnp.concatenate([g, g], axis=-1)
    kv = w_ukv.reshape(w_ukv.shape[:-1] + (MLA_HEADS, MLA_NOPE + MLA_V))
    flat = lambda t: t.reshape(t.shape[:-2] + (t.shape[-2] * t.shape[-1],))
    return {
        "g_ffn1": row(g_ffn1), "w1_ffn1": w1_ffn1.astype(BF16), "w3_ffn1": w3_ffn1.astype(BF16),
        "w2_ffn1": w2_ffn1.astype(BF16),
        "g_mix": row(g_mix), "w_in": _take_cols(w_in, _w_in_cols()).astype(BF16),
        "g_cq": row(g_cq), "w_uq": _pad_heads(w_uq, MLA_HEADS).astype(BF16),
        "g_mq": row(_pad_heads(g_mla_q, 1)),
        "g_ckv": row(g_ckv), "w_uk": _pad_heads(flat(kv[..., :MLA_NOPE]), MLA_HEADS).astype(BF16),
        "w_uv": flat(kv[..., MLA_NOPE:]).astype(BF16),
        "g_mk": row(_pad_heads(g_mla_k, 1)),
        "g_sq": row(two_heads(g_swa_q)),
        "g_sk": row(two_heads(g_swa_k)),
        "w_out": _take_cols(w_out, _w_out_rows(), axis=1).astype(BF16),
        "g_ffn2": row(g_ffn2), "w1_ffn2": w1_ffn2.astype(BF16), "w3_ffn2": w3_ffn2.astype(BF16),
        "w2_ffn2": w2_ffn2.astype(BF16),
        "dft64": _channel_dft(),
    }


def kernel(x, c, ctx, c_ctx, w_ada, b_ada, g_ffn1, w1_ffn1, w3_ffn1, w2_ffn1, g_mix, w_in, g_cq, w_uq,
           g_ckv, w_ukv, g_mla_q, g_mla_k, g_swa_q, g_swa_k, sink, w_out, g_ffn2, w1_ffn2, w3_ffn2,
           w2_ffn2):
    n_batch, seq, d = x.shape
    n_ctx = ctx.shape[1]
    depth = w_ada.shape[0]
    assert d == D_MODEL and seq % GRID_W == 0 and n_batch + 1 <= MOD_ROWS
    assert n_ctx % TOK_TILE == 0 and seq % TOK_TILE == 0 and seq % n_ctx == 0
    assert seq % (OUT_PARTS_LATENT * TOK_TILE) == 0 and (n_batch * (seq + n_ctx)) % (OUT_PARTS * TOK_TILE) == 0
    assert seq % Q_TILE == 0 and seq % MLA_Q_TILE == 0
    assert Q_TILE % ROW_CHUNK == 0 and ROW_CHUNK + 2 * SWA_WINDOW <= seq
    assert w_ada.shape[-1] % MOD_COL_TILE == 0

    p = _prepare_params(g_ffn1, w1_ffn1, w3_ffn1, w2_ffn1, g_mix, w_in, g_cq, w_uq, g_ckv, w_ukv,
                        g_mla_q, g_mla_k, g_swa_q, g_swa_k, w_out, g_ffn2, w1_ffn2, w3_ffn2, w2_ffn2)
    tabs = _rope_tables(seq, n_ctx)
    c_lat, s_lat = _dft_cos_sin(seq, seq ** -0.5)
    c_ctx_dft, s_ctx_dft = _dft_cos_sin(n_ctx, n_ctx ** -0.5)
    dfts = tuple(m.astype(BF16) for m in (c_lat, s_lat, c_ctx_dft, s_ctx_dft))

    cc = jnp.concatenate([c, c_ctx[None, :], jnp.zeros((MOD_ROWS - n_batch - 1, d), F32)], axis=0)
    mod = _modulation(cc, w_ada, b_ada).reshape(depth, MOD_ROWS, N_MOD, d)

    tokens = (x, ctx)
    for layer in range(depth):
        with_ctx = layer != depth - 1
        hh, q, k, v, sq, sk, sv, zcs = _ffn_proj(layer, tokens, mod, p, tabs, seq, seq + n_ctx)
        yf = _fourier(zcs, dfts, n_ctx, with_ctx)
        a = _mla(q, k, v, seq, with_ctx)
        w = _swa(layer, sink, sq, sk, sv, seq, with_ctx)
        hh = _out_ffn(layer, hh, yf, a, w, mod, p, seq, with_ctx)
        tokens = (hh,)
    return hh
```

```python
import functools

import numpy as np
import jax
import jax.numpy as jnp
from jax import lax
from jax.experimental import pallas as pl
from jax.experimental.pallas import tpu as pltpu

F32 = jnp.float32
BF16 = jnp.bfloat16

D_MODEL = 1024
GRID_W = 64
ROPE_BASE = 10000.0
EPS = 1e-6
NEG = -1e30
LOG2E = 1.4426950408889634
N_MOD = 9
D_FF = 2816

FOURIER_WIDTH = 256
FOURIER_GROUP_DIM = 64
MLA_HEADS = 8
MLA_NOPE = 64
MLA_ROPE = 32
MLA_V = 64
MLA_QK_DIM = MLA_NOPE + MLA_ROPE
MLA_Q_RANK = 256
MLA_KV_RANK = 128
SWA_Q_HEADS = 4
SWA_KV_HEADS = 2
SWA_HEAD_DIM = 64
SWA_WINDOW = 128
IN_SPLITS = (256, 256, 128, 32, 256, 128, 128)
IN_WIDTH = sum(IN_SPLITS)

LANES = 128
HALF = LANES // 2
IN_WIDTH_P = 10 * LANES
MLA_V_SLABS = MLA_HEADS
VMEM_LIMIT = 56 * 1024 * 1024

TOK_TILE = 256
OUT_PARTS = 2
OUT_PARTS_LATENT = 2
Q_TILE = 1024
MLA_Q_TILE = 1024
ROW_CHUNK = 256
MOD_ROWS = 24
MOD_COL_TILE = 2304

OFF_F, OFF_CQ, OFF_CKV, OFF_KR, OFF_SQA, OFF_SQB, OFF_SK, OFF_SV = (
    0, 256, 512, 640, 768, 896, 1024, 1152)


def _w_in_cols():
    zero = IN_WIDTH
    o_f, o_cq, o_ckv, o_kr, o_sq, o_sk, o_sv = np.cumsum((0,) + IN_SPLITS)[:-1]
    cols = np.full((IN_WIDTH_P,), zero, np.int32)
    cols[OFF_F:OFF_F + 256] = o_f + np.arange(256)
    cols[OFF_CQ:OFF_CQ + 256] = o_cq + np.arange(256)
    cols[OFF_CKV:OFF_CKV + 128] = o_ckv + np.arange(128)
    cols[OFF_KR + MLA_NOPE:OFF_KR + MLA_QK_DIM] = o_kr + np.arange(MLA_ROPE)
    head = np.arange(SWA_HEAD_DIM)
    cols[OFF_SQA:OFF_SQA + 64] = o_sq + 0 * 64 + head
    cols[OFF_SQA + 64:OFF_SQA + 128] = o_sq + 2 * 64 + head
    cols[OFF_SQB:OFF_SQB + 64] = o_sq + 1 * 64 + head
    cols[OFF_SQB + 64:OFF_SQB + 128] = o_sq + 3 * 64 + head
    cols[OFF_SK:OFF_SK + 64] = o_sk + head
    cols[OFF_SK + 64:OFF_SK + 128] = o_sk + 64 + head
    cols[OFF_SV:OFF_SV + 128] = o_sv + np.arange(128)
    return cols


def _pad_heads(w, n_heads):
    lead, width = w.shape[:-1], w.shape[-1] // n_heads
    w = jnp.pad(w.reshape(lead + (n_heads, width)), [(0, 0)] * (len(lead) + 1) + [(0, LANES - width)])
    return w.reshape(lead + (n_heads * LANES,))


def _w_out_rows():
    base = FOURIER_WIDTH + MLA_HEADS * MLA_V
    swa = np.concatenate([base + h * SWA_HEAD_DIM + np.arange(SWA_HEAD_DIM) for h in (0, 2, 1, 3)])
    return np.concatenate([np.arange(base), swa]).astype(np.int32)


def _take_cols(w, cols, axis=-1):
    axis = axis % w.ndim
    n = w.shape[axis]
    cols = [int(c) for c in cols]
    pieces, i = [], 0
    while i < len(cols):
        j = i + 1
        if cols[i] == n:
            while j < len(cols) and cols[j] == n:
                j += 1
            shape = w.shape[:axis] + (j - i,) + w.shape[axis + 1:]
            pieces.append(jnp.zeros(shape, w.dtype))
        else:
            stride = cols[j] - cols[i] if j < len(cols) and cols[j] - cols[i] in (1, 2) else 1
            while j < len(cols) and cols[j] != n and cols[j] == cols[j - 1] + stride:
                j += 1
            pieces.append(lax.slice_in_dim(w, cols[i], cols[j - 1] + 1, stride, axis))
        i = j
    return jnp.concatenate(pieces, axis=axis)


def _rope_tables(seq, ctx):
    rows = seq // GRID_W
    pad = jnp.zeros((ctx,), F32)
    row = jnp.concatenate([jnp.repeat(jnp.arange(rows, dtype=F32), GRID_W), pad])[:, None]
    col = jnp.concatenate([jnp.tile(jnp.arange(GRID_W, dtype=F32), rows), pad])[:, None]

    def build(dim, section_starts):
        axis_dim = dim // 2
        n_freq = axis_dim // 2
        freq = np.zeros((LANES,), np.float32)
        active = np.zeros((LANES,), bool)
        by_row = np.zeros((LANES,), bool)
        m_a = np.zeros((LANES,), np.float32)
        m_b = np.zeros((LANES,), np.float32)
        pair = np.arange(dim) // 2
        for lo in section_starts:
            freq[lo:lo + dim] = 2 * (pair % n_freq)
            active[lo:lo + dim] = True
            by_row[lo:lo + dim] = pair < n_freq
            m_a[lo:lo + dim:2] = 1.0
            m_b[lo + 1:lo + dim:2] = 1.0
        inv_lane = jnp.where(jnp.asarray(active), ROPE_BASE ** (-jnp.asarray(freq) / axis_dim), 0.0)[None, :]
        ang = jnp.where(jnp.asarray(by_row)[None, :], row * inv_lane, col * inv_lane)
        sin = jnp.sin(ang)
        return jnp.cos(ang), sin * jnp.asarray(-m_a)[None, :], sin * jnp.asarray(m_b)[None, :]

    return build(MLA_ROPE, (MLA_NOPE,)) + build(SWA_HEAD_DIM, (0, HALF))


def _dft_cos_sin(n, scale):
    def direct(rows_j, n_mod):
        k = jnp.arange(n, dtype=jnp.int32)
        ang = ((rows_j[:, None] * k[None, :]) % n_mod).astype(F32) * (2.0 * np.pi / n_mod)
        return jnp.cos(ang), jnp.sin(ang)

    inner = FOURIER_GROUP_DIM
    if n <= inner or n % inner:
        c, s = direct(jnp.arange(n, dtype=jnp.int32), n)
        return c * scale, s * scale
    outer = n // inner
    ca, sa = direct(jnp.arange(outer, dtype=jnp.int32), outer)
    cb, sb = direct(jnp.arange(inner, dtype=jnp.int32), n)
    cb, sb = cb * scale, sb * scale
    c = ca[:, None, :] * cb[None, :, :] - sa[:, None, :] * sb[None, :, :]
    s = sa[:, None, :] * cb[None, :, :] + ca[:, None, :] * sb[None, :, :]
    return c.reshape(n, n), s.reshape(n, n)


def _channel_dft():
    c, s = _dft_cos_sin(FOURIER_GROUP_DIM, FOURIER_GROUP_DIM ** -0.5)
    eye = jnp.eye(FOURIER_WIDTH // FOURIER_GROUP_DIM, dtype=F32)
    return jnp.concatenate([jnp.kron(eye, c), jnp.kron(eye, s)], axis=1).astype(BF16)


def _rms_scale(x, width):
    return lax.rsqrt(jnp.sum(x * x, axis=-1, keepdims=True) * (1.0 / width) + EPS)


def _norm_mod(x, g, shift, scale):
    y = x * _rms_scale(x, x.shape[-1]) * g
    return y * (1.0 + scale) + shift


def _swiglu(xn, w1_ref, w3_ref, w2_ref):
    a = jnp.dot(xn, w1_ref[...], preferred_element_type=F32)
    b = jnp.dot(xn, w3_ref[...], preferred_element_type=F32)
    g = (a / (1.0 + jnp.exp(-a))) * b
    return jnp.dot(g.astype(BF16), w2_ref[...], preferred_element_type=F32)


def _rope(x, cos, sin_a, sin_b):
    return x * cos + pltpu.roll(x, LANES - 1, 1) * sin_a + pltpu.roll(x, 1, 1) * sin_b


def _low_lanes(shape):
    return lax.broadcasted_iota(jnp.int32, shape, len(shape) - 1) < HALF


def _mod_kernel(c_ref, w_ref, b_ref, o_ref):
    cv = c_ref[...]
    s = (cv / (1.0 + jnp.exp(-cv))).astype(BF16)
    o_ref[...] = jnp.dot(s, w_ref[...].astype(BF16), preferred_element_type=F32) + b_ref[...]


def _modulation(cc, w_ada, b_ada):
    n_layers, d, width = w_ada.shape
    return pl.pallas_call(
        _mod_kernel,
        grid=(n_layers, width // MOD_COL_TILE),
        in_specs=[
            pl.BlockSpec((MOD_ROWS, d), lambda l, j: (0, 0)),
            pl.BlockSpec((None, d, MOD_COL_TILE), lambda l, j: (l, 0, j)),
            pl.BlockSpec((None, 1, MOD_COL_TILE), lambda l, j: (l, 0, j)),
        ],
        out_specs=pl.BlockSpec((None, MOD_ROWS, MOD_COL_TILE), lambda l, j: (l, 0, j)),
        out_shape=jax.ShapeDtypeStruct((n_layers, MOD_ROWS, width), F32),
        compiler_params=pltpu.CompilerParams(
            dimension_semantics=("arbitrary", "arbitrary"), vmem_limit_bytes=VMEM_LIMIT),
        name="adaln_modulation",
    )(cc, w_ada, b_ada.reshape(n_layers, 1, width))


def _ffn_proj_kernel(*refs, n_src, n_tiles, tpb, n_lat):
    h_refs = refs[:n_src]
    (mod_ref, modp_ref, g1_ref, w1_ref, w3_ref, w2_ref, gmix_ref, win_ref,
     gcq_ref, wuq_ref, gmq_ref, gckv_ref, wuk_ref, wuv_ref, gmk_ref, gsq_ref, gsk_ref, dft_ref,
     cm_ref, sam_ref, sbm_ref, cs_ref, sas_ref, sbs_ref,
     ho_ref, q_ref, k_ref, v_ref, sq_ref, sk_ref, sv_ref, z_ref, hprev_ref) = refs[n_src:]
    step = pl.program_id(0)

    @pl.when(step == 0)
    def _():
        hprev_ref[...] = jnp.zeros_like(hprev_ref)

    modp = modp_ref[...]
    n = _norm_mod(hprev_ref[...], gmix_ref[...], modp[3:4], modp[4:5]).astype(BF16)
    u = jnp.dot(n, win_ref[...], preferred_element_type=F32)

    f = u[:, OFF_F:OFF_F + FOURIER_WIDTH].astype(BF16)
    z_ref[...] = jnp.dot(f, dft_ref[...], preferred_element_type=F32).astype(BF16)

    cm, sam, sbm = cm_ref[...], sam_ref[...], sbm_ref[...]
    cq = u[:, OFF_CQ:OFF_CQ + MLA_Q_RANK]
    cqn = (cq * _rms_scale(cq, MLA_Q_RANK) * gcq_ref[...]).astype(BF16)
    q = jnp.dot(cqn, wuq_ref[...], preferred_element_type=F32)
    gmq = gmq_ref[...]
    q_scale = MLA_QK_DIM ** -0.5 * LOG2E
    for hd in range(MLA_HEADS):
        qh = q[:, hd * LANES:(hd + 1) * LANES]
        qg = qh * (_rms_scale(qh, MLA_QK_DIM) * q_scale) * gmq
        q_ref[:, hd * LANES:(hd + 1) * LANES] = _rope(qg, cm, sam, sbm).astype(BF16)

    ckv = u[:, OFF_CKV:OFF_CKV + MLA_KV_RANK]
    ckvn = (ckv * _rms_scale(ckv, MLA_KV_RANK) * gckv_ref[...]).astype(BF16)
    kn = jnp.dot(ckvn, wuk_ref[...], preferred_element_type=F32)
    vv = jnp.dot(ckvn, wuv_ref[...], preferred_element_type=F32).astype(BF16)
    ones = jnp.ones((vv.shape[0], LANES), BF16)
    for pair in range(MLA_HEADS // 2):
        v_ref[:, 2 * pair * LANES:(2 * pair + 1) * LANES] = vv[:, pair * LANES:(pair + 1) * LANES]
        v_ref[:, (2 * pair + 1) * LANES:(2 * pair + 2) * LANES] = ones
    gmk = gmk_ref[...]
    kr = u[:, OFF_KR:OFF_KR + LANES]
    kr_ss = jnp.sum(kr * kr, axis=-1, keepdims=True)
    kr_rot = _rope(kr * gmk, cm, sam, sbm)
    for hd in range(MLA_HEADS):
        kh = kn[:, hd * LANES:(hd + 1) * LANES]
        ss = jnp.sum(kh * kh, axis=-1, keepdims=True) + kr_ss
        rs = lax.rsqrt(ss * (1.0 / MLA_QK_DIM) + EPS)
        k_ref[:, hd * LANES:(hd + 1) * LANES] = (rs * (kh * gmk + kr_rot)).astype(BF16)

    cs, sas, sbs = cs_ref[...], sas_ref[...], sbs_ref[...]

    def two_head_norm_rope(x, g, scale):
        low = _low_lanes(x.shape)
        x2 = x * x
        lo = jnp.sum(jnp.where(low, x2, 0.0), axis=-1, keepdims=True)
        hi = jnp.sum(jnp.where(low, 0.0, x2), axis=-1, keepdims=True)
        rs = jnp.where(low, lax.rsqrt(lo * (1.0 / SWA_HEAD_DIM) + EPS),
                       lax.rsqrt(hi * (1.0 / SWA_HEAD_DIM) + EPS))
        return _rope(x * (rs * scale) * g, cs, sas, sbs).astype(BF16)

    gsq = gsq_ref[...]
    s_scale = SWA_HEAD_DIM ** -0.5 * LOG2E
    sq_ref[:, 0:LANES] = two_head_norm_rope(u[:, OFF_SQA:OFF_SQA + LANES], gsq, s_scale)
    sq_ref[:, LANES:2 * LANES] = two_head_norm_rope(u[:, OFF_SQB:OFF_SQB + LANES], gsq, s_scale)
    sk_ref[...] = two_head_norm_rope(u[:, OFF_SK:OFF_SK + LANES], gsk_ref[...], 1.0)
    sv_ref[:, 0:LANES] = u[:, OFF_SV:OFF_SV + LANES].astype(BF16)
    sv_ref[:, LANES:2 * LANES] = ones

    mod = mod_ref[...]
    if n_src == 1:
        h = h_refs[0][...]
    else:
        cur = jnp.minimum(step, n_tiles - 1)
        h = jnp.where(cur % tpb >= n_lat, h_refs[1][...], h_refs[0][...])
    xn = _norm_mod(h, g1_ref[...], mod[0:1], mod[1:2]).astype(BF16)
    h = h + (0.5 * mod[2:3]) * _swiglu(xn, w1_ref, w3_ref, w2_ref)
    ho_ref[...] = h
    hprev_ref[...] = h


def _const_spec(block_shape, index_map):
    return pl.BlockSpec(block_shape, index_map, pipeline_mode=pl.Buffered(1))


def _ffn_proj(layer, tokens, mod, p, tabs, seq, t_all):
    n_batch, _, d = tokens[0].shape
    tm = TOK_TILE
    tpb = t_all // tm
    n_tiles = n_batch * tpb
    n_lat = seq // tm
    cur = lambda s: jnp.minimum(s, n_tiles - 1)
    prev = lambda s: jnp.maximum(s - 1, 0)
    tile = lambda which, width: pl.BlockSpec(
        (None, tm, width), lambda s: (which(s) // tpb, which(s) % tpb, 0))

    def token_specs(which):
        if len(tokens) == 1:
            return [tile(which, d)]
        lat = pl.BlockSpec((None, tm, d),
                           lambda s: (which(s) // tpb, jnp.minimum(which(s) % tpb, n_lat - 1), 0))
        con = pl.BlockSpec((None, tm, d),
                           lambda s: (which(s) // tpb, jnp.maximum(which(s) % tpb - n_lat, 0), 0))
        return [lat, con]

    def mod_row(which):
        def index(s):
            b, t = which(s) // tpb, which(s) % tpb
            return (layer, jnp.where(t >= n_lat, n_batch, b), 0, 0)
        return pl.BlockSpec((None, None, N_MOD, d), index)

    lw = lambda *shape: _const_spec((None,) + shape, lambda s: (layer,) + (0,) * len(shape))
    tab = pl.BlockSpec((tm, LANES), lambda s: (prev(s) % tpb, 0))
    in_specs = token_specs(cur) + [
        mod_row(cur), mod_row(prev),
        lw(1, d), lw(d, D_FF), lw(d, D_FF), lw(D_FF, d), lw(1, d), lw(d, IN_WIDTH_P),
        lw(1, MLA_Q_RANK), lw(MLA_Q_RANK, MLA_HEADS * LANES), lw(1, LANES),
        lw(1, MLA_KV_RANK), lw(MLA_KV_RANK, MLA_HEADS * LANES), lw(MLA_KV_RANK, MLA_HEADS * MLA_V),
        lw(1, LANES), lw(1, LANES), lw(1, LANES),
        _const_spec((FOURIER_WIDTH, 2 * FOURIER_WIDTH), lambda s: (0, 0)),
        tab, tab, tab, tab, tab, tab,
    ]
    widths = (d, MLA_HEADS * LANES, MLA_HEADS * LANES, MLA_V_SLABS * LANES, 2 * LANES, LANES, 2 * LANES,
              2 * FOURIER_WIDTH)
    dtypes = (F32,) + (BF16,) * 7
    return pl.pallas_call(
        functools.partial(_ffn_proj_kernel, n_src=len(tokens), n_tiles=n_tiles, tpb=tpb, n_lat=n_lat),
        grid=(n_tiles + 1,),
        in_specs=in_specs,
        out_specs=[tile(cur, d)] + [tile(prev, w) for w in widths[1:]],
        out_shape=[jax.ShapeDtypeStruct((n_batch, t_all, w), dt) for w, dt in zip(widths, dtypes)],
        scratch_shapes=[pltpu.VMEM((tm, d), F32)],
        compiler_params=pltpu.CompilerParams(
            dimension_semantics=("arbitrary",), vmem_limit_bytes=VMEM_LIMIT),
        name="ffn_half_and_projections",
    )(*tokens, mod, mod, p["g_ffn1"], p["w1_ffn1"], p["w3_ffn1"], p["w2_ffn1"], p["g_mix"], p["w_in"],
      p["g_cq"], p["w_uq"], p["g_mq"], p["g_ckv"], p["w_uk"], p["w_uv"], p["g_mk"],
      p["g_sq"], p["g_sk"], p["dft64"], *tabs)


def _fourier_kernel(z_ref, cl_ref, sl_ref, cc_ref, sc_ref, o_ref, *, ctx, with_ctx):
    w = FOURIER_WIDTH

    def mix(c_ref, s_ref, lo, n):
        zc = z_ref[lo:lo + n, 0:w]
        zs = z_ref[lo:lo + n, w:2 * w]
        return (jnp.dot(c_ref[...], zc, preferred_element_type=F32)
                - jnp.dot(s_ref[...], zs, preferred_element_type=F32)).astype(BF16)

    n_lat = cl_ref.shape[0]
    o_ref[0:n_lat, :] = mix(cl_ref, sl_ref, 0, n_lat)
    if with_ctx:
        o_ref[n_lat:n_lat + ctx, :] = mix(cc_ref, sc_ref, n_lat, ctx)


def _fourier(zcs, dfts, ctx, with_ctx):
    n_batch, t_all, _ = zcs.shape
    seq = t_all - ctx
    rows = t_all if with_ctx else seq
    full = lambda a: _const_spec(a.shape, lambda b: (0, 0))
    return pl.pallas_call(
        functools.partial(_fourier_kernel, ctx=ctx, with_ctx=with_ctx),
        grid=(n_batch,),
        in_specs=[pl.BlockSpec((None, t_all, 2 * FOURIER_WIDTH), lambda b: (b, 0, 0))]
        + [full(a) for a in dfts],
        out_specs=pl.BlockSpec((None, rows, FOURIER_WIDTH), lambda b: (b, 0, 0)),
        out_shape=jax.ShapeDtypeStruct((n_batch, rows, FOURIER_WIDTH), BF16),
        compiler_params=pltpu.CompilerParams(
            dimension_semantics=("arbitrary",), vmem_limit_bytes=VMEM_LIMIT),
        name="fourier_positions",
    )(zcs, *dfts)


def _scores(q, k):
    return lax.dot_general(q, k, (((1,), (1,)), ((), ())), preferred_element_type=F32)


def _mla_heads(q_ref, k_ref, v_ref, o_ref, head0_scores=None):
    n_heads = q_ref.shape[-1] // LANES
    for pair in range(n_heads // 2):
        outs = []
        vp = v_ref[:, 2 * pair * LANES:(2 * pair + 2) * LANES]
        for hd in (2 * pair, 2 * pair + 1):
            if hd == 0 and head0_scores is not None:
                s = head0_scores
            else:
                s = _scores(q_ref[:, hd * LANES:(hd + 1) * LANES], k_ref[:, hd * LANES:(hd + 1) * LANES])
            m = jnp.max(s, axis=-1, keepdims=True)
            p = jnp.exp2(s - m).astype(BF16)
            r = jnp.dot(p, vp, preferred_element_type=F32)
            outs.append(r[:, 0:LANES] / r[:, LANES:2 * LANES])
        o_ref[:, pair * LANES:(pair + 1) * LANES] = jnp.where(
            _low_lanes(outs[0].shape), outs[0], outs[1]).astype(BF16)


def _mla_latent_kernel(q_ref, k_ref, v_ref, qn_ref, kn_ref, o_ref, s0_ref):
    @pl.when(pl.program_id(0) == 0)
    def _():
        s0_ref[...] = _scores(q_ref[:, 0:LANES], k_ref[:, 0:LANES])

    _mla_heads(q_ref, k_ref, v_ref, o_ref, head0_scores=s0_ref[...])
    s0_ref[...] = _scores(qn_ref[...], kn_ref[...])


def _mla_ctx_kernel(q_ref, k_ref, v_ref, prev_ref, o_ref):
    del prev_ref
    _mla_heads(q_ref, k_ref, v_ref, o_ref)


def _mla(q, k, v, seq, with_ctx):
    n_batch, t_all, _ = q.shape
    n_ctx = t_all - seq
    tq = MLA_Q_TILE
    tpb = seq // tq
    n_tiles = n_batch * tpb
    qk_w = MLA_HEADS * LANES
    v_in = MLA_V_SLABS * LANES
    v_w = MLA_HEADS * MLA_V
    params = lambda n: pltpu.CompilerParams(
        dimension_semantics=("arbitrary",) * n, vmem_limit_bytes=VMEM_LIMIT)
    nxt = lambda s: jnp.minimum(s + 1, n_tiles - 1)
    a = pl.pallas_call(
        _mla_latent_kernel,
        grid=(n_tiles,),
        in_specs=[
            pl.BlockSpec((None, tq, qk_w), lambda s: (s // tpb, s % tpb, 0)),
            pl.BlockSpec((None, t_all, qk_w), lambda s: (s // tpb, 0, 0)),
            pl.BlockSpec((None, t_all, v_in), lambda s: (s // tpb, 0, 0)),
            pl.BlockSpec((None, tq, LANES), lambda s: (nxt(s) // tpb, nxt(s) % tpb, 0)),
            pl.BlockSpec((None, t_all, LANES), lambda s: (nxt(s) // tpb, 0, 0)),
        ],
        out_specs=pl.BlockSpec((None, tq, v_w), lambda s: (s // tpb, s % tpb, 0)),
        out_shape=jax.ShapeDtypeStruct((n_batch, t_all, v_w), BF16),
        scratch_shapes=[pltpu.VMEM((tq, t_all), F32)],
        compiler_params=params(1),
        name="mla_attention",
    )(q, k, v, q, k)
    if not with_ctx:
        return a
    c_blk = seq // n_ctx
    ctx_rows = lambda width: pl.BlockSpec((None, n_ctx, width), lambda b: (b, c_blk, 0))
    return pl.pallas_call(
        _mla_ctx_kernel,
        grid=(n_batch,),
        in_specs=[ctx_rows(qk_w), ctx_rows(qk_w), ctx_rows(v_in), pl.BlockSpec(memory_space=pl.ANY)],
        out_specs=ctx_rows(v_w),
        out_shape=jax.ShapeDtypeStruct(a.shape, a.dtype),
        input_output_aliases={3: 0},
        compiler_params=params(1),
        name="mla_context_attention",
    )(q, k, v, a)


def _swa_heads(sink_ref, layer, q_ref, keys, vals, valid, o_ref, r0=0, rows=None):
    rows = q_ref.shape[0] if rows is None else rows
    res = []
    for hq in range(SWA_Q_HEADS):
        slab = hq % 2
        use_low = hq < 2
        qs = q_ref[r0:r0 + rows, slab * LANES:(slab + 1) * LANES]
        low = _low_lanes(qs.shape)
        qm = jnp.where(low if use_low else jnp.logical_not(low), qs, jnp.zeros_like(qs))
        sink = sink_ref[layer, hq] * LOG2E
        s = _scores(qm, keys)
        if valid is not None:
            s = jnp.where(valid, s, NEG)
        m = jnp.maximum(jnp.max(s, axis=-1, keepdims=True), sink)
        p = jnp.exp2(s - m).astype(BF16)
        r = jnp.dot(p, vals, preferred_element_type=F32)
        res.append(r[:, 0:LANES] / (r[:, LANES:2 * LANES] + jnp.exp2(sink - m)))
    low = _low_lanes(res[0].shape)
    o_ref[r0:r0 + rows, 0:LANES] = jnp.where(low, res[0], res[2]).astype(BF16)
    o_ref[r0:r0 + rows, LANES:2 * LANES] = jnp.where(low, res[1], res[3]).astype(BF16)


def _swa_latent_kernel(sink_ref, q_ref, k_ref, v_ref, o_ref, *, layer, seq):
    tq = q_ref.shape[0]
    rows = min(tq, ROW_CHUNK)
    span = rows + 2 * SWA_WINDOW
    n_keys = span + k_ref.shape[0] - seq
    kc, vc = k_ref[seq:, :], v_ref[seq:, :]
    for r0 in range(0, tq, rows):
        first = pl.program_id(1) * tq + r0
        start = pl.multiple_of(jnp.clip(first - SWA_WINDOW, 0, seq - span), LANES)
        keys = jnp.concatenate([k_ref[pl.ds(start, span), :], kc], axis=0)
        vals = jnp.concatenate([v_ref[pl.ds(start, span), :], vc], axis=0)
        col = lax.broadcasted_iota(jnp.int32, (rows, n_keys), 1)
        dist = first + lax.broadcasted_iota(jnp.int32, (rows, n_keys), 0) - (start + col)
        valid = ((dist <= SWA_WINDOW) & (dist >= -SWA_WINDOW)) | (col >= span)
        _swa_heads(sink_ref, layer, q_ref, keys, vals, valid, o_ref, r0, rows)


def _swa_ctx_kernel(sink_ref, q_ref, k_ref, v_ref, prev_ref, o_ref, *, layer):
    del prev_ref
    _swa_heads(sink_ref, layer, q_ref, k_ref[...], v_ref[...], None, o_ref)


def _swa(layer, sink, q, k, v, seq, with_ctx):
    n_batch, t_all, _ = q.shape
    n_ctx = t_all - seq
    tq = Q_TILE
    params = lambda n: pltpu.CompilerParams(
        dimension_semantics=("arbitrary",) * n, vmem_limit_bytes=VMEM_LIMIT)
    smem = pl.BlockSpec(memory_space=pltpu.SMEM)
    w = pl.pallas_call(
        functools.partial(_swa_latent_kernel, layer=layer, seq=seq),
        grid=(n_batch, seq // tq),
        in_specs=[
            smem,
            pl.BlockSpec((None, tq, 2 * LANES), lambda b, i: (b, i, 0)),
            pl.BlockSpec((None, t_all, LANES), lambda b, i: (b, 0, 0)),
            pl.BlockSpec((None, t_all, 2 * LANES), lambda b, i: (b, 0, 0)),
        ],
        out_specs=pl.BlockSpec((None, tq, 2 * LANES), lambda b, i: (b, i, 0)),
        out_shape=jax.ShapeDtypeStruct((n_batch, t_all, 2 * LANES), BF16),
        compiler_params=params(2),
        name="window_attention",
    )(sink, q, k, v)
    if not with_ctx:
        return w
    c_blk = seq // n_ctx
    ctx_rows = lambda width: pl.BlockSpec((None, n_ctx, width), lambda b: (b, c_blk, 0))
    return pl.pallas_call(
        functools.partial(_swa_ctx_kernel, layer=layer),
        grid=(n_batch,),
        in_specs=[smem, ctx_rows(2 * LANES), ctx_rows(LANES), ctx_rows(2 * LANES),
                  pl.BlockSpec(memory_space=pl.ANY)],
        out_specs=ctx_rows(2 * LANES),
        out_shape=jax.ShapeDtypeStruct(w.shape, w.dtype),
        input_output_aliases={4: 0},
        compiler_params=params(1),
        name="window_context_attention",
    )(sink, q, k, v, w)


def _out_ffn_kernel(h_ref, yf_ref, a_ref, w_ref, *refs):
    mod_refs = refs[:-6]
    wo_ref, g2_ref, w1_ref, w3_ref, w2_ref, o_ref = refs[-6:]
    half = TOK_TILE
    n_f = yf_ref.shape[-1]
    n_a = a_ref.shape[-1]
    mixed = (jnp.dot(yf_ref[...], wo_ref[0:n_f, :], preferred_element_type=F32)
             + jnp.dot(a_ref[...], wo_ref[n_f:n_f + n_a, :], preferred_element_type=F32)
             + jnp.dot(w_ref[...], wo_ref[n_f + n_a:, :], preferred_element_type=F32))
    g2 = g2_ref[...]
    hs, xs = [], []
    for i, mod_ref in enumerate(mod_refs):
        mod = mod_ref[...]
        rows = slice(i * half, (i + 1) * half)
        h = h_ref[rows, :] + mod[5:6] * mixed[rows, :]
        hs.append(h)
        xs.append(_norm_mod(h, g2, mod[6:7], mod[7:8]).astype(BF16))
    y = _swiglu(jnp.concatenate(xs, axis=0), w1_ref, w3_ref, w2_ref)
    for i, mod_ref in enumerate(mod_refs):
        rows = slice(i * half, (i + 1) * half)
        o_ref[rows, :] = hs[i] + (0.5 * mod_ref[8:9, :]) * y[rows, :]


def _out_ffn(layer, hh, yf, a, w, mod, p, seq, with_ctx):
    n_batch, t_all, d = hh.shape
    half = TOK_TILE
    parts = OUT_PARTS if with_ctx else OUT_PARTS_LATENT
    tm = parts * half
    hpb = t_all // half
    n_lat = seq // half
    lw = lambda *shape: _const_spec((None,) + shape, lambda *_: (layer,) + (0,) * len(shape))

    def mod_row(half_index):
        def index(*g):
            j = half_index(*g)
            return (layer, jnp.where(j % hpb >= n_lat, n_batch, j // hpb), 0, 0)
        return pl.BlockSpec((None, None, N_MOD, d), index)

    if with_ctx:
        flat = lambda t: t.reshape(n_batch * t_all, t.shape[-1])
        arrays = [flat(t) for t in (hh, yf, a, w)]
        grid = (n_batch * t_all // tm,)
        rows = lambda width: pl.BlockSpec((tm, width), lambda j: (j, 0))
        mods = [mod_row(lambda j, i=i: parts * j + i) for i in range(parts)]
        out_shape = jax.ShapeDtypeStruct((n_batch * t_all, d), F32)
    else:
        arrays = [hh, yf, a, w]
        grid = (n_batch, seq // tm)
        rows = lambda width: pl.BlockSpec((None, tm, width), lambda b, t: (b, t, 0))
        mods = [mod_row(lambda b, t: b * hpb)] * parts
        out_shape = jax.ShapeDtypeStruct((n_batch, seq, d), F32)
    out = pl.pallas_call(
        _out_ffn_kernel,
        grid=grid,
        in_specs=[rows(t.shape[-1]) for t in arrays] + mods
        + [lw(d, d), lw(1, d), lw(d, D_FF), lw(d, D_FF), lw(D_FF, d)],
        out_specs=rows(d),
        out_shape=out_shape,
        compiler_params=pltpu.CompilerParams(
            dimension_semantics=("arbitrary",) * len(grid), vmem_limit_bytes=VMEM_LIMIT),
        name="out_projection_and_ffn_half",
    )(*arrays, *([mod] * parts), p["w_out"], p["g_ffn2"], p["w1_ffn2"], p["w3_ffn2"], p["w2_ffn2"])
    return out.reshape(n_batch, t_all, d) if with_ctx else out


def _prepare_params(g_ffn1, w1_ffn1, w3_ffn1, w2_ffn1, g_mix, w_in, g_cq, w_uq, g_ckv, w_ukv,
                    g_mla_q, g_mla_k, g_swa_q, g_swa_k, w_out, g_ffn2, w1_ffn2, w3_ffn2, w2_ffn2):
    row = lambda g: g[:, None, :]
    two_heads = lambda g: jnp.concatenate([g, g], axis=-1)
    kv = w_ukv.reshape(w_ukv.shape[:-1] + (MLA_HEADS, MLA_NOPE + MLA_V))
    flat = lambda t: t.reshape(t.shape[:-2] + (t.shape[-2] * t.shape[-1],))
    return {
        "g_ffn1": row(g_ffn1), "w1_ffn1": w1_ffn1.astype(BF16), "w3_ffn1": w3_ffn1.astype(BF16),
        "w2_ffn1": w2_ffn1.astype(BF16),
        "g_mix": row(g_mix), "w_in": _take_cols(w_in, _w_in_cols()).astype(BF16),
        "g_cq": row(g_cq), "w_uq": _pad_heads(w_uq, MLA_HEADS).astype(BF16),
        "g_mq": row(_pad_heads(g_mla_q, 1)),
        "g_ckv": row(g_ckv), "w_uk": _pad_heads(flat(kv[..., :MLA_NOPE]), MLA_HEADS).astype(BF16),
        "w_uv": flat(kv[..., MLA_NOPE:]).astype(BF16),
        "g_mk": row(_pad_heads(g_mla_k, 1)),
        "g_sq": row(two_heads(g_swa_q)),
        "g_sk": row(two_heads(g_swa_k)),
        "w_out": _take_cols(w_out, _w_out_rows(), axis=1).astype(BF16),
        "g_ffn2": row(g_ffn2), "w1_ffn2": w1_ffn2.astype(BF16), "w3_ffn2": w3_ffn2.astype(BF16),
        "w2_ffn2": w2_ffn2.astype(BF16),
        "dft64": _channel_dft(),
    }


def kernel(x, c, ctx, c_ctx, w_ada, b_ada, g_ffn1, w1_ffn1, w3_ffn1, w2_ffn1, g_mix, w_in, g_cq, w_uq,
           g_ckv, w_ukv, g_mla_q, g_mla_k, g_swa_q, g_swa_k, sink, w_out, g_ffn2, w1_ffn2, w3_ffn2,
           w2_ffn2):
    n_batch, seq, d = x.shape
    n_ctx = ctx.shape[1]
    depth = w_ada.shape[0]
    assert d == D_MODEL and seq % GRID_W == 0 and n_batch + 1 <= MOD_ROWS
    assert n_ctx % TOK_TILE == 0 and seq % TOK_TILE == 0 and seq % n_ctx == 0
    assert seq % (OUT_PARTS_LATENT * TOK_TILE) == 0 and (n_batch * (seq + n_ctx)) % (OUT_PARTS * TOK_TILE) == 0
    assert seq % Q_TILE == 0 and seq % MLA_Q_TILE == 0
    assert Q_TILE % ROW_CHUNK == 0 and ROW_CHUNK + 2 * SWA_WINDOW <= seq
    assert w_ada.shape[-1] % MOD_COL_TILE == 0

    p = _prepare_params(g_ffn1, w1_ffn1, w3_ffn1, w2_ffn1, g_mix, w_in, g_cq, w_uq, g_ckv, w_ukv,
                        g_mla_q, g_mla_k, g_swa_q, g_swa_k, w_out, g_ffn2, w1_ffn2, w3_ffn2, w2_ffn2)
    tabs = _rope_tables(seq, n_ctx)
    c_lat, s_lat = _dft_cos_sin(seq, seq ** -0.5)
    c_ctx_dft, s_ctx_dft = _dft_cos_sin(n_ctx, n_ctx ** -0.5)
    dfts = tuple(m.astype(BF16) for m in (c_lat, s_lat, c_ctx_dft, s_ctx_dft))

    cc = jnp.concatenate([c, c_ctx[None, :], jnp.zeros((MOD_ROWS - n_batch - 1, d), F32)], axis=0)
    mod = _modulation(cc, w_ada, b_ada).reshape(depth, MOD_ROWS, N_MOD, d)

    tokens = (x, ctx)
    for layer in range(depth):
        with_ctx = layer != depth - 1
        hh, q, k, v, sq, sk, sv, zcs = _ffn_proj(layer, tokens, mod, p, tabs, seq, seq + n_ctx)
        yf = _fourier(zcs, dfts, n_ctx, with_ctx)
        a = _mla(q, k, v, seq, with_ctx)
        w = _swa(layer, sink, sq, sk, sv, seq, with_ctx)
        hh = _out_ffn(layer, hh, yf, a, w, mod, p, seq, with_ctx)
        tokens = (hh,)
    return hh
```

```python
import functools

import numpy as np
import jax
import jax.numpy as jnp
from jax import lax
from jax.experimental import pallas as pl
from jax.experimental.pallas import tpu as pltpu

F32 = jnp.float32
BF16 = jnp.bfloat16

D_MODEL = 1024
GRID_W = 64
ROPE_BASE = 10000.0
EPS = 1e-6
NEG = -1e30
LOG2E = 1.4426950408889634
N_MOD = 9
D_FF = 2816

FOURIER_WIDTH = 256
FOURIER_GROUP_DIM = 64
MLA_HEADS = 8
MLA_NOPE = 64
MLA_ROPE = 32
MLA_V = 64
MLA_QK_DIM = MLA_NOPE + MLA_ROPE
MLA_Q_RANK = 256
MLA_KV_RANK = 128
SWA_Q_HEADS = 4
SWA_KV_HEADS = 2
SWA_HEAD_DIM = 64
SWA_WINDOW = 128
IN_SPLITS = (256, 256, 128, 32, 256, 128, 128)
IN_WIDTH = sum(IN_SPLITS)

LANES = 128
HALF = LANES // 2
IN_WIDTH_P = 10 * LANES
MLA_V_SLABS = MLA_HEADS
VMEM_LIMIT = 56 * 1024 * 1024

TOK_TILE = 256
OUT_PARTS = 2
OUT_PARTS_LATENT = 2
Q_TILE = 1024
MLA_Q_TILE = 1024
ROW_CHUNK = 256
MOD_ROWS = 24
MOD_COL_TILE = 2304

OFF_F, OFF_CQ, OFF_CKV, OFF_KR, OFF_SQA, OFF_SQB, OFF_SK, OFF_SV = (
    0, 256, 512, 640, 768, 896, 1024, 1152)


def _w_in_cols():
    zero = IN_WIDTH
    o_f, o_cq, o_ckv, o_kr, o_sq, o_sk, o_sv = np.cumsum((0,) + IN_SPLITS)[:-1]
    cols = np.full((IN_WIDTH_P,), zero, np.int32)
    cols[OFF_F:OFF_F + 256] = o_f + np.arange(256)
    cols[OFF_CQ:OFF_CQ + 256] = o_cq + np.arange(256)
    cols[OFF_CKV:OFF_CKV + 128] = o_ckv + np.arange(128)
    cols[OFF_KR + MLA_NOPE:OFF_KR + MLA_QK_DIM] = o_kr + np.arange(MLA_ROPE)
    head = np.arange(SWA_HEAD_DIM)
    cols[OFF_SQA:OFF_SQA + 64] = o_sq + 0 * 64 + head
    cols[OFF_SQA + 64:OFF_SQA + 128] = o_sq + 2 * 64 + head
    cols[OFF_SQB:OFF_SQB + 64] = o_sq + 1 * 64 + head
    cols[OFF_SQB + 64:OFF_SQB + 128] = o_sq + 3 * 64 + head
    cols[OFF_SK:OFF_SK + 64] = o_sk + head
    cols[OFF_SK + 64:OFF_SK + 128] = o_sk + 64 + head
    cols[OFF_SV:OFF_SV + 128] = o_sv + np.arange(128)
    return cols


def _pad_heads(w, n_heads):
    lead, width = w.shape[:-1], w.shape[-1] // n_heads
    w = jnp.pad(w.reshape(lead + (n_heads, width)), [(0, 0)] * (len(lead) + 1) + [(0, LANES - width)])
    return w.reshape(lead + (n_heads * LANES,))


def _w_out_rows():
    base = FOURIER_WIDTH + MLA_HEADS * MLA_V
    swa = np.concatenate([base + h * SWA_HEAD_DIM + np.arange(SWA_HEAD_DIM) for h in (0, 2, 1, 3)])
    return np.concatenate([np.arange(base), swa]).astype(np.int32)


def _take_cols(w, cols, axis=-1):
    axis = axis % w.ndim
    n = w.shape[axis]
    cols = [int(c) for c in cols]
    pieces, i = [], 0
    while i < len(cols):
        j = i + 1
        if cols[i] == n:
            while j < len(cols) and cols[j] == n:
                j += 1
            shape = w.shape[:axis] + (j - i,) + w.shape[axis + 1:]
            pieces.append(jnp.zeros(shape, w.dtype))
        else:
            stride = cols[j] - cols[i] if j < len(cols) and cols[j] - cols[i] in (1, 2) else 1
            while j < len(cols) and cols[j] != n and cols[j] == cols[j - 1] + stride:
                j += 1
            pieces.append(lax.slice_in_dim(w, cols[i], cols[j - 1] + 1, stride, axis))
        i = j
    return jnp.concatenate(pieces, axis=axis)


def _rope_tables(seq, ctx):
    rows = seq // GRID_W
    pad = jnp.zeros((ctx,), F32)
    row = jnp.concatenate([jnp.repeat(jnp.arange(rows, dtype=F32), GRID_W), pad])[:, None]
    col = jnp.concatenate([jnp.tile(jnp.arange(GRID_W, dtype=F32), rows), pad])[:, None]

    def build(dim, section_starts):
        axis_dim = dim // 2
        n_freq = axis_dim // 2
        freq = np.zeros((LANES,), np.float32)
        active = np.zeros((LANES,), bool)
        by_row = np.zeros((LANES,), bool)
        m_a = np.zeros((LANES,), np.float32)
        m_b = np.zeros((LANES,), np.float32)
        pair = np.arange(dim) // 2
        for lo in section_starts:
            freq[lo:lo + dim] = 2 * (pair % n_freq)
            active[lo:lo + dim] = True
            by_row[lo:lo + dim] = pair < n_freq
            m_a[lo:lo + dim:2] = 1.0
            m_b[lo + 1:lo + dim:2] = 1.0
        inv_lane = jnp.where(jnp.asarray(active), ROPE_BASE ** (-jnp.asarray(freq) / axis_dim), 0.0)[None, :]
        ang = jnp.where(jnp.asarray(by_row)[None, :], row * inv_lane, col * inv_lane)
        sin = jnp.sin(ang)
        return jnp.cos(ang), sin * jnp.asarray(-m_a)[None, :], sin * jnp.asarray(m_b)[None, :]

    return build(MLA_ROPE, (MLA_NOPE,)) + build(SWA_HEAD_DIM, (0, HALF))


def _dft_cos_sin(n, scale):
    def direct(rows_j, n_mod):
        k = jnp.arange(n, dtype=jnp.int32)
        ang = ((rows_j[:, None] * k[None, :]) % n_mod).astype(F32) * (2.0 * np.pi / n_mod)
        return jnp.cos(ang), jnp.sin(ang)

    inner = FOURIER_GROUP_DIM
    if n <= inner or n % inner:
        c, s = direct(jnp.arange(n, dtype=jnp.int32), n)
        return c * scale, s * scale
    outer = n // inner
    ca, sa = direct(jnp.arange(outer, dtype=jnp.int32), outer)
    cb, sb = direct(jnp.arange(inner, dtype=jnp.int32), n)
    cb, sb = cb * scale, sb * scale
    c = ca[:, None, :] * cb[None, :, :] - sa[:, None, :] * sb[None, :, :]
    s = sa[:, None, :] * cb[None, :, :] + ca[:, None, :] * sb[None, :, :]
    return c.reshape(n, n), s.reshape(n, n)


def _channel_dft():
    c, s = _dft_cos_sin(FOURIER_GROUP_DIM, FOURIER_GROUP_DIM ** -0.5)
    eye = jnp.eye(FOURIER_WIDTH // FOURIER_GROUP_DIM, dtype=F32)
    return jnp.concatenate([jnp.kron(eye, c), jnp.kron(eye, s)], axis=1).astype(BF16)


def _rms_scale(x, width):
    return lax.rsqrt(jnp.sum(x * x, axis=-1, keepdims=True) * (1.0 / width) + EPS)


def _norm_mod(x, g, shift, scale):
    y = x * _rms_scale(x, x.shape[-1]) * g
    return y * (1.0 + scale) + shift


def _swiglu(xn, w1_ref, w3_ref, w2_ref):
    a = jnp.dot(xn, w1_ref[...], preferred_element_type=F32)
    b = jnp.dot(xn, w3_ref[...], preferred_element_type=F32)
    g = (a / (1.0 + jnp.exp(-a))) * b
    return jnp.dot(g.astype(BF16), w2_ref[...], preferred_element_type=F32)


def _rope(x, cos, sin_a, sin_b):
    return x * cos + pltpu.roll(x, LANES - 1, 1) * sin_a + pltpu.roll(x, 1, 1) * sin_b


def _low_lanes(shape):
    return lax.broadcasted_iota(jnp.int32, shape, len(shape) - 1) < HALF


def _mod_kernel(c_ref, w_ref, b_ref, o_ref):
    cv = c_ref[...]
    s = (cv / (1.0 + jnp.exp(-cv))).astype(BF16)
    o_ref[...] = jnp.dot(s, w_ref[...].astype(BF16), preferred_element_type=F32) + b_ref[...]


def _modulation(cc, w_ada, b_ada):
    n_layers, d, width = w_ada.shape
    return pl.pallas_call(
        _mod_kernel,
        grid=(n_layers, width // MOD_COL_TILE),
        in_specs=[
            pl.BlockSpec((MOD_ROWS, d), lambda l, j: (0, 0)),
            pl.BlockSpec((None, d, MOD_COL_TILE), lambda l, j: (l, 0, j)),
            pl.BlockSpec((None, 1, MOD_COL_TILE), lambda l, j: (l, 0, j)),
        ],
        out_specs=pl.BlockSpec((None, MOD_ROWS, MOD_COL_TILE), lambda l, j: (l, 0, j)),
        out_shape=jax.ShapeDtypeStruct((n_layers, MOD_ROWS, width), F32),
        compiler_params=pltpu.CompilerParams(
            dimension_semantics=("arbitrary", "arbitrary"), vmem_limit_bytes=VMEM_LIMIT),
        name="adaln_modulation",
    )(cc, w_ada, b_ada.reshape(n_layers, 1, width))


def _ffn_proj_kernel(*refs, n_src, n_tiles, tpb, n_lat):
    h_refs = refs[:n_src]
    (mod_ref, modp_ref, g1_ref, w1_ref, w3_ref, w2_ref, gmix_ref, win_ref,
     gcq_ref, wuq_ref, gmq_ref, gckv_ref, wuk_ref, wuv_ref, gmk_ref, gsq_ref, gsk_ref, dft_ref,
     cm_ref, sam_ref, sbm_ref, cs_ref, sas_ref, sbs_ref,
     ho_ref, q_ref, k_ref, v_ref, sq_ref, sk_ref, sv_ref, z_ref, hprev_ref) = refs[n_src:]
    step = pl.program_id(0)

    @pl.when(step == 0)
    def _():
        hprev_ref[...] = jnp.zeros_like(hprev_ref)

    modp = modp_ref[...]
    n = _norm_mod(hprev_ref[...], gmix_ref[...], modp[3:4], modp[4:5]).astype(BF16)
    u = jnp.dot(n, win_ref[...], preferred_element_type=F32)

    f = u[:, OFF_F:OFF_F + FOURIER_WIDTH].astype(BF16)
    z_ref[...] = jnp.dot(f, dft_ref[...], preferred_element_type=F32).astype(BF16)

    cm, sam, sbm = cm_ref[...], sam_ref[...], sbm_ref[...]
    cq = u[:, OFF_CQ:OFF_CQ + MLA_Q_RANK]
    cqn = (cq * _rms_scale(cq, MLA_Q_RANK) * gcq_ref[...]).astype(BF16)
    q = jnp.dot(cqn, wuq_ref[...], preferred_element_type=F32)
    gmq = gmq_ref[...]
    q_scale = MLA_QK_DIM ** -0.5 * LOG2E
    for hd in range(MLA_HEADS):
        qh = q[:, hd * LANES:(hd + 1) * LANES]
        qg = qh * (_rms_scale(qh, MLA_QK_DIM) * q_scale) * gmq
        q_ref[:, hd * LANES:(hd + 1) * LANES] = _rope(qg, cm, sam, sbm).astype(BF16)

    ckv = u[:, OFF_CKV:OFF_CKV + MLA_KV_RANK]
    ckvn = (ckv * _rms_scale(ckv, MLA_KV_RANK) * gckv_ref[...]).astype(BF16)
    kn = jnp.dot(ckvn, wuk_ref[...], preferred_element_type=F32)
    vv = jnp.dot(ckvn, wuv_ref[...], preferred_element_type=F32).astype(BF16)
    ones = jnp.ones((vv.shape[0], LANES), BF16)
    for pair in range(MLA_HEADS // 2):
        v_ref[:, 2 * pair * LANES:(2 * pair + 1) * LANES] = vv[:, pair * LANES:(pair + 1) * LANES]
        v_ref[:, (2 * pair + 1) * LANES:(2 * pair + 2) * LANES] = ones
    gmk = gmk_ref[...]
    kr = u[:, OFF_KR:OFF_KR + LANES]
    kr_ss = jnp.sum(kr * kr, axis=-1, keepdims=True)
    kr_rot = _rope(kr * gmk, cm, sam, sbm)
    for hd in range(MLA_HEADS):
        kh = kn[:, hd * LANES:(hd + 1) * LANES]
        ss = jnp.sum(kh * kh, axis=-1, keepdims=True) + kr_ss
        rs = lax.rsqrt(ss * (1.0 / MLA_QK_DIM) + EPS)
        k_ref[:, hd * LANES:(hd + 1) * LANES] = (rs * (kh * gmk + kr_rot)).astype(BF16)

    cs, sas, sbs = cs_ref[...], sas_ref[...], sbs_ref[...]

    def two_head_norm_rope(x, g, scale):
        low = _low_lanes(x.shape)
        x2 = x * x
        lo = jnp.sum(jnp.where(low, x2, 0.0), axis=-1, keepdims=True)
        hi = jnp.sum(jnp.where(low, 0.0, x2), axis=-1, keepdims=True)
        rs = jnp.where(low, lax.rsqrt(lo * (1.0 / SWA_HEAD_DIM) + EPS),
                       lax.rsqrt(hi * (1.0 / SWA_HEAD_DIM) + EPS))
        return _rope(x * (rs * scale) * g, cs, sas, sbs).astype(BF16)

    gsq = gsq_ref[...]
    s_scale = SWA_HEAD_DIM ** -0.5 * LOG2E
    sq_ref[:, 0:LANES] = two_head_norm_rope(u[:, OFF_SQA:OFF_SQA + LANES], gsq, s_scale)
    sq_ref[:, LANES:2 * LANES] = two_head_norm_rope(u[:, OFF_SQB:OFF_SQB + LANES], gsq, s_scale)
    sk_ref[...] = two_head_norm_rope(u[:, OFF_SK:OFF_SK + LANES], gsk_ref[...], 1.0)
    sv_ref[:, 0:LANES] = u[:, OFF_SV:OFF_SV + LANES].astype(BF16)
    sv_ref[:, LANES:2 * LANES] = ones

    mod = mod_ref[...]
    if n_src == 1:
        h = h_refs[0][...]
    else:
        cur = jnp.minimum(step, n_tiles - 1)
        h = jnp.where(cur % tpb >= n_lat, h_refs[1][...], h_refs[0][...])
    xn = _norm_mod(h, g1_ref[...], mod[0:1], mod[1:2]).astype(BF16)
    h = h + (0.5 * mod[2:3]) * _swiglu(xn, w1_ref, w3_ref, w2_ref)
    ho_ref[...] = h
    hprev_ref[...] = h


def _const_spec(block_shape, index_map):
    return pl.BlockSpec(block_shape, index_map, pipeline_mode=pl.Buffered(1))


def _ffn_proj(layer, tokens, mod, p, tabs, seq, t_all):
    n_batch, _, d = tokens[0].shape
    tm = TOK_TILE
    tpb = t_all // tm
    n_tiles = n_batch * tpb
    n_lat = seq // tm
    cur = lambda s: jnp.minimum(s, n_tiles - 1)
    prev = lambda s: jnp.maximum(s - 1, 0)
    tile = lambda which, width: pl.BlockSpec(
        (None, tm, width), lambda s: (which(s) // tpb, which(s) % tpb, 0))

    def token_specs(which):
        if len(tokens) == 1:
            return [tile(which, d)]
        lat = pl.BlockSpec((None, tm, d),
                           lambda s: (which(s) // tpb, jnp.minimum(which(s) % tpb, n_lat - 1), 0))
        con = pl.BlockSpec((None, tm, d),
                           lambda s: (which(s) // tpb, jnp.maximum(which(s) % tpb - n_lat, 0), 0))
        return [lat, con]

    def mod_row(which):
        def index(s):
            b, t = which(s) // tpb, which(s) % tpb
            return (layer, jnp.where(t >= n_lat, n_batch, b), 0, 0)
        return pl.BlockSpec((None, None, N_MOD, d), index)

    lw = lambda *shape: _const_spec((None,) + shape, lambda s: (layer,) + (0,) * len(shape))
    tab = pl.BlockSpec((tm, LANES), lambda s: (prev(s) % tpb, 0))
    in_specs = token_specs(cur) + [
        mod_row(cur), mod_row(prev),
        lw(1, d), lw(d, D_FF), lw(d, D_FF), lw(D_FF, d), lw(1, d), lw(d, IN_WIDTH_P),
        lw(1, MLA_Q_RANK), lw(MLA_Q_RANK, MLA_HEADS * LANES), lw(1, LANES),
        lw(1, MLA_KV_RANK), lw(MLA_KV_RANK, MLA_HEADS * LANES), lw(MLA_KV_RANK, MLA_HEADS * MLA_V),
        lw(1, LANES), lw(1, LANES), lw(1, LANES),
        _const_spec((FOURIER_WIDTH, 2 * FOURIER_WIDTH), lambda s: (0, 0)),
        tab, tab, tab, tab, tab, tab,
    ]
    widths = (d, MLA_HEADS * LANES, MLA_HEADS * LANES, MLA_V_SLABS * LANES, 2 * LANES, LANES, 2 * LANES,
              2 * FOURIER_WIDTH)
    dtypes = (F32,) + (BF16,) * 7
    return pl.pallas_call(
        functools.partial(_ffn_proj_kernel, n_src=len(tokens), n_tiles=n_tiles, tpb=tpb, n_lat=n_lat),
        grid=(n_tiles + 1,),
        in_specs=in_specs,
        out_specs=[tile(cur, d)] + [tile(prev, w) for w in widths[1:]],
        out_shape=[jax.ShapeDtypeStruct((n_batch, t_all, w), dt) for w, dt in zip(widths, dtypes)],
        scratch_shapes=[pltpu.VMEM((tm, d), F32)],
        compiler_params=pltpu.CompilerParams(
            dimension_semantics=("arbitrary",), vmem_limit_bytes=VMEM_LIMIT),
        name="ffn_half_and_projections",
    )(*tokens, mod, mod, p["g_ffn1"], p["w1_ffn1"], p["w3_ffn1"], p["w2_ffn1"], p["g_mix"], p["w_in"],
      p["g_cq"], p["w_uq"], p["g_mq"], p["g_ckv"], p["w_uk"], p["w_uv"], p["g_mk"],
      p["g_sq"], p["g_sk"], p["dft64"], *tabs)


def _fourier_kernel(z_ref, cl_ref, sl_ref, cc_ref, sc_ref, o_ref, *, ctx, with_ctx):
    w = FOURIER_WIDTH

    def mix(c_ref, s_ref, lo, n):
        zc = z_ref[lo:lo + n, 0:w]
        zs = z_ref[lo:lo + n, w:2 * w]
        return (jnp.dot(c_ref[...], zc, preferred_element_type=F32)
                - jnp.dot(s_ref[...], zs, preferred_element_type=F32)).astype(BF16)

    n_lat = cl_ref.shape[0]
    o_ref[0:n_lat, :] = mix(cl_ref, sl_ref, 0, n_lat)
    if with_ctx:
        o_ref[n_lat:n_lat + ctx, :] = mix(cc_ref, sc_ref, n_lat, ctx)


def _fourier(zcs, dfts, ctx, with_ctx):
    n_batch, t_all, _ = zcs.shape
    seq = t_all - ctx
    rows = t_all if with_ctx else seq
    full = lambda a: _const_spec(a.shape, lambda b: (0, 0))
    return pl.pallas_call(
        functools.partial(_fourier_kernel, ctx=ctx, with_ctx=with_ctx),
        grid=(n_batch,),
        in_specs=[pl.BlockSpec((None, t_all, 2 * FOURIER_WIDTH), lambda b: (b, 0, 0))]
        + [full(a) for a in dfts],
        out_specs=pl.BlockSpec((None, rows, FOURIER_WIDTH), lambda b: (b, 0, 0)),
        out_shape=jax.ShapeDtypeStruct((n_batch, rows, FOURIER_WIDTH), BF16),
        compiler_params=pltpu.CompilerParams(
            dimension_semantics=("arbitrary",), vmem_limit_bytes=VMEM_LIMIT),
        name="fourier_positions",
    )(zcs, *dfts)


def _scores(q, k):
    return lax.dot_general(q, k, (((1,), (1,)), ((), ())), preferred_element_type=F32)


def _mla_heads(q_ref, k_ref, v_ref, o_ref, head0_scores=None):
    n_heads = q_ref.shape[-1] // LANES
    for pair in range(n_heads // 2):
        outs = []
        vp = v_ref[:, 2 * pair * LANES:(2 * pair + 2) * LANES]
        for hd in (2 * pair, 2 * pair + 1):
            if hd == 0 and head0_scores is not None:
                s = head0_scores
            else:
                s = _scores(q_ref[:, hd * LANES:(hd + 1) * LANES], k_ref[:, hd * LANES:(hd + 1) * LANES])
            m = jnp.max(s, axis=-1, keepdims=True)
            p = jnp.exp2(s - m).astype(BF16)
            r = jnp.dot(p, vp, preferred_element_type=F32)
            outs.append(r[:, 0:LANES] / r[:, LANES:2 * LANES])
        o_ref[:, pair * LANES:(pair + 1) * LANES] = jnp.where(
            _low_lanes(outs[0].shape), outs[0], outs[1]).astype(BF16)


def _mla_latent_kernel(q_ref, k_ref, v_ref, qn_ref, kn_ref, o_ref, s0_ref):
    @pl.when(pl.program_id(0) == 0)
    def _():
        s0_ref[...] = _scores(q_ref[:, 0:LANES], k_ref[:, 0:LANES])

    _mla_heads(q_ref, k_ref, v_ref, o_ref, head0_scores=s0_ref[...])
    s0_ref[...] = _scores(qn_ref[...], kn_ref[...])


def _mla(q, k, v, seq, with_ctx):
    n_batch, t_all, _ = q.shape
    n_ctx = t_all - seq
    tq = MLA_Q_TILE
    tpb = seq // tq
    n_tiles = n_batch * tpb
    qk_w = MLA_HEADS * LANES
    v_in = MLA_V_SLABS * LANES
    v_w = MLA_HEADS * MLA_V
    params = lambda n: pltpu.CompilerParams(
        dimension_semantics=("arbitrary",) * n, vmem_limit_bytes=VMEM_LIMIT)
    nxt = lambda s: jnp.minimum(s + 1, n_tiles - 1)
    a = pl.pallas_call(
        _mla_latent_kernel,
        grid=(n_tiles,),
        in_specs=[
            pl.BlockSpec((None, tq, qk_w), lambda s: (s // tpb, s % tpb, 0)),
            pl.BlockSpec((None, t_all, qk_w), lambda s: (s // tpb, 0, 0)),
            pl.BlockSpec((None, t_all, v_in), lambda s: (s // tpb, 0, 0)),
            pl.BlockSpec((None, tq, LANES), lambda s: (nxt(s) // tpb, nxt(s) % tpb, 0)),
            pl.BlockSpec((None, t_all, LANES), lambda s: (nxt(s) // tpb, 0, 0)),
        ],
        out_specs=pl.BlockSpec((None, tq, v_w), lambda s: (s // tpb, s % tpb, 0)),
        out_shape=jax.ShapeDtypeStruct((n_batch, seq, v_w), BF16),
        scratch_shapes=[pltpu.VMEM((tq, t_all), F32)],
        compiler_params=params(1),
        name="mla_attention",
    )(q, k, v, q, k)
    if not with_ctx:
        return a, None
    c_blk = seq // n_ctx
    ctx_rows = lambda width: pl.BlockSpec((None, n_ctx, width), lambda b: (b, c_blk, 0))
    a_ctx = pl.pallas_call(
        _mla_heads,
        grid=(n_batch,),
        in_specs=[ctx_rows(qk_w), ctx_rows(qk_w), ctx_rows(v_in)],
        out_specs=pl.BlockSpec((None, n_ctx, v_w), lambda b: (b, 0, 0)),
        out_shape=jax.ShapeDtypeStruct((n_batch, n_ctx, v_w), BF16),
        compiler_params=params(1),
        name="mla_context_attention",
    )(q, k, v)
    return a, a_ctx


def _swa_heads(sink_ref, layer, q_ref, keys, vals, valid, o_ref, r0=0, rows=None):
    rows = q_ref.shape[0] if rows is None else rows
    res = []
    for hq in range(SWA_Q_HEADS):
        slab = hq % 2
        use_low = hq < 2
        qs = q_ref[r0:r0 + rows, slab * LANES:(slab + 1) * LANES]
        low = _low_lanes(qs.shape)
        qm = jnp.where(low if use_low else jnp.logical_not(low), qs, jnp.zeros_like(qs))
        sink = sink_ref[layer, hq] * LOG2E
        s = _scores(qm, keys)
        if valid is not None:
            s = jnp.where(valid, s, NEG)
        m = jnp.maximum(jnp.max(s, axis=-1, keepdims=True), sink)
        p = jnp.exp2(s - m).astype(BF16)
        r = jnp.dot(p, vals, preferred_element_type=F32)
        res.append(r[:, 0:LANES] / (r[:, LANES:2 * LANES] + jnp.exp2(sink - m)))
    low = _low_lanes(res[0].shape)
    o_ref[r0:r0 + rows, 0:LANES] = jnp.where(low, res[0], res[2]).astype(BF16)
    o_ref[r0:r0 + rows, LANES:2 * LANES] = jnp.where(low, res[1], res[3]).astype(BF16)


def _swa_latent_kernel(sink_ref, q_ref, k_ref, v_ref, o_ref, *, layer, seq):
    tq = q_ref.shape[0]
    rows = min(tq, ROW_CHUNK)
    span = rows + 2 * SWA_WINDOW
    n_keys = span + k_ref.shape[0] - seq
    kc, vc = k_ref[seq:, :], v_ref[seq:, :]
    for r0 in range(0, tq, rows):
        first = pl.program_id(1) * tq + r0
        start = pl.multiple_of(jnp.clip(first - SWA_WINDOW, 0, seq - span), LANES)
        keys = jnp.concatenate([k_ref[pl.ds(start, span), :], kc], axis=0)
        vals = jnp.concatenate([v_ref[pl.ds(start, span), :], vc], axis=0)
        col = lax.broadcasted_iota(jnp.int32, (rows, n_keys), 1)
        dist = first + lax.broadcasted_iota(jnp.int32, (rows, n_keys), 0) - (start + col)
        valid = ((dist <= SWA_WINDOW) & (dist >= -SWA_WINDOW)) | (col >= span)
        _swa_heads(sink_ref, layer, q_ref, keys, vals, valid, o_ref, r0, rows)


def _swa_ctx_kernel(sink_ref, q_ref, k_ref, v_ref, o_ref, *, layer):
    _swa_heads(sink_ref, layer, q_ref, k_ref[...], v_ref[...], None, o_ref)


def _swa(layer, sink, q, k, v, seq, with_ctx):
    n_batch, t_all, _ = q.shape
    n_ctx = t_all - seq
    tq = Q_TILE
    params = lambda n: pltpu.CompilerParams(
        dimension_semantics=("arbitrary",) * n, vmem_limit_bytes=VMEM_LIMIT)
    smem = pl.BlockSpec(memory_space=pltpu.SMEM)
    w = pl.pallas_call(
        functools.partial(_swa_latent_kernel, layer=layer, seq=seq),
        grid=(n_batch, seq // tq),
        in_specs=[
            smem,
            pl.BlockSpec((None, tq, 2 * LANES), lambda b, i: (b, i, 0)),
            pl.BlockSpec((None, t_all, LANES), lambda b, i: (b, 0, 0)),
            pl.BlockSpec((None, t_all, 2 * LANES), lambda b, i: (b, 0, 0)),
        ],
        out_specs=pl.BlockSpec((None, tq, 2 * LANES), lambda b, i: (b, i, 0)),
        out_shape=jax.ShapeDtypeStruct((n_batch, seq, 2 * LANES), BF16),
        compiler_params=params(2),
        name="window_attention",
    )(sink, q, k, v)
    if not with_ctx:
        return w, None
    c_blk = seq // n_ctx
    ctx_rows = lambda width: pl.BlockSpec((None, n_ctx, width), lambda b: (b, c_blk, 0))
    w_ctx = pl.pallas_call(
        functools.partial(_swa_ctx_kernel, layer=layer),
        grid=(n_batch,),
        in_specs=[smem, ctx_rows(2 * LANES), ctx_rows(LANES), ctx_rows(2 * LANES)],
        out_specs=pl.BlockSpec((None, n_ctx, 2 * LANES), lambda b: (b, 0, 0)),
        out_shape=jax.ShapeDtypeStruct((n_batch, n_ctx, 2 * LANES), BF16),
        compiler_params=params(1),
        name="window_context_attention",
    )(sink, q, k, v)
    return w, w_ctx


def _out_ffn_kernel(h_ref, yf_ref, *refs, parts, hpb, n_lat):
    half = TOK_TILE
    if hpb is None:
        a_ref, w_ref = refs[:2]
        a_all, w_all = a_ref[...], w_ref[...]
        refs = refs[2:]
    else:
        a_parts, w_parts = [], []
        for i in range(parts):
            a_lat, a_ctx, w_lat, w_ctx = refs[4 * i:4 * i + 4]
            is_ctx = (parts * pl.program_id(0) + i) % hpb >= n_lat
            a_parts.append(jnp.where(is_ctx, a_ctx[...], a_lat[...]))
            w_parts.append(jnp.where(is_ctx, w_ctx[...], w_lat[...]))
        a_all, w_all = jnp.concatenate(a_parts, axis=0), jnp.concatenate(w_parts, axis=0)
        refs = refs[4 * parts:]
    mod_refs = refs[:parts]
    wo_ref, g2_ref, w1_ref, w3_ref, w2_ref, o_ref = refs[parts:]
    n_f = yf_ref.shape[-1]
    n_a = a_all.shape[-1]
    mixed = (jnp.dot(yf_ref[...], wo_ref[0:n_f, :], preferred_element_type=F32)
             + jnp.dot(a_all, wo_ref[n_f:n_f + n_a, :], preferred_element_type=F32)
             + jnp.dot(w_all, wo_ref[n_f + n_a:, :], preferred_element_type=F32))
    g2 = g2_ref[...]
    hs, xs = [], []
    for i, mod_ref in enumerate(mod_refs):
        mod = mod_ref[...]
        rows = slice(i * half, (i + 1) * half)
        h = h_ref[rows, :] + mod[5:6] * mixed[rows, :]
        hs.append(h)
        xs.append(_norm_mod(h, g2, mod[6:7], mod[7:8]).astype(BF16))
    y = _swiglu(jnp.concatenate(xs, axis=0), w1_ref, w3_ref, w2_ref)
    for i, mod_ref in enumerate(mod_refs):
        rows = slice(i * half, (i + 1) * half)
        o_ref[rows, :] = hs[i] + (0.5 * mod_ref[8:9, :]) * y[rows, :]


def _out_ffn(layer, hh, yf, a, w, mod, p, seq, with_ctx):
    n_batch, t_all, d = hh.shape
    half = TOK_TILE
    parts = OUT_PARTS if with_ctx else OUT_PARTS_LATENT
    tm = parts * half
    hpb = t_all // half
    n_lat = seq // half
    lw = lambda *shape: _const_spec((None,) + shape, lambda *_: (layer,) + (0,) * len(shape))

    def mod_row(half_index):
        def index(*g):
            j = half_index(*g)
            return (layer, jnp.where(j % hpb >= n_lat, n_batch, j // hpb), 0, 0)
        return pl.BlockSpec((None, None, N_MOD, d), index)

    (a_lat, a_ctx), (w_lat, w_ctx) = a, w
    if with_ctx:
        flat = lambda t: t.reshape(n_batch * t_all, t.shape[-1])
        grid = (n_batch * t_all // tm,)
        rows = lambda width: pl.BlockSpec((tm, width), lambda j: (j, 0))
        part = lambda i: (lambda j: parts * j + i)
        lat = lambda i, width: pl.BlockSpec(
            (None, half, width), lambda j: (part(i)(j) // hpb, jnp.minimum(part(i)(j) % hpb, n_lat - 1), 0))
        con = lambda i, width: pl.BlockSpec(
            (None, half, width), lambda j: (part(i)(j) // hpb, jnp.maximum(part(i)(j) % hpb - n_lat, 0), 0))
        arrays, specs = [flat(hh), flat(yf)], [rows(d), rows(yf.shape[-1])]
        for i in range(parts):
            arrays += [a_lat, a_ctx, w_lat, w_ctx]
            specs += [lat(i, a_lat.shape[-1]), con(i, a_ctx.shape[-1]),
                      lat(i, w_lat.shape[-1]), con(i, w_ctx.shape[-1])]
        mods = [mod_row(part(i)) for i in range(parts)]
        out_shape = jax.ShapeDtypeStruct((n_batch * t_all, d), F32)
    else:
        grid = (n_batch, seq // tm)
        rows = lambda width: pl.BlockSpec((None, tm, width), lambda b, t: (b, t, 0))
        arrays = [hh, yf, a_lat, w_lat]
        specs = [rows(t.shape[-1]) for t in arrays]
        mods = [mod_row(lambda b, t: b * hpb)] * parts
        out_shape = jax.ShapeDtypeStruct((n_batch, seq, d), F32)
    out = pl.pallas_call(
        functools.partial(_out_ffn_kernel, parts=parts, hpb=hpb if with_ctx else None, n_lat=n_lat),
        grid=grid,
        in_specs=specs + mods + [lw(d, d), lw(1, d), lw(d, D_FF), lw(d, D_FF), lw(D_FF, d)],
        out_specs=rows(d),
        out_shape=out_shape,
        compiler_params=pltpu.CompilerParams(
            dimension_semantics=("arbitrary",) * len(grid), vmem_limit_bytes=VMEM_LIMIT),
        name="out_projection_and_ffn_half",
    )(*arrays, *([mod] * parts), p["w_out"], p["g_ffn2"], p["w1_ffn2"], p["w3_ffn2"], p["w2_ffn2"])
    return out.reshape(n_batch, t_all, d) if with_ctx else out


def _prepare_params(g_ffn1, w1_ffn1, w3_ffn1, w2_ffn1, g_mix, w_in, g_cq, w_uq, g_ckv, w_ukv,
                    g_mla_q, g_mla_k, g_swa_q, g_swa_k, w_out, g_ffn2, w1_ffn2, w3_ffn2, w2_ffn2):
    row = lambda g: g[:, None, :]
    two_heads = lambda g: jnp.concatenate([g, g], axis=-1)
    kv = w_ukv.reshape(w_ukv.shape[:-1] + (MLA_HEADS, MLA_NOPE + MLA_V))
    flat = lambda t: t.reshape(t.shape[:-2] + (t.shape[-2] * t.shape[-1],))
    return {
        "g_ffn1": row(g_ffn1), "w1_ffn1": w1_ffn1.astype(BF16), "w3_ffn1": w3_ffn1.astype(BF16),
        "w2_ffn1": w2_ffn1.astype(BF16),
        "g_mix": row(g_mix), "w_in": _take_cols(w_in, _w_in_cols()).astype(BF16),
        "g_cq": row(g_cq), "w_uq": _pad_heads(w_uq, MLA_HEADS).astype(BF16),
        "g_mq": row(_pad_heads(g_mla_q, 1)),
        "g_ckv": row(g_ckv), "w_uk": _pad_heads(flat(kv[..., :MLA_NOPE]), MLA_HEADS).astype(BF16),
        "w_uv": flat(kv[..., MLA_NOPE:]).astype(BF16),
        "g_mk": row(_pad_heads(g_mla_k, 1)),
        "g_sq": row(two_heads(g_swa_q)),
        "g_sk": row(two_heads(g_swa_k)),
        "w_out": _take_cols(w_out, _w_out_rows(), axis=1).astype(BF16),
        "g_ffn2": row(g_ffn2), "w1_ffn2": w1_ffn2.astype(BF16), "w3_ffn2": w3_ffn2.astype(BF16),
        "w2_ffn2": w2_ffn2.astype(BF16),
        "dft64": _channel_dft(),
    }


def kernel(x, c, ctx, c_ctx, w_ada, b_ada, g_ffn1, w1_ffn1, w3_ffn1, w2_ffn1, g_mix, w_in, g_cq, w_uq,
           g_ckv, w_ukv, g_mla_q, g_mla_k, g_swa_q, g_swa_k, sink, w_out, g_ffn2, w1_ffn2, w3_ffn2,
           w2_ffn2):
    n_batch, seq, d = x.shape
    n_ctx = ctx.shape[1]
    depth = w_ada.shape[0]
    assert d == D_MODEL and seq % GRID_W == 0 and n_batch + 1 <= MOD_ROWS
    assert n_ctx % TOK_TILE == 0 and seq % TOK_TILE == 0 and seq % n_ctx == 0
    assert seq % (OUT_PARTS_LATENT * TOK_TILE) == 0 and (n_batch * (seq + n_ctx)) % (OUT_PARTS * TOK_TILE) == 0
    assert seq % Q_TILE == 0 and seq % MLA_Q_TILE == 0
    assert Q_TILE % ROW_CHUNK == 0 and ROW_CHUNK + 2 * SWA_WINDOW <= seq
    assert w_ada.shape[-1] % MOD_COL_TILE == 0

    p = _prepare_params(g_ffn1, w1_ffn1, w3_ffn1, w2_ffn1, g_mix, w_in, g_cq, w_uq, g_ckv, w_ukv,
                        g_mla_q, g_mla_k, g_swa_q, g_swa_k, w_out, g_ffn2, w1_ffn2, w3_ffn2, w2_ffn2)
    tabs = _rope_tables(seq, n_ctx)
    c_lat, s_lat = _dft_cos_sin(seq, seq ** -0.5)
    c_ctx_dft, s_ctx_dft = _dft_cos_sin(n_ctx, n_ctx ** -0.5)
    dfts = tuple(m.astype(BF16) for m in (c_lat, s_lat, c_ctx_dft, s_ctx_dft))

    cc = jnp.concatenate([c, c_ctx[None, :], jnp.zeros((MOD_ROWS - n_batch - 1, d), F32)], axis=0)
    mod = _modulation(cc, w_ada, b_ada).reshape(depth, MOD_ROWS, N_MOD, d)

    tokens = (x, ctx)
    for layer in range(depth):
        with_ctx = layer != depth - 1
        hh, q, k, v, sq, sk, sv, zcs = _ffn_proj(layer, tokens, mod, p, tabs, seq, seq + n_ctx)
        yf = _fourier(zcs, dfts, n_ctx, with_ctx)
        a = _mla(q, k, v, seq, with_ctx)
        w = _swa(layer, sink, sq, sk, sv, seq, with_ctx)
        hh = _out_ffn(layer, hh, yf, a, w, mod, p, seq, with_ctx)
        tokens = (hh,)
    return hh
```

```python
import functools

import numpy as np
import jax
import jax.numpy as jnp
from jax import lax
from jax.experimental import pallas as pl
from jax.experimental.pallas import tpu as pltpu

F32 = jnp.float32
BF16 = jnp.bfloat16

D_MODEL = 1024
GRID_W = 64
ROPE_BASE = 10000.0
EPS = 1e-6
NEG = -1e30
LOG2E = 1.4426950408889634
N_MOD = 9
D_FF = 2816

FOURIER_WIDTH = 256
FOURIER_GROUP_DIM = 64
MLA_HEADS = 8
MLA_NOPE = 64
MLA_ROPE = 32
MLA_V = 64
MLA_QK_DIM = MLA_NOPE + MLA_ROPE
MLA_Q_RANK = 256
MLA_KV_RANK = 128
SWA_Q_HEADS = 4
SWA_KV_HEADS = 2
SWA_HEAD_DIM = 64
SWA_WINDOW = 128
IN_SPLITS = (256, 256, 128, 32, 256, 128, 128)
IN_WIDTH = sum(IN_SPLITS)

LANES = 128
HALF = LANES // 2
IN_WIDTH_P = 10 * LANES
MLA_V_SLABS = MLA_HEADS
VMEM_LIMIT = 56 * 1024 * 1024

TOK_TILE = 256
OUT_PARTS = 2
OUT_PARTS_LATENT = 2
Q_TILE = 1024
MLA_Q_TILE = 1024
ROW_CHUNK = 256
MOD_ROWS = 24
MOD_COL_TILE = 2304

OFF_F, OFF_CQ, OFF_CKV, OFF_KR, OFF_SQA, OFF_SQB, OFF_SK, OFF_SV = (
    0, 256, 512, 640, 768, 896, 1024, 1152)


def _w_in_cols():
    zero = IN_WIDTH
    o_f, o_cq, o_ckv, o_kr, o_sq, o_sk, o_sv = np.cumsum((0,) + IN_SPLITS)[:-1]
    cols = np.full((IN_WIDTH_P,), zero, np.int32)
    cols[OFF_F:OFF_F + 256] = o_f + np.arange(256)
    cols[OFF_CQ:OFF_CQ + 256] = o_cq + np.arange(256)
    cols[OFF_CKV:OFF_CKV + 128] = o_ckv + np.arange(128)
    cols[OFF_KR + MLA_NOPE:OFF_KR + MLA_QK_DIM] = o_kr + np.arange(MLA_ROPE)
    head = np.arange(SWA_HEAD_DIM)
    cols[OFF_SQA:OFF_SQA + 64] = o_sq + 0 * 64 + head
    cols[OFF_SQA + 64:OFF_SQA + 128] = o_sq + 2 * 64 + head
    cols[OFF_SQB:OFF_SQB + 64] = o_sq + 1 * 64 + head
    cols[OFF_SQB + 64:OFF_SQB + 128] = o_sq + 3 * 64 + head
    cols[OFF_SK:OFF_SK + 64] = o_sk + head
    cols[OFF_SK + 64:OFF_SK + 128] = o_sk + 64 + head
    cols[OFF_SV:OFF_SV + 128] = o_sv + np.arange(128)
    return cols


def _pad_heads(w, n_heads):
    lead, width = w.shape[:-1], w.shape[-1] // n_heads
    w = jnp.pad(w.reshape(lead + (n_heads, width)), [(0, 0)] * (len(lead) + 1) + [(0, LANES - width)])
    return w.reshape(lead + (n_heads * LANES,))


def _w_out_rows():
    base = FOURIER_WIDTH + MLA_HEADS * MLA_V
    swa = np.concatenate([base + h * SWA_HEAD_DIM + np.arange(SWA_HEAD_DIM) for h in (0, 2, 1, 3)])
    return np.concatenate([np.arange(base), swa]).astype(np.int32)


def _take_cols(w, cols, axis=-1):
    axis = axis % w.ndim
    n = w.shape[axis]
    cols = [int(c) for c in cols]
    pieces, i = [], 0
    while i < len(cols):
        j = i + 1
        if cols[i] == n:
            while j < len(cols) and cols[j] == n:
                j += 1
            shape = w.shape[:axis] + (j - i,) + w.shape[axis + 1:]
            pieces.append(jnp.zeros(shape, w.dtype))
        else:
            stride = cols[j] - cols[i] if j < len(cols) and cols[j] - cols[i] in (1, 2) else 1
            while j < len(cols) and cols[j] != n and cols[j] == cols[j - 1] + stride:
                j += 1
            pieces.append(lax.slice_in_dim(w, cols[i], cols[j - 1] + 1, stride, axis))
        i = j
    return jnp.concatenate(pieces, axis=axis)


def _rope_tables(seq, ctx):
    rows = seq // GRID_W
    pad = jnp.zeros((ctx,), F32)
    row = jnp.concatenate([jnp.repeat(jnp.arange(rows, dtype=F32), GRID_W), pad])[:, None]
    col = jnp.concatenate([jnp.tile(jnp.arange(GRID_W, dtype=F32), rows), pad])[:, None]

    def build(dim, section_starts):
        axis_dim = dim // 2
        n_freq = axis_dim // 2
        freq = np.zeros((LANES,), np.float32)
        active = np.zeros((LANES,), bool)
        by_row = np.zeros((LANES,), bool)
        m_a = np.zeros((LANES,), np.float32)
        m_b = np.zeros((LANES,), np.float32)
        pair = np.arange(dim) // 2
        for lo in section_starts:
            freq[lo:lo + dim] = 2 * (pair % n_freq)
            active[lo:lo + dim] = True
            by_row[lo:lo + dim] = pair < n_freq
            m_a[lo:lo + dim:2] = 1.0
            m_b[lo + 1:lo + dim:2] = 1.0
        inv_lane = jnp.where(jnp.asarray(active), ROPE_BASE ** (-jnp.asarray(freq) / axis_dim), 0.0)[None, :]
        ang = jnp.where(jnp.asarray(by_row)[None, :], row * inv_lane, col * inv_lane)
        sin = jnp.sin(ang)
        return jnp.cos(ang), sin * jnp.asarray(-m_a)[None, :], sin * jnp.asarray(m_b)[None, :]

    return build(MLA_ROPE, (MLA_NOPE,)) + build(SWA_HEAD_DIM, (0, HALF))


def _dft_cos_sin(n, scale):
    def direct(rows_j, n_mod):
        k = jnp.arange(n, dtype=jnp.int32)
        ang = ((rows_j[:, None] * k[None, :]) % n_mod).astype(F32) * (2.0 * np.pi / n_mod)
        return jnp.cos(ang), jnp.sin(ang)

    inner = FOURIER_GROUP_DIM
    if n <= inner or n % inner:
        c, s = direct(jnp.arange(n, dtype=jnp.int32), n)
        return c * scale, s * scale
    outer = n // inner
    ca, sa = direct(jnp.arange(outer, dtype=jnp.int32), outer)
    cb, sb = direct(jnp.arange(inner, dtype=jnp.int32), n)
    cb, sb = cb * scale, sb * scale
    c = ca[:, None, :] * cb[None, :, :] - sa[:, None, :] * sb[None, :, :]
    s = sa[:, None, :] * cb[None, :, :] + ca[:, None, :] * sb[None, :, :]
    return c.reshape(n, n), s.reshape(n, n)


def _channel_dft():
    c, s = _dft_cos_sin(FOURIER_GROUP_DIM, FOURIER_GROUP_DIM ** -0.5)
    eye = jnp.eye(FOURIER_WIDTH // FOURIER_GROUP_DIM, dtype=F32)
    return jnp.concatenate([jnp.kron(eye, c), jnp.kron(eye, s)], axis=1).astype(BF16)


def _rms_scale(x, width):
    return lax.rsqrt(jnp.sum(x * x, axis=-1, keepdims=True) * (1.0 / width) + EPS)


def _norm_mod(x, g, shift, scale):
    y = x * _rms_scale(x, x.shape[-1]) * g
    return y * (1.0 + scale) + shift


def _swiglu(xn, w1_ref, w3_ref, w2_ref):
    a = jnp.dot(xn, w1_ref[...], preferred_element_type=F32)
    b = jnp.dot(xn, w3_ref[...], preferred_element_type=F32)
    g = (a / (1.0 + jnp.exp(-a))) * b
    return jnp.dot(g.astype(BF16), w2_ref[...], preferred_element_type=F32)


def _rope(x, cos, sin_a, sin_b):
    return x * cos + pltpu.roll(x, LANES - 1, 1) * sin_a + pltpu.roll(x, 1, 1) * sin_b


def _low_lanes(shape):
    return lax.broadcasted_iota(jnp.int32, shape, len(shape) - 1) < HALF


def _mod_kernel(c_ref, w_ref, b_ref, o_ref):
    cv = c_ref[...]
    s = (cv / (1.0 + jnp.exp(-cv))).astype(BF16)
    o_ref[...] = jnp.dot(s, w_ref[...].astype(BF16), preferred_element_type=F32) + b_ref[...]


def _modulation(cc, w_ada, b_ada):
    n_layers, d, width = w_ada.shape
    return pl.pallas_call(
        _mod_kernel,
        grid=(n_layers, width // MOD_COL_TILE),
        in_specs=[
            pl.BlockSpec((MOD_ROWS, d), lambda l, j: (0, 0)),
            pl.BlockSpec((None, d, MOD_COL_TILE), lambda l, j: (l, 0, j)),
            pl.BlockSpec((None, 1, MOD_COL_TILE), lambda l, j: (l, 0, j)),
        ],
        out_specs=pl.BlockSpec((None, MOD_ROWS, MOD_COL_TILE), lambda l, j: (l, 0, j)),
        out_shape=jax.ShapeDtypeStruct((n_layers, MOD_ROWS, width), F32),
        compiler_params=pltpu.CompilerParams(
            dimension_semantics=("arbitrary", "arbitrary"), vmem_limit_bytes=VMEM_LIMIT),
        name="adaln_modulation",
    )(cc, w_ada, b_ada.reshape(n_layers, 1, width))


def _ffn_proj_kernel(*refs, n_src, n_tiles, tpb, n_lat):
    h_refs = refs[:n_src]
    (mod_ref, modp_ref, g1_ref, w1_ref, w3_ref, w2_ref, gmix_ref, win_ref,
     gcq_ref, wuq_ref, gmq_ref, gckv_ref, wuk_ref, wuv_ref, gmk_ref, gsq_ref, gsk_ref, dft_ref,
     cm_ref, sam_ref, sbm_ref, cs_ref, sas_ref, sbs_ref,
     ho_ref, q_ref, k_ref, v_ref, sq_ref, sk_ref, sv_ref, z_ref, hprev_ref) = refs[n_src:]
    step = pl.program_id(0)

    @pl.when(step == 0)
    def _():
        hprev_ref[...] = jnp.zeros_like(hprev_ref)

    modp = modp_ref[...]
    n = _norm_mod(hprev_ref[...], gmix_ref[...], modp[3:4], modp[4:5]).astype(BF16)
    u = jnp.dot(n, win_ref[...], preferred_element_type=F32)

    f = u[:, OFF_F:OFF_F + FOURIER_WIDTH].astype(BF16)
    z_ref[...] = jnp.dot(f, dft_ref[...], preferred_element_type=F32).astype(BF16)

    cm, sam, sbm = cm_ref[...], sam_ref[...], sbm_ref[...]
    cq = u[:, OFF_CQ:OFF_CQ + MLA_Q_RANK]
    cqn = (cq * _rms_scale(cq, MLA_Q_RANK) * gcq_ref[...]).astype(BF16)
    q = jnp.dot(cqn, wuq_ref[...], preferred_element_type=F32)
    gmq = gmq_ref[...]
    q_scale = MLA_QK_DIM ** -0.5 * LOG2E
    for hd in range(MLA_HEADS):
        qh = q[:, hd * LANES:(hd + 1) * LANES]
        qg = qh * (_rms_scale(qh, MLA_QK_DIM) * q_scale) * gmq
        q_ref[:, hd * LANES:(hd + 1) * LANES] = _rope(qg, cm, sam, sbm).astype(BF16)

    ckv = u[:, OFF_CKV:OFF_CKV + MLA_KV_RANK]
    ckvn = (ckv * _rms_scale(ckv, MLA_KV_RANK) * gckv_ref[...]).astype(BF16)
    kn = jnp.dot(ckvn, wuk_ref[...], preferred_element_type=F32)
    vv = jnp.dot(ckvn, wuv_ref[...], preferred_element_type=F32).astype(BF16)
    ones = jnp.ones((vv.shape[0], LANES), BF16)
    for pair in range(MLA_HEADS // 2):
        v_ref[:, 2 * pair * LANES:(2 * pair + 1) * LANES] = vv[:, pair * LANES:(pair + 1) * LANES]
        v_ref[:, (2 * pair + 1) * LANES:(2 * pair + 2) * LANES] = ones
    gmk = gmk_ref[...]
    kr = u[:, OFF_KR:OFF_KR + LANES]
    kr_ss = jnp.sum(kr * kr, axis=-1, keepdims=True)
    kr_rot = _rope(kr * gmk, cm, sam, sbm)
    for hd in range(MLA_HEADS):
        kh = kn[:, hd * LANES:(hd + 1) * LANES]
        ss = jnp.sum(kh * kh, axis=-1, keepdims=True) + kr_ss
        rs = lax.rsqrt(ss * (1.0 / MLA_QK_DIM) + EPS)
        k_ref[:, hd * LANES:(hd + 1) * LANES] = (rs * (kh * gmk + kr_rot)).astype(BF16)

    cs, sas, sbs = cs_ref[...], sas_ref[...], sbs_ref[...]

    def two_head_norm_rope(x, g, scale):
        low = _low_lanes(x.shape)
        x2 = x * x
        lo = jnp.sum(jnp.where(low, x2, 0.0), axis=-1, keepdims=True)
        hi = jnp.sum(jnp.where(low, 0.0, x2), axis=-1, keepdims=True)
        rs = jnp.where(low, lax.rsqrt(lo * (1.0 / SWA_HEAD_DIM) + EPS),
                       lax.rsqrt(hi * (1.0 / SWA_HEAD_DIM) + EPS))
        return _rope(x * (rs * scale) * g, cs, sas, sbs).astype(BF16)

    gsq = gsq_ref[...]
    s_scale = SWA_HEAD_DIM ** -0.5 * LOG2E
    sq_ref[:, 0:LANES] = two_head_norm_rope(u[:, OFF_SQA:OFF_SQA + LANES], gsq, s_scale)
    sq_ref[:, LANES:2 * LANES] = two_head_norm_rope(u[:, OFF_SQB:OFF_SQB + LANES], gsq, s_scale)
    sk_ref[...] = two_head_norm_rope(u[:, OFF_SK:OFF_SK + LANES], gsk_ref[...], 1.0)
    sv_ref[:, 0:LANES] = u[:, OFF_SV:OFF_SV + LANES].astype(BF16)
    sv_ref[:, LANES:2 * LANES] = ones

    mod = mod_ref[...]
    if n_src == 1:
        h = h_refs[0][...]
    else:
        cur = jnp.minimum(step, n_tiles - 1)
        h = jnp.where(cur % tpb >= n_lat, h_refs[1][...], h_refs[0][...])
    xn = _norm_mod(h, g1_ref[...], mod[0:1], mod[1:2]).astype(BF16)
    h = h + (0.5 * mod[2:3]) * _swiglu(xn, w1_ref, w3_ref, w2_ref)
    ho_ref[...] = h
    hprev_ref[...] = h


def _const_spec(block_shape, index_map):
    return pl.BlockSpec(block_shape, index_map, pipeline_mode=pl.Buffered(1))


def _ffn_proj(layer, tokens, mod, p, tabs, seq, t_all):
    n_batch, _, d = tokens[0].shape
    tm = TOK_TILE
    tpb = t_all // tm
    n_tiles = n_batch * tpb
    n_lat = seq // tm
    cur = lambda s: jnp.minimum(s, n_tiles - 1)
    prev = lambda s: jnp.maximum(s - 1, 0)
    tile = lambda which, width: pl.BlockSpec(
        (None, tm, width), lambda s: (which(s) // tpb, which(s) % tpb, 0))

    def token_specs(which):
        if len(tokens) == 1:
            return [tile(which, d)]
        lat = pl.BlockSpec((None, tm, d),
                           lambda s: (which(s) // tpb, jnp.minimum(which(s) % tpb, n_lat - 1), 0))
        con = pl.BlockSpec((None, tm, d),
                           lambda s: (which(s) // tpb, jnp.maximum(which(s) % tpb - n_lat, 0), 0))
        return [lat, con]

    def mod_row(which):
        def index(s):
            b, t = which(s) // tpb, which(s) % tpb
            return (layer, jnp.where(t >= n_lat, n_batch, b), 0, 0)
        return pl.BlockSpec((None, None, N_MOD, d), index)

    lw = lambda *shape: _const_spec((None,) + shape, lambda s: (layer,) + (0,) * len(shape))
    tab = pl.BlockSpec((tm, LANES), lambda s: (prev(s) % tpb, 0))
    in_specs = token_specs(cur) + [
        mod_row(cur), mod_row(prev),
        lw(1, d), lw(d, D_FF), lw(d, D_FF), lw(D_FF, d), lw(1, d), lw(d, IN_WIDTH_P),
        lw(1, MLA_Q_RANK), lw(MLA_Q_RANK, MLA_HEADS * LANES), lw(1, LANES),
        lw(1, MLA_KV_RANK), lw(MLA_KV_RANK, MLA_HEADS * LANES), lw(MLA_KV_RANK, MLA_HEADS * MLA_V),
        lw(1, LANES), lw(1, LANES), lw(1, LANES),
        _const_spec((FOURIER_WIDTH, 2 * FOURIER_WIDTH), lambda s: (0, 0)),
        tab, tab, tab, tab, tab, tab,
    ]
    widths = (d, MLA_HEADS * LANES, MLA_HEADS * LANES, MLA_V_SLABS * LANES, 2 * LANES, LANES, 2 * LANES,
              2 * FOURIER_WIDTH)
    dtypes = (F32,) + (BF16,) * 7
    return pl.pallas_call(
        functools.partial(_ffn_proj_kernel, n_src=len(tokens), n_tiles=n_tiles, tpb=tpb, n_lat=n_lat),
        grid=(n_tiles + 1,),
        in_specs=in_specs,
        out_specs=[tile(cur, d)] + [tile(prev, w) for w in widths[1:]],
        out_shape=[jax.ShapeDtypeStruct((n_batch, t_all, w), dt) for w, dt in zip(widths, dtypes)],
        scratch_shapes=[pltpu.VMEM((tm, d), F32)],
        compiler_params=pltpu.CompilerParams(
            dimension_semantics=("arbitrary",), vmem_limit_bytes=VMEM_LIMIT),
        name="ffn_half_and_projections",
    )(*tokens, mod, mod, p["g_ffn1"], p["w1_ffn1"], p["w3_ffn1"], p["w2_ffn1"], p["g_mix"], p["w_in"],
      p["g_cq"], p["w_uq"], p["g_mq"], p["g_ckv"], p["w_uk"], p["w_uv"], p["g_mk"],
      p["g_sq"], p["g_sk"], p["dft64"], *tabs)


def _scores(q, k):
    return lax.dot_general(q, k, (((1,), (1,)), ((), ())), preferred_element_type=F32)


def _mla_heads(q_ref, k_ref, v_ref, o_ref, head0_scores=None):
    n_heads = q_ref.shape[-1] // LANES
    for pair in range(n_heads // 2):
        outs = []
        vp = v_ref[:, 2 * pair * LANES:(2 * pair + 2) * LANES]
        for hd in (2 * pair, 2 * pair + 1):
            if hd == 0 and head0_scores is not None:
                s = head0_scores
            else:
                s = _scores(q_ref[:, hd * LANES:(hd + 1) * LANES], k_ref[:, hd * LANES:(hd + 1) * LANES])
            m = jnp.max(s, axis=-1, keepdims=True)
            p = jnp.exp2(s - m).astype(BF16)
            r = jnp.dot(p, vp, preferred_element_type=F32)
            outs.append(r[:, 0:LANES] / r[:, LANES:2 * LANES])
        o_ref[:, pair * LANES:(pair + 1) * LANES] = jnp.where(
            _low_lanes(outs[0].shape), outs[0], outs[1]).astype(BF16)


def _mla_latent_kernel(q_ref, k_ref, v_ref, qn_ref, kn_ref, o_ref, s0_ref):
    @pl.when(pl.program_id(0) == 0)
    def _():
        s0_ref[...] = _scores(q_ref[:, 0:LANES], k_ref[:, 0:LANES])

    _mla_heads(q_ref, k_ref, v_ref, o_ref, head0_scores=s0_ref[...])
    s0_ref[...] = _scores(qn_ref[...], kn_ref[...])


def _mla(q, k, v, seq, with_ctx):
    n_batch, t_all, _ = q.shape
    n_ctx = t_all - seq
    tq = MLA_Q_TILE
    tpb = seq // tq
    n_tiles = n_batch * tpb
    qk_w = MLA_HEADS * LANES
    v_in = MLA_V_SLABS * LANES
    v_w = MLA_HEADS * MLA_V
    params = lambda n: pltpu.CompilerParams(
        dimension_semantics=("arbitrary",) * n, vmem_limit_bytes=VMEM_LIMIT)
    nxt = lambda s: jnp.minimum(s + 1, n_tiles - 1)
    a = pl.pallas_call(
        _mla_latent_kernel,
        grid=(n_tiles,),
        in_specs=[
            pl.BlockSpec((None, tq, qk_w), lambda s: (s // tpb, s % tpb, 0)),
            pl.BlockSpec((None, t_all, qk_w), lambda s: (s // tpb, 0, 0)),
            pl.BlockSpec((None, t_all, v_in), lambda s: (s // tpb, 0, 0)),
            pl.BlockSpec((None, tq, LANES), lambda s: (nxt(s) // tpb, nxt(s) % tpb, 0)),
            pl.BlockSpec((None, t_all, LANES), lambda s: (nxt(s) // tpb, 0, 0)),
        ],
        out_specs=pl.BlockSpec((None, tq, v_w), lambda s: (s // tpb, s % tpb, 0)),
        out_shape=jax.ShapeDtypeStruct((n_batch, seq, v_w), BF16),
        scratch_shapes=[pltpu.VMEM((tq, t_all), F32)],
        compiler_params=params(1),
        name="mla_attention",
    )(q, k, v, q, k)
    if not with_ctx:
        return a, None
    c_blk = seq // n_ctx
    ctx_rows = lambda width: pl.BlockSpec((None, n_ctx, width), lambda b: (b, c_blk, 0))
    a_ctx = pl.pallas_call(
        _mla_heads,
        grid=(n_batch,),
        in_specs=[ctx_rows(qk_w), ctx_rows(qk_w), ctx_rows(v_in)],
        out_specs=pl.BlockSpec((None, n_ctx, v_w), lambda b: (b, 0, 0)),
        out_shape=jax.ShapeDtypeStruct((n_batch, n_ctx, v_w), BF16),
        compiler_params=params(1),
        name="mla_context_attention",
    )(q, k, v)
    return a, a_ctx


def _swa_heads(sink_ref, layer, q_ref, keys, vals, valid, o_ref, r0=0, rows=None):
    rows = q_ref.shape[0] if rows is None else rows
    res = []
    for hq in range(SWA_Q_HEADS):
        slab = hq % 2
        use_low = hq < 2
        qs = q_ref[r0:r0 + rows, slab * LANES:(slab + 1) * LANES]
        low = _low_lanes(qs.shape)
        qm = jnp.where(low if use_low else jnp.logical_not(low), qs, jnp.zeros_like(qs))
        sink = sink_ref[layer, hq] * LOG2E
        s = _scores(qm, keys)
        if valid is not None:
            s = jnp.where(valid, s, NEG)
        m = jnp.maximum(jnp.max(s, axis=-1, keepdims=True), sink)
        p = jnp.exp2(s - m).astype(BF16)
        r = jnp.dot(p, vals, preferred_element_type=F32)
        res.append(r[:, 0:LANES] / (r[:, LANES:2 * LANES] + jnp.exp2(sink - m)))
    low = _low_lanes(res[0].shape)
    o_ref[r0:r0 + rows, 0:LANES] = jnp.where(low, res[0], res[2]).astype(BF16)
    o_ref[r0:r0 + rows, LANES:2 * LANES] = jnp.where(low, res[1], res[3]).astype(BF16)


def _dft_rows(c_rows, s_rows, z_ref, lo, n):
    w = FOURIER_WIDTH
    return (jnp.dot(c_rows, z_ref[lo:lo + n, 0:w], preferred_element_type=F32)
            - jnp.dot(s_rows, z_ref[lo:lo + n, w:2 * w], preferred_element_type=F32)).astype(BF16)


def _swa_latent_kernel(sink_ref, q_ref, k_ref, v_ref, z_ref, cl_ref, sl_ref, o_ref, yf_ref, *, layer, seq):
    tq = q_ref.shape[0]
    row0 = pl.multiple_of(pl.program_id(1) * tq, tq)
    yf_ref[...] = _dft_rows(cl_ref[pl.ds(row0, tq), :], sl_ref[pl.ds(row0, tq), :], z_ref, 0, seq)
    rows = min(tq, ROW_CHUNK)
    span = rows + 2 * SWA_WINDOW
    n_keys = span + k_ref.shape[0] - seq
    kc, vc = k_ref[seq:, :], v_ref[seq:, :]
    for r0 in range(0, tq, rows):
        first = pl.program_id(1) * tq + r0
        start = pl.multiple_of(jnp.clip(first - SWA_WINDOW, 0, seq - span), LANES)
        keys = jnp.concatenate([k_ref[pl.ds(start, span), :], kc], axis=0)
        vals = jnp.concatenate([v_ref[pl.ds(start, span), :], vc], axis=0)
        col = lax.broadcasted_iota(jnp.int32, (rows, n_keys), 1)
        dist = first + lax.broadcasted_iota(jnp.int32, (rows, n_keys), 0) - (start + col)
        valid = ((dist <= SWA_WINDOW) & (dist >= -SWA_WINDOW)) | (col >= span)
        _swa_heads(sink_ref, layer, q_ref, keys, vals, valid, o_ref, r0, rows)


def _swa_ctx_kernel(sink_ref, q_ref, k_ref, v_ref, z_ref, cc_ref, sc_ref, o_ref, yf_ref, *, layer):
    yf_ref[...] = _dft_rows(cc_ref[...], sc_ref[...], z_ref, 0, z_ref.shape[0])
    _swa_heads(sink_ref, layer, q_ref, k_ref[...], v_ref[...], None, o_ref)


def _swa_fourier(layer, sink, q, k, v, zcs, dfts, seq, with_ctx):
    n_batch, t_all, _ = q.shape
    n_ctx = t_all - seq
    tq = Q_TILE
    c_lat, s_lat, c_ctx, s_ctx = dfts
    zw = zcs.shape[-1]
    params = lambda n: pltpu.CompilerParams(
        dimension_semantics=("arbitrary",) * n, vmem_limit_bytes=VMEM_LIMIT)
    smem = pl.BlockSpec(memory_space=pltpu.SMEM)
    w, yf = pl.pallas_call(
        functools.partial(_swa_latent_kernel, layer=layer, seq=seq),
        grid=(n_batch, seq // tq),
        in_specs=[
            smem,
            pl.BlockSpec((None, tq, 2 * LANES), lambda b, i: (b, i, 0)),
            pl.BlockSpec((None, t_all, LANES), lambda b, i: (b, 0, 0)),
            pl.BlockSpec((None, t_all, 2 * LANES), lambda b, i: (b, 0, 0)),
            pl.BlockSpec((None, t_all, zw), lambda b, i: (b, 0, 0)),
            _const_spec(c_lat.shape, lambda b, i: (0, 0)),
            _const_spec(s_lat.shape, lambda b, i: (0, 0)),
        ],
        out_specs=[pl.BlockSpec((None, tq, 2 * LANES), lambda b, i: (b, i, 0)),
                   pl.BlockSpec((None, tq, FOURIER_WIDTH), lambda b, i: (b, i, 0))],
        out_shape=[jax.ShapeDtypeStruct((n_batch, seq, 2 * LANES), BF16),
                   jax.ShapeDtypeStruct((n_batch, seq, FOURIER_WIDTH), BF16)],
        compiler_params=params(2),
        name="window_attention_and_position_dft",
    )(sink, q, k, v, zcs, c_lat, s_lat)
    if not with_ctx:
        return (w, None), (yf, None)
    c_blk = seq // n_ctx
    ctx_rows = lambda width: pl.BlockSpec((None, n_ctx, width), lambda b: (b, c_blk, 0))
    ctx_out = lambda width: pl.BlockSpec((None, n_ctx, width), lambda b: (b, 0, 0))
    w_ctx, yf_ctx = pl.pallas_call(
        functools.partial(_swa_ctx_kernel, layer=layer),
        grid=(n_batch,),
        in_specs=[smem, ctx_rows(2 * LANES), ctx_rows(LANES), ctx_rows(2 * LANES), ctx_rows(zw),
                  _const_spec(c_ctx.shape, lambda b: (0, 0)), _const_spec(s_ctx.shape, lambda b: (0, 0))],
        out_specs=[ctx_out(2 * LANES), ctx_out(FOURIER_WIDTH)],
        out_shape=[jax.ShapeDtypeStruct((n_batch, n_ctx, 2 * LANES), BF16),
                   jax.ShapeDtypeStruct((n_batch, n_ctx, FOURIER_WIDTH), BF16)],
        compiler_params=params(1),
        name="window_context_attention_and_dft",
    )(sink, q, k, v, zcs, c_ctx, s_ctx)
    return (w, w_ctx), (yf, yf_ctx)


def _out_ffn_kernel(h_ref, *refs, parts, hpb, n_lat):
    half = TOK_TILE
    n_mix = 3
    if hpb is None:
        mixers = [r[...] for r in refs[:n_mix]]
        refs = refs[n_mix:]
    else:
        per_part = []
        for i in range(parts):
            srcs = refs[2 * n_mix * i:2 * n_mix * (i + 1)]
            is_ctx = (parts * pl.program_id(0) + i) % hpb >= n_lat
            per_part.append([jnp.where(is_ctx, srcs[2 * m + 1][...], srcs[2 * m][...])
                             for m in range(n_mix)])
        mixers = [jnp.concatenate([pp[m] for pp in per_part], axis=0) for m in range(n_mix)]
        refs = refs[2 * n_mix * parts:]
    yf_all, a_all, w_all = mixers
    mod_refs = refs[:parts]
    wo_ref, g2_ref, w1_ref, w3_ref, w2_ref, o_ref = refs[parts:]
    n_f = yf_all.shape[-1]
    n_a = a_all.shape[-1]
    mixed = (jnp.dot(yf_all, wo_ref[0:n_f, :], preferred_element_type=F32)
             + jnp.dot(a_all, wo_ref[n_f:n_f + n_a, :], preferred_element_type=F32)
             + jnp.dot(w_all, wo_ref[n_f + n_a:, :], preferred_element_type=F32))
    g2 = g2_ref[...]
    hs, xs = [], []
    for i, mod_ref in enumerate(mod_refs):
        mod = mod_ref[...]
        rows = slice(i * half, (i + 1) * half)
        h = h_ref[rows, :] + mod[5:6] * mixed[rows, :]
        hs.append(h)
        xs.append(_norm_mod(h, g2, mod[6:7], mod[7:8]).astype(BF16))
    y = _swiglu(jnp.concatenate(xs, axis=0), w1_ref, w3_ref, w2_ref)
    for i, mod_ref in enumerate(mod_refs):
        rows = slice(i * half, (i + 1) * half)
        o_ref[rows, :] = hs[i] + (0.5 * mod_ref[8:9, :]) * y[rows, :]


def _out_ffn(layer, hh, yf, a, w, mod, p, seq, with_ctx):
    n_batch, t_all, d = hh.shape
    half = TOK_TILE
    parts = OUT_PARTS if with_ctx else OUT_PARTS_LATENT
    tm = parts * half
    hpb = t_all // half
    n_lat = seq // half
    lw = lambda *shape: _const_spec((None,) + shape, lambda *_: (layer,) + (0,) * len(shape))

    def mod_row(half_index):
        def index(*g):
            j = half_index(*g)
            return (layer, jnp.where(j % hpb >= n_lat, n_batch, j // hpb), 0, 0)
        return pl.BlockSpec((None, None, N_MOD, d), index)

    mixers = (yf, a, w)
    if with_ctx:
        flat = lambda t: t.reshape(n_batch * t_all, t.shape[-1])
        grid = (n_batch * t_all // tm,)
        rows = lambda width: pl.BlockSpec((tm, width), lambda j: (j, 0))
        part = lambda i: (lambda j: parts * j + i)
        lat = lambda i, width: pl.BlockSpec(
            (None, half, width), lambda j: (part(i)(j) // hpb, jnp.minimum(part(i)(j) % hpb, n_lat - 1), 0))
        con = lambda i, width: pl.BlockSpec(
            (None, half, width), lambda j: (part(i)(j) // hpb, jnp.maximum(part(i)(j) % hpb - n_lat, 0), 0))
        arrays, specs = [flat(hh)], [rows(d)]
        for i in range(parts):
            for m_lat, m_ctx in mixers:
                arrays += [m_lat, m_ctx]
                specs += [lat(i, m_lat.shape[-1]), con(i, m_ctx.shape[-1])]
        mods = [mod_row(part(i)) for i in range(parts)]
        out_shape = jax.ShapeDtypeStruct((n_batch * t_all, d), F32)
    else:
        grid = (n_batch, seq // tm)
        rows = lambda width: pl.BlockSpec((None, tm, width), lambda b, t: (b, t, 0))
        arrays = [hh] + [m_lat for m_lat, _ in mixers]
        specs = [rows(t.shape[-1]) for t in arrays]
        mods = [mod_row(lambda b, t: b * hpb)] * parts
        out_shape = jax.ShapeDtypeStruct((n_batch, seq, d), F32)
    out = pl.pallas_call(
        functools.partial(_out_ffn_kernel, parts=parts, hpb=hpb if with_ctx else None, n_lat=n_lat),
        grid=grid,
        in_specs=specs + mods + [lw(d, d), lw(1, d), lw(d, D_FF), lw(d, D_FF), lw(D_FF, d)],
        out_specs=rows(d),
        out_shape=out_shape,
        compiler_params=pltpu.CompilerParams(
            dimension_semantics=("arbitrary",) * len(grid), vmem_limit_bytes=VMEM_LIMIT),
        name="out_projection_and_ffn_half",
    )(*arrays, *([mod] * parts), p["w_out"], p["g_ffn2"], p["w1_ffn2"], p["w3_ffn2"], p["w2_ffn2"])
    return out.reshape(n_batch, t_all, d) if with_ctx else out


def _prepare_params(g_ffn1, w1_ffn1, w3_ffn1, w2_ffn1, g_mix, w_in, g_cq, w_uq, g_ckv, w_ukv,
                    g_mla_q, g_mla_k, g_swa_q, g_swa_k, w_out, g_ffn2, w1_ffn2, w3_ffn2, w2_ffn2):
    row = lambda g: g[:, None, :]
    two_heads = lambda g: jnp.concatenate([g, g], axis=-1)
    kv = w_ukv.reshape(w_ukv.shape[:-1] + (MLA_HEADS, MLA_NOPE + MLA_V))
    flat = lambda t: t.reshape(t.shape[:-2] + (t.shape[-2] * t.shape[-1],))
    return {
        "g_ffn1": row(g_ffn1), "w1_ffn1": w1_ffn1.astype(BF16), "w3_ffn1": w3_ffn1.astype(BF16),
        "w2_ffn1": w2_ffn1.astype(BF16),
        "g_mix": row(g_mix), "w_in": _take_cols(w_in, _w_in_cols()).astype(BF16),
        "g_cq": row(g_cq), "w_uq": _pad_heads(w_uq, MLA_HEADS).astype(BF16),
        "g_mq": row(_pad_heads(g_mla_q, 1)),
        "g_ckv": row(g_ckv), "w_uk": _pad_heads(flat(kv[..., :MLA_NOPE]), MLA_HEADS).astype(BF16),
        "w_uv": flat(kv[..., MLA_NOPE:]).astype(BF16),
        "g_mk": row(_pad_heads(g_mla_k, 1)),
        "g_sq": row(two_heads(g_swa_q)),
        "g_sk": row(two_heads(g_swa_k)),
        "w_out": _take_cols(w_out, _w_out_rows(), axis=1).astype(BF16),
        "g_ffn2": row(g_ffn2), "w1_ffn2": w1_ffn2.astype(BF16), "w3_ffn2": w3_ffn2.astype(BF16),
        "w2_ffn2": w2_ffn2.astype(BF16),
        "dft64": _channel_dft(),
    }


def kernel(x, c, ctx, c_ctx, w_ada, b_ada, g_ffn1, w1_ffn1, w3_ffn1, w2_ffn1, g_mix, w_in, g_cq, w_uq,
           g_ckv, w_ukv, g_mla_q, g_mla_k, g_swa_q, g_swa_k, sink, w_out, g_ffn2, w1_ffn2, w3_ffn2,
           w2_ffn2):
    n_batch, seq, d = x.shape
    n_ctx = ctx.shape[1]
    depth = w_ada.shape[0]
    assert d == D_MODEL and seq % GRID_W == 0 and n_batch + 1 <= MOD_ROWS
    assert n_ctx % TOK_TILE == 0 and seq % TOK_TILE == 0 and seq % n_ctx == 0
    assert seq % (OUT_PARTS_LATENT * TOK_TILE) == 0 and (n_batch * (seq + n_ctx)) % (OUT_PARTS * TOK_TILE) == 0
    assert seq % Q_TILE == 0 and seq % MLA_Q_TILE == 0
    assert Q_TILE % ROW_CHUNK == 0 and ROW_CHUNK + 2 * SWA_WINDOW <= seq
    assert w_ada.shape[-1] % MOD_COL_TILE == 0

    p = _prepare_params(g_ffn1, w1_ffn1, w3_ffn1, w2_ffn1, g_mix, w_in, g_cq, w_uq, g_ckv, w_ukv,
                        g_mla_q, g_mla_k, g_swa_q, g_swa_k, w_out, g_ffn2, w1_ffn2, w3_ffn2, w2_ffn2)
    tabs = _rope_tables(seq, n_ctx)
    c_lat, s_lat = _dft_cos_sin(seq, seq ** -0.5)
    c_ctx_dft, s_ctx_dft = _dft_cos_sin(n_ctx, n_ctx ** -0.5)
    dfts = tuple(m.astype(BF16) for m in (c_lat, s_lat, c_ctx_dft, s_ctx_dft))

    cc = jnp.concatenate([c, c_ctx[None, :], jnp.zeros((MOD_ROWS - n_batch - 1, d), F32)], axis=0)
    mod = _modulation(cc, w_ada, b_ada).reshape(depth, MOD_ROWS, N_MOD, d)

    tokens = (x, ctx)
    for layer in range(depth):
        with_ctx = layer != depth - 1
        hh, q, k, v, sq, sk, sv, zcs = _ffn_proj(layer, tokens, mod, p, tabs, seq, seq + n_ctx)
        a = _mla(q, k, v, seq, with_ctx)
        w, yf = _swa_fourier(layer, sink, sq, sk, sv, zcs, dfts, seq, with_ctx)
        hh = _out_ffn(layer, hh, yf, a, w, mod, p, seq, with_ctx)
        tokens = (hh,)
    return hh
```

```python
import functools

import numpy as np
import jax
import jax.numpy as jnp
from jax import lax
from jax.experimental import pallas as pl
from jax.experimental.pallas import tpu as pltpu

F32 = jnp.float32
BF16 = jnp.bfloat16

D_MODEL = 1024
GRID_W = 64
ROPE_BASE = 10000.0
EPS = 1e-6
NEG = -1e30
LOG2E = 1.4426950408889634
N_MOD = 9
D_FF = 2816

FOURIER_WIDTH = 256
FOURIER_GROUP_DIM = 64
MLA_HEADS = 8
MLA_NOPE = 64
MLA_ROPE = 32
MLA_V = 64
MLA_QK_DIM = MLA_NOPE + MLA_ROPE
MLA_Q_RANK = 256
MLA_KV_RANK = 128
SWA_Q_HEADS = 4
SWA_KV_HEADS = 2
SWA_HEAD_DIM = 64
SWA_WINDOW = 128
IN_SPLITS = (256, 256, 128, 32, 256, 128, 128)
IN_WIDTH = sum(IN_SPLITS)

LANES = 128
HALF = LANES // 2
IN_WIDTH_P = 10 * LANES
MLA_V_SLABS = MLA_HEADS
VMEM_LIMIT = 56 * 1024 * 1024

TOK_TILE = 256
OUT_PARTS = 2
OUT_PARTS_LATENT = 2
Q_TILE = 1024
MLA_Q_TILE = 1024
ROW_CHUNK = 256
MOD_ROWS = 24
MOD_COL_TILE = 2304

OFF_F, OFF_CQ, OFF_CKV, OFF_KR, OFF_SQA, OFF_SQB, OFF_SK, OFF_SV = (
    0, 256, 512, 640, 768, 896, 1024, 1152)


def _w_in_cols():
    zero = IN_WIDTH
    o_f, o_cq, o_ckv, o_kr, o_sq, o_sk, o_sv = np.cumsum((0,) + IN_SPLITS)[:-1]
    cols = np.full((IN_WIDTH_P,), zero, np.int32)
    cols[OFF_F:OFF_F + 256] = o_f + np.arange(256)
    cols[OFF_CQ:OFF_CQ + 256] = o_cq + np.arange(256)
    cols[OFF_CKV:OFF_CKV + 128] = o_ckv + np.arange(128)
    cols[OFF_KR + MLA_NOPE:OFF_KR + MLA_QK_DIM] = o_kr + np.arange(MLA_ROPE)
    head = np.arange(SWA_HEAD_DIM)
    cols[OFF_SQA:OFF_SQA + 64] = o_sq + 0 * 64 + head
    cols[OFF_SQA + 64:OFF_SQA + 128] = o_sq + 2 * 64 + head
    cols[OFF_SQB:OFF_SQB + 64] = o_sq + 1 * 64 + head
    cols[OFF_SQB + 64:OFF_SQB + 128] = o_sq + 3 * 64 + head
    cols[OFF_SK:OFF_SK + 64] = o_sk + head
    cols[OFF_SK + 64:OFF_SK + 128] = o_sk + 64 + head
    cols[OFF_SV:OFF_SV + 128] = o_sv + np.arange(128)
    return cols


def _pad_heads(w, n_heads):
    lead, width = w.shape[:-1], w.shape[-1] // n_heads
    w = jnp.pad(w.reshape(lead + (n_heads, width)), [(0, 0)] * (len(lead) + 1) + [(0, LANES - width)])
    return w.reshape(lead + (n_heads * LANES,))


def _w_out_rows():
    base = FOURIER_WIDTH + MLA_HEADS * MLA_V
    swa = np.concatenate([base + h * SWA_HEAD_DIM + np.arange(SWA_HEAD_DIM) for h in (0, 2, 1, 3)])
    return np.concatenate([np.arange(base), swa]).astype(np.int32)


def _take_cols(w, cols, axis=-1):
    axis = axis % w.ndim
    n = w.shape[axis]
    cols = [int(c) for c in cols]
    pieces, i = [], 0
    while i < len(cols):
        j = i + 1
        if cols[i] == n:
            while j < len(cols) and cols[j] == n:
                j += 1
            shape = w.shape[:axis] + (j - i,) + w.shape[axis + 1:]
            pieces.append(jnp.zeros(shape, w.dtype))
        else:
            stride = cols[j] - cols[i] if j < len(cols) and cols[j] - cols[i] in (1, 2) else 1
            while j < len(cols) and cols[j] != n and cols[j] == cols[j - 1] + stride:
                j += 1
            pieces.append(lax.slice_in_dim(w, cols[i], cols[j - 1] + 1, stride, axis))
        i = j
    return jnp.concatenate(pieces, axis=axis)


def _rope_tables(seq, ctx):
    rows = seq // GRID_W
    pad = jnp.zeros((ctx,), F32)
    row = jnp.concatenate([jnp.repeat(jnp.arange(rows, dtype=F32), GRID_W), pad])[:, None]
    col = jnp.concatenate([jnp.tile(jnp.arange(GRID_W, dtype=F32), rows), pad])[:, None]

    def build(dim, section_starts):
        axis_dim = dim // 2
        n_freq = axis_dim // 2
        freq = np.zeros((LANES,), np.float32)
        active = np.zeros((LANES,), bool)
        by_row = np.zeros((LANES,), bool)
        m_a = np.zeros((LANES,), np.float32)
        m_b = np.zeros((LANES,), np.float32)
        pair = np.arange(dim) // 2
        for lo in section_starts:
            freq[lo:lo + dim] = 2 * (pair % n_freq)
            active[lo:lo + dim] = True
            by_row[lo:lo + dim] = pair < n_freq
            m_a[lo:lo + dim:2] = 1.0
            m_b[lo + 1:lo + dim:2] = 1.0
        inv_lane = jnp.where(jnp.asarray(active), ROPE_BASE ** (-jnp.asarray(freq) / axis_dim), 0.0)[None, :]
        ang = jnp.where(jnp.asarray(by_row)[None, :], row * inv_lane, col * inv_lane)
        sin = jnp.sin(ang)
        return jnp.cos(ang), sin * jnp.asarray(-m_a)[None, :], sin * jnp.asarray(m_b)[None, :]

    return build(MLA_ROPE, (MLA_NOPE,)) + build(SWA_HEAD_DIM, (0, HALF))


def _dft_cos_sin(n, scale):
    def direct(rows_j, n_mod):
        k = jnp.arange(n, dtype=jnp.int32)
        ang = ((rows_j[:, None] * k[None, :]) % n_mod).astype(F32) * (2.0 * np.pi / n_mod)
        return jnp.cos(ang), jnp.sin(ang)

    inner = FOURIER_GROUP_DIM
    if n <= inner or n % inner:
        c, s = direct(jnp.arange(n, dtype=jnp.int32), n)
        return c * scale, s * scale
    outer = n // inner
    ca, sa = direct(jnp.arange(outer, dtype=jnp.int32), outer)
    cb, sb = direct(jnp.arange(inner, dtype=jnp.int32), n)
    cb, sb = cb * scale, sb * scale
    c = ca[:, None, :] * cb[None, :, :] - sa[:, None, :] * sb[None, :, :]
    s = sa[:, None, :] * cb[None, :, :] + ca[:, None, :] * sb[None, :, :]
    return c.reshape(n, n), s.reshape(n, n)


def _channel_dft():
    c, s = _dft_cos_sin(FOURIER_GROUP_DIM, FOURIER_GROUP_DIM ** -0.5)
    eye = jnp.eye(FOURIER_WIDTH // FOURIER_GROUP_DIM, dtype=F32)
    return jnp.concatenate([jnp.kron(eye, c), jnp.kron(eye, s)], axis=1).astype(BF16)


def _rms_scale(x, width):
    return lax.rsqrt(jnp.sum(x * x, axis=-1, keepdims=True) * (1.0 / width) + EPS)


def _norm_mod(x, g, shift, scale):
    y = x * _rms_scale(x, x.shape[-1]) * g
    return y * (1.0 + scale) + shift


def _swiglu(xn, w1_ref, w3_ref, w2_ref):
    a = jnp.dot(xn, w1_ref[...], preferred_element_type=F32)
    b = jnp.dot(xn, w3_ref[...], preferred_element_type=F32)
    g = (a / (1.0 + jnp.exp(-a))) * b
    return jnp.dot(g.astype(BF16), w2_ref[...], preferred_element_type=F32)


def _rope(x, cos, sin_a, sin_b):
    return x * cos + pltpu.roll(x, LANES - 1, 1) * sin_a + pltpu.roll(x, 1, 1) * sin_b


def _low_lanes(shape):
    return lax.broadcasted_iota(jnp.int32, shape, len(shape) - 1) < HALF


def _mod_kernel(c_ref, w_ref, b_ref, o_ref):
    cv = c_ref[...]
    s = (cv / (1.0 + jnp.exp(-cv))).astype(BF16)
    o_ref[...] = jnp.dot(s, w_ref[...].astype(BF16), preferred_element_type=F32) + b_ref[...]


def _modulation(cc, w_ada, b_ada):
    n_layers, d, width = w_ada.shape
    return pl.pallas_call(
        _mod_kernel,
        grid=(n_layers, width // MOD_COL_TILE),
        in_specs=[
            pl.BlockSpec((MOD_ROWS, d), lambda l, j: (0, 0)),
            pl.BlockSpec((None, d, MOD_COL_TILE), lambda l, j: (l, 0, j)),
            pl.BlockSpec((None, 1, MOD_COL_TILE), lambda l, j: (l, 0, j)),
        ],
        out_specs=pl.BlockSpec((None, MOD_ROWS, MOD_COL_TILE), lambda l, j: (l, 0, j)),
        out_shape=jax.ShapeDtypeStruct((n_layers, MOD_ROWS, width), F32),
        compiler_params=pltpu.CompilerParams(
            dimension_semantics=("arbitrary", "arbitrary"), vmem_limit_bytes=VMEM_LIMIT),
        name="adaln_modulation",
    )(cc, w_ada, b_ada.reshape(n_layers, 1, width))


def _ffn_proj_kernel(*refs, n_src, n_tiles, tpb, n_lat):
    h_refs = refs[:n_src]
    (mod_ref, modp_ref, g1_ref, w1_ref, w3_ref, w2_ref, gmix_ref, win_ref,
     gcq_ref, wuq_ref, gmq_ref, gckv_ref, wuk_ref, wuv_ref, gmk_ref, gsq_ref, gsk_ref, dft_ref,
     cm_ref, sam_ref, sbm_ref, cs_ref, sas_ref, sbs_ref,
     ho_ref, q_ref, k_ref, v_ref, sq_ref, sk_ref, sv_ref, z_ref, hprev_ref) = refs[n_src:]
    step = pl.program_id(0)

    @pl.when(step == 0)
    def _():
        hprev_ref[...] = jnp.zeros_like(hprev_ref)

    modp = modp_ref[...]
    n = _norm_mod(hprev_ref[...], gmix_ref[...], modp[3:4], modp[4:5]).astype(BF16)
    u = jnp.dot(n, win_ref[...], preferred_element_type=F32)

    f = u[:, OFF_F:OFF_F + FOURIER_WIDTH].astype(BF16)
    z_ref[...] = jnp.dot(f, dft_ref[...], preferred_element_type=F32).astype(BF16)

    cm, sam, sbm = cm_ref[...], sam_ref[...], sbm_ref[...]
    cq = u[:, OFF_CQ:OFF_CQ + MLA_Q_RANK]
    cqn = (cq * _rms_scale(cq, MLA_Q_RANK) * gcq_ref[...]).astype(BF16)
    q = jnp.dot(cqn, wuq_ref[...], preferred_element_type=F32)
    gmq = gmq_ref[...]
    q_scale = MLA_QK_DIM ** -0.5 * LOG2E
    for hd in range(MLA_HEADS):
        qh = q[:, hd * LANES:(hd + 1) * LANES]
        qg = qh * (_rms_scale(qh, MLA_QK_DIM) * q_scale) * gmq
        q_ref[:, hd * LANES:(hd + 1) * LANES] = _rope(qg, cm, sam, sbm).astype(BF16)

    ckv = u[:, OFF_CKV:OFF_CKV + MLA_KV_RANK]
    ckvn = (ckv * _rms_scale(ckv, MLA_KV_RANK) * gckv_ref[...]).astype(BF16)
    kn = jnp.dot(ckvn, wuk_ref[...], preferred_element_type=F32)
    vv = jnp.dot(ckvn, wuv_ref[...], preferred_element_type=F32).astype(BF16)
    ones = jnp.ones((vv.shape[0], LANES), BF16)
    for pair in range(MLA_HEADS // 2):
        v_ref[:, 2 * pair * LANES:(2 * pair + 1) * LANES] = vv[:, pair * LANES:(pair + 1) * LANES]
        v_ref[:, (2 * pair + 1) * LANES:(2 * pair + 2) * LANES] = ones
    gmk = gmk_ref[...]
    kr = u[:, OFF_KR:OFF_KR + LANES]
    kr_ss = jnp.sum(kr * kr, axis=-1, keepdims=True)
    kr_rot = _rope(kr * gmk, cm, sam, sbm)
    for hd in range(MLA_HEADS):
        kh = kn[:, hd * LANES:(hd + 1) * LANES]
        ss = jnp.sum(kh * kh, axis=-1, keepdims=True) + kr_ss
        rs = lax.rsqrt(ss * (1.0 / MLA_QK_DIM) + EPS)
        k_ref[:, hd * LANES:(hd + 1) * LANES] = (rs * (kh * gmk + kr_rot)).astype(BF16)

    cs, sas, sbs = cs_ref[...], sas_ref[...], sbs_ref[...]

    def two_head_norm_rope(x, g, scale):
        low = _low_lanes(x.shape)
        x2 = x * x
        lo = jnp.sum(jnp.where(low, x2, 0.0), axis=-1, keepdims=True)
        hi = jnp.sum(jnp.where(low, 0.0, x2), axis=-1, keepdims=True)
        rs = jnp.where(low, lax.rsqrt(lo * (1.0 / SWA_HEAD_DIM) + EPS),
                       lax.rsqrt(hi * (1.0 / SWA_HEAD_DIM) + EPS))
        return _rope(x * (rs * scale) * g, cs, sas, sbs).astype(BF16)

    gsq = gsq_ref[...]
    s_scale = SWA_HEAD_DIM ** -0.5 * LOG2E
    sq_ref[:, 0:LANES] = two_head_norm_rope(u[:, OFF_SQA:OFF_SQA + LANES], gsq, s_scale)
    sq_ref[:, LANES:2 * LANES] = two_head_norm_rope(u[:, OFF_SQB:OFF_SQB + LANES], gsq, s_scale)
    sk_ref[...] = two_head_norm_rope(u[:, OFF_SK:OFF_SK + LANES], gsk_ref[...], 1.0)
    sv_ref[:, 0:LANES] = u[:, OFF_SV:OFF_SV + LANES].astype(BF16)
    sv_ref[:, LANES:2 * LANES] = ones

    mod = mod_ref[...]
    if n_src == 1:
        h = h_refs[0][...]
    else:
        cur = jnp.minimum(step, n_tiles - 1)
        h = jnp.where(cur % tpb >= n_lat, h_refs[1][...], h_refs[0][...])
    xn = _norm_mod(h, g1_ref[...], mod[0:1], mod[1:2]).astype(BF16)
    h = h + (0.5 * mod[2:3]) * _swiglu(xn, w1_ref, w3_ref, w2_ref)
    ho_ref[...] = h
    hprev_ref[...] = h


def _const_spec(block_shape, index_map):
    return pl.BlockSpec(block_shape, index_map, pipeline_mode=pl.Buffered(1))


def _ffn_proj(layer, tokens, mod, p, tabs, seq, t_all):
    n_batch, _, d = tokens[0].shape
    tm = TOK_TILE
    tpb = t_all // tm
    n_tiles = n_batch * tpb
    n_lat = seq // tm
    cur = lambda s: jnp.minimum(s, n_tiles - 1)
    prev = lambda s: jnp.maximum(s - 1, 0)
    tile = lambda which, width: pl.BlockSpec(
        (None, tm, width), lambda s: (which(s) // tpb, which(s) % tpb, 0))

    def token_specs(which):
        if len(tokens) == 1:
            return [tile(which, d)]
        lat = pl.BlockSpec((None, tm, d),
                           lambda s: (which(s) // tpb, jnp.minimum(which(s) % tpb, n_lat - 1), 0))
        con = pl.BlockSpec((None, tm, d),
                           lambda s: (which(s) // tpb, jnp.maximum(which(s) % tpb - n_lat, 0), 0))
        return [lat, con]

    def mod_row(which):
        def index(s):
            b, t = which(s) // tpb, which(s) % tpb
            return (layer, jnp.where(t >= n_lat, n_batch, b), 0, 0)
        return pl.BlockSpec((None, None, N_MOD, d), index)

    lw = lambda *shape: _const_spec((None,) + shape, lambda s: (layer,) + (0,) * len(shape))
    tab = pl.BlockSpec((tm, LANES), lambda s: (prev(s) % tpb, 0))
    in_specs = token_specs(cur) + [
        mod_row(cur), mod_row(prev),
        lw(1, d), lw(d, D_FF), lw(d, D_FF), lw(D_FF, d), lw(1, d), lw(d, IN_WIDTH_P),
        lw(1, MLA_Q_RANK), lw(MLA_Q_RANK, MLA_HEADS * LANES), lw(1, LANES),
        lw(1, MLA_KV_RANK), lw(MLA_KV_RANK, MLA_HEADS * LANES), lw(MLA_KV_RANK, MLA_HEADS * MLA_V),
        lw(1, LANES), lw(1, LANES), lw(1, LANES),
        _const_spec((FOURIER_WIDTH, 2 * FOURIER_WIDTH), lambda s: (0, 0)),
        tab, tab, tab, tab, tab, tab,
    ]
    widths = (d, MLA_HEADS * LANES, MLA_HEADS * LANES, MLA_V_SLABS * LANES, 2 * LANES, LANES, 2 * LANES,
              2 * FOURIER_WIDTH)
    dtypes = (F32,) + (BF16,) * 7
    return pl.pallas_call(
        functools.partial(_ffn_proj_kernel, n_src=len(tokens), n_tiles=n_tiles, tpb=tpb, n_lat=n_lat),
        grid=(n_tiles + 1,),
        in_specs=in_specs,
        out_specs=[tile(cur, d)] + [tile(prev, w) for w in widths[1:]],
        out_shape=[jax.ShapeDtypeStruct((n_batch, t_all, w), dt) for w, dt in zip(widths, dtypes)],
        scratch_shapes=[pltpu.VMEM((tm, d), F32)],
        compiler_params=pltpu.CompilerParams(
            dimension_semantics=("arbitrary",), vmem_limit_bytes=VMEM_LIMIT),
        name="ffn_half_and_projections",
    )(*tokens, mod, mod, p["g_ffn1"], p["w1_ffn1"], p["w3_ffn1"], p["w2_ffn1"], p["g_mix"], p["w_in"],
      p["g_cq"], p["w_uq"], p["g_mq"], p["g_ckv"], p["w_uk"], p["w_uv"], p["g_mk"],
      p["g_sq"], p["g_sk"], p["dft64"], *tabs)


def _scores(q, k):
    return lax.dot_general(q, k, (((1,), (1,)), ((), ())), preferred_element_type=F32)


def _mla_heads(q_ref, k_ref, v_ref, o_ref, head0_scores=None):
    n_heads = q_ref.shape[-1] // LANES
    for pair in range(n_heads // 2):
        outs = []
        vp = v_ref[:, 2 * pair * LANES:(2 * pair + 2) * LANES]
        for hd in (2 * pair, 2 * pair + 1):
            if hd == 0 and head0_scores is not None:
                s = head0_scores
            else:
                s = _scores(q_ref[:, hd * LANES:(hd + 1) * LANES], k_ref[:, hd * LANES:(hd + 1) * LANES])
            m = jnp.max(s, axis=-1, keepdims=True)
            p = jnp.exp2(s - m).astype(BF16)
            r = jnp.dot(p, vp, preferred_element_type=F32)
            outs.append(r[:, 0:LANES] / r[:, LANES:2 * LANES])
        o_ref[:, pair * LANES:(pair + 1) * LANES] = jnp.where(
            _low_lanes(outs[0].shape), outs[0], outs[1]).astype(BF16)


def _mla_latent_kernel(q_ref, k_ref, v_ref, qn_ref, kn_ref, o_ref, s0_ref):
    @pl.when(pl.program_id(0) == 0)
    def _():
        s0_ref[...] = _scores(q_ref[:, 0:LANES], k_ref[:, 0:LANES])

    _mla_heads(q_ref, k_ref, v_ref, o_ref, head0_scores=s0_ref[...])
    s0_ref[...] = _scores(qn_ref[...], kn_ref[...])


def _mla(q, k, v, seq, with_ctx):
    n_batch, t_all, _ = q.shape
    n_ctx = t_all - seq
    tq = MLA_Q_TILE
    tpb = seq // tq
    n_tiles = n_batch * tpb
    qk_w = MLA_HEADS * LANES
    v_in = MLA_V_SLABS * LANES
    v_w = MLA_HEADS * MLA_V
    params = lambda n: pltpu.CompilerParams(
        dimension_semantics=("arbitrary",) * n, vmem_limit_bytes=VMEM_LIMIT)
    nxt = lambda s: jnp.minimum(s + 1, n_tiles - 1)
    a = pl.pallas_call(
        _mla_latent_kernel,
        grid=(n_tiles,),
        in_specs=[
            pl.BlockSpec((None, tq, qk_w), lambda s: (s // tpb, s % tpb, 0)),
            pl.BlockSpec((None, t_all, qk_w), lambda s: (s // tpb, 0, 0)),
            pl.BlockSpec((None, t_all, v_in), lambda s: (s // tpb, 0, 0)),
            pl.BlockSpec((None, tq, LANES), lambda s: (nxt(s) // tpb, nxt(s) % tpb, 0)),
            pl.BlockSpec((None, t_all, LANES), lambda s: (nxt(s) // tpb, 0, 0)),
        ],
        out_specs=pl.BlockSpec((None, tq, v_w), lambda s: (s // tpb, s % tpb, 0)),
        out_shape=jax.ShapeDtypeStruct((n_batch, seq, v_w), BF16),
        scratch_shapes=[pltpu.VMEM((tq, t_all), F32)],
        compiler_params=params(1),
        name="mla_attention",
    )(q, k, v, q, k)
    if not with_ctx:
        return a, None
    c_blk = seq // n_ctx
    ctx_rows = lambda width: pl.BlockSpec((None, n_ctx, width), lambda b: (b, c_blk, 0))
    a_ctx = pl.pallas_call(
        _mla_heads,
        grid=(n_batch,),
        in_specs=[ctx_rows(qk_w), ctx_rows(qk_w), ctx_rows(v_in)],
        out_specs=pl.BlockSpec((None, n_ctx, v_w), lambda b: (b, 0, 0)),
        out_shape=jax.ShapeDtypeStruct((n_batch, n_ctx, v_w), BF16),
        compiler_params=params(1),
        name="mla_context_attention",
    )(q, k, v)
    return a, a_ctx


def _swa_heads(sink_ref, layer, q_ref, keys, vals, valid, o_ref, r0=0, rows=None):
    rows = q_ref.shape[0] if rows is None else rows
    res = []
    for hq in range(SWA_Q_HEADS):
        slab = hq % 2
        use_low = hq < 2
        qs = q_ref[r0:r0 + rows, slab * LANES:(slab + 1) * LANES]
        low = _low_lanes(qs.shape)
        qm = jnp.where(low if use_low else jnp.logical_not(low), qs, jnp.zeros_like(qs))
        sink = sink_ref[layer, hq] * LOG2E
        s = _scores(qm, keys)
        if valid is not None:
            s = jnp.where(valid, s, NEG)
        m = jnp.maximum(jnp.max(s, axis=-1, keepdims=True), sink)
        p = jnp.exp2(s - m).astype(BF16)
        r = jnp.dot(p, vals, preferred_element_type=F32)
        res.append(r[:, 0:LANES] / (r[:, LANES:2 * LANES] + jnp.exp2(sink - m)))
    low = _low_lanes(res[0].shape)
    o_ref[r0:r0 + rows, 0:LANES] = jnp.where(low, res[0], res[2]).astype(BF16)
    o_ref[r0:r0 + rows, LANES:2 * LANES] = jnp.where(low, res[1], res[3]).astype(BF16)


def _dft_rows(c_rows, s_rows, z_ref, lo, n):
    w = FOURIER_WIDTH
    return (jnp.dot(c_rows, z_ref[lo:lo + n, 0:w], preferred_element_type=F32)
            - jnp.dot(s_rows, z_ref[lo:lo + n, w:2 * w], preferred_element_type=F32)).astype(BF16)


def _swa_latent_kernel(sink_ref, q_ref, k_ref, v_ref, z_ref, cl_ref, sl_ref, o_ref, yf_ref, *, layer, seq):
    tq = q_ref.shape[0]
    rows = min(tq, ROW_CHUNK)
    span = rows + 2 * SWA_WINDOW
    n_keys = span + k_ref.shape[0] - seq
    kc, vc = k_ref[seq:, :], v_ref[seq:, :]
    for r0 in range(0, tq, rows):
        first = pl.program_id(1) * tq + r0
        dft0 = pl.multiple_of(first, rows)
        yf_ref[r0:r0 + rows, :] = _dft_rows(
            cl_ref[pl.ds(dft0, rows), :], sl_ref[pl.ds(dft0, rows), :], z_ref, 0, seq)
        start = pl.multiple_of(jnp.clip(first - SWA_WINDOW, 0, seq - span), LANES)
        keys = jnp.concatenate([k_ref[pl.ds(start, span), :], kc], axis=0)
        vals = jnp.concatenate([v_ref[pl.ds(start, span), :], vc], axis=0)
        col = lax.broadcasted_iota(jnp.int32, (rows, n_keys), 1)
        dist = first + lax.broadcasted_iota(jnp.int32, (rows, n_keys), 0) - (start + col)
        valid = ((dist <= SWA_WINDOW) & (dist >= -SWA_WINDOW)) | (col >= span)
        _swa_heads(sink_ref, layer, q_ref, keys, vals, valid, o_ref, r0, rows)


def _swa_ctx_kernel(sink_ref, q_ref, k_ref, v_ref, z_ref, cc_ref, sc_ref, o_ref, yf_ref, *, layer):
    yf_ref[...] = _dft_rows(cc_ref[...], sc_ref[...], z_ref, 0, z_ref.shape[0])
    _swa_heads(sink_ref, layer, q_ref, k_ref[...], v_ref[...], None, o_ref)


def _swa_fourier(layer, sink, q, k, v, zcs, dfts, seq, with_ctx):
    n_batch, t_all, _ = q.shape
    n_ctx = t_all - seq
    tq = Q_TILE
    c_lat, s_lat, c_ctx, s_ctx = dfts
    zw = zcs.shape[-1]
    params = lambda n: pltpu.CompilerParams(
        dimension_semantics=("arbitrary",) * n, vmem_limit_bytes=VMEM_LIMIT)
    smem = pl.BlockSpec(memory_space=pltpu.SMEM)
    w, yf = pl.pallas_call(
        functools.partial(_swa_latent_kernel, layer=layer, seq=seq),
        grid=(n_batch, seq // tq),
        in_specs=[
            smem,
            pl.BlockSpec((None, tq, 2 * LANES), lambda b, i: (b, i, 0)),
            pl.BlockSpec((None, t_all, LANES), lambda b, i: (b, 0, 0)),
            pl.BlockSpec((None, t_all, 2 * LANES), lambda b, i: (b, 0, 0)),
            pl.BlockSpec((None, t_all, zw), lambda b, i: (b, 0, 0)),
            _const_spec(c_lat.shape, lambda b, i: (0, 0)),
            _const_spec(s_lat.shape, lambda b, i: (0, 0)),
        ],
        out_specs=[pl.BlockSpec((None, tq, 2 * LANES), lambda b, i: (b, i, 0)),
                   pl.BlockSpec((None, tq, FOURIER_WIDTH), lambda b, i: (b, i, 0))],
        out_shape=[jax.ShapeDtypeStruct((n_batch, seq, 2 * LANES), BF16),
                   jax.ShapeDtypeStruct((n_batch, seq, FOURIER_WIDTH), BF16)],
        compiler_params=params(2),
        name="window_attention_and_position_dft",
    )(sink, q, k, v, zcs, c_lat, s_lat)
    if not with_ctx:
        return (w, None), (yf, None)
    c_blk = seq // n_ctx
    ctx_rows = lambda width: pl.BlockSpec((None, n_ctx, width), lambda b: (b, c_blk, 0))
    ctx_out = lambda width: pl.BlockSpec((None, n_ctx, width), lambda b: (b, 0, 0))
    w_ctx, yf_ctx = pl.pallas_call(
        functools.partial(_swa_ctx_kernel, layer=layer),
        grid=(n_batch,),
        in_specs=[smem, ctx_rows(2 * LANES), ctx_rows(LANES), ctx_rows(2 * LANES), ctx_rows(zw),
                  _const_spec(c_ctx.shape, lambda b: (0, 0)), _const_spec(s_ctx.shape, lambda b: (0, 0))],
        out_specs=[ctx_out(2 * LANES), ctx_out(FOURIER_WIDTH)],
        out_shape=[jax.ShapeDtypeStruct((n_batch, n_ctx, 2 * LANES), BF16),
                   jax.ShapeDtypeStruct((n_batch, n_ctx, FOURIER_WIDTH), BF16)],
        compiler_params=params(1),
        name="window_context_attention_and_dft",
    )(sink, q, k, v, zcs, c_ctx, s_ctx)
    return (w, w_ctx), (yf, yf_ctx)


def _out_ffn_kernel(h_ref, *refs, parts, hpb, n_lat):
    half = TOK_TILE
    n_mix = 3
    if hpb is None:
        mixers = [r[...] for r in refs[:n_mix]]
        refs = refs[n_mix:]
    else:
        per_part = []
        for i in range(parts):
            srcs = refs[2 * n_mix * i:2 * n_mix * (i + 1)]
            is_ctx = (parts * pl.program_id(0) + i) % hpb >= n_lat
            per_part.append([jnp.where(is_ctx, srcs[2 * m + 1][...], srcs[2 * m][...])
                             for m in range(n_mix)])
        mixers = [jnp.concatenate([pp[m] for pp in per_part], axis=0) for m in range(n_mix)]
        refs = refs[2 * n_mix * parts:]
    yf_all, a_all, w_all = mixers
    mod_refs = refs[:parts]
    wo_ref, g2_ref, w1_ref, w3_ref, w2_ref, o_ref = refs[parts:]
    n_f = yf_all.shape[-1]
    n_a = a_all.shape[-1]
    mixed = (jnp.dot(yf_all, wo_ref[0:n_f, :], preferred_element_type=F32)
             + jnp.dot(a_all, wo_ref[n_f:n_f + n_a, :], preferred_element_type=F32)
             + jnp.dot(w_all, wo_ref[n_f + n_a:, :], preferred_element_type=F32))
    g2 = g2_ref[...]
    hs, xs = [], []
    for i, mod_ref in enumerate(mod_refs):
        mod = mod_ref[...]
        rows = slice(i * half, (i + 1) * half)
        h = h_ref[rows, :] + mod[5:6] * mixed[rows, :]
        hs.append(h)
        xs.append(_norm_mod(h, g2, mod[6:7], mod[7:8]).astype(BF16))
    y = _swiglu(jnp.concatenate(xs, axis=0), w1_ref, w3_ref, w2_ref)
    for i, mod_ref in enumerate(mod_refs):
        rows = slice(i * half, (i + 1) * half)
        o_ref[rows, :] = hs[i] + (0.5 * mod_ref[8:9, :]) * y[rows, :]


def _out_ffn(layer, hh, yf, a, w, mod, p, seq, with_ctx):
    n_batch, t_all, d = hh.shape
    half = TOK_TILE
    parts = OUT_PARTS if with_ctx else OUT_PARTS_LATENT
    tm = parts * half
    hpb = t_all // half
    n_lat = seq // half
    lw = lambda *shape: _const_spec((None,) + shape, lambda *_: (layer,) + (0,) * len(shape))

    def mod_row(half_index):
        def index(*g):
            j = half_index(*g)
            return (layer, jnp.where(j % hpb >= n_lat, n_batch, j // hpb), 0, 0)
        return pl.BlockSpec((None, None, N_MOD, d), index)

    mixers = (yf, a, w)
    if with_ctx:
        flat = lambda t: t.reshape(n_batch * t_all, t.shape[-1])
        grid = (n_batch * t_all // tm,)
        rows = lambda width: pl.BlockSpec((tm, width), lambda j: (j, 0))
        part = lambda i: (lambda j: parts * j + i)
        lat = lambda i, width: pl.BlockSpec(
            (None, half, width), lambda j: (part(i)(j) // hpb, jnp.minimum(part(i)(j) % hpb, n_lat - 1), 0))
        con = lambda i, width: pl.BlockSpec(
            (None, half, width), lambda j: (part(i)(j) // hpb, jnp.maximum(part(i)(j) % hpb - n_lat, 0), 0))
        arrays, specs = [flat(hh)], [rows(d)]
        for i in range(parts):
            for m_lat, m_ctx in mixers:
                arrays += [m_lat, m_ctx]
                specs += [lat(i, m_lat.shape[-1]), con(i, m_ctx.shape[-1])]
        mods = [mod_row(part(i)) for i in range(parts)]
        out_shape = jax.ShapeDtypeStruct((n_batch * t_all, d), F32)
    else:
        grid = (n_batch, seq // tm)
        rows = lambda width: pl.BlockSpec((None, tm, width), lambda b, t: (b, t, 0))
        arrays = [hh] + [m_lat for m_lat, _ in mixers]
        specs = [rows(t.shape[-1]) for t in arrays]
        mods = [mod_row(lambda b, t: b * hpb)] * parts
        out_shape = jax.ShapeDtypeStruct((n_batch, seq, d), F32)
    out = pl.pallas_call(
        functools.partial(_out_ffn_kernel, parts=parts, hpb=hpb if with_ctx else None, n_lat=n_lat),
        grid=grid,
        in_specs=specs + mods + [lw(d, d), lw(1, d), lw(d, D_FF), lw(d, D_FF), lw(D_FF, d)],
        out_specs=rows(d),
        out_shape=out_shape,
        compiler_params=pltpu.CompilerParams(
            dimension_semantics=("arbitrary",) * len(grid), vmem_limit_bytes=VMEM_LIMIT),
        name="out_projection_and_ffn_half",
    )(*arrays, *([mod] * parts), p["w_out"], p["g_ffn2"], p["w1_ffn2"], p["w3_ffn2"], p["w2_ffn2"])
    return out.reshape(n_batch, t_all, d) if with_ctx else out


def _prepare_params(g_ffn1, w1_ffn1, w3_ffn1, w2_ffn1, g_mix, w_in, g_cq, w_uq, g_ckv, w_ukv,
                    g_mla_q, g_mla_k, g_swa_q, g_swa_k, w_out, g_ffn2, w1_ffn2, w3_ffn2, w2_ffn2):
    row = lambda g: g[:, None, :]
    two_heads = lambda g: jnp.concatenate([g, g], axis=-1)
    kv = w_ukv.reshape(w_ukv.shape[:-1] + (MLA_HEADS, MLA_NOPE + MLA_V))
    flat = lambda t: t.reshape(t.shape[:-2] + (t.shape[-2] * t.shape[-1],))
    return {
        "g_ffn1": row(g_ffn1), "w1_ffn1": w1_ffn1.astype(BF16), "w3_ffn1": w3_ffn1.astype(BF16),
        "w2_ffn1": w2_ffn1.astype(BF16),
        "g_mix": row(g_mix), "w_in": _take_cols(w_in, _w_in_cols()).astype(BF16),
        "g_cq": row(g_cq), "w_uq": _pad_heads(w_uq, MLA_HEADS).astype(BF16),
        "g_mq": row(_pad_heads(g_mla_q, 1)),
        "g_ckv": row(g_ckv), "w_uk": _pad_heads(flat(kv[..., :MLA_NOPE]), MLA_HEADS).astype(BF16),
        "w_uv": flat(kv[..., MLA_NOPE:]).astype(BF16),
        "g_mk": row(_pad_heads(g_mla_k, 1)),
        "g_sq": row(two_heads(g_swa_q)),
        "g_sk": row(two_heads(g_swa_k)),
        "w_out": _take_cols(w_out, _w_out_rows(), axis=1).astype(BF16),
        "g_ffn2": row(g_ffn2), "w1_ffn2": w1_ffn2.astype(BF16), "w3_ffn2": w3_ffn2.astype(BF16),
        "w2_ffn2": w2_ffn2.astype(BF16),
        "dft64": _channel_dft(),
    }


def kernel(x, c, ctx, c_ctx, w_ada, b_ada, g_ffn1, w1_ffn1, w3_ffn1, w2_ffn1, g_mix, w_in, g_cq, w_uq,
           g_ckv, w_ukv, g_mla_q, g_mla_k, g_swa_q, g_swa_k, sink, w_out, g_ffn2, w1_ffn2, w3_ffn2,
           w2_ffn2):
    n_batch, seq, d = x.shape
    n_ctx = ctx.shape[1]
    depth = w_ada.shape[0]
    assert d == D_MODEL and seq % GRID_W == 0 and n_batch + 1 <= MOD_ROWS
    assert n_ctx % TOK_TILE == 0 and seq % TOK_TILE == 0 and seq % n_ctx == 0
    assert seq % (OUT_PARTS_LATENT * TOK_TILE) == 0 and (n_batch * (seq + n_ctx)) % (OUT_PARTS * TOK_TILE) == 0
    assert seq % Q_TILE == 0 and seq % MLA_Q_TILE == 0
    assert Q_TILE % ROW_CHUNK == 0 and ROW_CHUNK + 2 * SWA_WINDOW <= seq
    assert w_ada.shape[-1] % MOD_COL_TILE == 0

    p = _prepare_params(g_ffn1, w1_ffn1, w3_ffn1, w2_ffn1, g_mix, w_in, g_cq, w_uq, g_ckv, w_ukv,
                        g_mla_q, g_mla_k, g_swa_q, g_swa_k, w_out, g_ffn2, w1_ffn2, w3_ffn2, w2_ffn2)
    tabs = _rope_tables(seq, n_ctx)
    c_lat, s_lat = _dft_cos_sin(seq, seq ** -0.5)
    c_ctx_dft, s_ctx_dft = _dft_cos_sin(n_ctx, n_ctx ** -0.5)
    dfts = tuple(m.astype(BF16) for m in (c_lat, s_lat, c_ctx_dft, s_ctx_dft))

    cc = jnp.concatenate([c, c_ctx[None, :], jnp.zeros((MOD_ROWS - n_batch - 1, d), F32)], axis=0)
    mod = _modulation(cc, w_ada, b_ada).reshape(depth, MOD_ROWS, N_MOD, d)

    tokens = (x, ctx)
    for layer in range(depth):
        with_ctx = layer != depth - 1
        hh, q, k, v, sq, sk, sv, zcs = _ffn_proj(layer, tokens, mod, p, tabs, seq, seq + n_ctx)
        a = _mla(q, k, v, seq, with_ctx)
        w, yf = _swa_fourier(layer, sink, sq, sk, sv, zcs, dfts, seq, with_ctx)
        hh = _out_ffn(layer, hh, yf, a, w, mod, p, seq, with_ctx)
        tokens = (hh,)
    return hh
```

```python
import functools

import numpy as np
import jax
import jax.numpy as jnp
from jax import lax
from jax.experimental import pallas as pl
from jax.experimental.pallas import tpu as pltpu

F32 = jnp.float32
BF16 = jnp.bfloat16

D_MODEL = 1024
GRID_W = 64
ROPE_BASE = 10000.0
EPS = 1e-6
NEG = -1e30
LOG2E = 1.4426950408889634
N_MOD = 9
D_FF = 2816

FOURIER_WIDTH = 256
FOURIER_GROUP_DIM = 64
MLA_HEADS = 8
MLA_NOPE = 64
MLA_ROPE = 32
MLA_V = 64
MLA_QK_DIM = MLA_NOPE + MLA_ROPE
MLA_Q_RANK = 256
MLA_KV_RANK = 128
SWA_Q_HEADS = 4
SWA_KV_HEADS = 2
SWA_HEAD_DIM = 64
SWA_WINDOW = 128
IN_SPLITS = (256, 256, 128, 32, 256, 128, 128)
IN_WIDTH = sum(IN_SPLITS)

LANES = 128
HALF = LANES // 2
IN_WIDTH_P = 10 * LANES
MLA_V_SLABS = MLA_HEADS
VMEM_LIMIT = 56 * 1024 * 1024

TOK_TILE = 256
OUT_PARTS = 2
OUT_PARTS_LATENT = 2
Q_TILE = 1024
MLA_Q_TILE = 1024
ROW_CHUNK = 256
MOD_ROWS = 24
MOD_COL_TILE = 2304

OFF_F, OFF_CQ, OFF_CKV, OFF_KR, OFF_SQA, OFF_SQB, OFF_SK, OFF_SV = (
    0, 256, 512, 640, 768, 896, 1024, 1152)


def _w_in_cols():
    zero = IN_WIDTH
    o_f, o_cq, o_ckv, o_kr, o_sq, o_sk, o_sv = np.cumsum((0,) + IN_SPLITS)[:-1]
    cols = np.full((IN_WIDTH_P,), zero, np.int32)
    cols[OFF_F:OFF_F + 256] = o_f + np.arange(256)
    cols[OFF_CQ:OFF_CQ + 256] = o_cq + np.arange(256)
    cols[OFF_CKV:OFF_CKV + 128] = o_ckv + np.arange(128)
    cols[OFF_KR + MLA_NOPE:OFF_KR + MLA_QK_DIM] = o_kr + np.arange(MLA_ROPE)
    head = np.arange(SWA_HEAD_DIM)
    cols[OFF_SQA:OFF_SQA + 64] = o_sq + 0 * 64 + head
    cols[OFF_SQA + 64:OFF_SQA + 128] = o_sq + 2 * 64 + head
    cols[OFF_SQB:OFF_SQB + 64] = o_sq + 1 * 64 + head
    cols[OFF_SQB + 64:OFF_SQB + 128] = o_sq + 3 * 64 + head
    cols[OFF_SK:OFF_SK + 64] = o_sk + head
    cols[OFF_SK + 64:OFF_SK + 128] = o_sk + 64 + head
    cols[OFF_SV:OFF_SV + 128] = o_sv + np.arange(128)
    return cols


def _pad_heads(w, n_heads):
    lead, width = w.shape[:-1], w.shape[-1] // n_heads
    w = jnp.pad(w.reshape(lead + (n_heads, width)), [(0, 0)] * (len(lead) + 1) + [(0, LANES - width)])
    return w.reshape(lead + (n_heads * LANES,))


def _w_out_rows():
    base = FOURIER_WIDTH + MLA_HEADS * MLA_V
    swa = np.concatenate([base + h * SWA_HEAD_DIM + np.arange(SWA_HEAD_DIM) for h in (0, 2, 1, 3)])
    return np.concatenate([np.arange(base), swa]).astype(np.int32)


def _take_cols(w, cols, axis=-1):
    axis = axis % w.ndim
    n = w.shape[axis]
    cols = [int(c) for c in cols]
    pieces, i = [], 0
    while i < len(cols):
        j = i + 1
        if cols[i] == n:
            while j < len(cols) and cols[j] == n:
                j += 1
            shape = w.shape[:axis] + (j - i,) + w.shape[axis + 1:]
            pieces.append(jnp.zeros(shape, w.dtype))
        else:
            stride = cols[j] - cols[i] if j < len(cols) and cols[j] - cols[i] in (1, 2) else 1
            while j < len(cols) and cols[j] != n and cols[j] == cols[j - 1] + stride:
                j += 1
            pieces.append(lax.slice_in_dim(w, cols[i], cols[j - 1] + 1, stride, axis))
        i = j
    return jnp.concatenate(pieces, axis=axis)


def _rope_tables(seq, ctx):
    rows = seq // GRID_W
    pad = jnp.zeros((ctx,), F32)
    row = jnp.concatenate([jnp.repeat(jnp.arange(rows, dtype=F32), GRID_W), pad])[:, None]
    col = jnp.concatenate([jnp.tile(jnp.arange(GRID_W, dtype=F32), rows), pad])[:, None]

    def build(dim, section_starts):
        axis_dim = dim // 2
        n_freq = axis_dim // 2
        freq = np.zeros((LANES,), np.float32)
        active = np.zeros((LANES,), bool)
        by_row = np.zeros((LANES,), bool)
        m_a = np.zeros((LANES,), np.float32)
        m_b = np.zeros((LANES,), np.float32)
        pair = np.arange(dim) // 2
        for lo in section_starts:
            freq[lo:lo + dim] = 2 * (pair % n_freq)
            active[lo:lo + dim] = True
            by_row[lo:lo + dim] = pair < n_freq
            m_a[lo:lo + dim:2] = 1.0
            m_b[lo + 1:lo + dim:2] = 1.0
        inv_lane = jnp.where(jnp.asarray(active), ROPE_BASE ** (-jnp.asarray(freq) / axis_dim), 0.0)[None, :]
        ang = jnp.where(jnp.asarray(by_row)[None, :], row * inv_lane, col * inv_lane)
        sin = jnp.sin(ang)
        return jnp.cos(ang), sin * jnp.asarray(-m_a)[None, :], sin * jnp.asarray(m_b)[None, :]

    return build(MLA_ROPE, (MLA_NOPE,)) + build(SWA_HEAD_DIM, (0, HALF))


def _dft_cos_sin(n, scale):
    def direct(rows_j, n_mod):
        k = jnp.arange(n, dtype=jnp.int32)
        ang = ((rows_j[:, None] * k[None, :]) % n_mod).astype(F32) * (2.0 * np.pi / n_mod)
        return jnp.cos(ang), jnp.sin(ang)

    inner = FOURIER_GROUP_DIM
    if n <= inner or n % inner:
        c, s = direct(jnp.arange(n, dtype=jnp.int32), n)
        return c * scale, s * scale
    outer = n // inner
    ca, sa = direct(jnp.arange(outer, dtype=jnp.int32), outer)
    cb, sb = direct(jnp.arange(inner, dtype=jnp.int32), n)
    cb, sb = cb * scale, sb * scale
    c = ca[:, None, :] * cb[None, :, :] - sa[:, None, :] * sb[None, :, :]
    s = sa[:, None, :] * cb[None, :, :] + ca[:, None, :] * sb[None, :, :]
    return c.reshape(n, n), s.reshape(n, n)


def _channel_dft():
    c, s = _dft_cos_sin(FOURIER_GROUP_DIM, FOURIER_GROUP_DIM ** -0.5)
    eye = jnp.eye(FOURIER_WIDTH // FOURIER_GROUP_DIM, dtype=F32)
    return jnp.concatenate([jnp.kron(eye, c), jnp.kron(eye, s)], axis=1).astype(BF16)


def _rms_scale(x, width):
    return lax.rsqrt(jnp.sum(x * x, axis=-1, keepdims=True) * (1.0 / width) + EPS)


def _norm_mod(x, g, shift, scale):
    y = x * _rms_scale(x, x.shape[-1]) * g
    return y * (1.0 + scale) + shift


def _swiglu(xn, w1_ref, w3_ref, w2_ref):
    a = jnp.dot(xn, w1_ref[...], preferred_element_type=F32)
    b = jnp.dot(xn, w3_ref[...], preferred_element_type=F32)
    g = (a / (1.0 + jnp.exp(-a))) * b
    return jnp.dot(g.astype(BF16), w2_ref[...], preferred_element_type=F32)


def _rope(x, cos, sin_a, sin_b):
    return x * cos + pltpu.roll(x, LANES - 1, 1) * sin_a + pltpu.roll(x, 1, 1) * sin_b


def _low_lanes(shape):
    return lax.broadcasted_iota(jnp.int32, shape, len(shape) - 1) < HALF


def _mod_kernel(c_ref, w_ref, b_ref, o_ref):
    cv = c_ref[...]
    s = (cv / (1.0 + jnp.exp(-cv))).astype(BF16)
    o_ref[...] = jnp.dot(s, w_ref[...].astype(BF16), preferred_element_type=F32) + b_ref[...]


def _modulation(cc, w_ada, b_ada):
    n_layers, d, width = w_ada.shape
    return pl.pallas_call(
        _mod_kernel,
        grid=(n_layers, width // MOD_COL_TILE),
        in_specs=[
            pl.BlockSpec((MOD_ROWS, d), lambda l, j: (0, 0)),
            pl.BlockSpec((None, d, MOD_COL_TILE), lambda l, j: (l, 0, j)),
            pl.BlockSpec((None, 1, MOD_COL_TILE), lambda l, j: (l, 0, j)),
        ],
        out_specs=pl.BlockSpec((None, MOD_ROWS, MOD_COL_TILE), lambda l, j: (l, 0, j)),
        out_shape=jax.ShapeDtypeStruct((n_layers, MOD_ROWS, width), F32),
        compiler_params=pltpu.CompilerParams(
            dimension_semantics=("arbitrary", "arbitrary"), vmem_limit_bytes=VMEM_LIMIT),
        name="adaln_modulation",
    )(cc, w_ada, b_ada.reshape(n_layers, 1, width))


def _ffn_proj_kernel(*refs, n_src, n_tiles, tpb, n_lat):
    h_refs = refs[:n_src]
    (mod_ref, modp_ref, g1_ref, w1_ref, w3_ref, w2_ref, gmix_ref, win_ref,
     gcq_ref, wuq_ref, gmq_ref, gckv_ref, wuk_ref, wuv_ref, gmk_ref, gsq_ref, gsk_ref, dft_ref,
     cm_ref, sam_ref, sbm_ref, cs_ref, sas_ref, sbs_ref,
     ho_ref, q_ref, k_ref, v_ref, sq_ref, sk_ref, sv_ref, z_ref, hprev_ref) = refs[n_src:]
    step = pl.program_id(0)

    @pl.when(step == 0)
    def _():
        hprev_ref[...] = jnp.zeros_like(hprev_ref)

    modp = modp_ref[...]
    n = _norm_mod(hprev_ref[...], gmix_ref[...], modp[3:4], modp[4:5]).astype(BF16)
    u = jnp.dot(n, win_ref[...], preferred_element_type=F32)

    f = u[:, OFF_F:OFF_F + FOURIER_WIDTH].astype(BF16)
    z_ref[...] = jnp.dot(f, dft_ref[...], preferred_element_type=F32).astype(BF16)

    cm, sam, sbm = cm_ref[...], sam_ref[...], sbm_ref[...]
    cq = u[:, OFF_CQ:OFF_CQ + MLA_Q_RANK]
    cqn = (cq * _rms_scale(cq, MLA_Q_RANK) * gcq_ref[...]).astype(BF16)
    q = jnp.dot(cqn, wuq_ref[...], preferred_element_type=F32)
    gmq = gmq_ref[...]
    q_scale = MLA_QK_DIM ** -0.5 * LOG2E
    for hd in range(MLA_HEADS):
        qh = q[:, hd * LANES:(hd + 1) * LANES]
        qg = qh * (_rms_scale(qh, MLA_QK_DIM) * q_scale) * gmq
        q_ref[:, hd * LANES:(hd + 1) * LANES] = _rope(qg, cm, sam, sbm).astype(BF16)

    ckv = u[:, OFF_CKV:OFF_CKV + MLA_KV_RANK]
    ckvn = (ckv * _rms_scale(ckv, MLA_KV_RANK) * gckv_ref[...]).astype(BF16)
    kn = jnp.dot(ckvn, wuk_ref[...], preferred_element_type=F32)
    vv = jnp.dot(ckvn, wuv_ref[...], preferred_element_type=F32).astype(BF16)
    ones = jnp.ones((vv.shape[0], LANES), BF16)
    for pair in range(MLA_HEADS // 2):
        v_ref[:, 2 * pair * LANES:(2 * pair + 1) * LANES] = vv[:, pair * LANES:(pair + 1) * LANES]
        v_ref[:, (2 * pair + 1) * LANES:(2 * pair + 2) * LANES] = ones
    gmk = gmk_ref[...]
    kr = u[:, OFF_KR:OFF_KR + LANES]
    kr_ss = jnp.sum(kr * kr, axis=-1, keepdims=True)
    kr_rot = _rope(kr * gmk, cm, sam, sbm)
    for hd in range(MLA_HEADS):
        kh = kn[:, hd * LANES:(hd + 1) * LANES]
        ss = jnp.sum(kh * kh, axis=-1, keepdims=True) + kr_ss
        rs = lax.rsqrt(ss * (1.0 / MLA_QK_DIM) + EPS)
        k_ref[:, hd * LANES:(hd + 1) * LANES] = (rs * (kh * gmk + kr_rot)).astype(BF16)

    cs, sas, sbs = cs_ref[...], sas_ref[...], sbs_ref[...]

    def two_head_norm_rope(x, g, scale):
        low = _low_lanes(x.shape)
        x2 = x * x
        lo = jnp.sum(jnp.where(low, x2, 0.0), axis=-1, keepdims=True)
        hi = jnp.sum(jnp.where(low, 0.0, x2), axis=-1, keepdims=True)
        rs = jnp.where(low, lax.rsqrt(lo * (1.0 / SWA_HEAD_DIM) + EPS),
                       lax.rsqrt(hi * (1.0 / SWA_HEAD_DIM) + EPS))
        return _rope(x * (rs * scale) * g, cs, sas, sbs).astype(BF16)

    gsq = gsq_ref[...]
    s_scale = SWA_HEAD_DIM ** -0.5 * LOG2E
    sq_ref[:, 0:LANES] = two_head_norm_rope(u[:, OFF_SQA:OFF_SQA + LANES], gsq, s_scale)
    sq_ref[:, LANES:2 * LANES] = two_head_norm_rope(u[:, OFF_SQB:OFF_SQB + LANES], gsq, s_scale)
    sk_ref[...] = two_head_norm_rope(u[:, OFF_SK:OFF_SK + LANES], gsk_ref[...], 1.0)
    sv_ref[:, 0:LANES] = u[:, OFF_SV:OFF_SV + LANES].astype(BF16)
    sv_ref[:, LANES:2 * LANES] = ones

    mod = mod_ref[...]
    if n_src == 1:
        h = h_refs[0][...]
    else:
        cur = jnp.minimum(step, n_tiles - 1)
        h = jnp.where(cur % tpb >= n_lat, h_refs[1][...], h_refs[0][...])
    xn = _norm_mod(h, g1_ref[...], mod[0:1], mod[1:2]).astype(BF16)
    h = h + (0.5 * mod[2:3]) * _swiglu(xn, w1_ref, w3_ref, w2_ref)
    ho_ref[...] = h
    hprev_ref[...] = h


def _const_spec(block_shape, index_map):
    return pl.BlockSpec(block_shape, index_map, pipeline_mode=pl.Buffered(1))


def _ffn_proj(layer, tokens, mod, p, tabs, seq, t_all):
    n_batch, _, d = tokens[0].shape
    tm = TOK_TILE
    tpb = t_all // tm
    n_tiles = n_batch * tpb
    n_lat = seq // tm
    cur = lambda s: jnp.minimum(s, n_tiles - 1)
    prev = lambda s: jnp.maximum(s - 1, 0)
    tile = lambda which, width: pl.BlockSpec(
        (None, tm, width), lambda s: (which(s) // tpb, which(s) % tpb, 0))

    def token_specs(which):
        if len(tokens) == 1:
            return [tile(which, d)]
        lat = pl.BlockSpec((None, tm, d),
                           lambda s: (which(s) // tpb, jnp.minimum(which(s) % tpb, n_lat - 1), 0))
        con = pl.BlockSpec((None, tm, d),
                           lambda s: (which(s) // tpb, jnp.maximum(which(s) % tpb - n_lat, 0), 0))
        return [lat, con]

    def mod_row(which):
        def index(s):
            b, t = which(s) // tpb, which(s) % tpb
            return (layer, jnp.where(t >= n_lat, n_batch, b), 0, 0)
        return pl.BlockSpec((None, None, N_MOD, d), index)

    lw = lambda *shape: _const_spec((None,) + shape, lambda s: (layer,) + (0,) * len(shape))
    tab = pl.BlockSpec((tm, LANES), lambda s: (prev(s) % tpb, 0))
    in_specs = token_specs(cur) + [
        mod_row(cur), mod_row(prev),
        lw(1, d), lw(d, D_FF), lw(d, D_FF), lw(D_FF, d), lw(1, d), lw(d, IN_WIDTH_P),
        lw(1, MLA_Q_RANK), lw(MLA_Q_RANK, MLA_HEADS * LANES), lw(1, LANES),
        lw(1, MLA_KV_RANK), lw(MLA_KV_RANK, MLA_HEADS * LANES), lw(MLA_KV_RANK, MLA_HEADS * MLA_V),
        lw(1, LANES), lw(1, LANES), lw(1, LANES),
        _const_spec((FOURIER_WIDTH, 2 * FOURIER_WIDTH), lambda s: (0, 0)),
        tab, tab, tab, tab, tab, tab,
    ]
    widths = (d, MLA_HEADS * LANES, MLA_HEADS * LANES, MLA_V_SLABS * LANES, 2 * LANES, LANES, 2 * LANES,
              2 * FOURIER_WIDTH)
    dtypes = (F32,) + (BF16,) * 7
    return pl.pallas_call(
        functools.partial(_ffn_proj_kernel, n_src=len(tokens), n_tiles=n_tiles, tpb=tpb, n_lat=n_lat),
        grid=(n_tiles + 1,),
        in_specs=in_specs,
        out_specs=[tile(cur, d)] + [tile(prev, w) for w in widths[1:]],
        out_shape=[jax.ShapeDtypeStruct((n_batch, t_all, w), dt) for w, dt in zip(widths, dtypes)],
        scratch_shapes=[pltpu.VMEM((tm, d), F32)],
        compiler_params=pltpu.CompilerParams(
            dimension_semantics=("arbitrary",), vmem_limit_bytes=VMEM_LIMIT),
        name="ffn_half_and_projections",
    )(*tokens, mod, mod, p["g_ffn1"], p["w1_ffn1"], p["w3_ffn1"], p["w2_ffn1"], p["g_mix"], p["w_in"],
      p["g_cq"], p["w_uq"], p["g_mq"], p["g_ckv"], p["w_uk"], p["w_uv"], p["g_mk"],
      p["g_sq"], p["g_sk"], p["dft64"], *tabs)


def _fourier_kernel(z_ref, cl_ref, sl_ref, cc_ref, sc_ref, o_ref, *, ctx, with_ctx):
    w = FOURIER_WIDTH

    def mix(c_ref, s_ref, lo, n):
        zc = z_ref[lo:lo + n, 0:w]
        zs = z_ref[lo:lo + n, w:2 * w]
        return (jnp.dot(c_ref[...], zc, preferred_element_type=F32)
                - jnp.dot(s_ref[...], zs, preferred_element_type=F32)).astype(BF16)

    n_lat = cl_ref.shape[0]
    o_ref[0:n_lat, :] = mix(cl_ref, sl_ref, 0, n_lat)
    if with_ctx:
        o_ref[n_lat:n_lat + ctx, :] = mix(cc_ref, sc_ref, n_lat, ctx)


def _fourier(zcs, dfts, ctx, with_ctx):
    n_batch, t_all, _ = zcs.shape
    seq = t_all - ctx
    rows = t_all if with_ctx else seq
    full = lambda a: _const_spec(a.shape, lambda b: (0, 0))
    return pl.pallas_call(
        functools.partial(_fourier_kernel, ctx=ctx, with_ctx=with_ctx),
        grid=(n_batch,),
        in_specs=[pl.BlockSpec((None, t_all, 2 * FOURIER_WIDTH), lambda b: (b, 0, 0))]
        + [full(a) for a in dfts],
        out_specs=pl.BlockSpec((None, rows, FOURIER_WIDTH), lambda b: (b, 0, 0)),
        out_shape=jax.ShapeDtypeStruct((n_batch, rows, FOURIER_WIDTH), BF16),
        compiler_params=pltpu.CompilerParams(
            dimension_semantics=("arbitrary",), vmem_limit_bytes=VMEM_LIMIT),
        name="fourier_positions",
    )(zcs, *dfts)


def _scores(q, k):
    return lax.dot_general(q, k, (((1,), (1,)), ((), ())), preferred_element_type=F32)


def _mla_heads(q_ref, k_ref, v_ref, o_ref, head0_scores=None):
    n_heads = q_ref.shape[-1] // LANES
    for pair in range(n_heads // 2):
        outs = []
        vp = v_ref[:, 2 * pair * LANES:(2 * pair + 2) * LANES]
        for hd in (2 * pair, 2 * pair + 1):
            if hd == 0 and head0_scores is not None:
                s = head0_scores
            else:
                s = _scores(q_ref[:, hd * LANES:(hd + 1) * LANES], k_ref[:, hd * LANES:(hd + 1) * LANES])
            m = jnp.max(s, axis=-1, keepdims=True)
            p = jnp.exp2(s - m).astype(BF16)
            r = jnp.dot(p, vp, preferred_element_type=F32)
            outs.append(r[:, 0:LANES] / r[:, LANES:2 * LANES])
        o_ref[:, pair * LANES:(pair + 1) * LANES] = jnp.where(
            _low_lanes(outs[0].shape), outs[0], outs[1]).astype(BF16)


def _mla_latent_kernel(q_ref, k_ref, v_ref, qn_ref, kn_ref, o_ref, s0_ref):
    @pl.when(pl.program_id(0) == 0)
    def _():
        s0_ref[...] = _scores(q_ref[:, 0:LANES], k_ref[:, 0:LANES])

    _mla_heads(q_ref, k_ref, v_ref, o_ref, head0_scores=s0_ref[...])
    s0_ref[...] = _scores(qn_ref[...], kn_ref[...])


def _mla(q, k, v, seq, with_ctx):
    n_batch, t_all, _ = q.shape
    n_ctx = t_all - seq
    tq = MLA_Q_TILE
    tpb = seq // tq
    n_tiles = n_batch * tpb
    qk_w = MLA_HEADS * LANES
    v_in = MLA_V_SLABS * LANES
    v_w = MLA_HEADS * MLA_V
    params = lambda n: pltpu.CompilerParams(
        dimension_semantics=("arbitrary",) * n, vmem_limit_bytes=VMEM_LIMIT)
    nxt = lambda s: jnp.minimum(s + 1, n_tiles - 1)
    a = pl.pallas_call(
        _mla_latent_kernel,
        grid=(n_tiles,),
        in_specs=[
            pl.BlockSpec((None, tq, qk_w), lambda s: (s // tpb, s % tpb, 0)),
            pl.BlockSpec((None, t_all, qk_w), lambda s: (s // tpb, 0, 0)),
            pl.BlockSpec((None, t_all, v_in), lambda s: (s // tpb, 0, 0)),
            pl.BlockSpec((None, tq, LANES), lambda s: (nxt(s) // tpb, nxt(s) % tpb, 0)),
            pl.BlockSpec((None, t_all, LANES), lambda s: (nxt(s) // tpb, 0, 0)),
        ],
        out_specs=pl.BlockSpec((None, tq, v_w), lambda s: (s // tpb, s % tpb, 0)),
        out_shape=jax.ShapeDtypeStruct((n_batch, seq, v_w), BF16),
        scratch_shapes=[pltpu.VMEM((tq, t_all), F32)],
        compiler_params=params(1),
        name="mla_attention",
    )(q, k, v, q, k)
    if not with_ctx:
        return a, None
    c_blk = seq // n_ctx
    ctx_rows = lambda width: pl.BlockSpec((None, n_ctx, width), lambda b: (b, c_blk, 0))
    a_ctx = pl.pallas_call(
        _mla_heads,
        grid=(n_batch,),
        in_specs=[ctx_rows(qk_w), ctx_rows(qk_w), ctx_rows(v_in)],
        out_specs=pl.BlockSpec((None, n_ctx, v_w), lambda b: (b, 0, 0)),
        out_shape=jax.ShapeDtypeStruct((n_batch, n_ctx, v_w), BF16),
        compiler_params=params(1),
        name="mla_context_attention",
    )(q, k, v)
    return a, a_ctx


def _swa_heads(sink_ref, layer, q_ref, keys, vals, valid, o_ref, r0=0, rows=None):
    rows = q_ref.shape[0] if rows is None else rows
    res = []
    for hq in range(SWA_Q_HEADS):
        slab = hq % 2
        use_low = hq < 2
        qs = q_ref[r0:r0 + rows, slab * LANES:(slab + 1) * LANES]
        low = _low_lanes(qs.shape)
        qm = jnp.where(low if use_low else jnp.logical_not(low), qs, jnp.zeros_like(qs))
        sink = sink_ref[layer, hq] * LOG2E
        s = _scores(qm, keys)
        if valid is not None:
            s = jnp.where(valid, s, NEG)
        m = jnp.maximum(jnp.max(s, axis=-1, keepdims=True), sink)
        p = jnp.exp2(s - m).astype(BF16)
        r = jnp.dot(p, vals, preferred_element_type=F32)
        res.append(r[:, 0:LANES] / (r[:, LANES:2 * LANES] + jnp.exp2(sink - m)))
    low = _low_lanes(res[0].shape)
    o_ref[r0:r0 + rows, 0:LANES] = jnp.where(low, res[0], res[2]).astype(BF16)
    o_ref[r0:r0 + rows, LANES:2 * LANES] = jnp.where(low, res[1], res[3]).astype(BF16)


def _swa_latent_kernel(sink_ref, q_ref, k_ref, v_ref, o_ref, *, layer, seq):
    tq = q_ref.shape[0]
    rows = min(tq, ROW_CHUNK)
    span = rows + 2 * SWA_WINDOW
    n_keys = span + k_ref.shape[0] - seq
    kc, vc = k_ref[seq:, :], v_ref[seq:, :]
    for r0 in range(0, tq, rows):
        first = pl.program_id(1) * tq + r0
        start = pl.multiple_of(jnp.clip(first - SWA_WINDOW, 0, seq - span), LANES)
        keys = jnp.concatenate([k_ref[pl.ds(start, span), :], kc], axis=0)
        vals = jnp.concatenate([v_ref[pl.ds(start, span), :], vc], axis=0)
        col = lax.broadcasted_iota(jnp.int32, (rows, n_keys), 1)
        dist = first + lax.broadcasted_iota(jnp.int32, (rows, n_keys), 0) - (start + col)
        valid = ((dist <= SWA_WINDOW) & (dist >= -SWA_WINDOW)) | (col >= span)
        _swa_heads(sink_ref, layer, q_ref, keys, vals, valid, o_ref, r0, rows)


def _swa_ctx_kernel(sink_ref, q_ref, k_ref, v_ref, o_ref, *, layer):
    _swa_heads(sink_ref, layer, q_ref, k_ref[...], v_ref[...], None, o_ref)


def _swa(layer, sink, q, k, v, seq, with_ctx):
    n_batch, t_all, _ = q.shape
    n_ctx = t_all - seq
    tq = Q_TILE
    params = lambda n: pltpu.CompilerParams(
        dimension_semantics=("arbitrary",) * n, vmem_limit_bytes=VMEM_LIMIT)
    smem = pl.BlockSpec(memory_space=pltpu.SMEM)
    w = pl.pallas_call(
        functools.partial(_swa_latent_kernel, layer=layer, seq=seq),
        grid=(n_batch, seq // tq),
        in_specs=[
            smem,
            pl.BlockSpec((None, tq, 2 * LANES), lambda b, i: (b, i, 0)),
            pl.BlockSpec((None, t_all, LANES), lambda b, i: (b, 0, 0)),
            pl.BlockSpec((None, t_all, 2 * LANES), lambda b, i: (b, 0, 0)),
        ],
        out_specs=pl.BlockSpec((None, tq, 2 * LANES), lambda b, i: (b, i, 0)),
        out_shape=jax.ShapeDtypeStruct((n_batch, seq, 2 * LANES), BF16),
        compiler_params=params(2),
        name="window_attention",
    )(sink, q, k, v)
    if not with_ctx:
        return w, None
    c_blk = seq // n_ctx
    ctx_rows = lambda width: pl.BlockSpec((None, n_ctx, width), lambda b: (b, c_blk, 0))
    w_ctx = pl.pallas_call(
        functools.partial(_swa_ctx_kernel, layer=layer),
        grid=(n_batch,),
        in_specs=[smem, ctx_rows(2 * LANES), ctx_rows(LANES), ctx_rows(2 * LANES)],
        out_specs=pl.BlockSpec((None, n_ctx, 2 * LANES), lambda b: (b, 0, 0)),
        out_shape=jax.ShapeDtypeStruct((n_batch, n_ctx, 2 * LANES), BF16),
        compiler_params=params(1),
        name="window_context_attention",
    )(sink, q, k, v)
    return w, w_ctx


def _ctx_mixers_kernel(sink_ref, q_ref, k_ref, v_ref, sq_ref, sk_ref, sv_ref, a_ref, w_ref, *, layer):
    _mla_heads(q_ref, k_ref, v_ref, a_ref)
    _swa_heads(sink_ref, layer, sq_ref, sk_ref[...], sv_ref[...], None, w_ref)


def _ctx_mixers(layer, sink, q, k, v, sq, sk, sv, seq):
    n_batch, t_all, _ = q.shape
    n_ctx = t_all - seq
    c_blk = seq // n_ctx
    rows_in = lambda t: pl.BlockSpec((None, n_ctx, t.shape[-1]), lambda b: (b, c_blk, 0))
    rows_out = lambda width: pl.BlockSpec((None, n_ctx, width), lambda b: (b, 0, 0))
    widths = (MLA_HEADS * MLA_V, 2 * LANES)
    return pl.pallas_call(
        functools.partial(_ctx_mixers_kernel, layer=layer),
        grid=(n_batch,),
        in_specs=[pl.BlockSpec(memory_space=pltpu.SMEM)] + [rows_in(t) for t in (q, k, v, sq, sk, sv)],
        out_specs=[rows_out(w) for w in widths],
        out_shape=[jax.ShapeDtypeStruct((n_batch, n_ctx, w), BF16) for w in widths],
        compiler_params=pltpu.CompilerParams(
            dimension_semantics=("arbitrary",), vmem_limit_bytes=VMEM_LIMIT),
        name="context_query_mixers",
    )(sink, q, k, v, sq, sk, sv)


def _out_ffn_kernel(h_ref, yf_ref, *refs, parts, hpb, n_lat):
    half = TOK_TILE
    if hpb is None:
        a_ref, w_ref = refs[:2]
        a_all, w_all = a_ref[...], w_ref[...]
        refs = refs[2:]
    else:
        a_parts, w_parts = [], []
        for i in range(parts):
            a_lat, a_ctx, w_lat, w_ctx = refs[4 * i:4 * i + 4]
            is_ctx = (parts * pl.program_id(0) + i) % hpb >= n_lat
            a_parts.append(jnp.where(is_ctx, a_ctx[...], a_lat[...]))
            w_parts.append(jnp.where(is_ctx, w_ctx[...], w_lat[...]))
        a_all, w_all = jnp.concatenate(a_parts, axis=0), jnp.concatenate(w_parts, axis=0)
        refs = refs[4 * parts:]
    mod_refs = refs[:parts]
    wo_ref, g2_ref, w1_ref, w3_ref, w2_ref, o_ref = refs[parts:]
    n_f = yf_ref.shape[-1]
    n_a = a_all.shape[-1]
    mixed = (jnp.dot(yf_ref[...], wo_ref[0:n_f, :], preferred_element_type=F32)
             + jnp.dot(a_all, wo_ref[n_f:n_f + n_a, :], preferred_element_type=F32)
             + jnp.dot(w_all, wo_ref[n_f + n_a:, :], preferred_element_type=F32))
    g2 = g2_ref[...]
    hs, xs = [], []
    for i, mod_ref in enumerate(mod_refs):
        mod = mod_ref[...]
        rows = slice(i * half, (i + 1) * half)
        h = h_ref[rows, :] + mod[5:6] * mixed[rows, :]
        hs.append(h)
        xs.append(_norm_mod(h, g2, mod[6:7], mod[7:8]).astype(BF16))
    y = _swiglu(jnp.concatenate(xs, axis=0), w1_ref, w3_ref, w2_ref)
    for i, mod_ref in enumerate(mod_refs):
        rows = slice(i * half, (i + 1) * half)
        o_ref[rows, :] = hs[i] + (0.5 * mod_ref[8:9, :]) * y[rows, :]


def _out_ffn(layer, hh, yf, a, w, mod, p, seq, with_ctx):
    n_batch, t_all, d = hh.shape
    half = TOK_TILE
    parts = OUT_PARTS if with_ctx else OUT_PARTS_LATENT
    tm = parts * half
    hpb = t_all // half
    n_lat = seq // half
    lw = lambda *shape: _const_spec((None,) + shape, lambda *_: (layer,) + (0,) * len(shape))

    def mod_row(half_index):
        def index(*g):
            j = half_index(*g)
            return (layer, jnp.where(j % hpb >= n_lat, n_batch, j // hpb), 0, 0)
        return pl.BlockSpec((None, None, N_MOD, d), index)

    (a_lat, a_ctx), (w_lat, w_ctx) = a, w
    if with_ctx:
        flat = lambda t: t.reshape(n_batch * t_all, t.shape[-1])
        grid = (n_batch * t_all // tm,)
        rows = lambda width: pl.BlockSpec((tm, width), lambda j: (j, 0))
        part = lambda i: (lambda j: parts * j + i)
        lat = lambda i, width: pl.BlockSpec(
            (None, half, width), lambda j: (part(i)(j) // hpb, jnp.minimum(part(i)(j) % hpb, n_lat - 1), 0))
        con = lambda i, width: pl.BlockSpec(
            (None, half, width), lambda j: (part(i)(j) // hpb, jnp.maximum(part(i)(j) % hpb - n_lat, 0), 0))
        arrays, specs = [flat(hh), flat(yf)], [rows(d), rows(yf.shape[-1])]
        for i in range(parts):
            arrays += [a_lat, a_ctx, w_lat, w_ctx]
            specs += [lat(i, a_lat.shape[-1]), con(i, a_ctx.shape[-1]),
                      lat(i, w_lat.shape[-1]), con(i, w_ctx.shape[-1])]
        mods = [mod_row(part(i)) for i in range(parts)]
        out_shape = jax.ShapeDtypeStruct((n_batch * t_all, d), F32)
    else:
        grid = (n_batch, seq // tm)
        rows = lambda width: pl.BlockSpec((None, tm, width), lambda b, t: (b, t, 0))
        arrays = [hh, yf, a_lat, w_lat]
        specs = [rows(t.shape[-1]) for t in arrays]
        mods = [mod_row(lambda b, t: b * hpb)] * parts
        out_shape = jax.ShapeDtypeStruct((n_batch, seq, d), F32)
    out = pl.pallas_call(
        functools.partial(_out_ffn_kernel, parts=parts, hpb=hpb if with_ctx else None, n_lat=n_lat),
        grid=grid,
        in_specs=specs + mods + [lw(d, d), lw(1, d), lw(d, D_FF), lw(d, D_FF), lw(D_FF, d)],
        out_specs=rows(d),
        out_shape=out_shape,
        compiler_params=pltpu.CompilerParams(
            dimension_semantics=("arbitrary",) * len(grid), vmem_limit_bytes=VMEM_LIMIT),
        name="out_projection_and_ffn_half",
    )(*arrays, *([mod] * parts), p["w_out"], p["g_ffn2"], p["w1_ffn2"], p["w3_ffn2"], p["w2_ffn2"])
    return out.reshape(n_batch, t_all, d) if with_ctx else out


def _prepare_params(g_ffn1, w1_ffn1, w3_ffn1, w2_ffn1, g_mix, w_in, g_cq, w_uq, g_ckv, w_ukv,
                    g_mla_q, g_mla_k, g_swa_q, g_swa_k, w_out, g_ffn2, w1_ffn2, w3_ffn2, w2_ffn2):
    row = lambda g: g[:, None, :]
    two_heads = lambda g: jnp.concatenate([g, g], axis=-1)
    kv = w_ukv.reshape(w_ukv.shape[:-1] + (MLA_HEADS, MLA_NOPE + MLA_V))
    flat = lambda t: t.reshape(t.shape[:-2] + (t.shape[-2] * t.shape[-1],))
    return {
        "g_ffn1": row(g_ffn1), "w1_ffn1": w1_ffn1.astype(BF16), "w3_ffn1": w3_ffn1.astype(BF16),
        "w2_ffn1": w2_ffn1.astype(BF16),
        "g_mix": row(g_mix), "w_in": _take_cols(w_in, _w_in_cols()).astype(BF16),
        "g_cq": row(g_cq), "w_uq": _pad_heads(w_uq, MLA_HEADS).astype(BF16),
        "g_mq": row(_pad_heads(g_mla_q, 1)),
        "g_ckv": row(g_ckv), "w_uk": _pad_heads(flat(kv[..., :MLA_NOPE]), MLA_HEADS).astype(BF16),
        "w_uv": flat(kv[..., MLA_NOPE:]).astype(BF16),
        "g_mk": row(_pad_heads(g_mla_k, 1)),
        "g_sq": row(two_heads(g_swa_q)),
        "g_sk": row(two_heads(g_swa_k)),
        "w_out": _take_cols(w_out, _w_out_rows(), axis=1).astype(BF16),
        "g_ffn2": row(g_ffn2), "w1_ffn2": w1_ffn2.astype(BF16), "w3_ffn2": w3_ffn2.astype(BF16),
        "w2_ffn2": w2_ffn2.astype(BF16),
        "dft64": _channel_dft(),
    }


def kernel(x, c, ctx, c_ctx, w_ada, b_ada, g_ffn1, w1_ffn1, w3_ffn1, w2_ffn1, g_mix, w_in, g_cq, w_uq,
           g_ckv, w_ukv, g_mla_q, g_mla_k, g_swa_q, g_swa_k, sink, w_out, g_ffn2, w1_ffn2, w3_ffn2,
           w2_ffn2):
    n_batch, seq, d = x.shape
    n_ctx = ctx.shape[1]
    depth = w_ada.shape[0]
    assert d == D_MODEL and seq % GRID_W == 0 and n_batch + 1 <= MOD_ROWS
    assert n_ctx % TOK_TILE == 0 and seq % TOK_TILE == 0 and seq % n_ctx == 0
    assert seq % (OUT_PARTS_LATENT * TOK_TILE) == 0 and (n_batch * (seq + n_ctx)) % (OUT_PARTS * TOK_TILE) == 0
    assert seq % Q_TILE == 0 and seq % MLA_Q_TILE == 0
    assert Q_TILE % ROW_CHUNK == 0 and ROW_CHUNK + 2 * SWA_WINDOW <= seq
    assert w_ada.shape[-1] % MOD_COL_TILE == 0

    p = _prepare_params(g_ffn1, w1_ffn1, w3_ffn1, w2_ffn1, g_mix, w_in, g_cq, w_uq, g_ckv, w_ukv,
                        g_mla_q, g_mla_k, g_swa_q, g_swa_k, w_out, g_ffn2, w1_ffn2, w3_ffn2, w2_ffn2)
    tabs = _rope_tables(seq, n_ctx)
    c_lat, s_lat = _dft_cos_sin(seq, seq ** -0.5)
    c_ctx_dft, s_ctx_dft = _dft_cos_sin(n_ctx, n_ctx ** -0.5)
    dfts = tuple(m.astype(BF16) for m in (c_lat, s_lat, c_ctx_dft, s_ctx_dft))

    cc = jnp.concatenate([c, c_ctx[None, :], jnp.zeros((MOD_ROWS - n_batch - 1, d), F32)], axis=0)
    mod = _modulation(cc, w_ada, b_ada).reshape(depth, MOD_ROWS, N_MOD, d)

    tokens = (x, ctx)
    for layer in range(depth):
        with_ctx = layer != depth - 1
        hh, q, k, v, sq, sk, sv, zcs = _ffn_proj(layer, tokens, mod, p, tabs, seq, seq + n_ctx)
        yf = _fourier(zcs, dfts, n_ctx, with_ctx)
        a = _mla(q, k, v, seq, False)
        w = _swa(layer, sink, sq, sk, sv, seq, False)
        if with_ctx:
            a_ctx, w_ctx = _ctx_mixers(layer, sink, q, k, v, sq, sk, sv, seq)
            a, w = (a[0], a_ctx), (w[0], w_ctx)
        hh = _out_ffn(layer, hh, yf, a, w, mod, p, seq, with_ctx)
        tokens = (hh,)
    return hh
```

```python
import functools

import numpy as np
import jax
import jax.numpy as jnp
from jax import lax
from jax.experimental import pallas as pl
from jax.experimental.pallas import tpu as pltpu

F32 = jnp.float32
BF16 = jnp.bfloat16

D_MODEL = 1024
GRID_W = 64
ROPE_BASE = 10000.0
EPS = 1e-6
NEG = -1e30
LOG2E = 1.4426950408889634
N_MOD = 9
D_FF = 2816

FOURIER_WIDTH = 256
FOURIER_GROUP_DIM = 64
MLA_HEADS = 8
MLA_NOPE = 64
MLA_ROPE = 32
MLA_V = 64
MLA_QK_DIM = MLA_NOPE + MLA_ROPE
MLA_Q_RANK = 256
MLA_KV_RANK = 128
SWA_Q_HEADS = 4
SWA_KV_HEADS = 2
SWA_HEAD_DIM = 64
SWA_WINDOW = 128
IN_SPLITS = (256, 256, 128, 32, 256, 128, 128)
IN_WIDTH = sum(IN_SPLITS)

LANES = 128
HALF = LANES // 2
IN_WIDTH_P = 10 * LANES
MLA_V_SLABS = MLA_HEADS
VMEM_LIMIT = 56 * 1024 * 1024

TOK_TILE = 256
OUT_PARTS = 2
OUT_PARTS_LATENT = 2
Q_TILE = 1024
MLA_Q_TILE = 1024
ROW_CHUNK = 256
MOD_ROWS = 24
MOD_COL_TILE = 2304

OFF_F, OFF_CQ, OFF_CKV, OFF_KR, OFF_SQA, OFF_SQB, OFF_SK, OFF_SV = (
    0, 256, 512, 640, 768, 896, 1024, 1152)


def _w_in_cols():
    zero = IN_WIDTH
    o_f, o_cq, o_ckv, o_kr, o_sq, o_sk, o_sv = np.cumsum((0,) + IN_SPLITS)[:-1]
    cols = np.full((IN_WIDTH_P,), zero, np.int32)
    cols[OFF_F:OFF_F + 256] = o_f + np.arange(256)
    cols[OFF_CQ:OFF_CQ + 256] = o_cq + np.arange(256)
    cols[OFF_CKV:OFF_CKV + 128] = o_ckv + np.arange(128)
    cols[OFF_KR + MLA_NOPE:OFF_KR + MLA_QK_DIM] = o_kr + np.arange(MLA_ROPE)
    head = np.arange(SWA_HEAD_DIM)
    cols[OFF_SQA:OFF_SQA + 64] = o_sq + 0 * 64 + head
    cols[OFF_SQA + 64:OFF_SQA + 128] = o_sq + 2 * 64 + head
    cols[OFF_SQB:OFF_SQB + 64] = o_sq + 1 * 64 + head
    cols[OFF_SQB + 64:OFF_SQB + 128] = o_sq + 3 * 64 + head
    cols[OFF_SK:OFF_SK + 64] = o_sk + head
    cols[OFF_SK + 64:OFF_SK + 128] = o_sk + 64 + head
    cols[OFF_SV:OFF_SV + 128] = o_sv + np.arange(128)
    return cols


def _pad_heads(w, n_heads):
    lead, width = w.shape[:-1], w.shape[-1] // n_heads
    w = jnp.pad(w.reshape(lead + (n_heads, width)), [(0, 0)] * (len(lead) + 1) + [(0, LANES - width)])
    return w.reshape(lead + (n_heads * LANES,))


def _w_out_rows():
    base = FOURIER_WIDTH + MLA_HEADS * MLA_V
    swa = np.concatenate([base + h * SWA_HEAD_DIM + np.arange(SWA_HEAD_DIM) for h in (0, 2, 1, 3)])
    return np.concatenate([np.arange(base), swa]).astype(np.int32)


def _take_cols(w, cols, axis=-1):
    axis = axis % w.ndim
    n = w.shape[axis]
    cols = [int(c) for c in cols]
    pieces, i = [], 0
    while i < len(cols):
        j = i + 1
        if cols[i] == n:
            while j < len(cols) and cols[j] == n:
                j += 1
            shape = w.shape[:axis] + (j - i,) + w.shape[axis + 1:]
            pieces.append(jnp.zeros(shape, w.dtype))
        else:
            stride = cols[j] - cols[i] if j < len(cols) and cols[j] - cols[i] in (1, 2) else 1
            while j < len(cols) and cols[j] != n and cols[j] == cols[j - 1] + stride:
                j += 1
            pieces.append(lax.slice_in_dim(w, cols[i], cols[j - 1] + 1, stride, axis))
        i = j
    return jnp.concatenate(pieces, axis=axis)


def _rope_tables(seq, ctx):
    rows = seq // GRID_W
    pad = jnp.zeros((ctx,), F32)
    row = jnp.concatenate([jnp.repeat(jnp.arange(rows, dtype=F32), GRID_W), pad])[:, None]
    col = jnp.concatenate([jnp.tile(jnp.arange(GRID_W, dtype=F32), rows), pad])[:, None]

    def build(dim, section_starts):
        axis_dim = dim // 2
        n_freq = axis_dim // 2
        freq = np.zeros((LANES,), np.float32)
        active = np.zeros((LANES,), bool)
        by_row = np.zeros((LANES,), bool)
        m_a = np.zeros((LANES,), np.float32)
        m_b = np.zeros((LANES,), np.float32)
        pair = np.arange(dim) // 2
        for lo in section_starts:
            freq[lo:lo + dim] = 2 * (pair % n_freq)
            active[lo:lo + dim] = True
            by_row[lo:lo + dim] = pair < n_freq
            m_a[lo:lo + dim:2] = 1.0
            m_b[lo + 1:lo + dim:2] = 1.0
        inv_lane = jnp.where(jnp.asarray(active), ROPE_BASE ** (-jnp.asarray(freq) / axis_dim), 0.0)[None, :]
        ang = jnp.where(jnp.asarray(by_row)[None, :], row * inv_lane, col * inv_lane)
        sin = jnp.sin(ang)
        return jnp.cos(ang), sin * jnp.asarray(-m_a)[None, :], sin * jnp.asarray(m_b)[None, :]

    return build(MLA_ROPE, (MLA_NOPE,)) + build(SWA_HEAD_DIM, (0, HALF))


def _dft_cos_sin(n, scale):
    def direct(rows_j, n_mod):
        k = jnp.arange(n, dtype=jnp.int32)
        ang = ((rows_j[:, None] * k[None, :]) % n_mod).astype(F32) * (2.0 * np.pi / n_mod)
        return jnp.cos(ang), jnp.sin(ang)

    inner = FOURIER_GROUP_DIM
    if n <= inner or n % inner:
        c, s = direct(jnp.arange(n, dtype=jnp.int32), n)
        return c * scale, s * scale
    outer = n // inner
    ca, sa = direct(jnp.arange(outer, dtype=jnp.int32), outer)
    cb, sb = direct(jnp.arange(inner, dtype=jnp.int32), n)
    cb, sb = cb * scale, sb * scale
    c = ca[:, None, :] * cb[None, :, :] - sa[:, None, :] * sb[None, :, :]
    s = sa[:, None, :] * cb[None, :, :] + ca[:, None, :] * sb[None, :, :]
    return c.reshape(n, n), s.reshape(n, n)


def _channel_dft():
    c, s = _dft_cos_sin(FOURIER_GROUP_DIM, FOURIER_GROUP_DIM ** -0.5)
    eye = jnp.eye(FOURIER_WIDTH // FOURIER_GROUP_DIM, dtype=F32)
    return jnp.concatenate([jnp.kron(eye, c), jnp.kron(eye, s)], axis=1).astype(BF16)


def _rms_scale(x, width):
    return lax.rsqrt(jnp.sum(x * x, axis=-1, keepdims=True) * (1.0 / width) + EPS)


def _norm_mod(x, g, shift, scale):
    y = x * _rms_scale(x, x.shape[-1]) * g
    return y * (1.0 + scale) + shift


def _swiglu(xn, w1_ref, w3_ref, w2_ref):
    a = jnp.dot(xn, w1_ref[...], preferred_element_type=F32)
    b = jnp.dot(xn, w3_ref[...], preferred_element_type=F32)
    g = (a / (1.0 + jnp.exp(-a))) * b
    return jnp.dot(g.astype(BF16), w2_ref[...], preferred_element_type=F32)


def _rope(x, cos, sin_a, sin_b):
    return x * cos + pltpu.roll(x, LANES - 1, 1) * sin_a + pltpu.roll(x, 1, 1) * sin_b


def _low_lanes(shape):
    return lax.broadcasted_iota(jnp.int32, shape, len(shape) - 1) < HALF


def _mod_kernel(c_ref, w_ref, b_ref, o_ref):
    cv = c_ref[...]
    s = (cv / (1.0 + jnp.exp(-cv))).astype(BF16)
    o_ref[...] = jnp.dot(s, w_ref[...].astype(BF16), preferred_element_type=F32) + b_ref[...]


def _modulation(cc, w_ada, b_ada):
    n_layers, d, width = w_ada.shape
    return pl.pallas_call(
        _mod_kernel,
        grid=(n_layers, width // MOD_COL_TILE),
        in_specs=[
            pl.BlockSpec((MOD_ROWS, d), lambda l, j: (0, 0)),
            pl.BlockSpec((None, d, MOD_COL_TILE), lambda l, j: (l, 0, j)),
            pl.BlockSpec((None, 1, MOD_COL_TILE), lambda l, j: (l, 0, j)),
        ],
        out_specs=pl.BlockSpec((None, MOD_ROWS, MOD_COL_TILE), lambda l, j: (l, 0, j)),
        out_shape=jax.ShapeDtypeStruct((n_layers, MOD_ROWS, width), F32),
        compiler_params=pltpu.CompilerParams(
            dimension_semantics=("arbitrary", "arbitrary"), vmem_limit_bytes=VMEM_LIMIT),
        name="adaln_modulation",
    )(cc, w_ada, b_ada.reshape(n_layers, 1, width))


def _ffn_proj_kernel(*refs, n_src, n_tiles, tpb, n_lat):
    h_refs = refs[:n_src]
    (mod_ref, modp_ref, g1_ref, w1_ref, w3_ref, w2_ref, gmix_ref, win_ref,
     gcq_ref, wuq_ref, gmq_ref, gckv_ref, wuk_ref, wuv_ref, gmk_ref, gsq_ref, gsk_ref, dft_ref,
     cm_ref, sam_ref, sbm_ref, cs_ref, sas_ref, sbs_ref,
     ho_ref, q_ref, k_ref, v_ref, sq_ref, sk_ref, sv_ref, z_ref, hprev_ref) = refs[n_src:]
    step = pl.program_id(0)

    @pl.when(step == 0)
    def _():
        hprev_ref[...] = jnp.zeros_like(hprev_ref)

    modp = modp_ref[...]
    n = _norm_mod(hprev_ref[...], gmix_ref[...], modp[3:4], modp[4:5]).astype(BF16)
    u = jnp.dot(n, win_ref[...], preferred_element_type=F32)

    f = u[:, OFF_F:OFF_F + FOURIER_WIDTH].astype(BF16)
    z_ref[...] = jnp.dot(f, dft_ref[...], preferred_element_type=F32).astype(BF16)

    cm, sam, sbm = cm_ref[...], sam_ref[...], sbm_ref[...]
    cq = u[:, OFF_CQ:OFF_CQ + MLA_Q_RANK]
    cqn = (cq * _rms_scale(cq, MLA_Q_RANK) * gcq_ref[...]).astype(BF16)
    q = jnp.dot(cqn, wuq_ref[...], preferred_element_type=F32)
    gmq = gmq_ref[...]
    q_scale = MLA_QK_DIM ** -0.5 * LOG2E
    for hd in range(MLA_HEADS):
        qh = q[:, hd * LANES:(hd + 1) * LANES]
        qg = qh * (_rms_scale(qh, MLA_QK_DIM) * q_scale) * gmq
        q_ref[:, hd * LANES:(hd + 1) * LANES] = _rope(qg, cm, sam, sbm).astype(BF16)

    ckv = u[:, OFF_CKV:OFF_CKV + MLA_KV_RANK]
    ckvn = (ckv * _rms_scale(ckv, MLA_KV_RANK) * gckv_ref[...]).astype(BF16)
    kn = jnp.dot(ckvn, wuk_ref[...], preferred_element_type=F32)
    vv = jnp.dot(ckvn, wuv_ref[...], preferred_element_type=F32).astype(BF16)
    ones = jnp.ones((vv.shape[0], LANES), BF16)
    for pair in range(MLA_HEADS // 2):
        v_ref[:, 2 * pair * LANES:(2 * pair + 1) * LANES] = vv[:, pair * LANES:(pair + 1) * LANES]
        v_ref[:, (2 * pair + 1) * LANES:(2 * pair + 2) * LANES] = ones
    gmk = gmk_ref[...]
    kr = u[:, OFF_KR:OFF_KR + LANES]
    kr_ss = jnp.sum(kr * kr, axis=-1, keepdims=True)
    kr_rot = _rope(kr * gmk, cm, sam, sbm)
    for hd in range(MLA_HEADS):
        kh = kn[:, hd * LANES:(hd + 1) * LANES]
        ss = jnp.sum(kh * kh, axis=-1, keepdims=True) + kr_ss
        rs = lax.rsqrt(ss * (1.0 / MLA_QK_DIM) + EPS)
        k_ref[:, hd * LANES:(hd + 1) * LANES] = (rs * (kh * gmk + kr_rot)).astype(BF16)

    cs, sas, sbs = cs_ref[...], sas_ref[...], sbs_ref[...]

    def two_head_norm_rope(x, g, scale):
        low = _low_lanes(x.shape)
        x2 = x * x
        lo = jnp.sum(jnp.where(low, x2, 0.0), axis=-1, keepdims=True)
        hi = jnp.sum(jnp.where(low, 0.0, x2), axis=-1, keepdims=True)
        rs = jnp.where(low, lax.rsqrt(lo * (1.0 / SWA_HEAD_DIM) + EPS),
                       lax.rsqrt(hi * (1.0 / SWA_HEAD_DIM) + EPS))
        return _rope(x * (rs * scale) * g, cs, sas, sbs).astype(BF16)

    gsq = gsq_ref[...]
    s_scale = SWA_HEAD_DIM ** -0.5 * LOG2E
    sq_ref[:, 0:LANES] = two_head_norm_rope(u[:, OFF_SQA:OFF_SQA + LANES], gsq, s_scale)
    sq_ref[:, LANES:2 * LANES] = two_head_norm_rope(u[:, OFF_SQB:OFF_SQB + LANES], gsq, s_scale)
    sk_ref[...] = two_head_norm_rope(u[:, OFF_SK:OFF_SK + LANES], gsk_ref[...], 1.0)
    sv_ref[:, 0:LANES] = u[:, OFF_SV:OFF_SV + LANES].astype(BF16)
    sv_ref[:, LANES:2 * LANES] = ones

    mod = mod_ref[...]
    if n_src == 1:
        h = h_refs[0][...]
    else:
        cur = jnp.minimum(step, n_tiles - 1)
        h = jnp.where(cur % tpb >= n_lat, h_refs[1][...], h_refs[0][...])
    xn = _norm_mod(h, g1_ref[...], mod[0:1], mod[1:2]).astype(BF16)
    h = h + (0.5 * mod[2:3]) * _swiglu(xn, w1_ref, w3_ref, w2_ref)
    ho_ref[...] = h
    hprev_ref[...] = h


def _const_spec(block_shape, index_map):
    return pl.BlockSpec(block_shape, index_map, pipeline_mode=pl.Buffered(1))


def _ffn_proj(layer, tokens, mod, p, tabs, seq, t_all):
    n_batch, _, d = tokens[0].shape
    tm = TOK_TILE
    tpb = t_all // tm
    n_tiles = n_batch * tpb
    n_lat = seq // tm
    cur = lambda s: jnp.minimum(s, n_tiles - 1)
    prev = lambda s: jnp.maximum(s - 1, 0)
    tile = lambda which, width: pl.BlockSpec(
        (None, tm, width), lambda s: (which(s) // tpb, which(s) % tpb, 0))

    def token_specs(which):
        if len(tokens) == 1:
            return [tile(which, d)]
        lat = pl.BlockSpec((None, tm, d),
                           lambda s: (which(s) // tpb, jnp.minimum(which(s) % tpb, n_lat - 1), 0))
        con = pl.BlockSpec((None, tm, d),
                           lambda s: (which(s) // tpb, jnp.maximum(which(s) % tpb - n_lat, 0), 0))
        return [lat, con]

    def mod_row(which):
        def index(s):
            b, t = which(s) // tpb, which(s) % tpb
            return (layer, jnp.where(t >= n_lat, n_batch, b), 0, 0)
        return pl.BlockSpec((None, None, N_MOD, d), index)

    lw = lambda *shape: _const_spec((None,) + shape, lambda s: (layer,) + (0,) * len(shape))
    tab = pl.BlockSpec((tm, LANES), lambda s: (prev(s) % tpb, 0))
    in_specs = token_specs(cur) + [
        mod_row(cur), mod_row(prev),
        lw(1, d), lw(d, D_FF), lw(d, D_FF), lw(D_FF, d), lw(1, d), lw(d, IN_WIDTH_P),
        lw(1, MLA_Q_RANK), lw(MLA_Q_RANK, MLA_HEADS * LANES), lw(1, LANES),
        lw(1, MLA_KV_RANK), lw(MLA_KV_RANK, MLA_HEADS * LANES), lw(MLA_KV_RANK, MLA_HEADS * MLA_V),
        lw(1, LANES), lw(1, LANES), lw(1, LANES),
        _const_spec((FOURIER_WIDTH, 2 * FOURIER_WIDTH), lambda s: (0, 0)),
        tab, tab, tab, tab, tab, tab,
    ]
    widths = (d, MLA_HEADS * LANES, MLA_HEADS * LANES, MLA_V_SLABS * LANES, 2 * LANES, LANES, 2 * LANES,
              2 * FOURIER_WIDTH)
    dtypes = (F32,) + (BF16,) * 7
    return pl.pallas_call(
        functools.partial(_ffn_proj_kernel, n_src=len(tokens), n_tiles=n_tiles, tpb=tpb, n_lat=n_lat),
        grid=(n_tiles + 1,),
        in_specs=in_specs,
        out_specs=[tile(cur, d)] + [tile(prev, w) for w in widths[1:]],
        out_shape=[jax.ShapeDtypeStruct((n_batch, t_all, w), dt) for w, dt in zip(widths, dtypes)],
        scratch_shapes=[pltpu.VMEM((tm, d), F32)],
        compiler_params=pltpu.CompilerParams(
            dimension_semantics=("arbitrary",), vmem_limit_bytes=VMEM_LIMIT),
        name="ffn_half_and_projections",
    )(*tokens, mod, mod, p["g_ffn1"], p["w1_ffn1"], p["w3_ffn1"], p["w2_ffn1"], p["g_mix"], p["w_in"],
      p["g_cq"], p["w_uq"], p["g_mq"], p["g_ckv"], p["w_uk"], p["w_uv"], p["g_mk"],
      p["g_sq"], p["g_sk"], p["dft64"], *tabs)


def _fourier_kernel(z_ref, cl_ref, sl_ref, cc_ref, sc_ref, o_ref, *, ctx, with_ctx):
    w = FOURIER_WIDTH

    def mix(c_ref, s_ref, lo, n):
        zc = z_ref[lo:lo + n, 0:w]
        zs = z_ref[lo:lo + n, w:2 * w]
        return (jnp.dot(c_ref[...], zc, preferred_element_type=F32)
                - jnp.dot(s_ref[...], zs, preferred_element_type=F32)).astype(BF16)

    n_lat = cl_ref.shape[0]
    o_ref[0:n_lat, :] = mix(cl_ref, sl_ref, 0, n_lat)
    if with_ctx:
        o_ref[n_lat:n_lat + ctx, :] = mix(cc_ref, sc_ref, n_lat, ctx)


def _fourier(zcs, dfts, ctx, with_ctx):
    n_batch, t_all, _ = zcs.shape
    seq = t_all - ctx
    rows = t_all if with_ctx else seq
    full = lambda a: _const_spec(a.shape, lambda b: (0, 0))
    return pl.pallas_call(
        functools.partial(_fourier_kernel, ctx=ctx, with_ctx=with_ctx),
        grid=(n_batch,),
        in_specs=[pl.BlockSpec((None, t_all, 2 * FOURIER_WIDTH), lambda b: (b, 0, 0))]
        + [full(a) for a in dfts],
        out_specs=pl.BlockSpec((None, rows, FOURIER_WIDTH), lambda b: (b, 0, 0)),
        out_shape=jax.ShapeDtypeStruct((n_batch, rows, FOURIER_WIDTH), BF16),
        compiler_params=pltpu.CompilerParams(
            dimension_semantics=("arbitrary",), vmem_limit_bytes=VMEM_LIMIT),
        name="fourier_positions",
    )(zcs, *dfts)


def _scores(q, k):
    return lax.dot_general(q, k, (((1,), (1,)), ((), ())), preferred_element_type=F32)


def _mla_heads(q_ref, k_ref, v_ref, o_ref, head0_scores=None):
    n_heads = q_ref.shape[-1] // LANES
    for pair in range(n_heads // 2):
        outs = []
        vp = v_ref[:, 2 * pair * LANES:(2 * pair + 2) * LANES]
        for hd in (2 * pair, 2 * pair + 1):
            if hd == 0 and head0_scores is not None:
                s = head0_scores
            else:
                s = _scores(q_ref[:, hd * LANES:(hd + 1) * LANES], k_ref[:, hd * LANES:(hd + 1) * LANES])
            m = jnp.max(s, axis=-1, keepdims=True)
            p = jnp.exp2(s - m).astype(BF16)
            r = jnp.dot(p, vp, preferred_element_type=F32)
            outs.append(r[:, 0:LANES] / r[:, LANES:2 * LANES])
        o_ref[:, pair * LANES:(pair + 1) * LANES] = jnp.where(
            _low_lanes(outs[0].shape), outs[0], outs[1]).astype(BF16)


def _mla_latent_kernel(q_ref, k_ref, v_ref, qn_ref, kn_ref, o_ref, s0_ref):
    @pl.when(pl.program_id(0) == 0)
    def _():
        s0_ref[...] = _scores(q_ref[:, 0:LANES], k_ref[:, 0:LANES])

    _mla_heads(q_ref, k_ref, v_ref, o_ref, head0_scores=s0_ref[...])
    s0_ref[...] = _scores(qn_ref[...], kn_ref[...])


def _mla(q, k, v, seq):
    n_batch, t_all, _ = q.shape
    tq = MLA_Q_TILE
    tpb = seq // tq
    n_tiles = n_batch * tpb
    qk_w = MLA_HEADS * LANES
    v_in = MLA_V_SLABS * LANES
    v_w = MLA_HEADS * MLA_V
    params = lambda n: pltpu.CompilerParams(
        dimension_semantics=("arbitrary",) * n, vmem_limit_bytes=VMEM_LIMIT)
    nxt = lambda s: jnp.minimum(s + 1, n_tiles - 1)
    return pl.pallas_call(
        _mla_latent_kernel,
        grid=(n_tiles,),
        in_specs=[
            pl.BlockSpec((None, tq, qk_w), lambda s: (s // tpb, s % tpb, 0)),
            pl.BlockSpec((None, t_all, qk_w), lambda s: (s // tpb, 0, 0)),
            pl.BlockSpec((None, t_all, v_in), lambda s: (s // tpb, 0, 0)),
            pl.BlockSpec((None, tq, LANES), lambda s: (nxt(s) // tpb, nxt(s) % tpb, 0)),
            pl.BlockSpec((None, t_all, LANES), lambda s: (nxt(s) // tpb, 0, 0)),
        ],
        out_specs=pl.BlockSpec((None, tq, v_w), lambda s: (s // tpb, s % tpb, 0)),
        out_shape=jax.ShapeDtypeStruct((n_batch, seq, v_w), BF16),
        scratch_shapes=[pltpu.VMEM((tq, t_all), F32)],
        compiler_params=params(1),
        name="mla_attention",
    )(q, k, v, q, k)


def _swa_heads(sink_ref, layer, q_ref, keys, vals, valid, o_ref, r0=0, rows=None):
    rows = q_ref.shape[0] if rows is None else rows
    res = []
    for hq in range(SWA_Q_HEADS):
        slab = hq % 2
        use_low = hq < 2
        qs = q_ref[r0:r0 + rows, slab * LANES:(slab + 1) * LANES]
        low = _low_lanes(qs.shape)
        qm = jnp.where(low if use_low else jnp.logical_not(low), qs, jnp.zeros_like(qs))
        sink = sink_ref[layer, hq] * LOG2E
        s = _scores(qm, keys)
        if valid is not None:
            s = jnp.where(valid, s, NEG)
        m = jnp.maximum(jnp.max(s, axis=-1, keepdims=True), sink)
        p = jnp.exp2(s - m).astype(BF16)
        r = jnp.dot(p, vals, preferred_element_type=F32)
        res.append(r[:, 0:LANES] / (r[:, LANES:2 * LANES] + jnp.exp2(sink - m)))
    low = _low_lanes(res[0].shape)
    o_ref[r0:r0 + rows, 0:LANES] = jnp.where(low, res[0], res[2]).astype(BF16)
    o_ref[r0:r0 + rows, LANES:2 * LANES] = jnp.where(low, res[1], res[3]).astype(BF16)


def _swa_latent_kernel(sink_ref, q_ref, k_ref, v_ref, o_ref, *, layer, seq):
    tq = q_ref.shape[0]
    rows = min(tq, ROW_CHUNK)
    span = rows + 2 * SWA_WINDOW
    n_keys = span + k_ref.shape[0] - seq
    kc, vc = k_ref[seq:, :], v_ref[seq:, :]
    for r0 in range(0, tq, rows):
        first = pl.program_id(1) * tq + r0
        start = pl.multiple_of(jnp.clip(first - SWA_WINDOW, 0, seq - span), LANES)
        keys = jnp.concatenate([k_ref[pl.ds(start, span), :], kc], axis=0)
        vals = jnp.concatenate([v_ref[pl.ds(start, span), :], vc], axis=0)
        col = lax.broadcasted_iota(jnp.int32, (rows, n_keys), 1)
        dist = first + lax.broadcasted_iota(jnp.int32, (rows, n_keys), 0) - (start + col)
        valid = ((dist <= SWA_WINDOW) & (dist >= -SWA_WINDOW)) | (col >= span)
        _swa_heads(sink_ref, layer, q_ref, keys, vals, valid, o_ref, r0, rows)


def _swa(layer, sink, q, k, v, seq):
    n_batch, t_all, _ = q.shape
    tq = Q_TILE
    params = lambda n: pltpu.CompilerParams(
        dimension_semantics=("arbitrary",) * n, vmem_limit_bytes=VMEM_LIMIT)
    smem = pl.BlockSpec(memory_space=pltpu.SMEM)
    return pl.pallas_call(
        functools.partial(_swa_latent_kernel, layer=layer, seq=seq),
        grid=(n_batch, seq // tq),
        in_specs=[
            smem,
            pl.BlockSpec((None, tq, 2 * LANES), lambda b, i: (b, i, 0)),
            pl.BlockSpec((None, t_all, LANES), lambda b, i: (b, 0, 0)),
            pl.BlockSpec((None, t_all, 2 * LANES), lambda b, i: (b, 0, 0)),
        ],
        out_specs=pl.BlockSpec((None, tq, 2 * LANES), lambda b, i: (b, i, 0)),
        out_shape=jax.ShapeDtypeStruct((n_batch, seq, 2 * LANES), BF16),
        compiler_params=params(2),
        name="window_attention",
    )(sink, q, k, v)


def _ctx_mixers_kernel(sink_ref, q_ref, k_ref, v_ref, sq_ref, sk_ref, sv_ref, a_ref, w_ref, *, layer):
    _mla_heads(q_ref, k_ref, v_ref, a_ref)
    _swa_heads(sink_ref, layer, sq_ref, sk_ref[...], sv_ref[...], None, w_ref)


def _ctx_mixers(layer, sink, q, k, v, sq, sk, sv, seq):
    n_batch, t_all, _ = q.shape
    n_ctx = t_all - seq
    c_blk = seq // n_ctx
    rows_in = lambda t: pl.BlockSpec((None, n_ctx, t.shape[-1]), lambda b: (b, c_blk, 0))
    rows_out = lambda width: pl.BlockSpec((None, n_ctx, width), lambda b: (b, 0, 0))
    widths = (MLA_HEADS * MLA_V, 2 * LANES)
    return pl.pallas_call(
        functools.partial(_ctx_mixers_kernel, layer=layer),
        grid=(n_batch,),
        in_specs=[pl.BlockSpec(memory_space=pltpu.SMEM)] + [rows_in(t) for t in (q, k, v, sq, sk, sv)],
        out_specs=[rows_out(w) for w in widths],
        out_shape=[jax.ShapeDtypeStruct((n_batch, n_ctx, w), BF16) for w in widths],
        compiler_params=pltpu.CompilerParams(
            dimension_semantics=("arbitrary",), vmem_limit_bytes=VMEM_LIMIT),
        name="context_query_mixers",
    )(sink, q, k, v, sq, sk, sv)


def _out_ffn_kernel(h_ref, yf_ref, *refs, parts, hpb, n_lat):
    half = TOK_TILE
    if hpb is None:
        a_ref, w_ref = refs[:2]
        a_all, w_all = a_ref[...], w_ref[...]
        refs = refs[2:]
    else:
        a_parts, w_parts = [], []
        for i in range(parts):
            a_lat, a_ctx, w_lat, w_ctx = refs[4 * i:4 * i + 4]
            is_ctx = (parts * pl.program_id(0) + i) % hpb >= n_lat
            a_parts.append(jnp.where(is_ctx, a_ctx[...], a_lat[...]))
            w_parts.append(jnp.where(is_ctx, w_ctx[...], w_lat[...]))
        a_all, w_all = jnp.concatenate(a_parts, axis=0), jnp.concatenate(w_parts, axis=0)
        refs = refs[4 * parts:]
    mod_refs = refs[:parts]
    wo_ref, g2_ref, w1_ref, w3_ref, w2_ref, o_ref = refs[parts:]
    n_f = yf_ref.shape[-1]
    n_a = a_all.shape[-1]
    mixed = (jnp.dot(yf_ref[...], wo_ref[0:n_f, :], preferred_element_type=F32)
             + jnp.dot(a_all, wo_ref[n_f:n_f + n_a, :], preferred_element_type=F32)
             + jnp.dot(w_all, wo_ref[n_f + n_a:, :], preferred_element_type=F32))
    g2 = g2_ref[...]
    hs, xs = [], []
    for i, mod_ref in enumerate(mod_refs):
        mod = mod_ref[...]
        rows = slice(i * half, (i + 1) * half)
        h = h_ref[rows, :] + mod[5:6] * mixed[rows, :]
        hs.append(h)
        xs.append(_norm_mod(h, g2, mod[6:7], mod[7:8]).astype(BF16))
    y = _swiglu(jnp.concatenate(xs, axis=0), w1_ref, w3_ref, w2_ref)
    for i, mod_ref in enumerate(mod_refs):
        rows = slice(i * half, (i + 1) * half)
        o_ref[rows, :] = hs[i] + (0.5 * mod_ref[8:9, :]) * y[rows, :]


def _out_ffn(layer, hh, yf, a, w, mod, p, seq, with_ctx):
    n_batch, t_all, d = hh.shape
    half = TOK_TILE
    parts = OUT_PARTS if with_ctx else OUT_PARTS_LATENT
    tm = parts * half
    hpb = t_all // half
    n_lat = seq // half
    lw = lambda *shape: _const_spec((None,) + shape, lambda *_: (layer,) + (0,) * len(shape))

    def mod_row(half_index):
        def index(*g):
            j = half_index(*g)
            return (layer, jnp.where(j % hpb >= n_lat, n_batch, j // hpb), 0, 0)
        return pl.BlockSpec((None, None, N_MOD, d), index)

    (a_lat, a_ctx), (w_lat, w_ctx) = a, w
    if with_ctx:
        flat = lambda t: t.reshape(n_batch * t_all, t.shape[-1])
        grid = (n_batch * t_all // tm,)
        rows = lambda width: pl.BlockSpec((tm, width), lambda j: (j, 0))
        part = lambda i: (lambda j: parts * j + i)
        lat = lambda i, width: pl.BlockSpec(
            (None, half, width), lambda j: (part(i)(j) // hpb, jnp.minimum(part(i)(j) % hpb, n_lat - 1), 0))
        con = lambda i, width: pl.BlockSpec(
            (None, half, width), lambda j: (part(i)(j) // hpb, jnp.maximum(part(i)(j) % hpb - n_lat, 0), 0))
        arrays, specs = [flat(hh), flat(yf)], [rows(d), rows(yf.shape[-1])]
        for i in range(parts):
            arrays += [a_lat, a_ctx, w_lat, w_ctx]
            specs += [lat(i, a_lat.shape[-1]), con(i, a_ctx.shape[-1]),
                      lat(i, w_lat.shape[-1]), con(i, w_ctx.shape[-1])]
        mods = [mod_row(part(i)) for i in range(parts)]
        out_shape = jax.ShapeDtypeStruct((n_batch * t_all, d), F32)
    else:
        grid = (n_batch, seq // tm)
        rows = lambda width: pl.BlockSpec((None, tm, width), lambda b, t: (b, t, 0))
        arrays = [hh, yf, a_lat, w_lat]
        specs = [rows(t.shape[-1]) for t in arrays]
        mods = [mod_row(lambda b, t: b * hpb)] * parts
        out_shape = jax.ShapeDtypeStruct((n_batch, seq, d), F32)
    out = pl.pallas_call(
        functools.partial(_out_ffn_kernel, parts=parts, hpb=hpb if with_ctx else None, n_lat=n_lat),
        grid=grid,
        in_specs=specs + mods + [lw(d, d), lw(1, d), lw(d, D_FF), lw(d, D_FF), lw(D_FF, d)],
        out_specs=rows(d),
        out_shape=out_shape,
        compiler_params=pltpu.CompilerParams(
            dimension_semantics=("arbitrary",) * len(grid), vmem_limit_bytes=VMEM_LIMIT),
        name="out_projection_and_ffn_half",
    )(*arrays, *([mod] * parts), p["w_out"], p["g_ffn2"], p["w1_ffn2"], p["w3_ffn2"], p["w2_ffn2"])
    return out.reshape(n_batch, t_all, d) if with_ctx else out


def _prepare_params(g_ffn1, w1_ffn1, w3_ffn1, w2_ffn1, g_mix, w_in, g_cq, w_uq, g_ckv, w_ukv,
                    g_mla_q, g_mla_k, g_swa_q, g_swa_k, w_out, g_ffn2, w1_ffn2, w3_ffn2, w2_ffn2):
    row = lambda g: g[:, None, :]
    two_heads = lambda g: jnp.concatenate([g, g], axis=-1)
    kv = w_ukv.reshape(w_ukv.shape[:-1] + (MLA_HEADS, MLA_NOPE + MLA_V))
    flat = lambda t: t.reshape(t.shape[:-2] + (t.shape[-2] * t.shape[-1],))
    return {
        "g_ffn1": row(g_ffn1), "w1_ffn1": w1_ffn1.astype(BF16), "w3_ffn1": w3_ffn1.astype(BF16),
        "w2_ffn1": w2_ffn1.astype(BF16),
        "g_mix": row(g_mix), "w_in": _take_cols(w_in, _w_in_cols()).astype(BF16),
        "g_cq": row(g_cq), "w_uq": _pad_heads(w_uq, MLA_HEADS).astype(BF16),
        "g_mq": row(_pad_heads(g_mla_q, 1)),
        "g_ckv": row(g_ckv), "w_uk": _pad_heads(flat(kv[..., :MLA_NOPE]), MLA_HEADS).astype(BF16),
        "w_uv": flat(kv[..., MLA_NOPE:]).astype(BF16),
        "g_mk": row(_pad_heads(g_mla_k, 1)),
        "g_sq": row(two_heads(g_swa_q)),
        "g_sk": row(two_heads(g_swa_k)),
        "w_out": _take_cols(w_out, _w_out_rows(), axis=1).astype(BF16),
        "g_ffn2": row(g_ffn2), "w1_ffn2": w1_ffn2.astype(BF16), "w3_ffn2": w3_ffn2.astype(BF16),
        "w2_ffn2": w2_ffn2.astype(BF16),
        "dft64": _channel_dft(),
    }


def kernel(x, c, ctx, c_ctx, w_ada, b_ada, g_ffn1, w1_ffn1, w3_ffn1, w2_ffn1, g_mix, w_in, g_cq, w_uq,
           g_ckv, w_ukv, g_mla_q, g_mla_k, g_swa_q, g_swa_k, sink, w_out, g_ffn2, w1_ffn2, w3_ffn2,
           w2_ffn2):
    n_batch, seq, d = x.shape
    n_ctx = ctx.shape[1]
    depth = w_ada.shape[0]
    assert d == D_MODEL and seq % GRID_W == 0 and n_batch + 1 <= MOD_ROWS
    assert n_ctx % TOK_TILE == 0 and seq % TOK_TILE == 0 and seq % n_ctx == 0
    assert seq % (OUT_PARTS_LATENT * TOK_TILE) == 0 and (n_batch * (seq + n_ctx)) % (OUT_PARTS * TOK_TILE) == 0
    assert seq % Q_TILE == 0 and seq % MLA_Q_TILE == 0
    assert Q_TILE % ROW_CHUNK == 0 and ROW_CHUNK + 2 * SWA_WINDOW <= seq
    assert w_ada.shape[-1] % MOD_COL_TILE == 0

    p = _prepare_params(g_ffn1, w1_ffn1, w3_ffn1, w2_ffn1, g_mix, w_in, g_cq, w_uq, g_ckv, w_ukv,
                        g_mla_q, g_mla_k, g_swa_q, g_swa_k, w_out, g_ffn2, w1_ffn2, w3_ffn2, w2_ffn2)
    tabs = _rope_tables(seq, n_ctx)
    c_lat, s_lat = _dft_cos_sin(seq, seq ** -0.5)
    c_ctx_dft, s_ctx_dft = _dft_cos_sin(n_ctx, n_ctx ** -0.5)
    dfts = tuple(m.astype(BF16) for m in (c_lat, s_lat, c_ctx_dft, s_ctx_dft))

    cc = jnp.concatenate([c, c_ctx[None, :], jnp.zeros((MOD_ROWS - n_batch - 1, d), F32)], axis=0)
    mod = _modulation(cc, w_ada, b_ada).reshape(depth, MOD_ROWS, N_MOD, d)

    tokens = (x, ctx)
    for layer in range(depth):
        with_ctx = layer != depth - 1
        hh, q, k, v, sq, sk, sv, zcs = _ffn_proj(layer, tokens, mod, p, tabs, seq, seq + n_ctx)
        yf = _fourier(zcs, dfts, n_ctx, with_ctx)
        a_ctx, w_ctx = _ctx_mixers(layer, sink, q, k, v, sq, sk, sv, seq) if with_ctx else (None, None)
        a = (_mla(q, k, v, seq), a_ctx)
        w = (_swa(layer, sink, sq, sk, sv, seq), w_ctx)
        hh = _out_ffn(layer, hh, yf, a, w, mod, p, seq, with_ctx)
        tokens = (hh,)
    return hh
```

```python
import functools

import numpy as np
import jax
import jax.numpy as jnp
from jax import lax
from jax.experimental import pallas as pl
from jax.experimental.pallas import tpu as pltpu

F32 = jnp.float32
BF16 = jnp.bfloat16

D_MODEL = 1024
GRID_W = 64
ROPE_BASE = 10000.0
EPS = 1e-6
NEG = -1e30
LOG2E = 1.4426950408889634
N_MOD = 9
D_FF = 2816

FOURIER_WIDTH = 256
FOURIER_GROUP_DIM = 64
MLA_HEADS = 8
MLA_NOPE = 64
MLA_ROPE = 32
MLA_V = 64
MLA_QK_DIM = MLA_NOPE + MLA_ROPE
MLA_Q_RANK = 256
MLA_KV_RANK = 128
SWA_Q_HEADS = 4
SWA_KV_HEADS = 2
SWA_HEAD_DIM = 64
SWA_WINDOW = 128
IN_SPLITS = (256, 256, 128, 32, 256, 128, 128)
IN_WIDTH = sum(IN_SPLITS)

LANES = 128
HALF = LANES // 2
IN_WIDTH_P = 10 * LANES
MLA_V_SLABS = MLA_HEADS
VMEM_LIMIT = 56 * 1024 * 1024

TOK_TILE = 256
OUT_PARTS = 2
OUT_PARTS_LATENT = 2
Q_TILE = 1024
MLA_Q_TILE = 1024
ROW_CHUNK = 256
MOD_ROWS = 24
MOD_COL_TILE = 2304

OFF_F, OFF_CQ, OFF_CKV, OFF_KR, OFF_SQA, OFF_SQB, OFF_SK, OFF_SV = (
    0, 256, 512, 640, 768, 896, 1024, 1152)


def _w_in_cols():
    zero = IN_WIDTH
    o_f, o_cq, o_ckv, o_kr, o_sq, o_sk, o_sv = np.cumsum((0,) + IN_SPLITS)[:-1]
    cols = np.full((IN_WIDTH_P,), zero, np.int32)
    cols[OFF_F:OFF_F + 256] = o_f + np.arange(256)
    cols[OFF_CQ:OFF_CQ + 256] = o_cq + np.arange(256)
    cols[OFF_CKV:OFF_CKV + 128] = o_ckv + np.arange(128)
    cols[OFF_KR + MLA_NOPE:OFF_KR + MLA_QK_DIM] = o_kr + np.arange(MLA_ROPE)
    head = np.arange(SWA_HEAD_DIM)
    cols[OFF_SQA:OFF_SQA + 64] = o_sq + 0 * 64 + head
    cols[OFF_SQA + 64:OFF_SQA + 128] = o_sq + 2 * 64 + head
    cols[OFF_SQB:OFF_SQB + 64] = o_sq + 1 * 64 + head
    cols[OFF_SQB + 64:OFF_SQB + 128] = o_sq + 3 * 64 + head
    cols[OFF_SK:OFF_SK + 64] = o_sk + head
    cols[OFF_SK + 64:OFF_SK + 128] = o_sk + 64 + head
    cols[OFF_SV:OFF_SV + 128] = o_sv + np.arange(128)
    return cols


def _pad_heads(w, n_heads):
    lead, width = w.shape[:-1], w.shape[-1] // n_heads
    w = jnp.pad(w.reshape(lead + (n_heads, width)), [(0, 0)] * (len(lead) + 1) + [(0, LANES - width)])
    return w.reshape(lead + (n_heads * LANES,))


def _w_out_rows():
    base = FOURIER_WIDTH + MLA_HEADS * MLA_V
    swa = np.concatenate([base + h * SWA_HEAD_DIM + np.arange(SWA_HEAD_DIM) for h in (0, 2, 1, 3)])
    return np.concatenate([np.arange(base), swa]).astype(np.int32)


def _take_cols(w, cols, axis=-1):
    axis = axis % w.ndim
    n = w.shape[axis]
    cols = [int(c) for c in cols]
    pieces, i = [], 0
    while i < len(cols):
        j = i + 1
        if cols[i] == n:
            while j < len(cols) and cols[j] == n:
                j += 1
            shape = w.shape[:axis] + (j - i,) + w.shape[axis + 1:]
            pieces.append(jnp.zeros(shape, w.dtype))
        else:
            stride = cols[j] - cols[i] if j < len(cols) and cols[j] - cols[i] in (1, 2) else 1
            while j < len(cols) and cols[j] != n and cols[j] == cols[j - 1] + stride:
                j += 1
            pieces.append(lax.slice_in_dim(w, cols[i], cols[j - 1] + 1, stride, axis))
        i = j
    return jnp.concatenate(pieces, axis=axis)


def _rope_tables(seq, ctx):
    rows = seq // GRID_W
    pad = jnp.zeros((ctx,), F32)
    row = jnp.concatenate([jnp.repeat(jnp.arange(rows, dtype=F32), GRID_W), pad])[:, None]
    col = jnp.concatenate([jnp.tile(jnp.arange(GRID_W, dtype=F32), rows), pad])[:, None]

    def build(dim, section_starts):
        axis_dim = dim // 2
        n_freq = axis_dim // 2
        freq = np.zeros((LANES,), np.float32)
        active = np.zeros((LANES,), bool)
        by_row = np.zeros((LANES,), bool)
        m_a = np.zeros((LANES,), np.float32)
        m_b = np.zeros((LANES,), np.float32)
        pair = np.arange(dim) // 2
        for lo in section_starts:
            freq[lo:lo + dim] = 2 * (pair % n_freq)
            active[lo:lo + dim] = True
            by_row[lo:lo + dim] = pair < n_freq
            m_a[lo:lo + dim:2] = 1.0
            m_b[lo + 1:lo + dim:2] = 1.0
        inv_lane = jnp.where(jnp.asarray(active), ROPE_BASE ** (-jnp.asarray(freq) / axis_dim), 0.0)[None, :]
        ang = jnp.where(jnp.asarray(by_row)[None, :], row * inv_lane, col * inv_lane)
        sin = jnp.sin(ang)
        return jnp.cos(ang), sin * jnp.asarray(-m_a)[None, :], sin * jnp.asarray(m_b)[None, :]

    return build(MLA_ROPE, (MLA_NOPE,)) + build(SWA_HEAD_DIM, (0, HALF))


def _dft_cos_sin(n, scale):
    def direct(rows_j, n_mod):
        k = jnp.arange(n, dtype=jnp.int32)
        ang = ((rows_j[:, None] * k[None, :]) % n_mod).astype(F32) * (2.0 * np.pi / n_mod)
        return jnp.cos(ang), jnp.sin(ang)

    inner = FOURIER_GROUP_DIM
    if n <= inner or n % inner:
        c, s = direct(jnp.arange(n, dtype=jnp.int32), n)
        return c * scale, s * scale
    outer = n // inner
    ca, sa = direct(jnp.arange(outer, dtype=jnp.int32), outer)
    cb, sb = direct(jnp.arange(inner, dtype=jnp.int32), n)
    cb, sb = cb * scale, sb * scale
    c = ca[:, None, :] * cb[None, :, :] - sa[:, None, :] * sb[None, :, :]
    s = sa[:, None, :] * cb[None, :, :] + ca[:, None, :] * sb[None, :, :]
    return c.reshape(n, n), s.reshape(n, n)


def _channel_dft():
    c, s = _dft_cos_sin(FOURIER_GROUP_DIM, FOURIER_GROUP_DIM ** -0.5)
    eye = jnp.eye(FOURIER_WIDTH // FOURIER_GROUP_DIM, dtype=F32)
    return jnp.concatenate([jnp.kron(eye, c), jnp.kron(eye, s)], axis=1).astype(BF16)


def _rms_scale(x, width):
    return lax.rsqrt(jnp.sum(x * x, axis=-1, keepdims=True) * (1.0 / width) + EPS)


def _norm_mod(x, g, shift, scale):
    y = x * _rms_scale(x, x.shape[-1]) * g
    return y * (1.0 + scale) + shift


def _swiglu(xn, w1_ref, w3_ref, w2_ref):
    a = jnp.dot(xn, w1_ref[...], preferred_element_type=F32)
    b = jnp.dot(xn, w3_ref[...], preferred_element_type=F32)
    g = (a / (1.0 + jnp.exp(-a))) * b
    return jnp.dot(g.astype(BF16), w2_ref[...], preferred_element_type=F32)


def _rope(x, cos, sin_a, sin_b):
    return x * cos + pltpu.roll(x, LANES - 1, 1) * sin_a + pltpu.roll(x, 1, 1) * sin_b


def _low_lanes(shape):
    return lax.broadcasted_iota(jnp.int32, shape, len(shape) - 1) < HALF


def _mod_kernel(c_ref, w_ref, b_ref, o_ref):
    cv = c_ref[...]
    s = (cv / (1.0 + jnp.exp(-cv))).astype(BF16)
    o_ref[...] = jnp.dot(s, w_ref[...].astype(BF16), preferred_element_type=F32) + b_ref[...]


def _modulation(cc, w_ada, b_ada):
    n_layers, d, width = w_ada.shape
    return pl.pallas_call(
        _mod_kernel,
        grid=(n_layers, width // MOD_COL_TILE),
        in_specs=[
            pl.BlockSpec((MOD_ROWS, d), lambda l, j: (0, 0)),
            pl.BlockSpec((None, d, MOD_COL_TILE), lambda l, j: (l, 0, j)),
            pl.BlockSpec((None, 1, MOD_COL_TILE), lambda l, j: (l, 0, j)),
        ],
        out_specs=pl.BlockSpec((None, MOD_ROWS, MOD_COL_TILE), lambda l, j: (l, 0, j)),
        out_shape=jax.ShapeDtypeStruct((n_layers, MOD_ROWS, width), F32),
        compiler_params=pltpu.CompilerParams(
            dimension_semantics=("arbitrary", "arbitrary"), vmem_limit_bytes=VMEM_LIMIT),
        name="adaln_modulation",
    )(cc, w_ada, b_ada.reshape(n_layers, 1, width))


def _ffn_proj_kernel(*refs, n_src, n_tiles, tpb, n_lat):
    h_refs = refs[:n_src]
    (mod_ref, modp_ref, g1_ref, w1_ref, w3_ref, w2_ref, gmix_ref, win_ref,
     gcq_ref, wuq_ref, gmq_ref, gckv_ref, wukv_ref, gmk_ref, gsq_ref, gsk_ref, dft_ref,
     cm_ref, sam_ref, sbm_ref, cs_ref, sas_ref, sbs_ref,
     ho_ref, q_ref, k_ref, v_ref, sq_ref, sk_ref, sv_ref, z_ref, hprev_ref) = refs[n_src:]
    step = pl.program_id(0)

    @pl.when(step == 0)
    def _():
        hprev_ref[...] = jnp.zeros_like(hprev_ref)

    modp = modp_ref[...]
    n = _norm_mod(hprev_ref[...], gmix_ref[...], modp[3:4], modp[4:5]).astype(BF16)
    u = jnp.dot(n, win_ref[...], preferred_element_type=F32)

    f = u[:, OFF_F:OFF_F + FOURIER_WIDTH].astype(BF16)
    z_ref[...] = jnp.dot(f, dft_ref[...], preferred_element_type=F32).astype(BF16)

    cm, sam, sbm = cm_ref[...], sam_ref[...], sbm_ref[...]
    cq = u[:, OFF_CQ:OFF_CQ + MLA_Q_RANK]
    cqn = (cq * _rms_scale(cq, MLA_Q_RANK) * gcq_ref[...]).astype(BF16)
    q = jnp.dot(cqn, wuq_ref[...], preferred_element_type=F32)
    gmq = gmq_ref[...]
    q_scale = MLA_QK_DIM ** -0.5 * LOG2E
    for hd in range(MLA_HEADS):
        qh = q[:, hd * LANES:(hd + 1) * LANES]
        qg = qh * (_rms_scale(qh, MLA_QK_DIM) * q_scale) * gmq
        q_ref[:, hd * LANES:(hd + 1) * LANES] = _rope(qg, cm, sam, sbm).astype(BF16)

    ckv = u[:, OFF_CKV:OFF_CKV + MLA_KV_RANK]
    ckvn = (ckv * _rms_scale(ckv, MLA_KV_RANK) * gckv_ref[...]).astype(BF16)
    kv_up = jnp.dot(ckvn, wukv_ref[...], preferred_element_type=F32)
    kn = kv_up[:, 0:MLA_HEADS * LANES]
    vv = kv_up[:, MLA_HEADS * LANES:].astype(BF16)
    ones = jnp.ones((vv.shape[0], LANES), BF16)
    for pair in range(MLA_HEADS // 2):
        v_ref[:, 2 * pair * LANES:(2 * pair + 1) * LANES] = vv[:, pair * LANES:(pair + 1) * LANES]
        v_ref[:, (2 * pair + 1) * LANES:(2 * pair + 2) * LANES] = ones
    gmk = gmk_ref[...]
    kr = u[:, OFF_KR:OFF_KR + LANES]
    kr_ss = jnp.sum(kr * kr, axis=-1, keepdims=True)
    kr_rot = _rope(kr * gmk, cm, sam, sbm)
    for hd in range(MLA_HEADS):
        kh = kn[:, hd * LANES:(hd + 1) * LANES]
        ss = jnp.sum(kh * kh, axis=-1, keepdims=True) + kr_ss
        rs = lax.rsqrt(ss * (1.0 / MLA_QK_DIM) + EPS)
        k_ref[:, hd * LANES:(hd + 1) * LANES] = (rs * (kh * gmk + kr_rot)).astype(BF16)

    cs, sas, sbs = cs_ref[...], sas_ref[...], sbs_ref[...]

    def two_head_norm_rope(x, g, scale):
        low = _low_lanes(x.shape)
        x2 = x * x
        lo = jnp.sum(jnp.where(low, x2, 0.0), axis=-1, keepdims=True)
        hi = jnp.sum(jnp.where(low, 0.0, x2), axis=-1, keepdims=True)
        rs = jnp.where(low, lax.rsqrt(lo * (1.0 / SWA_HEAD_DIM) + EPS),
                       lax.rsqrt(hi * (1.0 / SWA_HEAD_DIM) + EPS))
        return _rope(x * (rs * scale) * g, cs, sas, sbs).astype(BF16)

    gsq = gsq_ref[...]
    s_scale = SWA_HEAD_DIM ** -0.5 * LOG2E
    sq_ref[:, 0:LANES] = two_head_norm_rope(u[:, OFF_SQA:OFF_SQA + LANES], gsq, s_scale)
    sq_ref[:, LANES:2 * LANES] = two_head_norm_rope(u[:, OFF_SQB:OFF_SQB + LANES], gsq, s_scale)
    sk_ref[...] = two_head_norm_rope(u[:, OFF_SK:OFF_SK + LANES], gsk_ref[...], 1.0)
    sv_ref[:, 0:LANES] = u[:, OFF_SV:OFF_SV + LANES].astype(BF16)
    sv_ref[:, LANES:2 * LANES] = ones

    mod = mod_ref[...]
    if n_src == 1:
        h = h_refs[0][...]
    else:
        cur = jnp.minimum(step, n_tiles - 1)
        h = jnp.where(cur % tpb >= n_lat, h_refs[1][...], h_refs[0][...])
    xn = _norm_mod(h, g1_ref[...], mod[0:1], mod[1:2]).astype(BF16)
    h = h + (0.5 * mod[2:3]) * _swiglu(xn, w1_ref, w3_ref, w2_ref)
    ho_ref[...] = h
    hprev_ref[...] = h


def _const_spec(block_shape, index_map):
    return pl.BlockSpec(block_shape, index_map, pipeline_mode=pl.Buffered(1))


def _ffn_proj(layer, tokens, mod, p, tabs, seq, t_all):
    n_batch, _, d = tokens[0].shape
    tm = TOK_TILE
    tpb = t_all // tm
    n_tiles = n_batch * tpb
    n_lat = seq // tm
    cur = lambda s: jnp.minimum(s, n_tiles - 1)
    prev = lambda s: jnp.maximum(s - 1, 0)
    tile = lambda which, width: pl.BlockSpec(
        (None, tm, width), lambda s: (which(s) // tpb, which(s) % tpb, 0))

    def token_specs(which):
        if len(tokens) == 1:
            return [tile(which, d)]
        lat = pl.BlockSpec((None, tm, d),
                           lambda s: (which(s) // tpb, jnp.minimum(which(s) % tpb, n_lat - 1), 0))
        con = pl.BlockSpec((None, tm, d),
                           lambda s: (which(s) // tpb, jnp.maximum(which(s) % tpb - n_lat, 0), 0))
        return [lat, con]

    def mod_row(which):
        def index(s):
            b, t = which(s) // tpb, which(s) % tpb
            return (layer, jnp.where(t >= n_lat, n_batch, b), 0, 0)
        return pl.BlockSpec((None, None, N_MOD, d), index)

    lw = lambda *shape: _const_spec((None,) + shape, lambda s: (layer,) + (0,) * len(shape))
    tab = pl.BlockSpec((tm, LANES), lambda s: (prev(s) % tpb, 0))
    in_specs = token_specs(cur) + [
        mod_row(cur), mod_row(prev),
        lw(1, d), lw(d, D_FF), lw(d, D_FF), lw(D_FF, d), lw(1, d), lw(d, IN_WIDTH_P),
        lw(1, MLA_Q_RANK), lw(MLA_Q_RANK, MLA_HEADS * LANES), lw(1, LANES),
        lw(1, MLA_KV_RANK), lw(MLA_KV_RANK, MLA_HEADS * (LANES + MLA_V)),
        lw(1, LANES), lw(1, LANES), lw(1, LANES),
        _const_spec((FOURIER_WIDTH, 2 * FOURIER_WIDTH), lambda s: (0, 0)),
        tab, tab, tab, tab, tab, tab,
    ]
    widths = (d, MLA_HEADS * LANES, MLA_HEADS * LANES, MLA_V_SLABS * LANES, 2 * LANES, LANES, 2 * LANES,
              2 * FOURIER_WIDTH)
    dtypes = (F32,) + (BF16,) * 7
    return pl.pallas_call(
        functools.partial(_ffn_proj_kernel, n_src=len(tokens), n_tiles=n_tiles, tpb=tpb, n_lat=n_lat),
        grid=(n_tiles + 1,),
        in_specs=in_specs,
        out_specs=[tile(cur, d)] + [tile(prev, w) for w in widths[1:]],
        out_shape=[jax.ShapeDtypeStruct((n_batch, t_all, w), dt) for w, dt in zip(widths, dtypes)],
        scratch_shapes=[pltpu.VMEM((tm, d), F32)],
        compiler_params=pltpu.CompilerParams(
            dimension_semantics=("arbitrary",), vmem_limit_bytes=VMEM_LIMIT),
        name="ffn_half_and_projections",
    )(*tokens, mod, mod, p["g_ffn1"], p["w1_ffn1"], p["w3_ffn1"], p["w2_ffn1"], p["g_mix"], p["w_in"],
      p["g_cq"], p["w_uq"], p["g_mq"], p["g_ckv"], p["w_ukv"], p["g_mk"],
      p["g_sq"], p["g_sk"], p["dft64"], *tabs)


def _fourier_kernel(z_ref, cl_ref, sl_ref, cc_ref, sc_ref, o_ref, *, ctx, with_ctx):
    w = FOURIER_WIDTH

    def mix(c_ref, s_ref, lo, n):
        zc = z_ref[lo:lo + n, 0:w]
        zs = z_ref[lo:lo + n, w:2 * w]
        return (jnp.dot(c_ref[...], zc, preferred_element_type=F32)
                - jnp.dot(s_ref[...], zs, preferred_element_type=F32)).astype(BF16)

    n_lat = cl_ref.shape[0]
    o_ref[0:n_lat, :] = mix(cl_ref, sl_ref, 0, n_lat)
    if with_ctx:
        o_ref[n_lat:n_lat + ctx, :] = mix(cc_ref, sc_ref, n_lat, ctx)


def _fourier(zcs, dfts, ctx, with_ctx):
    n_batch, t_all, _ = zcs.shape
    seq = t_all - ctx
    rows = t_all if with_ctx else seq
    full = lambda a: _const_spec(a.shape, lambda b: (0, 0))
    return pl.pallas_call(
        functools.partial(_fourier_kernel, ctx=ctx, with_ctx=with_ctx),
        grid=(n_batch,),
        in_specs=[pl.BlockSpec((None, t_all, 2 * FOURIER_WIDTH), lambda b: (b, 0, 0))]
        + [full(a) for a in dfts],
        out_specs=pl.BlockSpec((None, rows, FOURIER_WIDTH), lambda b: (b, 0, 0)),
        out_shape=jax.ShapeDtypeStruct((n_batch, rows, FOURIER_WIDTH), BF16),
        compiler_params=pltpu.CompilerParams(
            dimension_semantics=("arbitrary",), vmem_limit_bytes=VMEM_LIMIT),
        name="fourier_positions",
    )(zcs, *dfts)


def _scores(q, k):
    return lax.dot_general(q, k, (((1,), (1,)), ((), ())), preferred_element_type=F32)


def _mla_heads(q_ref, k_ref, v_ref, o_ref, head0_scores=None):
    n_heads = q_ref.shape[-1] // LANES
    for pair in range(n_heads // 2):
        outs = []
        vp = v_ref[:, 2 * pair * LANES:(2 * pair + 2) * LANES]
        for hd in (2 * pair, 2 * pair + 1):
            if hd == 0 and head0_scores is not None:
                s = head0_scores
            else:
                s = _scores(q_ref[:, hd * LANES:(hd + 1) * LANES], k_ref[:, hd * LANES:(hd + 1) * LANES])
            m = jnp.max(s, axis=-1, keepdims=True)
            p = jnp.exp2(s - m).astype(BF16)
            r = jnp.dot(p, vp, preferred_element_type=F32)
            outs.append(r[:, 0:LANES] / r[:, LANES:2 * LANES])
        o_ref[:, pair * LANES:(pair + 1) * LANES] = jnp.where(
            _low_lanes(outs[0].shape), outs[0], outs[1]).astype(BF16)


def _mla_latent_kernel(q_ref, k_ref, v_ref, qn_ref, kn_ref, o_ref, s0_ref):
    @pl.when(pl.program_id(0) == 0)
    def _():
        s0_ref[...] = _scores(q_ref[:, 0:LANES], k_ref[:, 0:LANES])

    _mla_heads(q_ref, k_ref, v_ref, o_ref, head0_scores=s0_ref[...])
    s0_ref[...] = _scores(qn_ref[...], kn_ref[...])


def _mla(q, k, v, seq):
    n_batch, t_all, _ = q.shape
    tq = MLA_Q_TILE
    tpb = seq // tq
    n_tiles = n_batch * tpb
    qk_w = MLA_HEADS * LANES
    v_in = MLA_V_SLABS * LANES
    v_w = MLA_HEADS * MLA_V
    params = lambda n: pltpu.CompilerParams(
        dimension_semantics=("arbitrary",) * n, vmem_limit_bytes=VMEM_LIMIT)
    nxt = lambda s: jnp.minimum(s + 1, n_tiles - 1)
    return pl.pallas_call(
        _mla_latent_kernel,
        grid=(n_tiles,),
        in_specs=[
            pl.BlockSpec((None, tq, qk_w), lambda s: (s // tpb, s % tpb, 0)),
            pl.BlockSpec((None, t_all, qk_w), lambda s: (s // tpb, 0, 0)),
            pl.BlockSpec((None, t_all, v_in), lambda s: (s // tpb, 0, 0)),
            pl.BlockSpec((None, tq, LANES), lambda s: (nxt(s) // tpb, nxt(s) % tpb, 0)),
            pl.BlockSpec((None, t_all, LANES), lambda s: (nxt(s) // tpb, 0, 0)),
        ],
        out_specs=pl.BlockSpec((None, tq, v_w), lambda s: (s // tpb, s % tpb, 0)),
        out_shape=jax.ShapeDtypeStruct((n_batch, seq, v_w), BF16),
        scratch_shapes=[pltpu.VMEM((tq, t_all), F32)],
        compiler_params=params(1),
        name="mla_attention",
    )(q, k, v, q, k)


def _swa_heads(sink_ref, layer, q_ref, keys, vals, valid, o_ref, r0=0, rows=None):
    rows = q_ref.shape[0] if rows is None else rows
    res = []
    for hq in range(SWA_Q_HEADS):
        slab = hq % 2
        use_low = hq < 2
        qs = q_ref[r0:r0 + rows, slab * LANES:(slab + 1) * LANES]
        low = _low_lanes(qs.shape)
        qm = jnp.where(low if use_low else jnp.logical_not(low), qs, jnp.zeros_like(qs))
        sink = sink_ref[layer, hq] * LOG2E
        s = _scores(qm, keys)
        if valid is not None:
            s = jnp.where(valid, s, NEG)
        m = jnp.maximum(jnp.max(s, axis=-1, keepdims=True), sink)
        p = jnp.exp2(s - m).astype(BF16)
        r = jnp.dot(p, vals, preferred_element_type=F32)
        res.append(r[:, 0:LANES] / (r[:, LANES:2 * LANES] + jnp.exp2(sink - m)))
    low = _low_lanes(res[0].shape)
    o_ref[r0:r0 + rows, 0:LANES] = jnp.where(low, res[0], res[2]).astype(BF16)
    o_ref[r0:r0 + rows, LANES:2 * LANES] = jnp.where(low, res[1], res[3]).astype(BF16)


def _swa_latent_kernel(sink_ref, q_ref, k_ref, v_ref, o_ref, *, layer, seq):
    tq = q_ref.shape[0]
    rows = min(tq, ROW_CHUNK)
    span = rows + 2 * SWA_WINDOW
    n_keys = span + k_ref.shape[0] - seq
    kc, vc = k_ref[seq:, :], v_ref[seq:, :]
    for r0 in range(0, tq, rows):
        first = pl.program_id(1) * tq + r0
        start = pl.multiple_of(jnp.clip(first - SWA_WINDOW, 0, seq - span), LANES)
        keys = jnp.concatenate([k_ref[pl.ds(start, span), :], kc], axis=0)
        vals = jnp.concatenate([v_ref[pl.ds(start, span), :], vc], axis=0)
        col = lax.broadcasted_iota(jnp.int32, (rows, n_keys), 1)
        dist = first + lax.broadcasted_iota(jnp.int32, (rows, n_keys), 0) - (start + col)
        valid = ((dist <= SWA_WINDOW) & (dist >= -SWA_WINDOW)) | (col >= span)
        _swa_heads(sink_ref, layer, q_ref, keys, vals, valid, o_ref, r0, rows)


def _swa(layer, sink, q, k, v, seq):
    n_batch, t_all, _ = q.shape
    tq = Q_TILE
    params = lambda n: pltpu.CompilerParams(
        dimension_semantics=("arbitrary",) * n, vmem_limit_bytes=VMEM_LIMIT)
    smem = pl.BlockSpec(memory_space=pltpu.SMEM)
    return pl.pallas_call(
        functools.partial(_swa_latent_kernel, layer=layer, seq=seq),
        grid=(n_batch, seq // tq),
        in_specs=[
            smem,
            pl.BlockSpec((None, tq, 2 * LANES), lambda b, i: (b, i, 0)),
            pl.BlockSpec((None, t_all, LANES), lambda b, i: (b, 0, 0)),
            pl.BlockSpec((None, t_all, 2 * LANES), lambda b, i: (b, 0, 0)),
        ],
        out_specs=pl.BlockSpec((None, tq, 2 * LANES), lambda b, i: (b, i, 0)),
        out_shape=jax.ShapeDtypeStruct((n_batch, seq, 2 * LANES), BF16),
        compiler_params=params(2),
        name="window_attention",
    )(sink, q, k, v)


def _ctx_mixers_kernel(sink_ref, q_ref, k_ref, v_ref, sq_ref, sk_ref, sv_ref, a_ref, w_ref, *, layer):
    _mla_heads(q_ref, k_ref, v_ref, a_ref)
    _swa_heads(sink_ref, layer, sq_ref, sk_ref[...], sv_ref[...], None, w_ref)


def _ctx_mixers(layer, sink, q, k, v, sq, sk, sv, seq):
    n_batch, t_all, _ = q.shape
    n_ctx = t_all - seq
    c_blk = seq // n_ctx
    rows_in = lambda t: pl.BlockSpec((None, n_ctx, t.shape[-1]), lambda b: (b, c_blk, 0))
    rows_out = lambda width: pl.BlockSpec((None, n_ctx, width), lambda b: (b, 0, 0))
    widths = (MLA_HEADS * MLA_V, 2 * LANES)
    return pl.pallas_call(
        functools.partial(_ctx_mixers_kernel, layer=layer),
        grid=(n_batch,),
        in_specs=[pl.BlockSpec(memory_space=pltpu.SMEM)] + [rows_in(t) for t in (q, k, v, sq, sk, sv)],
        out_specs=[rows_out(w) for w in widths],
        out_shape=[jax.ShapeDtypeStruct((n_batch, n_ctx, w), BF16) for w in widths],
        compiler_params=pltpu.CompilerParams(
            dimension_semantics=("arbitrary",), vmem_limit_bytes=VMEM_LIMIT),
        name="context_query_mixers",
    )(sink, q, k, v, sq, sk, sv)


def _out_ffn_kernel(h_ref, yf_ref, *refs, parts, hpb, n_lat):
    half = TOK_TILE
    if hpb is None:
        a_ref, w_ref = refs[:2]
        a_all, w_all = a_ref[...], w_ref[...]
        refs = refs[2:]
    else:
        a_parts, w_parts = [], []
        for i in range(parts):
            a_lat, a_ctx, w_lat, w_ctx = refs[4 * i:4 * i + 4]
            is_ctx = (parts * pl.program_id(0) + i) % hpb >= n_lat
            a_parts.append(jnp.where(is_ctx, a_ctx[...], a_lat[...]))
            w_parts.append(jnp.where(is_ctx, w_ctx[...], w_lat[...]))
        a_all, w_all = jnp.concatenate(a_parts, axis=0), jnp.concatenate(w_parts, axis=0)
        refs = refs[4 * parts:]
    mod_refs = refs[:parts]
    wo_ref, g2_ref, w1_ref, w3_ref, w2_ref, o_ref = refs[parts:]
    n_f = yf_ref.shape[-1]
    n_a = a_all.shape[-1]
    mixed = (jnp.dot(yf_ref[...], wo_ref[0:n_f, :], preferred_element_type=F32)
             + jnp.dot(a_all, wo_ref[n_f:n_f + n_a, :], preferred_element_type=F32)
             + jnp.dot(w_all, wo_ref[n_f + n_a:, :], preferred_element_type=F32))
    g2 = g2_ref[...]
    hs, xs = [], []
    for i, mod_ref in enumerate(mod_refs):
        mod = mod_ref[...]
        rows = slice(i * half, (i + 1) * half)
        h = h_ref[rows, :] + mod[5:6] * mixed[rows, :]
        hs.append(h)
        xs.append(_norm_mod(h, g2, mod[6:7], mod[7:8]).astype(BF16))
    y = _swiglu(jnp.concatenate(xs, axis=0), w1_ref, w3_ref, w2_ref)
    for i, mod_ref in enumerate(mod_refs):
        rows = slice(i * half, (i + 1) * half)
        o_ref[rows, :] = hs[i] + (0.5 * mod_ref[8:9, :]) * y[rows, :]


def _out_ffn(layer, hh, yf, a, w, mod, p, seq, with_ctx):
    n_batch, t_all, d = hh.shape
    half = TOK_TILE
    parts = OUT_PARTS if with_ctx else OUT_PARTS_LATENT
    tm = parts * half
    hpb = t_all // half
    n_lat = seq // half
    lw = lambda *shape: _const_spec((None,) + shape, lambda *_: (layer,) + (0,) * len(shape))

    def mod_row(half_index):
        def index(*g):
            j = half_index(*g)
            return (layer, jnp.where(j % hpb >= n_lat, n_batch, j // hpb), 0, 0)
        return pl.BlockSpec((None, None, N_MOD, d), index)

    (a_lat, a_ctx), (w_lat, w_ctx) = a, w
    if with_ctx:
        flat = lambda t: t.reshape(n_batch * t_all, t.shape[-1])
        grid = (n_batch * t_all // tm,)
        rows = lambda width: pl.BlockSpec((tm, width), lambda j: (j, 0))
        part = lambda i: (lambda j: parts * j + i)
        lat = lambda i, width: pl.BlockSpec(
            (None, half, width), lambda j: (part(i)(j) // hpb, jnp.minimum(part(i)(j) % hpb, n_lat - 1), 0))
        con = lambda i, width: pl.BlockSpec(
            (None, half, width), lambda j: (part(i)(j) // hpb, jnp.maximum(part(i)(j) % hpb - n_lat, 0), 0))
        arrays, specs = [flat(hh), flat(yf)], [rows(d), rows(yf.shape[-1])]
        for i in range(parts):
            arrays += [a_lat, a_ctx, w_lat, w_ctx]
            specs += [lat(i, a_lat.shape[-1]), con(i, a_ctx.shape[-1]),
                      lat(i, w_lat.shape[-1]), con(i, w_ctx.shape[-1])]
        mods = [mod_row(part(i)) for i in range(parts)]
        out_shape = jax.ShapeDtypeStruct((n_batch * t_all, d), F32)
    else:
        grid = (n_batch, seq // tm)
        rows = lambda width: pl.BlockSpec((None, tm, width), lambda b, t: (b, t, 0))
        arrays = [hh, yf, a_lat, w_lat]
        specs = [rows(t.shape[-1]) for t in arrays]
        mods = [mod_row(lambda b, t: b * hpb)] * parts
        out_shape = jax.ShapeDtypeStruct((n_batch, seq, d), F32)
    out = pl.pallas_call(
        functools.partial(_out_ffn_kernel, parts=parts, hpb=hpb if with_ctx else None, n_lat=n_lat),
        grid=grid,
        in_specs=specs + mods + [lw(d, d), lw(1, d), lw(d, D_FF), lw(d, D_FF), lw(D_FF, d)],
        out_specs=rows(d),
        out_shape=out_shape,
        compiler_params=pltpu.CompilerParams(
            dimension_semantics=("arbitrary",) * len(grid), vmem_limit_bytes=VMEM_LIMIT),
        name="out_projection_and_ffn_half",
    )(*arrays, *([mod] * parts), p["w_out"], p["g_ffn2"], p["w1_ffn2"], p["w3_ffn2"], p["w2_ffn2"])
    return out.reshape(n_batch, t_all, d) if with_ctx else out


def _prepare_params(g_ffn1, w1_ffn1, w3_ffn1, w2_ffn1, g_mix, w_in, g_cq, w_uq, g_ckv, w_ukv,
                    g_mla_q, g_mla_k, g_swa_q, g_swa_k, w_out, g_ffn2, w1_ffn2, w3_ffn2, w2_ffn2):
    row = lambda g: g[:, None, :]
    two_heads = lambda g: jnp.concatenate([g, g], axis=-1)
    kv = w_ukv.reshape(w_ukv.shape[:-1] + (MLA_HEADS, MLA_NOPE + MLA_V))
    flat = lambda t: t.reshape(t.shape[:-2] + (t.shape[-2] * t.shape[-1],))
    return {
        "g_ffn1": row(g_ffn1), "w1_ffn1": w1_ffn1.astype(BF16), "w3_ffn1": w3_ffn1.astype(BF16),
        "w2_ffn1": w2_ffn1.astype(BF16),
        "g_mix": row(g_mix), "w_in": _take_cols(w_in, _w_in_cols()).astype(BF16),
        "g_cq": row(g_cq), "w_uq": _pad_heads(w_uq, MLA_HEADS).astype(BF16),
        "g_mq": row(_pad_heads(g_mla_q, 1)),
        "g_ckv": row(g_ckv),
        "w_ukv": jnp.concatenate([_pad_heads(flat(kv[..., :MLA_NOPE]), MLA_HEADS),
                                  flat(kv[..., MLA_NOPE:])], axis=-1).astype(BF16),
        "g_mk": row(_pad_heads(g_mla_k, 1)),
        "g_sq": row(two_heads(g_swa_q)),
        "g_sk": row(two_heads(g_swa_k)),
        "w_out": _take_cols(w_out, _w_out_rows(), axis=1).astype(BF16),
        "g_ffn2": row(g_ffn2), "w1_ffn2": w1_ffn2.astype(BF16), "w3_ffn2": w3_ffn2.astype(BF16),
        "w2_ffn2": w2_ffn2.astype(BF16),
        "dft64": _channel_dft(),
    }


def kernel(x, c, ctx, c_ctx, w_ada, b_ada, g_ffn1, w1_ffn1, w3_ffn1, w2_ffn1, g_mix, w_in, g_cq, w_uq,
           g_ckv, w_ukv, g_mla_q, g_mla_k, g_swa_q, g_swa_k, sink, w_out, g_ffn2, w1_ffn2, w3_ffn2,
           w2_ffn2):
    n_batch, seq, d = x.shape
    n_ctx = ctx.shape[1]
    depth = w_ada.shape[0]
    assert d == D_MODEL and seq % GRID_W == 0 and n_batch + 1 <= MOD_ROWS
    assert n_ctx % TOK_TILE == 0 and seq % TOK_TILE == 0 and seq % n_ctx == 0
    assert seq % (OUT_PARTS_LATENT * TOK_TILE) == 0 and (n_batch * (seq + n_ctx)) % (OUT_PARTS * TOK_TILE) == 0
    assert seq % Q_TILE == 0 and seq % MLA_Q_TILE == 0
    assert Q_TILE % ROW_CHUNK == 0 and ROW_CHUNK + 2 * SWA_WINDOW <= seq
    assert w_ada.shape[-1] % MOD_COL_TILE == 0

    p = _prepare_params(g_ffn1, w1_ffn1, w3_ffn1, w2_ffn1, g_mix, w_in, g_cq, w_uq, g_ckv, w_ukv,
                        g_mla_q, g_mla_k, g_swa_q, g_swa_k, w_out, g_ffn2, w1_ffn2, w3_ffn2, w2_ffn2)
    tabs = _rope_tables(seq, n_ctx)
    c_lat, s_lat = _dft_cos_sin(seq, seq ** -0.5)
    c_ctx_dft, s_ctx_dft = _dft_cos_sin(n_ctx, n_ctx ** -0.5)
    dfts = tuple(m.astype(BF16) for m in (c_lat, s_lat, c_ctx_dft, s_ctx_dft))

    cc = jnp.concatenate([c, c_ctx[None, :], jnp.zeros((MOD_ROWS - n_batch - 1, d), F32)], axis=0)
    mod = _modulation(cc, w_ada, b_ada).reshape(depth, MOD_ROWS, N_MOD, d)

    tokens = (x, ctx)
    for layer in range(depth):
        with_ctx = layer != depth - 1
        hh, q, k, v, sq, sk, sv, zcs = _ffn_proj(layer, tokens, mod, p, tabs, seq, seq + n_ctx)
        yf = _fourier(zcs, dfts, n_ctx, with_ctx)
        a_ctx, w_ctx = _ctx_mixers(layer, sink, q, k, v, sq, sk, sv, seq) if with_ctx else (None, None)
        a = (_mla(q, k, v, seq), a_ctx)
        w = (_swa(layer, sink, sq, sk, sv, seq), w_ctx)
        hh = _out_ffn(layer, hh, yf, a, w, mod, p, seq, with_ctx)
        tokens = (hh,)
    return hh
```
